```python
import jax
import jax.numpy as jnp
from jax import lax
import numpy as np

D_MODEL = 2048
BATCH = 2
SEQ = 16384
DEPTH = 1
DEC_BATCH = 8
DEC_SEQ = 16
PAST_LEN = 4096

CHUNK = 64
N_MEM = 256
Q_BLOCK = 128
EPS = 1e-6
MLA_HEADS = D_MODEL // 256
QK_NOPE = 128
QK_ROPE = 64
QK_HEAD = QK_NOPE + QK_ROPE
V_DIM = 128
Q_RANK = 512
KV_RANK = 512
ROPE_THETA = 10000.0
GDN_HEADS = D_MODEL // 256
GDN_DK = 128
GDN_DV = 128
CONV_W = 4
GDN_CONV_DIM = 2 * GDN_HEADS * GDN_DK + GDN_HEADS * GDN_DV
MEM_HEADS = 4
MEM_DIM = 128
D_FF = 4 * D_MODEL
MLA_WIDTH = MLA_HEADS * V_DIM
GDN_WIDTH = GDN_HEADS * GDN_DV
D_MIX = MLA_WIDTH + GDN_WIDTH
IN_SPLITS = (Q_RANK, Q_RANK + KV_RANK, Q_RANK + KV_RANK + QK_ROPE, Q_RANK + KV_RANK + QK_ROPE + GDN_CONV_DIM, Q_RANK + KV_RANK + QK_ROPE + GDN_CONV_DIM + GDN_WIDTH, Q_RANK + KV_RANK + QK_ROPE + GDN_CONV_DIM + GDN_WIDTH + GDN_HEADS)
D_IN = Q_RANK + KV_RANK + QK_ROPE + GDN_CONV_DIM + GDN_WIDTH + 2 * GDN_HEADS

kernel_name = 'mla_gdn_hybrid_stream_step'


def rms_norm(x, g):
    xf = x.astype(jnp.float32)
    xf = xf * lax.rsqrt(jnp.mean(xf * xf, axis=-1, keepdims=True) + EPS)
    return (xf * g.astype(jnp.float32)).astype(x.dtype)


def l2_norm(x):
    return x * lax.rsqrt(jnp.sum(x * x, axis=-1, keepdims=True) + EPS)


def rope(x, pos):
    half = x.shape[-1] // 2
    inv_freq = ROPE_THETA ** (-jnp.arange(half, dtype=jnp.float32) / half)
    ang = pos.astype(jnp.float32)[:, None] * inv_freq[None, :]
    cos = jnp.cos(ang)[:, None, :]
    sin = jnp.sin(ang)[:, None, :]
    xf = x.astype(jnp.float32)
    x1, x2 = xf[..., :half], xf[..., half:]
    return jnp.concatenate([x1 * cos - x2 * sin, x1 * sin + x2 * cos], axis=-1).astype(x.dtype)


def chunk_causal_attention(q, k, v, q_pos, k_pos):
    scale = q.shape[-1] ** -0.5
    s = jnp.einsum('bqhd,bkhd->bhqk', q, k).astype(jnp.float32) * scale
    allowed = (k_pos[None, :] // CHUNK) <= (q_pos[:, None] // CHUNK)
    s = jnp.where(allowed[None, None], s, -jnp.inf)
    p = jax.nn.softmax(s, axis=-1).astype(v.dtype)
    return jnp.einsum('bhqk,bkhd->bqhd', p, v)


def mla_mixer(c_q, c_kv, k_pe, pos, ckv_past, kpe_past, g_cq, w_uq, g_ckv, w_ukv, g_q, g_k, g_out):
    B, S = c_q.shape[:2]
    q = (rms_norm(c_q, g_cq) @ w_uq).reshape(B, S, MLA_HEADS, QK_HEAD)
    q = jnp.concatenate([q[..., :QK_NOPE], rope(q[..., QK_NOPE:], pos)], axis=-1)
    q = rms_norm(q, g_q)
    ckv_new = rms_norm(c_kv, g_ckv)
    kpe_new = rope(k_pe[:, :, None, :], pos)[:, :, 0]
    if ckv_past is None:
        ckv_all, kpe_all, k_pos = ckv_new, kpe_new, pos
    else:
        ckv_all = jnp.concatenate([ckv_past.astype(ckv_new.dtype), ckv_new], axis=1)
        kpe_all = jnp.concatenate([kpe_past.astype(kpe_new.dtype), kpe_new], axis=1)
        k_pos = jnp.arange(ckv_all.shape[1])
    T = ckv_all.shape[1]
    kv = (ckv_all @ w_ukv).reshape(B, T, MLA_HEADS, QK_NOPE + V_DIM)
    k = jnp.concatenate([kv[..., :QK_NOPE], jnp.broadcast_to(kpe_all[:, :, None, :], (B, T, MLA_HEADS, QK_ROPE))], axis=-1)
    k = rms_norm(k, g_k)
    v = kv[..., QK_NOPE:]
    if S > Q_BLOCK:
        nb = S // Q_BLOCK
        qb = jnp.moveaxis(q.reshape(B, nb, Q_BLOCK, MLA_HEADS, QK_HEAD), 1, 0)
        pb = pos.reshape(nb, Q_BLOCK)
        o = lax.map(lambda args: chunk_causal_attention(args[0], k, v, args[1], k_pos), (qb, pb))
        o = jnp.moveaxis(o, 0, 1).reshape(B, S, MLA_HEADS, V_DIM)
    else:
        o = chunk_causal_attention(q, k, v, pos, k_pos)
    o = rms_norm(o, g_out).reshape(B, S, MLA_WIDTH)
    return o, ckv_new, kpe_new


def gated_delta_chunked(q, k, v, g, beta, s0):
    L = q.shape[-2]
    G = jnp.cumsum(g, axis=-1)
    idx = jnp.arange(L)
    incl = idx[:, None] >= idx[None, :]
    strict = idx[:, None] > idx[None, :]
    diff = G[..., :, None] - G[..., None, :]
    decay = jnp.where(incl, jnp.exp(jnp.where(incl, diff, 0.0)), 0.0)
    kk = jnp.einsum('bnhld,bnhmd->bnhlm', k, k)
    a_mat = jnp.where(strict, beta[..., None] * kk * decay, 0.0) + jnp.eye(L, dtype=q.dtype)
    u = lax.linalg.triangular_solve(a_mat, v * beta[..., None], left_side=True, lower=True, unit_diagonal=True)
    w = lax.linalg.triangular_solve(a_mat, k * (beta * jnp.exp(G))[..., None], left_side=True, lower=True, unit_diagonal=True)
    attn = jnp.einsum('bnhld,bnhmd->bnhlm', q, k) * decay
    q_dec = q * jnp.exp(G)[..., None]
    k_dec = k * jnp.exp(G[..., -1:] - G)[..., None]
    g_tot = jnp.exp(G[..., -1])

    def step(s, xs):
        u_c, w_c, attn_c, qd_c, kd_c, gt_c = xs
        v_new = u_c - jnp.einsum('bhlk,bhkv->bhlv', w_c, s)
        o_c = jnp.einsum('bhlk,bhkv->bhlv', qd_c, s) + jnp.einsum('bhlm,bhmv->bhlv', attn_c, v_new)
        s = s * gt_c[..., None, None] + jnp.einsum('bhlk,bhlv->bhkv', kd_c, v_new)
        return s, o_c

    xs = (jnp.moveaxis(u, 1, 0), jnp.moveaxis(w, 1, 0), jnp.moveaxis(attn, 1, 0), jnp.moveaxis(q_dec, 1, 0), jnp.moveaxis(k_dec, 1, 0), jnp.moveaxis(g_tot, 1, 0))
    s_final, o = lax.scan(step, s0, xs)
    return jnp.moveaxis(o, 0, 1), s_final


def gdn_mixer(qkv, z, b, a, conv_past, s0, conv_w, a_log, dt_bias, g_out):
    B, S = qkv.shape[:2]
    conv_in = jnp.concatenate([conv_past.astype(qkv.dtype), qkv], axis=1)
    conv_new = conv_in[:, conv_in.shape[1] - (CONV_W - 1):]
    y = conv_in[:, 0:S] * conv_w[0]
    for i in range(1, CONV_W):
        y = y + conv_in[:, i:i + S] * conv_w[i]
    y = jax.nn.silu(y.astype(jnp.float32))
    L = min(S, CHUNK)
    N = S // L

    def heads(t, d):
        return t.reshape(B, N, L, GDN_HEADS, d).transpose(0, 1, 3, 2, 4)

    def per_head(t):
        return t.reshape(B, N, L, GDN_HEADS).transpose(0, 1, 3, 2)

    nk = GDN_HEADS * GDN_DK
    q = l2_norm(heads(y[..., :nk], GDN_DK)) * GDN_DK ** -0.5
    k = l2_norm(heads(y[..., nk:2 * nk], GDN_DK))
    v = heads(y[..., 2 * nk:], GDN_DV)
    beta = jax.nn.sigmoid(b.astype(jnp.float32))
    g = -jnp.exp(a_log.astype(jnp.float32)) * jax.nn.softplus(a.astype(jnp.float32) + dt_bias.astype(jnp.float32))
    o, s_new = gated_delta_chunked(q, k, v, per_head(g), per_head(beta), s0.astype(jnp.float32))
    o = o.transpose(0, 1, 3, 2, 4).reshape(B, S, GDN_HEADS, GDN_DV)
    o = rms_norm(o, g_out) * jax.nn.silu(z.astype(jnp.float32).reshape(B, S, GDN_HEADS, GDN_DV))
    return o.reshape(B, S, GDN_WIDTH).astype(qkv.dtype), conv_new, s_new.astype(s0.dtype)


def memory_kv(mem, g_norm, w_mk, w_mv, g_mk):
    B, M = mem.shape[:2]
    mn = rms_norm(mem, g_norm)
    k = rms_norm((mn @ w_mk).reshape(B, M, MEM_HEADS, MEM_DIM), g_mk)
    v = (mn @ w_mv).reshape(B, M, MEM_HEADS, MEM_DIM)
    return k, v


def memory_cross_attention(hn, mem_k, mem_v, w_mq, g_mq, w_mo):
    B, S = hn.shape[:2]
    q = rms_norm((hn @ w_mq).reshape(B, S, MEM_HEADS, MEM_DIM), g_mq)
    s = jnp.einsum('bshd,bmhd->bhsm', q, mem_k.astype(q.dtype)).astype(jnp.float32) * MEM_DIM ** -0.5
    p = jax.nn.softmax(s, axis=-1).astype(q.dtype)
    o = jnp.einsum('bhsm,bmhd->bshd', p, mem_v.astype(q.dtype)).reshape(B, S, MEM_HEADS * MEM_DIM)
    return o @ w_mo


def layer_forward(x, pos, ckv_past, kpe_past, conv_past, s0, mem_k, mem_v,
                  g_norm_mix, w_in, g_cq, w_uq, g_ckv, w_ukv, g_q_mla, g_k_mla, g_mla_out,
                  conv_w, a_log, dt_bias, g_gdn_out, w_o, g_norm_mem_q, w_mq, g_mq, w_mo,
                  g_norm_ffn, w_ff1, w_ff2):
    xn = rms_norm(x, g_norm_mix)
    c_q, c_kv, k_pe, qkv, z, b, a = jnp.split(xn @ w_in, IN_SPLITS, axis=-1)
    o_mla, ckv_new, kpe_new = mla_mixer(c_q, c_kv, k_pe, pos, ckv_past, kpe_past, g_cq, w_uq, g_ckv, w_ukv, g_q_mla, g_k_mla, g_mla_out)
    o_gdn, conv_new, s_new = gdn_mixer(qkv, z, b, a, conv_past, s0, conv_w, a_log, dt_bias, g_gdn_out)
    h = x + jnp.concatenate([o_mla, o_gdn], axis=-1) @ w_o
    h = h + memory_cross_attention(rms_norm(h, g_norm_mem_q), mem_k, mem_v, w_mq, g_mq, w_mo)
    y = h + jnp.square(jax.nn.relu(rms_norm(h, g_norm_ffn) @ w_ff1)) @ w_ff2
    return y, ckv_new, kpe_new, conv_new, s_new


def setup_inputs(seed: int = 0) -> dict:
    key = jax.random.key(seed)
    ks = iter(jax.random.split(key, 48))

    def nrm(shape, scale=1.0):
        return jax.random.normal(next(ks), shape, jnp.float32) * scale

    def gain(n):
        return 1.0 + nrm((DEPTH, n), 0.02)

    return {
        'x_prompt': nrm((BATCH, SEQ, D_MODEL)),
        'x_sample': nrm((DEC_BATCH, DEC_SEQ, D_MODEL)),
        'mem_prompt': nrm((BATCH, N_MEM, D_MODEL)),
        'cache_mla_ckv': nrm((DEPTH, DEC_BATCH, PAST_LEN, KV_RANK)),
        'cache_mla_kpe': nrm((DEPTH, DEC_BATCH, PAST_LEN, QK_ROPE)),
        'cache_gdn_conv': nrm((DEPTH, DEC_BATCH, CONV_W - 1, GDN_CONV_DIM)),
        'state_gdn': nrm((DEPTH, DEC_BATCH, GDN_HEADS, GDN_DK, GDN_DV), GDN_DK ** -0.5),
        'cache_mem_k': nrm((DEPTH, DEC_BATCH, N_MEM, MEM_HEADS, MEM_DIM)),
        'cache_mem_v': nrm((DEPTH, DEC_BATCH, N_MEM, MEM_HEADS, MEM_DIM)),
        'g_norm_mix': gain(D_MODEL),
        'w_in': nrm((DEPTH, D_MODEL, D_IN), D_MODEL ** -0.5),
        'g_cq': gain(Q_RANK),
        'w_uq': nrm((DEPTH, Q_RANK, MLA_HEADS * QK_HEAD), Q_RANK ** -0.5),
        'g_ckv': gain(KV_RANK),
        'w_ukv': nrm((DEPTH, KV_RANK, MLA_HEADS * (QK_NOPE + V_DIM)), KV_RANK ** -0.5),
        'g_q_mla': gain(QK_HEAD),
        'g_k_mla': gain(QK_HEAD),
        'g_mla_out': gain(V_DIM),
        'conv_w': nrm((DEPTH, CONV_W, GDN_CONV_DIM), CONV_W ** -0.5),
        'a_log': jnp.log(jax.random.uniform(next(ks), (DEPTH, GDN_HEADS), jnp.float32, 1.0, 16.0)),
        'dt_bias': jnp.log(jnp.expm1(jax.random.uniform(next(ks), (DEPTH, GDN_HEADS), jnp.float32, 0.001, 0.1))),
        'g_gdn_out': gain(GDN_DV),
        'w_o': nrm((DEPTH, D_MIX, D_MODEL), D_MIX ** -0.5),
        'g_norm_mem_q': gain(D_MODEL),
        'g_norm_mem_kv': gain(D_MODEL),
        'w_mq': nrm((DEPTH, D_MODEL, MEM_HEADS * MEM_DIM), D_MODEL ** -0.5),
        'w_mk': nrm((DEPTH, D_MODEL, MEM_HEADS * MEM_DIM), D_MODEL ** -0.5),
        'w_mv': nrm((DEPTH, D_MODEL, MEM_HEADS * MEM_DIM), D_MODEL ** -0.5),
        'g_mq': gain(MEM_DIM),
        'g_mk': gain(MEM_DIM),
        'w_mo': nrm((DEPTH, MEM_HEADS * MEM_DIM, D_MODEL), (MEM_HEADS * MEM_DIM) ** -0.5),
        'g_norm_ffn': gain(D_MODEL),
        'w_ff1': nrm((DEPTH, D_MODEL, D_FF), D_MODEL ** -0.5),
        'w_ff2': nrm((DEPTH, D_FF, D_MODEL), D_FF ** -0.5),
    }


def reference(x_prompt, x_sample, mem_prompt, cache_mla_ckv, cache_mla_kpe, cache_gdn_conv, state_gdn,
              cache_mem_k, cache_mem_v, g_norm_mix, w_in, g_cq, w_uq, g_ckv, w_ukv, g_q_mla, g_k_mla,
              g_mla_out, conv_w, a_log, dt_bias, g_gdn_out, w_o, g_norm_mem_q, g_norm_mem_kv, w_mq, w_mk,
              w_mv, g_mq, g_mk, w_mo, g_norm_ffn, w_ff1, w_ff2):
    B = x_prompt.shape[0]
    pos_p = jnp.arange(x_prompt.shape[1])
    pos_s = cache_mla_ckv.shape[2] + jnp.arange(x_sample.shape[1])
    zeros_conv = jnp.zeros((B, CONV_W - 1, GDN_CONV_DIM), x_prompt.dtype)
    zeros_state = jnp.zeros((B, GDN_HEADS, GDN_DK, GDN_DV), x_prompt.dtype)
    hp, hs = x_prompt, x_sample
    ckv_p, kpe_p, conv_p, st_p, mk_p, mv_p = [], [], [], [], [], []
    ckv_s, kpe_s, conv_s, st_s = [], [], [], []
    for l in range(DEPTH):
        lw = (g_norm_mix[l], w_in[l], g_cq[l], w_uq[l], g_ckv[l], w_ukv[l], g_q_mla[l], g_k_mla[l],
              g_mla_out[l], conv_w[l], a_log[l], dt_bias[l], g_gdn_out[l], w_o[l], g_norm_mem_q[l],
              w_mq[l], g_mq[l], w_mo[l], g_norm_ffn[l], w_ff1[l], w_ff2[l])
        mem_k, mem_v = memory_kv(mem_prompt, g_norm_mem_kv[l], w_mk[l], w_mv[l], g_mk[l])
        hp, c1, c2, c3, c4 = layer_forward(hp, pos_p, None, None, zeros_conv, zeros_state, mem_k, mem_v, *lw)
        ckv_p.append(c1)
        kpe_p.append(c2)
        conv_p.append(c3)
        st_p.append(c4)
        mk_p.append(mem_k)
        mv_p.append(mem_v)
        hs, d1, d2, d3, d4 = layer_forward(hs, pos_s, cache_mla_ckv[l], cache_mla_kpe[l], cache_gdn_conv[l],
                                           state_gdn[l], cache_mem_k[l], cache_mem_v[l], *lw)
        ckv_s.append(d1)
        kpe_s.append(d2)
        conv_s.append(d3)
        st_s.append(d4)
    return (hp, hs, jnp.stack(ckv_p), jnp.stack(kpe_p), jnp.stack(conv_p), jnp.stack(st_p),
            jnp.stack(mk_p), jnp.stack(mv_p), jnp.stack(ckv_s), jnp.stack(kpe_s), jnp.stack(conv_s),
            jnp.stack(st_s))
```

```python
import functools
import math

import jax
import jax.numpy as jnp
from jax import lax
from jax.experimental import pallas as pl
from jax.experimental.pallas import tpu as pltpu

F32 = jnp.float32
BF = jnp.bfloat16

EPS = 1e-6
CHUNK = 64
ROPE_THETA = 10000.0
MLA_HEADS = 8
QK_NOPE = 128
QK_ROPE = 64
QK_HEAD = QK_NOPE + QK_ROPE
V_DIM = 128
Q_RANK = 512
KV_RANK = 512
GDN_HEADS = 8
GDN_DK = 128
GDN_DV = 128
CONV_W = 4
MEM_HEADS = 4
MEM_DIM = 128
LANE = 128
QPAD = 2 * LANE
COL_TILE = 512
HALO = 8
NEG_BIG = -1e30

VMEM_LIMIT = 56 * 1024 * 1024

IN_TM = 1024
PROJ_TM = 512
ATTN_TQ = 1024
ATTN_TK = 512
GDN_TM = 256
GDN_HG = 2
MIX_TM = 512
FFN_TM = 1024
FFN_TF = 512


def _tile(n, pref, mult=8):
    if n <= pref:
        return n
    t = (pref // mult) * mult
    while t >= mult:
        if n % t == 0:
            return t
        t -= mult
    return n


def _nt(a, b):
    return lax.dot_general(a, b, (((1,), (1,)), ((), ())), preferred_element_type=F32)


def _tn(a, b):
    return lax.dot_general(a, b, (((0,), (0,)), ((), ())), preferred_element_type=F32)


def _mm(a, b):
    return jnp.dot(a, b, preferred_element_type=F32)


def _sigmoid(x):
    return 1.0 / (1.0 + jnp.exp(-x))


def _softplus(x):
    return jnp.maximum(x, 0.0) + jnp.log(1.0 + jnp.exp(-jnp.abs(x)))


def _rms(x, g, n=None):
    n = x.shape[-1] if n is None else n
    ms = jnp.sum(x * x, axis=-1, keepdims=True) * (1.0 / n)
    return (x * lax.rsqrt(ms + EPS)) * g


def _split3(x):
    hi = x.astype(BF)
    r1 = x - hi.astype(F32)
    mid = r1.astype(BF)
    lo = (r1 - mid.astype(F32)).astype(BF)
    return hi, mid, lo


def _params(sem):
    return pltpu.CompilerParams(dimension_semantics=sem, vmem_limit_bytes=VMEM_LIMIT)


def _in_proj_body(x_ref, g_ref, w_ref, wbat_ref, gcq_ref, gckv_ref, ctab_ref, stab_ref,
                  cqn_ref, ckv_ref, kpe_ref, qkv_ref, z_ref, ba_ref, bat_ref, xn_scr, *, nqkv, nz):
    j = pl.program_id(1)

    @pl.when(j == 0)
    def _():
        xn_scr[...] = _rms(x_ref[...], g_ref[...]).astype(BF)

    acc = _mm(xn_scr[...], w_ref[...])

    @pl.when(j == 0)
    def _():
        cqn_ref[...] = _rms(acc, gcq_ref[...]).astype(BF)

    @pl.when(j == 1)
    def _():
        ckv_ref[...] = _rms(acc, gckv_ref[...])

    @pl.when((j >= 2) & (j < 2 + nqkv))
    def _():
        qkv_ref[...] = acc

    @pl.when((j >= 2 + nqkv) & (j < 2 + nqkv + nz))
    def _():
        z_ref[...] = acc.astype(BF)

    @pl.when(j == 2 + nqkv + nz)
    def _():
        kpe_ref[...] = acc[:, 0:LANE] * ctab_ref[...] + acc[:, LANE:2 * LANE] * stab_ref[...]
        ba_ref[...] = acc[:, 2 * LANE:3 * LANE]
        bat_ref[...] = _nt(wbat_ref[...], xn_scr[...])


def _in_proj(x, g, w_p, wbat, gcq, gckv, ctab, stab, n_conv, n_z):
    t, d = x.shape
    tm = _tile(min(t, ctab.shape[0]), IN_TM)
    ntab = ctab.shape[0] // tm
    nqkv = n_conv // COL_TILE
    nz = n_z // COL_TILE
    ncol = 2 + nqkv + nz + 1
    assert w_p.shape[1] == ncol * COL_TILE
    row = lambda i, j: (i, 0)
    const = lambda i, j: (0, 0)
    body = functools.partial(_in_proj_body, nqkv=nqkv, nz=nz)
    return pl.pallas_call(
        body,
        grid=(t // tm, ncol),
        in_specs=[
            pl.BlockSpec((tm, d), row),
            pl.BlockSpec((1, d), const),
            pl.BlockSpec((d, COL_TILE), lambda i, j: (0, j)),
            pl.BlockSpec((16, d), const),
            pl.BlockSpec((1, Q_RANK), const),
            pl.BlockSpec((1, KV_RANK), const),
            pl.BlockSpec((tm, LANE), lambda i, j: (i % ntab, 0)),
            pl.BlockSpec((tm, LANE), lambda i, j: (i % ntab, 0)),
        ],
        out_specs=[
            pl.BlockSpec((tm, Q_RANK), row),
            pl.BlockSpec((tm, KV_RANK), row),
            pl.BlockSpec((tm, LANE), row),
            pl.BlockSpec((tm, COL_TILE), lambda i, j: (i, jnp.clip(j - 2, 0, nqkv - 1))),
            pl.BlockSpec((tm, COL_TILE), lambda i, j: (i, jnp.clip(j - 2 - nqkv, 0, nz - 1))),
            pl.BlockSpec((tm, LANE), row),
            pl.BlockSpec((16, tm), lambda i, j: (0, i)),
        ],
        out_shape=[
            jax.ShapeDtypeStruct((t, Q_RANK), BF),
            jax.ShapeDtypeStruct((t, KV_RANK), F32),
            jax.ShapeDtypeStruct((t, LANE), F32),
            jax.ShapeDtypeStruct((t, n_conv), F32),
            jax.ShapeDtypeStruct((t, n_z), BF),
            jax.ShapeDtypeStruct((t, LANE), F32),
            jax.ShapeDtypeStruct((16, t), F32),
        ],
        scratch_shapes=[pltpu.VMEM((tm, d), BF)],
        compiler_params=_params(("arbitrary", "arbitrary")),
        name="in_proj",
    )(x, g, w_p, wbat, gcq, gckv, ctab, stab)


def _q_proj_body(c_ref, w1_ref, w2_ref, ctab_ref, stab_ref, gn_ref, gr_ref, q_ref):
    c = c_ref[...]
    qf = _mm(c, w1_ref[...])
    qs = _mm(c, w2_ref[...])
    ct = ctab_ref[...]
    st = stab_ref[...]
    gn = gn_ref[...]
    gr = gr_ref[...]
    for h in range(MLA_HEADS):
        nope = qf[:, h * QPAD:h * QPAD + LANE]
        rot = qf[:, h * QPAD + LANE:(h + 1) * QPAD] * ct + qs[:, h * LANE:(h + 1) * LANE] * st
        ss = jnp.sum(nope * nope, axis=-1, keepdims=True) + jnp.sum(rot * rot, axis=-1, keepdims=True)
        rs = lax.rsqrt(ss * (1.0 / QK_HEAD) + EPS) * (QK_HEAD ** -0.5)
        q_ref[:, h * QPAD:h * QPAD + LANE] = (nope * rs * gn).astype(BF)
        q_ref[:, h * QPAD + LANE:(h + 1) * QPAD] = (rot * rs * gr).astype(BF)


def _q_proj(cqn, w1, w2, ctab, stab, gn, gr):
    t = cqn.shape[0]
    tm = _tile(min(t, ctab.shape[0]), PROJ_TM)
    ntab = ctab.shape[0] // tm
    const = lambda i: (0, 0)
    return pl.pallas_call(
        _q_proj_body,
        grid=(t // tm,),
        in_specs=[
            pl.BlockSpec((tm, Q_RANK), lambda i: (i, 0)),
            pl.BlockSpec(w1.shape, const),
            pl.BlockSpec(w2.shape, const),
            pl.BlockSpec((tm, LANE), lambda i: (i % ntab, 0)),
            pl.BlockSpec((tm, LANE), lambda i: (i % ntab, 0)),
            pl.BlockSpec((1, LANE), const),
            pl.BlockSpec((1, LANE), const),
        ],
        out_specs=pl.BlockSpec((tm, MLA_HEADS * QPAD), lambda i: (i, 0)),
        out_shape=jax.ShapeDtypeStruct((t, MLA_HEADS * QPAD), BF),
        compiler_params=_params(("arbitrary",)),
        name="q_proj",
    )(cqn, w1, w2, ctab, stab, gn, gr)


def _kv_proj_body(c_ref, kpe_ref, w_ref, gn_ref, gr_ref, k_ref, v_ref):
    kv = _mm(c_ref[...].astype(BF), w_ref[...])
    kp = kpe_ref[...]
    kps = jnp.sum(kp * kp, axis=-1, keepdims=True)
    gn = gn_ref[...]
    gr = gr_ref[...]
    for h in range(MLA_HEADS):
        kn = kv[:, h * 2 * LANE:h * 2 * LANE + LANE]
        rs = lax.rsqrt((jnp.sum(kn * kn, axis=-1, keepdims=True) + kps) * (1.0 / QK_HEAD) + EPS)
        k_ref[:, h * QPAD:h * QPAD + LANE] = (kn * rs * gn).astype(BF)
        k_ref[:, h * QPAD + LANE:(h + 1) * QPAD] = (kp * rs * gr).astype(BF)
        v_ref[:, h * V_DIM:(h + 1) * V_DIM] = kv[:, h * 2 * LANE + LANE:(h + 1) * 2 * LANE].astype(BF)


def _kv_proj(ckv, kpe_pad, w, gn, gr):
    t = ckv.shape[0]
    tm = _tile(t, PROJ_TM)
    const = lambda i: (0, 0)
    return pl.pallas_call(
        _kv_proj_body,
        grid=(t // tm,),
        in_specs=[
            pl.BlockSpec((tm, KV_RANK), lambda i: (i, 0)),
            pl.BlockSpec((tm, LANE), lambda i: (i, 0)),
            pl.BlockSpec(w.shape, const),
            pl.BlockSpec((1, LANE), const),
            pl.BlockSpec((1, LANE), const),
        ],
        out_specs=[
            pl.BlockSpec((tm, MLA_HEADS * QPAD), lambda i: (i, 0)),
            pl.BlockSpec((tm, MLA_HEADS * V_DIM), lambda i: (i, 0)),
        ],
        out_shape=[
            jax.ShapeDtypeStruct((t, MLA_HEADS * QPAD), BF),
            jax.ShapeDtypeStruct((t, MLA_HEADS * V_DIM), BF),
        ],
        compiler_params=_params(("arbitrary",)),
        name="kv_proj",
    )(ckv, kpe_pad, w, gn, gr)


def _attn_body(q_ref, k_ref, v_ref, g_ref, o_ref, m_scr, l_scr, acc_scr, *, tq, tk, q_off, t_valid):
    qi = pl.program_id(2)
    q = q_ref[...]
    qpos0 = q_off + qi * tq
    n_full = jnp.minimum((qpos0 // CHUNK * CHUNK + CHUNK) // tk, t_valid // tk)
    hi = jnp.minimum((qpos0 + tq - 1) // CHUNK * CHUNK + CHUNK, t_valid)
    n_total = (hi + tk - 1) // tk

    m_scr[...] = jnp.full(m_scr.shape, NEG_BIG, F32)
    l_scr[...] = jnp.zeros(l_scr.shape, F32)
    acc_scr[...] = jnp.zeros(acc_scr.shape, F32)

    def step(kc, masked):
        k0 = pl.multiple_of(kc * tk, tk)
        s = _nt(q, k_ref[pl.ds(k0, tk), :])
        if masked:
            qpos = qpos0 + lax.broadcasted_iota(jnp.int32, (tq, tk), 0)
            kpos = k0 + lax.broadcasted_iota(jnp.int32, (tq, tk), 1)
            ok = (kpos // CHUNK <= qpos // CHUNK) & (kpos < t_valid)
            s = jnp.where(ok, s, NEG_BIG)
        m_prev = m_scr[...]
        m_new = jnp.maximum(m_prev, jnp.max(s, axis=-1, keepdims=True))
        alpha = jnp.exp(m_prev - m_new)
        p = jnp.exp(s - m_new)
        l_scr[...] = alpha * l_scr[...] + jnp.sum(p, axis=-1, keepdims=True)
        acc_scr[...] = acc_scr[...] * alpha + _mm(p.astype(BF), v_ref[pl.ds(k0, tk), :])
        m_scr[...] = m_new

    def full_step(kc, c):
        step(kc, False)
        return c

    def masked_step(kc, c):
        step(kc, True)
        return c

    lax.fori_loop(0, n_full, full_step, 0)
    lax.fori_loop(n_full, n_total, masked_step, 0)
    o = acc_scr[...] / l_scr[...]
    o_ref[...] = _rms(o, g_ref[...]).astype(BF)


def _attention(q, k, v, g_out, nb, q_off, t_valid):
    tq_total = q.shape[0] // nb
    tk_total = k.shape[0] // nb
    tq = _tile(tq_total, ATTN_TQ)
    tk = _tile(tk_total, ATTN_TK)
    nq = tq_total // tq
    body = functools.partial(_attn_body, tq=tq, tk=tk, q_off=q_off, t_valid=t_valid)
    return pl.pallas_call(
        body,
        grid=(nb, MLA_HEADS, nq),
        in_specs=[
            pl.BlockSpec((tq, QPAD), lambda b, h, i: (b * nq + i, h)),
            pl.BlockSpec((tk_total, QPAD), lambda b, h, i: (b, h)),
            pl.BlockSpec((tk_total, V_DIM), lambda b, h, i: (b, h)),
            pl.BlockSpec((1, V_DIM), lambda b, h, i: (0, 0)),
        ],
        out_specs=pl.BlockSpec((tq, V_DIM), lambda b, h, i: (b * nq + i, h)),
        out_shape=jax.ShapeDtypeStruct((q.shape[0], MLA_HEADS * V_DIM), BF),
        scratch_shapes=[
            pltpu.VMEM((tq, 1), F32),
            pltpu.VMEM((tq, 1), F32),
            pltpu.VMEM((tq, V_DIM), F32),
        ],
        compiler_params=_params(("arbitrary", "arbitrary", "arbitrary")),
        name="mla_attention",
    )(q, k, v, g_out)


def _gdn_body(q_ref, k_ref, v_ref, wq_ref, wk_ref, wv_ref, pq_ref, pk_ref, pv_ref,
              ba_ref, bat_ref, alog_l_ref, dtb_l_ref, alog_c_ref, dtb_c_ref,
              z_ref, s0_ref, gout_ref,
              o_ref, sout_ref,
              s_scr, halo_scr, buf_scr, vnew_scr, *, tm, L, hg):
    hb = pl.program_id(1)
    t = pl.program_id(2)
    nt = pl.num_programs(2)
    n_chunks = tm // L
    n_factors = int(math.log2(L)) - 1

    @pl.when(t == 0)
    def _():
        s_scr[...] = s0_ref[0]
        halo_scr[0] = pq_ref[...]
        halo_scr[1] = pk_ref[...]
        halo_scr[2] = pv_ref[...]
        vnew_scr[...] = jnp.zeros(vnew_scr.shape, BF)

    def conv(idx, x_ref, w_ref):
        buf_scr[0:HALO, :] = halo_scr[idx]
        buf_scr[HALO:HALO + tm, :] = x_ref[...]
        w = w_ref[...]
        y = buf_scr[HALO - 3:HALO - 3 + tm, :] * w[0:1]
        for i in range(1, CONV_W):
            y = y + buf_scr[HALO - 3 + i:HALO - 3 + i + tm, :] * w[i:i + 1]
        halo_scr[idx] = buf_scr[tm:tm + HALO, :]
        return y * _sigmoid(y)

    yq = conv(0, q_ref, wq_ref)
    yk = conv(1, k_ref, wk_ref)
    yv = conv(2, v_ref, wv_ref)

    ba = ba_ref[...]
    beta_all = _sigmoid(ba)
    g_all = -jnp.exp(alog_l_ref[...]) * _softplus(ba + dtb_l_ref[...])
    gt_all = -jnp.exp(alog_c_ref[...]) * _softplus(bat_ref[0] + dtb_c_ref[...])

    r = lax.broadcasted_iota(jnp.int32, (tm, tm), 0)
    c = lax.broadcasted_iota(jnp.int32, (tm, tm), 1)
    same = (r // L) == (c // L)
    lower = same & (c <= r)
    strict = same & (c < r)
    cs = jnp.where(lower, 1.0, 0.0).astype(BF)
    bd = jnp.where(same, 1.0, 0.0).astype(BF)
    g3 = _split3(g_all)
    gcol_all = _mm(cs, g3[0]) + _mm(cs, g3[1]) + _mm(cs, g3[2])
    glast_all = _mm(bd, g3[0]) + _mm(bd, g3[1]) + _mm(bd, g3[2])
    gt3 = _split3(gt_all)
    grow_all = _nt(gt3[0], cs) + _nt(gt3[1], cs) + _nt(gt3[2], cs)

    lane = lax.broadcasted_iota(jnp.int32, (1, LANE), 1)
    sub = lax.broadcasted_iota(jnp.int32, (16, 1), 0)
    gout = gout_ref[...]

    for hh in range(hg):
        hglob = hb * hg + hh
        sel_b = lane == hglob
        sel_g = lane == GDN_HEADS + hglob
        beta_c = jnp.sum(jnp.where(sel_b, beta_all, 0.0), axis=-1, keepdims=True)
        gc = jnp.sum(jnp.where(sel_g, gcol_all, 0.0), axis=-1, keepdims=True)
        glast_c = jnp.sum(jnp.where(sel_g, glast_all, 0.0), axis=-1, keepdims=True)
        gr = jnp.sum(jnp.where(sub == GDN_HEADS + hglob, grow_all, 0.0), axis=0, keepdims=True)

        qh = yq[:, hh * GDN_DK:(hh + 1) * GDN_DK]
        kh = yk[:, hh * GDN_DK:(hh + 1) * GDN_DK]
        vh = yv[:, hh * GDN_DV:(hh + 1) * GDN_DV]
        qn = qh * lax.rsqrt(jnp.sum(qh * qh, axis=-1, keepdims=True) + EPS) * (GDN_DK ** -0.5)
        kn = kh * lax.rsqrt(jnp.sum(kh * kh, axis=-1, keepdims=True) + EPS)

        decay = jnp.where(lower, jnp.exp(jnp.where(lower, gc - gr, 0.0)), 0.0)
        kb = kn.astype(BF)
        kk = _nt(kb, kb)
        qk = _nt(qn.astype(BF), kb)
        nb = jnp.where(strict, beta_c * kk * decay, 0.0).astype(BF)
        attn_b = (qk * decay).astype(BF)
        eg = jnp.exp(gc)
        rhs = jnp.concatenate([vh * beta_c, kn * (beta_c * eg)], axis=1)
        y = rhs - _mm(nb, rhs.astype(BF))
        m = _mm(nb, nb)
        for s in range(n_factors):
            mb = m.astype(BF)
            y = y + _mm(mb, y.astype(BF))
            if s < n_factors - 1:
                m = _mm(mb, mb)
        u = y[:, :GDN_DV]
        w = y[:, GDN_DV:]
        qd = qn * eg
        kd = kn * jnp.exp(glast_c - gc)
        gtot = jnp.exp(glast_c)

        st = s_scr[hh]
        outs = []
        for ci in range(n_chunks):
            r0 = ci * L
            wq = jnp.concatenate([w[r0:r0 + L], qd[r0:r0 + L]], axis=0).astype(BF)
            ws = _mm(wq, st.astype(BF))
            vn = (u[r0:r0 + L] - ws[:L]).astype(BF)
            vnew_scr[hh, r0:r0 + L, :] = vn
            outs.append(ws[L:] + _mm(attn_b[r0:r0 + L, :], vnew_scr[hh]))
            st = st * gtot[r0:r0 + 1, :] + _tn(kd[r0:r0 + L].astype(BF), vn)
        s_scr[hh] = st
        o = outs[0] if n_chunks == 1 else jnp.concatenate(outs, axis=0)
        zz = z_ref[:, hh * GDN_DV:(hh + 1) * GDN_DV].astype(F32)
        o_ref[:, hh * GDN_DV:(hh + 1) * GDN_DV] = (_rms(o, gout) * (zz * _sigmoid(zz))).astype(BF)

    @pl.when(t == nt - 1)
    def _():
        sout_ref[0] = s_scr[...]


def _gdn(qkv, conv_w, conv_past8, ba, bat, alog_l, dtb_l, alog_c, dtb_c, z, s0, gout, nb):
    t = qkv.shape[0]
    s_len = t // nb
    L = min(s_len, CHUNK)
    tm = _tile(s_len, GDN_TM, mult=L)
    nt = s_len // tm
    hg = GDN_HG
    ng = GDN_HEADS // hg
    hw = hg * GDN_DK
    body = functools.partial(_gdn_body, tm=tm, L=L, hg=hg)
    rows = lambda off: (lambda b, h, i: (b * nt + i, off + h))
    wcol = lambda off: (lambda b, h, i: (0, off + h))
    pcol = lambda off: (lambda b, h, i: (b, off + h))
    const = lambda b, h, i: (0, 0)
    return pl.pallas_call(
        body,
        grid=(nb, ng, nt),
        in_specs=[
            pl.BlockSpec((tm, hw), rows(0)),
            pl.BlockSpec((tm, hw), rows(ng)),
            pl.BlockSpec((tm, hw), rows(2 * ng)),
            pl.BlockSpec((CONV_W, hw), wcol(0)),
            pl.BlockSpec((CONV_W, hw), wcol(ng)),
            pl.BlockSpec((CONV_W, hw), wcol(2 * ng)),
            pl.BlockSpec((HALO, hw), pcol(0)),
            pl.BlockSpec((HALO, hw), pcol(ng)),
            pl.BlockSpec((HALO, hw), pcol(2 * ng)),
            pl.BlockSpec((tm, LANE), lambda b, h, i: (b * nt + i, 0)),
            pl.BlockSpec((1, 16, tm), lambda b, h, i: (b, 0, i)),
            pl.BlockSpec((1, LANE), const),
            pl.BlockSpec((1, LANE), const),
            pl.BlockSpec((16, 1), const),
            pl.BlockSpec((16, 1), const),
            pl.BlockSpec((tm, hw), rows(0)),
            pl.BlockSpec((1, hg, GDN_DK, GDN_DV), lambda b, h, i: (b, h, 0, 0)),
            pl.BlockSpec((1, GDN_DV), const),
        ],
        out_specs=[
            pl.BlockSpec((tm, hw), rows(0)),
            pl.BlockSpec((1, hg, GDN_DK, GDN_DV), lambda b, h, i: (b, h, 0, 0)),
        ],
        out_shape=[
            jax.ShapeDtypeStruct((t, GDN_HEADS * GDN_DV), BF),
            jax.ShapeDtypeStruct((nb, GDN_HEADS, GDN_DK, GDN_DV), F32),
        ],
        scratch_shapes=[
            pltpu.VMEM((hg, GDN_DK, GDN_DV), F32),
            pltpu.VMEM((3, HALO, hw), F32),
            pltpu.VMEM((HALO + tm, hw), F32),
            pltpu.VMEM((hg, tm, GDN_DV), BF),
        ],
        compiler_params=_params(("arbitrary", "arbitrary", "arbitrary")),
        name="gdn",
    )(qkv, qkv, qkv, conv_w, conv_w, conv_w, conv_past8, conv_past8, conv_past8,
      ba, bat, alog_l, dtb_l, alog_c, dtb_c, z, s0, gout)


def _mix_mem_body(x_ref, a1_ref, a2_ref, wo1_ref, wo2_ref, gq_ref, wmq_ref, gmq_ref,
                  mk_ref, mv_ref, wmo_ref, o_ref):
    h1 = x_ref[...] + _mm(a1_ref[...], wo1_ref[...]) + _mm(a2_ref[...], wo2_ref[...])
    hn = _rms(h1, gq_ref[...]).astype(BF)
    q = _mm(hn, wmq_ref[...])
    gmq = gmq_ref[...]
    mk = mk_ref[0]
    mv = mv_ref[0]
    outs = []
    for h in range(MEM_HEADS):
        sl = slice(h * MEM_DIM, (h + 1) * MEM_DIM)
        qh = _rms(q[:, sl], gmq).astype(BF)
        s = _nt(qh, mk[:, sl]) * (MEM_DIM ** -0.5)
        p = jnp.exp(s - jnp.max(s, axis=-1, keepdims=True))
        p = p / jnp.sum(p, axis=-1, keepdims=True)
        outs.append(_mm(p.astype(BF), mv[:, sl]))
    o = jnp.concatenate(outs, axis=1).astype(BF)
    o_ref[...] = h1 + _mm(o, wmo_ref[...])


def _mix_mem(x, a1, a2, wo1, wo2, gq, wmq, gmq, mk, mv, wmo, nb):
    t, d = x.shape
    s_len = t // nb
    tm = _tile(s_len, MIX_TM)
    nt = s_len // tm
    n_mem = mk.shape[1]
    mw = MEM_HEADS * MEM_DIM
    row = lambda i: (i, 0)
    const = lambda i: (0, 0)
    return pl.pallas_call(
        _mix_mem_body,
        grid=(t // tm,),
        in_specs=[
            pl.BlockSpec((tm, d), row),
            pl.BlockSpec((tm, a1.shape[1]), row),
            pl.BlockSpec((tm, a2.shape[1]), row),
            pl.BlockSpec(wo1.shape, const),
            pl.BlockSpec(wo2.shape, const),
            pl.BlockSpec((1, d), const),
            pl.BlockSpec(wmq.shape, const),
            pl.BlockSpec((1, MEM_DIM), const),
            pl.BlockSpec((1, n_mem, mw), lambda i: (i // nt, 0, 0)),
            pl.BlockSpec((1, n_mem, mw), lambda i: (i // nt, 0, 0)),
            pl.BlockSpec(wmo.shape, const),
        ],
        out_specs=pl.BlockSpec((tm, d), row),
        out_shape=jax.ShapeDtypeStruct((t, d), F32),
        compiler_params=_params(("arbitrary",)),
        name="mix_mem",
    )(x, a1, a2, wo1, wo2, gq, wmq, gmq, mk, mv, wmo)


def _ffn_body(x_ref, g_ref, w1_ref, w2_ref, o_ref, hn_scr):
    f = pl.program_id(1)

    @pl.when(f == 0)
    def _():
        x = x_ref[...]
        hn_scr[...] = _rms(x, g_ref[...]).astype(BF)
        o_ref[...] = x

    a = jnp.maximum(_mm(hn_scr[...], w1_ref[...]), 0.0)
    o_ref[...] += _mm((a * a).astype(BF), w2_ref[...])


def _ffn(x, g, w1, w2):
    t, d = x.shape
    dff = w1.shape[1]
    tm = _tile(t, FFN_TM)
    tf = _tile(dff, FFN_TF, mult=LANE)
    return pl.pallas_call(
        _ffn_body,
        grid=(t // tm, dff // tf),
        in_specs=[
            pl.BlockSpec((tm, d), lambda i, f: (i, 0)),
            pl.BlockSpec((1, d), lambda i, f: (0, 0)),
            pl.BlockSpec((d, tf), lambda i, f: (0, f)),
            pl.BlockSpec((tf, d), lambda i, f: (f, 0)),
        ],
        out_specs=pl.BlockSpec((tm, d), lambda i, f: (i, 0)),
        out_shape=jax.ShapeDtypeStruct((t, d), F32),
        scratch_shapes=[pltpu.VMEM((tm, d), BF)],
        compiler_params=_params(("arbitrary", "arbitrary")),
        name="ffn",
    )(x, g, w1, w2)


def _mem_kv_body(m_ref, g_ref, wk_ref, wv_ref, gk_ref, k_ref, v_ref):
    mn = _rms(m_ref[...], g_ref[...]).astype(BF)
    k = _mm(mn, wk_ref[...])
    gk = gk_ref[...]
    for h in range(MEM_HEADS):
        sl = slice(h * MEM_DIM, (h + 1) * MEM_DIM)
        k_ref[:, sl] = _rms(k[:, sl], gk)
    v_ref[...] = _mm(mn, wv_ref[...])


def _mem_kv(mem, g, wk, wv, gk):
    t, d = mem.shape
    tm = _tile(t, 256)
    mw = MEM_HEADS * MEM_DIM
    const = lambda i: (0, 0)
    return pl.pallas_call(
        _mem_kv_body,
        grid=(t // tm,),
        in_specs=[
            pl.BlockSpec((tm, d), lambda i: (i, 0)),
            pl.BlockSpec((1, d), const),
            pl.BlockSpec(wk.shape, const),
            pl.BlockSpec(wv.shape, const),
            pl.BlockSpec((1, MEM_DIM), const),
        ],
        out_specs=[pl.BlockSpec((tm, mw), lambda i: (i, 0)), pl.BlockSpec((tm, mw), lambda i: (i, 0))],
        out_shape=[jax.ShapeDtypeStruct((t, mw), F32), jax.ShapeDtypeStruct((t, mw), F32)],
        compiler_params=_params(("arbitrary",)),
        name="mem_kv",
    )(mem, g, wk, wv, gk)


def _rope_tables(pos):
    half = QK_ROPE // 2
    inv_freq = ROPE_THETA ** (-jnp.arange(half, dtype=F32) / half)
    ang = pos.astype(F32)[:, None] * inv_freq[None, :]
    cos, sin = jnp.cos(ang), jnp.sin(ang)
    zeros = jnp.zeros((pos.shape[0], LANE - QK_ROPE), F32)
    return (jnp.concatenate([cos, cos, zeros], axis=1), jnp.concatenate([-sin, sin, zeros], axis=1))


def _pad_lanes(v, n=LANE):
    return jnp.pad(v, ((0, 0), (0, n - v.shape[1])))


def _prep_weights(w_in, g_cq, w_uq, g_ckv, w_ukv, g_q, g_k, w_o, w_mq, w_mo, w_ff1, w_ff2, w_mk, w_mv):
    d = w_in.shape[0]
    n_conv = 2 * GDN_HEADS * GDN_DK + GDN_HEADS * GDN_DV
    n_z = GDN_HEADS * GDN_DV
    o = 0
    w_cq = w_in[:, o:o + Q_RANK]; o += Q_RANK
    w_ckv = w_in[:, o:o + KV_RANK]; o += KV_RANK
    w_kpe = w_in[:, o:o + QK_ROPE]; o += QK_ROPE
    w_qkv = w_in[:, o:o + n_conv]; o += n_conv
    w_z = w_in[:, o:o + n_z]; o += n_z
    w_b = w_in[:, o:o + GDN_HEADS]; o += GDN_HEADS
    w_a = w_in[:, o:o + GDN_HEADS]
    half = QK_ROPE // 2
    swap = jnp.concatenate([jnp.arange(half, QK_ROPE), jnp.arange(0, half)])
    z64 = jnp.zeros((d, LANE - QK_ROPE), w_in.dtype)
    misc = jnp.concatenate([w_kpe, z64, w_kpe[:, swap], z64, w_b, w_a], axis=1)
    misc = _pad_lanes(misc, COL_TILE)
    w_in_p = jnp.concatenate([w_cq, w_ckv, w_qkv, w_z, misc], axis=1).astype(BF)
    wbat = jnp.concatenate([w_b, w_a], axis=1).T.astype(BF)

    r = w_uq.shape[0]
    wq3 = w_uq.reshape(r, MLA_HEADS, QK_HEAD)
    zq = jnp.zeros((r, MLA_HEADS, LANE - QK_ROPE), w_uq.dtype)
    w1 = jnp.concatenate([wq3, zq], axis=2).reshape(r, MLA_HEADS * QPAD).astype(BF)
    w2 = jnp.concatenate([wq3[:, :, QK_NOPE:][:, :, swap], zq], axis=2).reshape(r, MLA_HEADS * LANE).astype(BF)

    dmla = MLA_HEADS * V_DIM
    return dict(
        n_conv=n_conv, n_z=n_z, w_in_p=w_in_p, wbat=wbat, w1=w1, w2=w2,
        gcq=g_cq[None, :], gckv=g_ckv[None, :],
        gq_n=g_q[None, :QK_NOPE], gq_r=_pad_lanes(g_q[None, QK_NOPE:]),
        gk_n=g_k[None, :QK_NOPE], gk_r=_pad_lanes(g_k[None, QK_NOPE:]),
        w_ukv=w_ukv.astype(BF), wo1=w_o[:dmla].astype(BF), wo2=w_o[dmla:].astype(BF),
        w_mq=w_mq.astype(BF), w_mo=w_mo.astype(BF), w_ff1=w_ff1.astype(BF), w_ff2=w_ff2.astype(BF),
        w_mk=w_mk.astype(BF), w_mv=w_mv.astype(BF),
    )


def _gate_params(a_log, dt_bias):
    z8 = jnp.zeros((GDN_HEADS,), F32)
    al = jnp.concatenate([z8, a_log.astype(F32)])
    db = jnp.concatenate([z8, dt_bias.astype(F32)])
    return _pad_lanes(al[None, :]), _pad_lanes(db[None, :]), al[:, None], db[:, None]


def _layer(x, pos, past, conv_past, s0, mem_k, mem_v, wp, lw):
    nb, s_len, d = x.shape
    t = nb * s_len
    xf = x.reshape(t, d)
    ctab, stab = _rope_tables(pos)
    if ctab.shape[0] % 8 != 0 or (s_len < IN_TM and nb > 1):
        ctab, stab = jnp.tile(ctab, (nb, 1)), jnp.tile(stab, (nb, 1))

    cqn, ckv, kpe_pad, qkv, z, ba, bat = _in_proj(
        xf, lw["g_norm_mix"], wp["w_in_p"], wp["wbat"], wp["gcq"], wp["gckv"], ctab, stab,
        wp["n_conv"], wp["n_z"])
    q = _q_proj(cqn, wp["w1"], wp["w2"], ctab, stab, wp["gq_n"], wp["gq_r"])

    if past is None:
        ckv_all, kpe_all, q_off, t_valid = ckv, kpe_pad, 0, s_len
    else:
        ckv_past, kpe_past = past
        p_len = ckv_past.shape[1]
        t_valid = p_len + s_len
        t_pad = -(-t_valid // ATTN_TK) * ATTN_TK
        ckv_all = jnp.concatenate(
            [ckv_past.astype(F32), ckv.reshape(nb, s_len, KV_RANK),
             jnp.zeros((nb, t_pad - t_valid, KV_RANK), F32)], axis=1).reshape(nb * t_pad, KV_RANK)
        kpe_all = jnp.concatenate(
            [jnp.pad(kpe_past.astype(F32), ((0, 0), (0, 0), (0, LANE - QK_ROPE))),
             kpe_pad.reshape(nb, s_len, LANE),
             jnp.zeros((nb, t_pad - t_valid, LANE), F32)], axis=1).reshape(nb * t_pad, LANE)
        q_off = p_len
    k, v = _kv_proj(ckv_all, kpe_all, wp["w_ukv"], wp["gk_n"], wp["gk_r"])
    o_mla = _attention(q, k, v, lw["g_mla_out"], nb, q_off, t_valid)

    conv_past8 = jnp.pad(conv_past.astype(F32), ((0, 0), (HALO - (CONV_W - 1), 0), (0, 0)))
    conv_past8 = conv_past8.reshape(nb * HALO, -1)
    bat3 = bat.reshape(16, nb, s_len).transpose(1, 0, 2)
    alog_l, dtb_l, alog_c, dtb_c = _gate_params(lw["a_log"], lw["dt_bias"])
    o_gdn, s_new = _gdn(qkv, lw["conv_w"], conv_past8, ba, bat3, alog_l, dtb_l, alog_c, dtb_c,
                        z, s0.astype(F32), lw["g_gdn_out"], nb)

    h2 = _mix_mem(xf, o_mla, o_gdn, wp["wo1"], wp["wo2"], lw["g_norm_mem_q"], wp["w_mq"],
                  lw["g_mq"], mem_k, mem_v, wp["w_mo"], nb)
    y = _ffn(h2, lw["g_norm_ffn"], wp["w_ff1"], wp["w_ff2"])

    conv_in_tail = jnp.concatenate([conv_past.astype(F32), qkv.reshape(nb, s_len, -1)[:, -(CONV_W - 1):]], axis=1)
    conv_new = conv_in_tail[:, -(CONV_W - 1):]
    return (y.reshape(nb, s_len, d), ckv.reshape(nb, s_len, KV_RANK),
            kpe_pad[:, :QK_ROPE].reshape(nb, s_len, QK_ROPE), conv_new, s_new)


def kernel(x_prompt, x_sample, mem_prompt, cache_mla_ckv, cache_mla_kpe, cache_gdn_conv, state_gdn, cache_mem_k, cache_mem_v, g_norm_mix, w_in, g_cq, w_uq, g_ckv, w_ukv, g_q_mla, g_k_mla, g_mla_out, conv_w, a_log, dt_bias, g_gdn_out, w_o, g_norm_mem_q, g_norm_mem_kv, w_mq, w_mk, w_mv, g_mq, g_mk, w_mo, g_norm_ffn, w_ff1, w_ff2):
    depth = w_in.shape[0]
    nbp, sp, d = x_prompt.shape
    nbs, ss, _ = x_sample.shape
    n_mem = mem_prompt.shape[1]
    mw = MEM_HEADS * MEM_DIM
    n_conv = 2 * GDN_HEADS * GDN_DK + GDN_HEADS * GDN_DV
    pos_p = jnp.arange(sp)
    pos_s = cache_mla_ckv.shape[2] + jnp.arange(ss)
    zeros_conv = jnp.zeros((nbp, CONV_W - 1, n_conv), F32)
    zeros_state = jnp.zeros((nbp, GDN_HEADS, GDN_DK, GDN_DV), F32)
    hp, hs = x_prompt, x_sample
    outs_p = [[] for _ in range(6)]
    outs_s = [[] for _ in range(4)]
    for l in range(depth):
        wp = _prep_weights(w_in[l], g_cq[l], w_uq[l], g_ckv[l], w_ukv[l], g_q_mla[l], g_k_mla[l], w_o[l],
                           w_mq[l], w_mo[l], w_ff1[l], w_ff2[l], w_mk[l], w_mv[l])
        lw = dict(g_norm_mix=g_norm_mix[l][None, :], g_mla_out=g_mla_out[l][None, :], conv_w=conv_w[l],
                  a_log=a_log[l], dt_bias=dt_bias[l], g_gdn_out=g_gdn_out[l][None, :],
                  g_norm_mem_q=g_norm_mem_q[l][None, :], g_mq=g_mq[l][None, :],
                  g_norm_ffn=g_norm_ffn[l][None, :])
        mk, mv = _mem_kv(mem_prompt.reshape(nbp * n_mem, d), g_norm_mem_kv[l][None, :], wp["w_mk"], wp["w_mv"],
                         g_mk[l][None, :])
        mk3, mv3 = mk.reshape(nbp, n_mem, mw), mv.reshape(nbp, n_mem, mw)
        hp, c1, c2, c3, c4 = _layer(hp, pos_p, None, zeros_conv, zeros_state, mk3.astype(BF), mv3.astype(BF), wp, lw)
        for lst, val in zip(outs_p, (c1, c2, c3, c4, mk3.reshape(nbp, n_mem, MEM_HEADS, MEM_DIM),
                                     mv3.reshape(nbp, n_mem, MEM_HEADS, MEM_DIM))):
            lst.append(val)
        hs, d1, d2, d3, d4 = _layer(hs, pos_s, (cache_mla_ckv[l], cache_mla_kpe[l]), cache_gdn_conv[l], state_gdn[l],
                                    cache_mem_k[l].reshape(nbs, n_mem, mw).astype(BF),
                                    cache_mem_v[l].reshape(nbs, n_mem, mw).astype(BF), wp, lw)
        for lst, val in zip(outs_s, (d1, d2, d3, d4)):
            lst.append(val)
    return (hp, hs, *(jnp.stack(v) for v in outs_p), *(jnp.stack(v) for v in outs_s))
```

```python
import functools
import math

import jax
import jax.numpy as jnp
from jax import lax
from jax.experimental import pallas as pl
from jax.experimental.pallas import tpu as pltpu

F32 = jnp.float32
BF = jnp.bfloat16

EPS = 1e-6
CHUNK = 64
ROPE_THETA = 10000.0
MLA_HEADS = 8
QK_NOPE = 128
QK_ROPE = 64
QK_HEAD = QK_NOPE + QK_ROPE
V_DIM = 128
Q_RANK = 512
KV_RANK = 512
GDN_HEADS = 8
GDN_DK = 128
GDN_DV = 128
CONV_W = 4
MEM_HEADS = 4
MEM_DIM = 128
LANE = 128
QPAD = 2 * LANE
COL_TILE = 512
HALO = 8
NEG_BIG = -1e30
QSCALE = (QK_HEAD ** -0.5) * math.log2(math.e)

VMEM_LIMIT = 56 * 1024 * 1024

IN_TM = 1024
PROJ_TM = 512
ATTN_TQ = 1024
ATTN_TK = 1024
GDN_TM = 256
GDN_HG = 2
MIX_TM = 512
FFN_TM = 1024
FFN_TF = 512


def _tile(n, pref, mult=8):
    if n <= pref:
        return n
    t = (pref // mult) * mult
    while t >= mult:
        if n % t == 0:
            return t
        t -= mult
    return n


def _nt(a, b):
    return lax.dot_general(a, b, (((1,), (1,)), ((), ())), preferred_element_type=F32)


def _tn(a, b):
    return lax.dot_general(a, b, (((0,), (0,)), ((), ())), preferred_element_type=F32)


def _mm(a, b):
    return jnp.dot(a, b, preferred_element_type=F32)


def _sigmoid(x):
    return 1.0 / (1.0 + jnp.exp(-x))


def _softplus(x):
    return jnp.maximum(x, 0.0) + jnp.log(1.0 + jnp.exp(-jnp.abs(x)))


def _rms(x, g, n=None):
    n = x.shape[-1] if n is None else n
    ms = jnp.sum(x * x, axis=-1, keepdims=True) * (1.0 / n)
    return (x * lax.rsqrt(ms + EPS)) * g


def _split3(x):
    hi = x.astype(BF)
    r1 = x - hi.astype(F32)
    mid = r1.astype(BF)
    lo = (r1 - mid.astype(F32)).astype(BF)
    return hi, mid, lo


def _params(sem):
    return pltpu.CompilerParams(dimension_semantics=sem, vmem_limit_bytes=VMEM_LIMIT)


def _in_proj_body(x_ref, g_ref, w_ref, wbat_ref, gcq_ref, gckv_ref, ctab_ref, stab_ref,
                  cqn_ref, ckv_ref, kpe_ref, qkv_ref, z_ref, ba_ref, bat_ref, xn_scr, *, nqkv, nz):
    j = pl.program_id(1)

    @pl.when(j == 0)
    def _():
        xn_scr[...] = _rms(x_ref[...], g_ref[...]).astype(BF)

    acc = _mm(xn_scr[...], w_ref[...])

    @pl.when(j == 0)
    def _():
        cqn_ref[...] = _rms(acc, gcq_ref[...]).astype(BF)

    @pl.when(j == 1)
    def _():
        ckv_ref[...] = _rms(acc, gckv_ref[...])

    @pl.when((j >= 2) & (j < 2 + nqkv))
    def _():
        qkv_ref[...] = acc

    @pl.when((j >= 2 + nqkv) & (j < 2 + nqkv + nz))
    def _():
        z_ref[...] = acc.astype(BF)

    @pl.when(j == 2 + nqkv + nz)
    def _():
        kpe_ref[...] = acc[:, 0:LANE] * ctab_ref[...] + acc[:, LANE:2 * LANE] * stab_ref[...]
        ba_ref[...] = acc[:, 2 * LANE:3 * LANE]
        bat_ref[...] = _nt(wbat_ref[...], xn_scr[...])


def _in_proj(x, g, w_p, wbat, gcq, gckv, ctab, stab, n_conv, n_z):
    t, d = x.shape
    tm = _tile(min(t, ctab.shape[0]), IN_TM)
    ntab = ctab.shape[0] // tm
    nqkv = n_conv // COL_TILE
    nz = n_z // COL_TILE
    ncol = 2 + nqkv + nz + 1
    assert w_p.shape[1] == ncol * COL_TILE
    row = lambda i, j: (i, 0)
    const = lambda i, j: (0, 0)
    body = functools.partial(_in_proj_body, nqkv=nqkv, nz=nz)
    return pl.pallas_call(
        body,
        grid=(t // tm, ncol),
        in_specs=[
            pl.BlockSpec((tm, d), row),
            pl.BlockSpec((1, d), const),
            pl.BlockSpec((d, COL_TILE), lambda i, j: (0, j)),
            pl.BlockSpec((16, d), const),
            pl.BlockSpec((1, Q_RANK), const),
            pl.BlockSpec((1, KV_RANK), const),
            pl.BlockSpec((tm, LANE), lambda i, j: (i % ntab, 0)),
            pl.BlockSpec((tm, LANE), lambda i, j: (i % ntab, 0)),
        ],
        out_specs=[
            pl.BlockSpec((tm, Q_RANK), row),
            pl.BlockSpec((tm, KV_RANK), row),
            pl.BlockSpec((tm, LANE), row),
            pl.BlockSpec((tm, COL_TILE), lambda i, j: (i, jnp.clip(j - 2, 0, nqkv - 1))),
            pl.BlockSpec((tm, COL_TILE), lambda i, j: (i, jnp.clip(j - 2 - nqkv, 0, nz - 1))),
            pl.BlockSpec((tm, LANE), row),
            pl.BlockSpec((16, tm), lambda i, j: (0, i)),
        ],
        out_shape=[
            jax.ShapeDtypeStruct((t, Q_RANK), BF),
            jax.ShapeDtypeStruct((t, KV_RANK), F32),
            jax.ShapeDtypeStruct((t, LANE), F32),
            jax.ShapeDtypeStruct((t, n_conv), F32),
            jax.ShapeDtypeStruct((t, n_z), BF),
            jax.ShapeDtypeStruct((t, LANE), F32),
            jax.ShapeDtypeStruct((16, t), F32),
        ],
        scratch_shapes=[pltpu.VMEM((tm, d), BF)],
        compiler_params=_params(("arbitrary", "arbitrary")),
        name="in_proj",
    )(x, g, w_p, wbat, gcq, gckv, ctab, stab)


def _q_proj_body(c_ref, w1_ref, w2_ref, ctab_ref, stab_ref, gn_ref, gr_ref, q_ref):
    c = c_ref[...]
    qf = _mm(c, w1_ref[...])
    qs = _mm(c, w2_ref[...])
    ct = ctab_ref[...]
    st = stab_ref[...]
    gn = gn_ref[...]
    gr = gr_ref[...]
    for h in range(MLA_HEADS):
        nope = qf[:, h * QPAD:h * QPAD + LANE]
        rot = qf[:, h * QPAD + LANE:(h + 1) * QPAD] * ct + qs[:, h * LANE:(h + 1) * LANE] * st
        ss = jnp.sum(nope * nope, axis=-1, keepdims=True) + jnp.sum(rot * rot, axis=-1, keepdims=True)
        rs = lax.rsqrt(ss * (1.0 / QK_HEAD) + EPS) * QSCALE
        q_ref[:, h * QPAD:h * QPAD + LANE] = (nope * rs * gn).astype(BF)
        q_ref[:, h * QPAD + LANE:(h + 1) * QPAD] = (rot * rs * gr).astype(BF)


def _q_proj(cqn, w1, w2, ctab, stab, gn, gr):
    t = cqn.shape[0]
    tm = _tile(min(t, ctab.shape[0]), PROJ_TM)
    ntab = ctab.shape[0] // tm
    const = lambda i: (0, 0)
    return pl.pallas_call(
        _q_proj_body,
        grid=(t // tm,),
        in_specs=[
            pl.BlockSpec((tm, Q_RANK), lambda i: (i, 0)),
            pl.BlockSpec(w1.shape, const),
            pl.BlockSpec(w2.shape, const),
            pl.BlockSpec((tm, LANE), lambda i: (i % ntab, 0)),
            pl.BlockSpec((tm, LANE), lambda i: (i % ntab, 0)),
            pl.BlockSpec((1, LANE), const),
            pl.BlockSpec((1, LANE), const),
        ],
        out_specs=pl.BlockSpec((tm, MLA_HEADS * QPAD), lambda i: (i, 0)),
        out_shape=jax.ShapeDtypeStruct((t, MLA_HEADS * QPAD), BF),
        compiler_params=_params(("arbitrary",)),
        name="q_proj",
    )(cqn, w1, w2, ctab, stab, gn, gr)


def _kv_proj_body(c_ref, kpe_ref, w_ref, gn_ref, gr_ref, k_ref, v_ref):
    kv = _mm(c_ref[...].astype(BF), w_ref[...])
    kp = kpe_ref[...]
    kps = jnp.sum(kp * kp, axis=-1, keepdims=True)
    gn = gn_ref[...]
    gr = gr_ref[...]
    for h in range(MLA_HEADS):
        kn = kv[:, h * 2 * LANE:h * 2 * LANE + LANE]
        rs = lax.rsqrt((jnp.sum(kn * kn, axis=-1, keepdims=True) + kps) * (1.0 / QK_HEAD) + EPS)
        k_ref[:, h * QPAD:h * QPAD + LANE] = (kn * rs * gn).astype(BF)
        k_ref[:, h * QPAD + LANE:(h + 1) * QPAD] = (kp * rs * gr).astype(BF)
        v_ref[:, h * V_DIM:(h + 1) * V_DIM] = kv[:, h * 2 * LANE + LANE:(h + 1) * 2 * LANE].astype(BF)


def _kv_proj(ckv, kpe_pad, w, gn, gr):
    t = ckv.shape[0]
    tm = _tile(t, PROJ_TM)
    const = lambda i: (0, 0)
    return pl.pallas_call(
        _kv_proj_body,
        grid=(t // tm,),
        in_specs=[
            pl.BlockSpec((tm, KV_RANK), lambda i: (i, 0)),
            pl.BlockSpec((tm, LANE), lambda i: (i, 0)),
            pl.BlockSpec(w.shape, const),
            pl.BlockSpec((1, LANE), const),
            pl.BlockSpec((1, LANE), const),
        ],
        out_specs=[
            pl.BlockSpec((tm, MLA_HEADS * QPAD), lambda i: (i, 0)),
            pl.BlockSpec((tm, MLA_HEADS * V_DIM), lambda i: (i, 0)),
        ],
        out_shape=[
            jax.ShapeDtypeStruct((t, MLA_HEADS * QPAD), BF),
            jax.ShapeDtypeStruct((t, MLA_HEADS * V_DIM), BF),
        ],
        compiler_params=_params(("arbitrary",)),
        name="kv_proj",
    )(ckv, kpe_pad, w, gn, gr)


def _attn_body(q_ref, k_ref, v_ref, g_ref, o_ref, m_scr, l_scr, acc_scr, *, tq, tk, q_off, t_valid):
    qi = pl.program_id(2)
    qpos0 = q_off + qi * tq
    n_full = jnp.minimum((qpos0 // CHUNK * CHUNK + CHUNK) // tk, t_valid // tk)
    hi = jnp.minimum((qpos0 + tq - 1) // CHUNK * CHUNK + CHUNK, t_valid)
    n_total = (hi + tk - 1) // tk

    m_scr[...] = jnp.full(m_scr.shape, NEG_BIG, F32)
    l_scr[...] = jnp.zeros(l_scr.shape, F32)
    acc_scr[...] = jnp.zeros(acc_scr.shape, F32)
    nlt = tk // LANE

    def step(kc, masked):
        k0 = pl.multiple_of(kc * tk, tk)
        s = _nt(q_ref[...], k_ref[pl.ds(k0, tk), :])
        if masked:
            qpos = qpos0 + lax.broadcasted_iota(jnp.int32, (tq, 1), 0)
            last = jnp.minimum(qpos | (CHUNK - 1), t_valid - 1) - k0
            s = jnp.where(lax.broadcasted_iota(jnp.int32, (tq, tk), 1) <= last, s, NEG_BIG)
        m_prev = m_scr[...]
        m_new = jnp.maximum(m_prev, jnp.max(s, axis=-1, keepdims=True))
        alpha = jnp.exp2(m_prev - m_new)
        ps = [jnp.exp2(s[:, j * LANE:(j + 1) * LANE] - m_new) for j in range(nlt)]
        psum = ps[0]
        for pj in ps[1:]:
            psum = psum + pj
        l_scr[...] = alpha * l_scr[...] + psum
        p = jnp.concatenate(ps, axis=1).astype(BF) if nlt > 1 else ps[0].astype(BF)
        acc_scr[...] = acc_scr[...] * alpha + _mm(p, v_ref[pl.ds(k0, tk), :])
        m_scr[...] = m_new

    def full_step(kc, c):
        step(kc, False)
        return c

    def masked_step(kc, c):
        step(kc, True)
        return c

    lax.fori_loop(0, n_full, full_step, 0)
    lax.fori_loop(n_full, n_total, masked_step, 0)
    o = acc_scr[...] / jnp.sum(l_scr[...], axis=-1, keepdims=True)
    o_ref[...] = _rms(o, g_ref[...]).astype(BF)


def _attention(q, k, v, g_out, nb, q_off, t_valid):
    tq_total = q.shape[0] // nb
    tk_total = k.shape[0] // nb
    tq = _tile(tq_total, ATTN_TQ)
    tk = _tile(tk_total, ATTN_TK)
    nq = tq_total // tq
    body = functools.partial(_attn_body, tq=tq, tk=tk, q_off=q_off, t_valid=t_valid)
    return pl.pallas_call(
        body,
        grid=(nb, MLA_HEADS, nq),
        in_specs=[
            pl.BlockSpec((tq, QPAD), lambda b, h, i: (b * nq + i, h)),
            pl.BlockSpec((tk_total, QPAD), lambda b, h, i: (b, h)),
            pl.BlockSpec((tk_total, V_DIM), lambda b, h, i: (b, h)),
            pl.BlockSpec((1, V_DIM), lambda b, h, i: (0, 0)),
        ],
        out_specs=pl.BlockSpec((tq, V_DIM), lambda b, h, i: (b * nq + i, h)),
        out_shape=jax.ShapeDtypeStruct((q.shape[0], MLA_HEADS * V_DIM), BF),
        scratch_shapes=[
            pltpu.VMEM((tq, LANE), F32),
            pltpu.VMEM((tq, LANE), F32),
            pltpu.VMEM((tq, V_DIM), F32),
        ],
        compiler_params=_params(("arbitrary", "arbitrary", "arbitrary")),
        name="mla_attention",
    )(q, k, v, g_out)


def _gdn_body(q_ref, k_ref, v_ref, wq_ref, wk_ref, wv_ref, pq_ref, pk_ref, pv_ref,
              ba_ref, bat_ref, alog_l_ref, dtb_l_ref, alog_c_ref, dtb_c_ref,
              z_ref, s0_ref, gout_ref,
              o_ref, sout_ref,
              s_scr, halo_scr, buf_scr, vnew_scr, *, tm, L, hg):
    hb = pl.program_id(1)
    t = pl.program_id(2)
    nt = pl.num_programs(2)
    n_chunks = tm // L
    n_factors = int(math.log2(L)) - 1

    @pl.when(t == 0)
    def _():
        s_scr[...] = s0_ref[0]
        halo_scr[0] = pq_ref[...]
        halo_scr[1] = pk_ref[...]
        halo_scr[2] = pv_ref[...]
        vnew_scr[...] = jnp.zeros(vnew_scr.shape, BF)

    def conv(idx, x_ref, w_ref):
        buf_scr[0:HALO, :] = halo_scr[idx]
        buf_scr[HALO:HALO + tm, :] = x_ref[...]
        w = w_ref[...]
        y = buf_scr[HALO - 3:HALO - 3 + tm, :] * w[0:1]
        for i in range(1, CONV_W):
            y = y + buf_scr[HALO - 3 + i:HALO - 3 + i + tm, :] * w[i:i + 1]
        halo_scr[idx] = buf_scr[tm:tm + HALO, :]
        return y * _sigmoid(y)

    yq = conv(0, q_ref, wq_ref)
    yk = conv(1, k_ref, wk_ref)
    yv = conv(2, v_ref, wv_ref)

    ba = ba_ref[...]
    beta_all = _sigmoid(ba)
    g_all = -jnp.exp(alog_l_ref[...]) * _softplus(ba + dtb_l_ref[...])
    gt_all = -jnp.exp(alog_c_ref[...]) * _softplus(bat_ref[0] + dtb_c_ref[...])

    r = lax.broadcasted_iota(jnp.int32, (tm, tm), 0)
    c = lax.broadcasted_iota(jnp.int32, (tm, tm), 1)
    same = (r // L) == (c // L)
    lower = same & (c <= r)
    strict = same & (c < r)
    cs = jnp.where(lower, 1.0, 0.0).astype(BF)
    bd = jnp.where(same, 1.0, 0.0).astype(BF)
    g3 = _split3(g_all)
    gcol_all = _mm(cs, g3[0]) + _mm(cs, g3[1]) + _mm(cs, g3[2])
    glast_all = _mm(bd, g3[0]) + _mm(bd, g3[1]) + _mm(bd, g3[2])
    gt3 = _split3(gt_all)
    grow_all = _nt(gt3[0], cs) + _nt(gt3[1], cs) + _nt(gt3[2], cs)

    lane = lax.broadcasted_iota(jnp.int32, (1, LANE), 1)
    sub = lax.broadcasted_iota(jnp.int32, (16, 1), 0)
    gout = gout_ref[...]

    for hh in range(hg):
        hglob = hb * hg + hh
        sel_b = lane == hglob
        sel_g = lane == GDN_HEADS + hglob
        beta_c = jnp.sum(jnp.where(sel_b, beta_all, 0.0), axis=-1, keepdims=True)
        gc = jnp.sum(jnp.where(sel_g, gcol_all, 0.0), axis=-1, keepdims=True)
        glast_c = jnp.sum(jnp.where(sel_g, glast_all, 0.0), axis=-1, keepdims=True)
        gr = jnp.sum(jnp.where(sub == GDN_HEADS + hglob, grow_all, 0.0), axis=0, keepdims=True)

        qh = yq[:, hh * GDN_DK:(hh + 1) * GDN_DK]
        kh = yk[:, hh * GDN_DK:(hh + 1) * GDN_DK]
        vh = yv[:, hh * GDN_DV:(hh + 1) * GDN_DV]
        qn = qh * lax.rsqrt(jnp.sum(qh * qh, axis=-1, keepdims=True) + EPS) * (GDN_DK ** -0.5)
        kn = kh * lax.rsqrt(jnp.sum(kh * kh, axis=-1, keepdims=True) + EPS)

        decay = jnp.where(lower, jnp.exp(jnp.where(lower, gc - gr, 0.0)), 0.0)
        kb = kn.astype(BF)
        kk = _nt(kb, kb)
        qk = _nt(qn.astype(BF), kb)
        nb = jnp.where(strict, beta_c * kk * decay, 0.0).astype(BF)
        attn_b = (qk * decay).astype(BF)
        eg = jnp.exp(gc)
        rhs = jnp.concatenate([vh * beta_c, kn * (beta_c * eg)], axis=1)
        y = rhs - _mm(nb, rhs.astype(BF))
        m = _mm(nb, nb)
        for s in range(n_factors):
            mb = m.astype(BF)
            y = y + _mm(mb, y.astype(BF))
            if s < n_factors - 1:
                m = _mm(mb, mb)
        u = y[:, :GDN_DV]
        w = y[:, GDN_DV:]
        qd = qn * eg
        kd = kn * jnp.exp(glast_c - gc)
        gtot = jnp.exp(glast_c)

        st = s_scr[hh]
        outs = []
        for ci in range(n_chunks):
            r0 = ci * L
            wq = jnp.concatenate([w[r0:r0 + L], qd[r0:r0 + L]], axis=0).astype(BF)
            ws = _mm(wq, st.astype(BF))
            vn = (u[r0:r0 + L] - ws[:L]).astype(BF)
            vnew_scr[hh, r0:r0 + L, :] = vn
            outs.append(ws[L:] + _mm(attn_b[r0:r0 + L, :], vnew_scr[hh]))
            st = st * gtot[r0:r0 + 1, :] + _tn(kd[r0:r0 + L].astype(BF), vn)
        s_scr[hh] = st
        o = outs[0] if n_chunks == 1 else jnp.concatenate(outs, axis=0)
        zz = z_ref[:, hh * GDN_DV:(hh + 1) * GDN_DV].astype(F32)
        o_ref[:, hh * GDN_DV:(hh + 1) * GDN_DV] = (_rms(o, gout) * (zz * _sigmoid(zz))).astype(BF)

    @pl.when(t == nt - 1)
    def _():
        sout_ref[0] = s_scr[...]


def _gdn(qkv, conv_w, conv_past8, ba, bat, alog_l, dtb_l, alog_c, dtb_c, z, s0, gout, nb):
    t = qkv.shape[0]
    s_len = t // nb
    L = min(s_len, CHUNK)
    tm = _tile(s_len, GDN_TM, mult=L)
    nt = s_len // tm
    hg = GDN_HG
    ng = GDN_HEADS // hg
    hw = hg * GDN_DK
    body = functools.partial(_gdn_body, tm=tm, L=L, hg=hg)
    rows = lambda off: (lambda b, h, i: (b * nt + i, off + h))
    wcol = lambda off: (lambda b, h, i: (0, off + h))
    pcol = lambda off: (lambda b, h, i: (b, off + h))
    const = lambda b, h, i: (0, 0)
    return pl.pallas_call(
        body,
        grid=(nb, ng, nt),
        in_specs=[
            pl.BlockSpec((tm, hw), rows(0)),
            pl.BlockSpec((tm, hw), rows(ng)),
            pl.BlockSpec((tm, hw), rows(2 * ng)),
            pl.BlockSpec((CONV_W, hw), wcol(0)),
            pl.BlockSpec((CONV_W, hw), wcol(ng)),
            pl.BlockSpec((CONV_W, hw), wcol(2 * ng)),
            pl.BlockSpec((HALO, hw), pcol(0)),
            pl.BlockSpec((HALO, hw), pcol(ng)),
            pl.BlockSpec((HALO, hw), pcol(2 * ng)),
            pl.BlockSpec((tm, LANE), lambda b, h, i: (b * nt + i, 0)),
            pl.BlockSpec((1, 16, tm), lambda b, h, i: (b, 0, i)),
            pl.BlockSpec((1, LANE), const),
            pl.BlockSpec((1, LANE), const),
            pl.BlockSpec((16, 1), const),
            pl.BlockSpec((16, 1), const),
            pl.BlockSpec((tm, hw), rows(0)),
            pl.BlockSpec((1, hg, GDN_DK, GDN_DV), lambda b, h, i: (b, h, 0, 0)),
            pl.BlockSpec((1, GDN_DV), const),
        ],
        out_specs=[
            pl.BlockSpec((tm, hw), rows(0)),
            pl.BlockSpec((1, hg, GDN_DK, GDN_DV), lambda b, h, i: (b, h, 0, 0)),
        ],
        out_shape=[
            jax.ShapeDtypeStruct((t, GDN_HEADS * GDN_DV), BF),
            jax.ShapeDtypeStruct((nb, GDN_HEADS, GDN_DK, GDN_DV), F32),
        ],
        scratch_shapes=[
            pltpu.VMEM((hg, GDN_DK, GDN_DV), F32),
            pltpu.VMEM((3, HALO, hw), F32),
            pltpu.VMEM((HALO + tm, hw), F32),
            pltpu.VMEM((hg, tm, GDN_DV), BF),
        ],
        compiler_params=_params(("arbitrary", "arbitrary", "arbitrary")),
        name="gdn",
    )(qkv, qkv, qkv, conv_w, conv_w, conv_w, conv_past8, conv_past8, conv_past8,
      ba, bat, alog_l, dtb_l, alog_c, dtb_c, z, s0, gout)


def _mix_mem_body(x_ref, a1_ref, a2_ref, wo1_ref, wo2_ref, gq_ref, wmq_ref, gmq_ref,
                  mk_ref, mv_ref, wmo_ref, o_ref):
    h1 = x_ref[...] + _mm(a1_ref[...], wo1_ref[...]) + _mm(a2_ref[...], wo2_ref[...])
    hn = _rms(h1, gq_ref[...]).astype(BF)
    q = _mm(hn, wmq_ref[...])
    gmq = gmq_ref[...]
    mk = mk_ref[0]
    mv = mv_ref[0]
    outs = []
    for h in range(MEM_HEADS):
        sl = slice(h * MEM_DIM, (h + 1) * MEM_DIM)
        qh = _rms(q[:, sl], gmq).astype(BF)
        s = _nt(qh, mk[:, sl]) * (MEM_DIM ** -0.5)
        p = jnp.exp(s - jnp.max(s, axis=-1, keepdims=True))
        p = p / jnp.sum(p, axis=-1, keepdims=True)
        outs.append(_mm(p.astype(BF), mv[:, sl]))
    o = jnp.concatenate(outs, axis=1).astype(BF)
    o_ref[...] = h1 + _mm(o, wmo_ref[...])


def _mix_mem(x, a1, a2, wo1, wo2, gq, wmq, gmq, mk, mv, wmo, nb):
    t, d = x.shape
    s_len = t // nb
    tm = _tile(s_len, MIX_TM)
    nt = s_len // tm
    n_mem = mk.shape[1]
    mw = MEM_HEADS * MEM_DIM
    row = lambda i: (i, 0)
    const = lambda i: (0, 0)
    return pl.pallas_call(
        _mix_mem_body,
        grid=(t // tm,),
        in_specs=[
            pl.BlockSpec((tm, d), row),
            pl.BlockSpec((tm, a1.shape[1]), row),
            pl.BlockSpec((tm, a2.shape[1]), row),
            pl.BlockSpec(wo1.shape, const),
            pl.BlockSpec(wo2.shape, const),
            pl.BlockSpec((1, d), const),
            pl.BlockSpec(wmq.shape, const),
            pl.BlockSpec((1, MEM_DIM), const),
            pl.BlockSpec((1, n_mem, mw), lambda i: (i // nt, 0, 0)),
            pl.BlockSpec((1, n_mem, mw), lambda i: (i // nt, 0, 0)),
            pl.BlockSpec(wmo.shape, const),
        ],
        out_specs=pl.BlockSpec((tm, d), row),
        out_shape=jax.ShapeDtypeStruct((t, d), F32),
        compiler_params=_params(("arbitrary",)),
        name="mix_mem",
    )(x, a1, a2, wo1, wo2, gq, wmq, gmq, mk, mv, wmo)


def _ffn_body(x_ref, g_ref, w1_ref, w2_ref, o_ref, hn_scr):
    f = pl.program_id(1)

    @pl.when(f == 0)
    def _():
        x = x_ref[...]
        hn_scr[...] = _rms(x, g_ref[...]).astype(BF)
        o_ref[...] = x

    a = jnp.maximum(_mm(hn_scr[...], w1_ref[...]), 0.0)
    o_ref[...] += _mm((a * a).astype(BF), w2_ref[...])


def _ffn(x, g, w1, w2):
    t, d = x.shape
    dff = w1.shape[1]
    tm = _tile(t, FFN_TM)
    tf = _tile(dff, FFN_TF, mult=LANE)
    return pl.pallas_call(
        _ffn_body,
        grid=(t // tm, dff // tf),
        in_specs=[
            pl.BlockSpec((tm, d), lambda i, f: (i, 0)),
            pl.BlockSpec((1, d), lambda i, f: (0, 0)),
            pl.BlockSpec((d, tf), lambda i, f: (0, f)),
            pl.BlockSpec((tf, d), lambda i, f: (f, 0)),
        ],
        out_specs=pl.BlockSpec((tm, d), lambda i, f: (i, 0)),
        out_shape=jax.ShapeDtypeStruct((t, d), F32),
        scratch_shapes=[pltpu.VMEM((tm, d), BF)],
        compiler_params=_params(("arbitrary", "arbitrary")),
        name="ffn",
    )(x, g, w1, w2)


def _mem_kv_body(m_ref, g_ref, wk_ref, wv_ref, gk_ref, k_ref, v_ref):
    mn = _rms(m_ref[...], g_ref[...]).astype(BF)
    k = _mm(mn, wk_ref[...])
    gk = gk_ref[...]
    for h in range(MEM_HEADS):
        sl = slice(h * MEM_DIM, (h + 1) * MEM_DIM)
        k_ref[:, sl] = _rms(k[:, sl], gk)
    v_ref[...] = _mm(mn, wv_ref[...])


def _mem_kv(mem, g, wk, wv, gk):
    t, d = mem.shape
    tm = _tile(t, 256)
    mw = MEM_HEADS * MEM_DIM
    const = lambda i: (0, 0)
    return pl.pallas_call(
        _mem_kv_body,
        grid=(t // tm,),
        in_specs=[
            pl.BlockSpec((tm, d), lambda i: (i, 0)),
            pl.BlockSpec((1, d), const),
            pl.BlockSpec(wk.shape, const),
            pl.BlockSpec(wv.shape, const),
            pl.BlockSpec((1, MEM_DIM), const),
        ],
        out_specs=[pl.BlockSpec((tm, mw), lambda i: (i, 0)), pl.BlockSpec((tm, mw), lambda i: (i, 0))],
        out_shape=[jax.ShapeDtypeStruct((t, mw), F32), jax.ShapeDtypeStruct((t, mw), F32)],
        compiler_params=_params(("arbitrary",)),
        name="mem_kv",
    )(mem, g, wk, wv, gk)


def _rope_tables(pos):
    half = QK_ROPE // 2
    inv_freq = ROPE_THETA ** (-jnp.arange(half, dtype=F32) / half)
    ang = pos.astype(F32)[:, None] * inv_freq[None, :]
    cos, sin = jnp.cos(ang), jnp.sin(ang)
    zeros = jnp.zeros((pos.shape[0], LANE - QK_ROPE), F32)
    return (jnp.concatenate([cos, cos, zeros], axis=1), jnp.concatenate([-sin, sin, zeros], axis=1))


def _pad_lanes(v, n=LANE):
    return jnp.pad(v, ((0, 0), (0, n - v.shape[1])))


def _prep_weights(w_in, g_cq, w_uq, g_ckv, w_ukv, g_q, g_k, w_o, w_mq, w_mo, w_ff1, w_ff2, w_mk, w_mv):
    d = w_in.shape[0]
    n_conv = 2 * GDN_HEADS * GDN_DK + GDN_HEADS * GDN_DV
    n_z = GDN_HEADS * GDN_DV
    o = 0
    w_cq = w_in[:, o:o + Q_RANK]; o += Q_RANK
    w_ckv = w_in[:, o:o + KV_RANK]; o += KV_RANK
    w_kpe = w_in[:, o:o + QK_ROPE]; o += QK_ROPE
    w_qkv = w_in[:, o:o + n_conv]; o += n_conv
    w_z = w_in[:, o:o + n_z]; o += n_z
    w_b = w_in[:, o:o + GDN_HEADS]; o += GDN_HEADS
    w_a = w_in[:, o:o + GDN_HEADS]
    half = QK_ROPE // 2
    swap = jnp.concatenate([jnp.arange(half, QK_ROPE), jnp.arange(0, half)])
    z64 = jnp.zeros((d, LANE - QK_ROPE), w_in.dtype)
    misc = jnp.concatenate([w_kpe, z64, w_kpe[:, swap], z64, w_b, w_a], axis=1)
    misc = _pad_lanes(misc, COL_TILE)
    w_in_p = jnp.concatenate([w_cq, w_ckv, w_qkv, w_z, misc], axis=1).astype(BF)
    wbat = jnp.concatenate([w_b, w_a], axis=1).T.astype(BF)

    r = w_uq.shape[0]
    wq3 = w_uq.reshape(r, MLA_HEADS, QK_HEAD)
    zq = jnp.zeros((r, MLA_HEADS, LANE - QK_ROPE), w_uq.dtype)
    w1 = jnp.concatenate([wq3, zq], axis=2).reshape(r, MLA_HEADS * QPAD).astype(BF)
    w2 = jnp.concatenate([wq3[:, :, QK_NOPE:][:, :, swap], zq], axis=2).reshape(r, MLA_HEADS * LANE).astype(BF)

    dmla = MLA_HEADS * V_DIM
    return dict(
        n_conv=n_conv, n_z=n_z, w_in_p=w_in_p, wbat=wbat, w1=w1, w2=w2,
        gcq=g_cq[None, :], gckv=g_ckv[None, :],
        gq_n=g_q[None, :QK_NOPE], gq_r=_pad_lanes(g_q[None, QK_NOPE:]),
        gk_n=g_k[None, :QK_NOPE], gk_r=_pad_lanes(g_k[None, QK_NOPE:]),
        w_ukv=w_ukv.astype(BF), wo1=w_o[:dmla].astype(BF), wo2=w_o[dmla:].astype(BF),
        w_mq=w_mq.astype(BF), w_mo=w_mo.astype(BF), w_ff1=w_ff1.astype(BF), w_ff2=w_ff2.astype(BF),
        w_mk=w_mk.astype(BF), w_mv=w_mv.astype(BF),
    )


def _gate_params(a_log, dt_bias):
    z8 = jnp.zeros((GDN_HEADS,), F32)
    al = jnp.concatenate([z8, a_log.astype(F32)])
    db = jnp.concatenate([z8, dt_bias.astype(F32)])
    return _pad_lanes(al[None, :]), _pad_lanes(db[None, :]), al[:, None], db[:, None]


def _layer(x, pos, past, conv_past, s0, mem_k, mem_v, wp, lw):
    nb, s_len, d = x.shape
    t = nb * s_len
    xf = x.reshape(t, d)
    ctab, stab = _rope_tables(pos)
    if ctab.shape[0] % 8 != 0 or (s_len < IN_TM and nb > 1):
        ctab, stab = jnp.tile(ctab, (nb, 1)), jnp.tile(stab, (nb, 1))

    cqn, ckv, kpe_pad, qkv, z, ba, bat = _in_proj(
        xf, lw["g_norm_mix"], wp["w_in_p"], wp["wbat"], wp["gcq"], wp["gckv"], ctab, stab,
        wp["n_conv"], wp["n_z"])
    q = _q_proj(cqn, wp["w1"], wp["w2"], ctab, stab, wp["gq_n"], wp["gq_r"])

    if past is None:
        ckv_all, kpe_all, q_off, t_valid = ckv, kpe_pad, 0, s_len
    else:
        ckv_past, kpe_past = past
        p_len = ckv_past.shape[1]
        t_valid = p_len + s_len
        t_pad = -(-t_valid // ATTN_TK) * ATTN_TK
        ckv_all = jnp.concatenate(
            [ckv_past.astype(F32), ckv.reshape(nb, s_len, KV_RANK),
             jnp.zeros((nb, t_pad - t_valid, KV_RANK), F32)], axis=1).reshape(nb * t_pad, KV_RANK)
        kpe_all = jnp.concatenate(
            [jnp.pad(kpe_past.astype(F32), ((0, 0), (0, 0), (0, LANE - QK_ROPE))),
             kpe_pad.reshape(nb, s_len, LANE),
             jnp.zeros((nb, t_pad - t_valid, LANE), F32)], axis=1).reshape(nb * t_pad, LANE)
        q_off = p_len
    k, v = _kv_proj(ckv_all, kpe_all, wp["w_ukv"], wp["gk_n"], wp["gk_r"])
    o_mla = _attention(q, k, v, lw["g_mla_out"], nb, q_off, t_valid)

    conv_past8 = jnp.pad(conv_past.astype(F32), ((0, 0), (HALO - (CONV_W - 1), 0), (0, 0)))
    conv_past8 = conv_past8.reshape(nb * HALO, -1)
    bat3 = bat.reshape(16, nb, s_len).transpose(1, 0, 2)
    alog_l, dtb_l, alog_c, dtb_c = _gate_params(lw["a_log"], lw["dt_bias"])
    o_gdn, s_new = _gdn(qkv, lw["conv_w"], conv_past8, ba, bat3, alog_l, dtb_l, alog_c, dtb_c,
                        z, s0.astype(F32), lw["g_gdn_out"], nb)

    h2 = _mix_mem(xf, o_mla, o_gdn, wp["wo1"], wp["wo2"], lw["g_norm_mem_q"], wp["w_mq"],
                  lw["g_mq"], mem_k, mem_v, wp["w_mo"], nb)
    y = _ffn(h2, lw["g_norm_ffn"], wp["w_ff1"], wp["w_ff2"])

    conv_in_tail = jnp.concatenate([conv_past.astype(F32), qkv.reshape(nb, s_len, -1)[:, -(CONV_W - 1):]], axis=1)
    conv_new = conv_in_tail[:, -(CONV_W - 1):]
    return (y.reshape(nb, s_len, d), ckv.reshape(nb, s_len, KV_RANK),
            kpe_pad[:, :QK_ROPE].reshape(nb, s_len, QK_ROPE), conv_new, s_new)


def kernel(x_prompt, x_sample, mem_prompt, cache_mla_ckv, cache_mla_kpe, cache_gdn_conv, state_gdn, cache_mem_k, cache_mem_v, g_norm_mix, w_in, g_cq, w_uq, g_ckv, w_ukv, g_q_mla, g_k_mla, g_mla_out, conv_w, a_log, dt_bias, g_gdn_out, w_o, g_norm_mem_q, g_norm_mem_kv, w_mq, w_mk, w_mv, g_mq, g_mk, w_mo, g_norm_ffn, w_ff1, w_ff2):
    depth = w_in.shape[0]
    nbp, sp, d = x_prompt.shape
    nbs, ss, _ = x_sample.shape
    n_mem = mem_prompt.shape[1]
    mw = MEM_HEADS * MEM_DIM
    n_conv = 2 * GDN_HEADS * GDN_DK + GDN_HEADS * GDN_DV
    pos_p = jnp.arange(sp)
    pos_s = cache_mla_ckv.shape[2] + jnp.arange(ss)
    zeros_conv = jnp.zeros((nbp, CONV_W - 1, n_conv), F32)
    zeros_state = jnp.zeros((nbp, GDN_HEADS, GDN_DK, GDN_DV), F32)
    hp, hs = x_prompt, x_sample
    outs_p = [[] for _ in range(6)]
    outs_s = [[] for _ in range(4)]
    for l in range(depth):
        wp = _prep_weights(w_in[l], g_cq[l], w_uq[l], g_ckv[l], w_ukv[l], g_q_mla[l], g_k_mla[l], w_o[l],
                           w_mq[l], w_mo[l], w_ff1[l], w_ff2[l], w_mk[l], w_mv[l])
        lw = dict(g_norm_mix=g_norm_mix[l][None, :], g_mla_out=g_mla_out[l][None, :], conv_w=conv_w[l],
                  a_log=a_log[l], dt_bias=dt_bias[l], g_gdn_out=g_gdn_out[l][None, :],
                  g_norm_mem_q=g_norm_mem_q[l][None, :], g_mq=g_mq[l][None, :],
                  g_norm_ffn=g_norm_ffn[l][None, :])
        mk, mv = _mem_kv(mem_prompt.reshape(nbp * n_mem, d), g_norm_mem_kv[l][None, :], wp["w_mk"], wp["w_mv"],
                         g_mk[l][None, :])
        mk3, mv3 = mk.reshape(nbp, n_mem, mw), mv.reshape(nbp, n_mem, mw)
        hp, c1, c2, c3, c4 = _layer(hp, pos_p, None, zeros_conv, zeros_state, mk3.astype(BF), mv3.astype(BF), wp, lw)
        for lst, val in zip(outs_p, (c1, c2, c3, c4, mk3.reshape(nbp, n_mem, MEM_HEADS, MEM_DIM),
                                     mv3.reshape(nbp, n_mem, MEM_HEADS, MEM_DIM))):
            lst.append(val)
        hs, d1, d2, d3, d4 = _layer(hs, pos_s, (cache_mla_ckv[l], cache_mla_kpe[l]), cache_gdn_conv[l], state_gdn[l],
                                    cache_mem_k[l].reshape(nbs, n_mem, mw).astype(BF),
                                    cache_mem_v[l].reshape(nbs, n_mem, mw).astype(BF), wp, lw)
        for lst, val in zip(outs_s, (d1, d2, d3, d4)):
            lst.append(val)
    return (hp, hs, *(jnp.stack(v) for v in outs_p), *(jnp.stack(v) for v in outs_s))
```

```python
import functools
import math

import jax
import jax.numpy as jnp
from jax import lax
from jax.experimental import pallas as pl
from jax.experimental.pallas import tpu as pltpu

F32 = jnp.float32
BF = jnp.bfloat16

EPS = 1e-6
CHUNK = 64
ROPE_THETA = 10000.0
MLA_HEADS = 8
QK_NOPE = 128
QK_ROPE = 64
QK_HEAD = QK_NOPE + QK_ROPE
V_DIM = 128
Q_RANK = 512
KV_RANK = 512
GDN_HEADS = 8
GDN_DK = 128
GDN_DV = 128
CONV_W = 4
MEM_HEADS = 4
MEM_DIM = 128
LANE = 128
QPAD = 2 * LANE
COL_TILE = 512
HALO = 8
NEG_BIG = -1e30
QSCALE = (QK_HEAD ** -0.5) * math.log2(math.e)

VMEM_LIMIT = 56 * 1024 * 1024

IN_TM = 1024
PROJ_TM = 512
ATTN_TQ = 1024
ATTN_TK = 1024
GDN_TM = 256
GDN_HG = 8
MIX_TM = 512
FFN_TM = 1024
FFN_TF = 512


def _tile(n, pref, mult=8):
    if n <= pref:
        return n
    t = (pref // mult) * mult
    while t >= mult:
        if n % t == 0:
            return t
        t -= mult
    return n


def _nt(a, b):
    return lax.dot_general(a, b, (((1,), (1,)), ((), ())), preferred_element_type=F32)


def _tn(a, b):
    return lax.dot_general(a, b, (((0,), (0,)), ((), ())), preferred_element_type=F32)


def _mm(a, b):
    return jnp.dot(a, b, preferred_element_type=F32)


def _mm_split(a, b):
    ah = a.astype(BF)
    al = (a - ah.astype(F32)).astype(BF)
    bh = b.astype(BF)
    bl = (b - bh.astype(F32)).astype(BF)
    return _mm(ah, bh) + (_mm(al, bh) + _mm(ah, bl))


def _sigmoid(x):
    return 1.0 / (1.0 + jnp.exp(-x))


def _softplus(x):
    return jnp.maximum(x, 0.0) + jnp.log(1.0 + jnp.exp(-jnp.abs(x)))


def _rms(x, g, n=None):
    n = x.shape[-1] if n is None else n
    ms = jnp.sum(x * x, axis=-1, keepdims=True) * (1.0 / n)
    return (x * lax.rsqrt(ms + EPS)) * g


def _split3(x):
    hi = x.astype(BF)
    r1 = x - hi.astype(F32)
    mid = r1.astype(BF)
    lo = (r1 - mid.astype(F32)).astype(BF)
    return hi, mid, lo


def _params(sem):
    return pltpu.CompilerParams(dimension_semantics=sem, vmem_limit_bytes=VMEM_LIMIT)


def _in_proj_body(x_ref, g_ref, w_ref, wbat_ref, gcq_ref, gckv_ref, ctab_ref, stab_ref,
                  cqn_ref, ckv_ref, kpe_ref, qkv_ref, z_ref, ba_ref, bat_ref, xn_scr, *, nqkv, nz):
    j = pl.program_id(1)

    @pl.when(j == 0)
    def _():
        xn_scr[...] = _rms(x_ref[...], g_ref[...]).astype(BF)

    acc = _mm(xn_scr[...], w_ref[...])

    @pl.when(j == 0)
    def _():
        cqn_ref[...] = _rms(acc, gcq_ref[...]).astype(BF)

    @pl.when(j == 1)
    def _():
        ckv_ref[...] = _rms(acc, gckv_ref[...])

    @pl.when((j >= 2) & (j < 2 + nqkv))
    def _():
        qkv_ref[...] = acc

    @pl.when((j >= 2 + nqkv) & (j < 2 + nqkv + nz))
    def _():
        z_ref[...] = acc.astype(BF)

    @pl.when(j == 2 + nqkv + nz)
    def _():
        kpe_ref[...] = acc[:, 0:LANE] * ctab_ref[...] + acc[:, LANE:2 * LANE] * stab_ref[...]
        ba_ref[...] = acc[:, 2 * LANE:3 * LANE]
        bat_ref[...] = _nt(wbat_ref[...], xn_scr[...])


def _in_proj(x, g, w_p, wbat, gcq, gckv, ctab, stab, n_conv, n_z):
    t, d = x.shape
    tm = _tile(min(t, ctab.shape[0]), IN_TM)
    ntab = ctab.shape[0] // tm
    nqkv = n_conv // COL_TILE
    nz = n_z // COL_TILE
    ncol = 2 + nqkv + nz + 1
    assert w_p.shape[1] == ncol * COL_TILE
    row = lambda i, j: (i, 0)
    const = lambda i, j: (0, 0)
    body = functools.partial(_in_proj_body, nqkv=nqkv, nz=nz)
    return pl.pallas_call(
        body,
        grid=(t // tm, ncol),
        in_specs=[
            pl.BlockSpec((tm, d), row),
            pl.BlockSpec((1, d), const),
            pl.BlockSpec((d, COL_TILE), lambda i, j: (0, j)),
            pl.BlockSpec((16, d), const),
            pl.BlockSpec((1, Q_RANK), const),
            pl.BlockSpec((1, KV_RANK), const),
            pl.BlockSpec((tm, LANE), lambda i, j: (i % ntab, 0)),
            pl.BlockSpec((tm, LANE), lambda i, j: (i % ntab, 0)),
        ],
        out_specs=[
            pl.BlockSpec((tm, Q_RANK), row),
            pl.BlockSpec((tm, KV_RANK), row),
            pl.BlockSpec((tm, LANE), row),
            pl.BlockSpec((tm, COL_TILE), lambda i, j: (i, jnp.clip(j - 2, 0, nqkv - 1))),
            pl.BlockSpec((tm, COL_TILE), lambda i, j: (i, jnp.clip(j - 2 - nqkv, 0, nz - 1))),
            pl.BlockSpec((tm, LANE), row),
            pl.BlockSpec((16, tm), lambda i, j: (0, i)),
        ],
        out_shape=[
            jax.ShapeDtypeStruct((t, Q_RANK), BF),
            jax.ShapeDtypeStruct((t, KV_RANK), F32),
            jax.ShapeDtypeStruct((t, LANE), F32),
            jax.ShapeDtypeStruct((t, n_conv), F32),
            jax.ShapeDtypeStruct((t, n_z), BF),
            jax.ShapeDtypeStruct((t, LANE), F32),
            jax.ShapeDtypeStruct((16, t), F32),
        ],
        scratch_shapes=[pltpu.VMEM((tm, d), BF)],
        compiler_params=_params(("arbitrary", "arbitrary")),
        name="in_proj",
    )(x, g, w_p, wbat, gcq, gckv, ctab, stab)


def _q_proj_body(c_ref, w1_ref, w2_ref, ctab_ref, stab_ref, gn_ref, gr_ref, q_ref):
    c = c_ref[...]
    qf = _mm(c, w1_ref[...])
    qs = _mm(c, w2_ref[...])
    ct = ctab_ref[...]
    st = stab_ref[...]
    gn = gn_ref[...]
    gr = gr_ref[...]
    for h in range(MLA_HEADS):
        nope = qf[:, h * QPAD:h * QPAD + LANE]
        rot = qf[:, h * QPAD + LANE:(h + 1) * QPAD] * ct + qs[:, h * LANE:(h + 1) * LANE] * st
        ss = jnp.sum(nope * nope, axis=-1, keepdims=True) + jnp.sum(rot * rot, axis=-1, keepdims=True)
        rs = lax.rsqrt(ss * (1.0 / QK_HEAD) + EPS) * QSCALE
        q_ref[:, h * QPAD:h * QPAD + LANE] = (nope * rs * gn).astype(BF)
        q_ref[:, h * QPAD + LANE:(h + 1) * QPAD] = (rot * rs * gr).astype(BF)


def _q_proj(cqn, w1, w2, ctab, stab, gn, gr):
    t = cqn.shape[0]
    tm = _tile(min(t, ctab.shape[0]), PROJ_TM)
    ntab = ctab.shape[0] // tm
    const = lambda i: (0, 0)
    return pl.pallas_call(
        _q_proj_body,
        grid=(t // tm,),
        in_specs=[
            pl.BlockSpec((tm, Q_RANK), lambda i: (i, 0)),
            pl.BlockSpec(w1.shape, const),
            pl.BlockSpec(w2.shape, const),
            pl.BlockSpec((tm, LANE), lambda i: (i % ntab, 0)),
            pl.BlockSpec((tm, LANE), lambda i: (i % ntab, 0)),
            pl.BlockSpec((1, LANE), const),
            pl.BlockSpec((1, LANE), const),
        ],
        out_specs=pl.BlockSpec((tm, MLA_HEADS * QPAD), lambda i: (i, 0)),
        out_shape=jax.ShapeDtypeStruct((t, MLA_HEADS * QPAD), BF),
        compiler_params=_params(("arbitrary",)),
        name="q_proj",
    )(cqn, w1, w2, ctab, stab, gn, gr)


def _kv_proj_body(c_ref, kpe_ref, w_ref, gn_ref, gr_ref, k_ref, v_ref):
    kv = _mm(c_ref[...].astype(BF), w_ref[...])
    kp = kpe_ref[...]
    kps = jnp.sum(kp * kp, axis=-1, keepdims=True)
    gn = gn_ref[...]
    gr = gr_ref[...]
    for h in range(MLA_HEADS):
        kn = kv[:, h * 2 * LANE:h * 2 * LANE + LANE]
        rs = lax.rsqrt((jnp.sum(kn * kn, axis=-1, keepdims=True) + kps) * (1.0 / QK_HEAD) + EPS)
        k_ref[:, h * QPAD:h * QPAD + LANE] = (kn * rs * gn).astype(BF)
        k_ref[:, h * QPAD + LANE:(h + 1) * QPAD] = (kp * rs * gr).astype(BF)
        v_ref[:, h * V_DIM:(h + 1) * V_DIM] = kv[:, h * 2 * LANE + LANE:(h + 1) * 2 * LANE].astype(BF)


def _kv_proj(ckv, kpe_pad, w, gn, gr):
    t = ckv.shape[0]
    tm = _tile(t, PROJ_TM)
    const = lambda i: (0, 0)
    return pl.pallas_call(
        _kv_proj_body,
        grid=(t // tm,),
        in_specs=[
            pl.BlockSpec((tm, KV_RANK), lambda i: (i, 0)),
            pl.BlockSpec((tm, LANE), lambda i: (i, 0)),
            pl.BlockSpec(w.shape, const),
            pl.BlockSpec((1, LANE), const),
            pl.BlockSpec((1, LANE), const),
        ],
        out_specs=[
            pl.BlockSpec((tm, MLA_HEADS * QPAD), lambda i: (i, 0)),
            pl.BlockSpec((tm, MLA_HEADS * V_DIM), lambda i: (i, 0)),
        ],
        out_shape=[
            jax.ShapeDtypeStruct((t, MLA_HEADS * QPAD), BF),
            jax.ShapeDtypeStruct((t, MLA_HEADS * V_DIM), BF),
        ],
        compiler_params=_params(("arbitrary",)),
        name="kv_proj",
    )(ckv, kpe_pad, w, gn, gr)


def _attn_body(q_ref, k_ref, v_ref, g_ref, o_ref, m_scr, l_scr, acc_scr, *, tq, tk, q_off, t_valid):
    qi = pl.program_id(2)
    qpos0 = q_off + qi * tq
    n_full = jnp.minimum((qpos0 // CHUNK * CHUNK + CHUNK) // tk, t_valid // tk)
    hi = jnp.minimum((qpos0 + tq - 1) // CHUNK * CHUNK + CHUNK, t_valid)
    n_total = (hi + tk - 1) // tk

    m_scr[...] = jnp.full(m_scr.shape, NEG_BIG, F32)
    l_scr[...] = jnp.zeros(l_scr.shape, F32)
    acc_scr[...] = jnp.zeros(acc_scr.shape, F32)
    nlt = tk // LANE

    def step(kc, masked):
        k0 = pl.multiple_of(kc * tk, tk)
        s = _nt(q_ref[...], k_ref[pl.ds(k0, tk), :])
        if masked:
            qpos = qpos0 + lax.broadcasted_iota(jnp.int32, (tq, 1), 0)
            last = jnp.minimum(qpos | (CHUNK - 1), t_valid - 1) - k0
            s = jnp.where(lax.broadcasted_iota(jnp.int32, (tq, tk), 1) <= last, s, NEG_BIG)
        m_prev = m_scr[...]
        m_new = jnp.maximum(m_prev, jnp.max(s, axis=-1, keepdims=True))
        alpha = jnp.exp2(m_prev - m_new)
        ps = [jnp.exp2(s[:, j * LANE:(j + 1) * LANE] - m_new) for j in range(nlt)]
        psum = ps[0]
        for pj in ps[1:]:
            psum = psum + pj
        l_scr[...] = alpha * l_scr[...] + psum
        p = jnp.concatenate(ps, axis=1).astype(BF) if nlt > 1 else ps[0].astype(BF)
        acc_scr[...] = acc_scr[...] * alpha + _mm(p, v_ref[pl.ds(k0, tk), :])
        m_scr[...] = m_new

    def full_step(kc, c):
        step(kc, False)
        return c

    def masked_step(kc, c):
        step(kc, True)
        return c

    lax.fori_loop(0, n_full, full_step, 0)
    lax.fori_loop(n_full, n_total, masked_step, 0)
    o = acc_scr[...] / jnp.sum(l_scr[...], axis=-1, keepdims=True)
    o_ref[...] = _rms(o, g_ref[...]).astype(BF)


def _attention(q, k, v, g_out, nb, q_off, t_valid):
    tq_total = q.shape[0] // nb
    tk_total = k.shape[0] // nb
    tq = _tile(tq_total, ATTN_TQ)
    tk = _tile(tk_total, ATTN_TK)
    nq = tq_total // tq
    body = functools.partial(_attn_body, tq=tq, tk=tk, q_off=q_off, t_valid=t_valid)
    return pl.pallas_call(
        body,
        grid=(nb, MLA_HEADS, nq),
        in_specs=[
            pl.BlockSpec((tq, QPAD), lambda b, h, i: (b * nq + i, h)),
            pl.BlockSpec((tk_total, QPAD), lambda b, h, i: (b, h)),
            pl.BlockSpec((tk_total, V_DIM), lambda b, h, i: (b, h)),
            pl.BlockSpec((1, V_DIM), lambda b, h, i: (0, 0)),
        ],
        out_specs=pl.BlockSpec((tq, V_DIM), lambda b, h, i: (b * nq + i, h)),
        out_shape=jax.ShapeDtypeStruct((q.shape[0], MLA_HEADS * V_DIM), BF),
        scratch_shapes=[
            pltpu.VMEM((tq, LANE), F32),
            pltpu.VMEM((tq, LANE), F32),
            pltpu.VMEM((tq, V_DIM), F32),
        ],
        compiler_params=_params(("arbitrary", "arbitrary", "arbitrary")),
        name="mla_attention",
    )(q, k, v, g_out)


def _gdn_body(q_ref, k_ref, v_ref, wq_ref, wk_ref, wv_ref, pq_ref, pk_ref, pv_ref,
              ba_ref, bat_ref, alog_l_ref, dtb_l_ref, alog_c_ref, dtb_c_ref,
              z_ref, s0_ref, gout_ref,
              o_ref, sout_ref,
              s_scr, halo_scr, buf_scr, vnew_scr, *, tm, L, hg):
    hb = pl.program_id(1)
    t = pl.program_id(2)
    nt = pl.num_programs(2)
    n_chunks = tm // L
    n_factors = int(math.log2(L)) - 1

    @pl.when(t == 0)
    def _():
        s_scr[...] = s0_ref[0]
        halo_scr[0] = pq_ref[...]
        halo_scr[1] = pk_ref[...]
        halo_scr[2] = pv_ref[...]
        vnew_scr[...] = jnp.zeros(vnew_scr.shape, BF)

    def conv(idx, x_ref, w_ref):
        buf_scr[0:HALO, :] = halo_scr[idx]
        buf_scr[HALO:HALO + tm, :] = x_ref[...]
        w = w_ref[...]
        y = buf_scr[HALO - 3:HALO - 3 + tm, :] * w[0:1]
        for i in range(1, CONV_W):
            y = y + buf_scr[HALO - 3 + i:HALO - 3 + i + tm, :] * w[i:i + 1]
        halo_scr[idx] = buf_scr[tm:tm + HALO, :]
        return y * _sigmoid(y)

    yq = conv(0, q_ref, wq_ref)
    yk = conv(1, k_ref, wk_ref)
    yv = conv(2, v_ref, wv_ref)

    ba = ba_ref[...]
    beta_all = _sigmoid(ba)
    g_all = -jnp.exp(alog_l_ref[...]) * _softplus(ba + dtb_l_ref[...])
    gt_all = -jnp.exp(alog_c_ref[...]) * _softplus(bat_ref[0] + dtb_c_ref[...])

    r = lax.broadcasted_iota(jnp.int32, (tm, tm), 0)
    c = lax.broadcasted_iota(jnp.int32, (tm, tm), 1)
    same = (r // L) == (c // L)
    lower = same & (c <= r)
    strict = same & (c < r)
    cs = jnp.where(lower, 1.0, 0.0).astype(BF)
    bd = jnp.where(same, 1.0, 0.0).astype(BF)
    g3 = _split3(g_all)
    gcol_all = _mm(cs, g3[0]) + _mm(cs, g3[1]) + _mm(cs, g3[2])
    glast_all = _mm(bd, g3[0]) + _mm(bd, g3[1]) + _mm(bd, g3[2])
    gt3 = _split3(gt_all)
    grow_all = _nt(gt3[0], cs) + _nt(gt3[1], cs) + _nt(gt3[2], cs)

    lane = lax.broadcasted_iota(jnp.int32, (1, LANE), 1)
    sub = lax.broadcasted_iota(jnp.int32, (16, 1), 0)
    gout = gout_ref[...]

    heads = range(hg)
    beta_c, gc, glast_c, gr = [], [], [], []
    for hh in heads:
        hglob = hb * hg + hh
        sel_g = lane == GDN_HEADS + hglob
        beta_c.append(jnp.sum(jnp.where(lane == hglob, beta_all, 0.0), axis=-1, keepdims=True))
        gc.append(jnp.sum(jnp.where(sel_g, gcol_all, 0.0), axis=-1, keepdims=True))
        glast_c.append(jnp.sum(jnp.where(sel_g, glast_all, 0.0), axis=-1, keepdims=True))
        gr.append(jnp.sum(jnp.where(sub == GDN_HEADS + hglob, grow_all, 0.0), axis=0, keepdims=True))

    qn, kn, vh = [], [], []
    for hh in heads:
        qh = yq[:, hh * GDN_DK:(hh + 1) * GDN_DK]
        kh = yk[:, hh * GDN_DK:(hh + 1) * GDN_DK]
        qn.append(qh * lax.rsqrt(jnp.sum(qh * qh, axis=-1, keepdims=True) + EPS) * (GDN_DK ** -0.5))
        kn.append(kh * lax.rsqrt(jnp.sum(kh * kh, axis=-1, keepdims=True) + EPS))
        vh.append(yv[:, hh * GDN_DV:(hh + 1) * GDN_DV])
    kb = [kn[h].astype(BF) for h in heads]
    kk = [_nt(kb[h], kb[h]) for h in heads]
    qk = [_nt(qn[h].astype(BF), kb[h]) for h in heads]
    decay = [jnp.where(lower, jnp.exp(jnp.where(lower, gc[h] - gr[h], 0.0)), 0.0) for h in heads]
    nm = [jnp.where(strict, beta_c[h] * kk[h] * decay[h], 0.0) for h in heads]
    attn_b = [(qk[h] * decay[h]).astype(BF) for h in heads]
    eg = [jnp.exp(gc[h]) for h in heads]
    rhs = [jnp.concatenate([vh[h] * beta_c[h], kn[h] * (beta_c[h] * eg[h])], axis=1) for h in heads]

    nb = [nm[h].astype(BF) for h in heads]
    m = [_mm(nb[h], nb[h]) for h in heads]
    dm = [-nm[h] for h in heads]
    for s in range(n_factors):
        mb = [m[h].astype(BF) for h in heads]
        if s < n_factors - 1:
            prod = [_mm(jnp.concatenate([mb[h], dm[h].astype(BF)], axis=0), mb[h]) for h in heads]
            dm = [dm[h] + m[h] + prod[h][tm:] for h in heads]
            m = [prod[h][:tm] for h in heads]
        else:
            dm = [dm[h] + m[h] + _mm(dm[h].astype(BF), mb[h]) for h in heads]
    db = [dm[h].astype(BF) for h in heads]
    y0 = [rhs[h] + _mm(db[h], rhs[h].astype(BF)) for h in heads]
    res = []
    for h in heads:
        yh = y0[h].astype(BF)
        yl = (y0[h] - yh.astype(F32)).astype(BF)
        nl = (nm[h] - nb[h].astype(F32)).astype(BF)
        p1 = _mm(jnp.concatenate([nb[h], nl], axis=0), yh)
        res.append(rhs[h] - y0[h] - (p1[:tm] + p1[tm:] + _mm(nb[h], yl)))
    y = [y0[h] + res[h] + _mm(db[h], res[h].astype(BF)) for h in heads]

    u = [y[h][:, :GDN_DV] for h in heads]
    wb = [y[h][:, GDN_DV:].astype(BF) for h in heads]
    qdb = [(qn[h] * eg[h]).astype(BF) for h in heads]
    kdb = [(kn[h] * jnp.exp(glast_c[h] - gc[h])).astype(BF) for h in heads]
    gtot = [jnp.exp(glast_c[h]) for h in heads]

    st = [s_scr[h] for h in heads]
    outs = [[] for _ in heads]
    for ci in range(n_chunks):
        r0 = ci * L
        ws = [_mm(jnp.concatenate([wb[h][r0:r0 + L], qdb[h][r0:r0 + L]], axis=0), st[h].astype(BF))
              for h in heads]
        vn = [(u[h][r0:r0 + L] - ws[h][:L]).astype(BF) for h in heads]
        for h in heads:
            vnew_scr[h, r0:r0 + L, :] = vn[h]
        for h in heads:
            outs[h].append(ws[h][L:] + _mm(attn_b[h][r0:r0 + L, :], vnew_scr[h]))
        st = [st[h] * gtot[h][r0:r0 + 1, :] + _tn(kdb[h][r0:r0 + L], vn[h]) for h in heads]
    for h in heads:
        s_scr[h] = st[h]
        o = outs[h][0] if n_chunks == 1 else jnp.concatenate(outs[h], axis=0)
        zz = z_ref[:, h * GDN_DV:(h + 1) * GDN_DV].astype(F32)
        o_ref[:, h * GDN_DV:(h + 1) * GDN_DV] = (_rms(o, gout) * (zz * _sigmoid(zz))).astype(BF)

    @pl.when(t == nt - 1)
    def _():
        sout_ref[0] = s_scr[...]


def _gdn(qkv, conv_w, conv_past8, ba, bat, alog_l, dtb_l, alog_c, dtb_c, z, s0, gout, nb):
    t = qkv.shape[0]
    s_len = t // nb
    L = min(s_len, CHUNK)
    tm = _tile(s_len, GDN_TM, mult=L)
    nt = s_len // tm
    hg = GDN_HG
    ng = GDN_HEADS // hg
    hw = hg * GDN_DK
    body = functools.partial(_gdn_body, tm=tm, L=L, hg=hg)
    rows = lambda off: (lambda b, h, i: (b * nt + i, off + h))
    wcol = lambda off: (lambda b, h, i: (0, off + h))
    pcol = lambda off: (lambda b, h, i: (b, off + h))
    const = lambda b, h, i: (0, 0)
    return pl.pallas_call(
        body,
        grid=(nb, ng, nt),
        in_specs=[
            pl.BlockSpec((tm, hw), rows(0)),
            pl.BlockSpec((tm, hw), rows(ng)),
            pl.BlockSpec((tm, hw), rows(2 * ng)),
            pl.BlockSpec((CONV_W, hw), wcol(0)),
            pl.BlockSpec((CONV_W, hw), wcol(ng)),
            pl.BlockSpec((CONV_W, hw), wcol(2 * ng)),
            pl.BlockSpec((HALO, hw), pcol(0)),
            pl.BlockSpec((HALO, hw), pcol(ng)),
            pl.BlockSpec((HALO, hw), pcol(2 * ng)),
            pl.BlockSpec((tm, LANE), lambda b, h, i: (b * nt + i, 0)),
            pl.BlockSpec((1, 16, tm), lambda b, h, i: (b, 0, i)),
            pl.BlockSpec((1, LANE), const),
            pl.BlockSpec((1, LANE), const),
            pl.BlockSpec((16, 1), const),
            pl.BlockSpec((16, 1), const),
            pl.BlockSpec((tm, hw), rows(0)),
            pl.BlockSpec((1, hg, GDN_DK, GDN_DV), lambda b, h, i: (b, h, 0, 0)),
            pl.BlockSpec((1, GDN_DV), const),
        ],
        out_specs=[
            pl.BlockSpec((tm, hw), rows(0)),
            pl.BlockSpec((1, hg, GDN_DK, GDN_DV), lambda b, h, i: (b, h, 0, 0)),
        ],
        out_shape=[
            jax.ShapeDtypeStruct((t, GDN_HEADS * GDN_DV), BF),
            jax.ShapeDtypeStruct((nb, GDN_HEADS, GDN_DK, GDN_DV), F32),
        ],
        scratch_shapes=[
            pltpu.VMEM((hg, GDN_DK, GDN_DV), F32),
            pltpu.VMEM((3, HALO, hw), F32),
            pltpu.VMEM((HALO + tm, hw), F32),
            pltpu.VMEM((hg, tm, GDN_DV), BF),
        ],
        compiler_params=_params(("arbitrary", "arbitrary", "arbitrary")),
        name="gdn",
    )(qkv, qkv, qkv, conv_w, conv_w, conv_w, conv_past8, conv_past8, conv_past8,
      ba, bat, alog_l, dtb_l, alog_c, dtb_c, z, s0, gout)


def _mix_mem_body(x_ref, a1_ref, a2_ref, wo1_ref, wo2_ref, gq_ref, wmq_ref, gmq_ref,
                  mk_ref, mv_ref, wmo_ref, o_ref):
    h1 = x_ref[...] + _mm(a1_ref[...], wo1_ref[...]) + _mm(a2_ref[...], wo2_ref[...])
    hn = _rms(h1, gq_ref[...]).astype(BF)
    q = _mm(hn, wmq_ref[...])
    gmq = gmq_ref[...]
    mk = mk_ref[0]
    mv = mv_ref[0]
    outs = []
    for h in range(MEM_HEADS):
        sl = slice(h * MEM_DIM, (h + 1) * MEM_DIM)
        qh = _rms(q[:, sl], gmq).astype(BF)
        s = _nt(qh, mk[:, sl]) * (MEM_DIM ** -0.5)
        p = jnp.exp(s - jnp.max(s, axis=-1, keepdims=True))
        p = p / jnp.sum(p, axis=-1, keepdims=True)
        outs.append(_mm(p.astype(BF), mv[:, sl]))
    o = jnp.concatenate(outs, axis=1).astype(BF)
    o_ref[...] = h1 + _mm(o, wmo_ref[...])


def _mix_mem(x, a1, a2, wo1, wo2, gq, wmq, gmq, mk, mv, wmo, nb):
    t, d = x.shape
    s_len = t // nb
    tm = _tile(s_len, MIX_TM)
    nt = s_len // tm
    n_mem = mk.shape[1]
    mw = MEM_HEADS * MEM_DIM
    row = lambda i: (i, 0)
    const = lambda i: (0, 0)
    return pl.pallas_call(
        _mix_mem_body,
        grid=(t // tm,),
        in_specs=[
            pl.BlockSpec((tm, d), row),
            pl.BlockSpec((tm, a1.shape[1]), row),
            pl.BlockSpec((tm, a2.shape[1]), row),
            pl.BlockSpec(wo1.shape, const),
            pl.BlockSpec(wo2.shape, const),
            pl.BlockSpec((1, d), const),
            pl.BlockSpec(wmq.shape, const),
            pl.BlockSpec((1, MEM_DIM), const),
            pl.BlockSpec((1, n_mem, mw), lambda i: (i // nt, 0, 0)),
            pl.BlockSpec((1, n_mem, mw), lambda i: (i // nt, 0, 0)),
            pl.BlockSpec(wmo.shape, const),
        ],
        out_specs=pl.BlockSpec((tm, d), row),
        out_shape=jax.ShapeDtypeStruct((t, d), F32),
        compiler_params=_params(("arbitrary",)),
        name="mix_mem",
    )(x, a1, a2, wo1, wo2, gq, wmq, gmq, mk, mv, wmo)


def _ffn_body(x_ref, g_ref, w1_ref, w2_ref, o_ref, hn_scr):
    f = pl.program_id(1)

    @pl.when(f == 0)
    def _():
        x = x_ref[...]
        hn_scr[...] = _rms(x, g_ref[...]).astype(BF)
        o_ref[...] = x

    a = jnp.maximum(_mm(hn_scr[...], w1_ref[...]), 0.0)
    o_ref[...] += _mm((a * a).astype(BF), w2_ref[...])


def _ffn(x, g, w1, w2):
    t, d = x.shape
    dff = w1.shape[1]
    tm = _tile(t, FFN_TM)
    tf = _tile(dff, FFN_TF, mult=LANE)
    return pl.pallas_call(
        _ffn_body,
        grid=(t // tm, dff // tf),
        in_specs=[
            pl.BlockSpec((tm, d), lambda i, f: (i, 0)),
            pl.BlockSpec((1, d), lambda i, f: (0, 0)),
            pl.BlockSpec((d, tf), lambda i, f: (0, f)),
            pl.BlockSpec((tf, d), lambda i, f: (f, 0)),
        ],
        out_specs=pl.BlockSpec((tm, d), lambda i, f: (i, 0)),
        out_shape=jax.ShapeDtypeStruct((t, d), F32),
        scratch_shapes=[pltpu.VMEM((tm, d), BF)],
        compiler_params=_params(("arbitrary", "arbitrary")),
        name="ffn",
    )(x, g, w1, w2)


def _mem_kv_body(m_ref, g_ref, wk_ref, wv_ref, gk_ref, k_ref, v_ref):
    mn = _rms(m_ref[...], g_ref[...]).astype(BF)
    k = _mm(mn, wk_ref[...])
    gk = gk_ref[...]
    for h in range(MEM_HEADS):
        sl = slice(h * MEM_DIM, (h + 1) * MEM_DIM)
        k_ref[:, sl] = _rms(k[:, sl], gk)
    v_ref[...] = _mm(mn, wv_ref[...])


def _mem_kv(mem, g, wk, wv, gk):
    t, d = mem.shape
    tm = _tile(t, 256)
    mw = MEM_HEADS * MEM_DIM
    const = lambda i: (0, 0)
    return pl.pallas_call(
        _mem_kv_body,
        grid=(t // tm,),
        in_specs=[
            pl.BlockSpec((tm, d), lambda i: (i, 0)),
            pl.BlockSpec((1, d), const),
            pl.BlockSpec(wk.shape, const),
            pl.BlockSpec(wv.shape, const),
            pl.BlockSpec((1, MEM_DIM), const),
        ],
        out_specs=[pl.BlockSpec((tm, mw), lambda i: (i, 0)), pl.BlockSpec((tm, mw), lambda i: (i, 0))],
        out_shape=[jax.ShapeDtypeStruct((t, mw), F32), jax.ShapeDtypeStruct((t, mw), F32)],
        compiler_params=_params(("arbitrary",)),
        name="mem_kv",
    )(mem, g, wk, wv, gk)


def _rope_tables(pos):
    half = QK_ROPE // 2
    inv_freq = ROPE_THETA ** (-jnp.arange(half, dtype=F32) / half)
    ang = pos.astype(F32)[:, None] * inv_freq[None, :]
    cos, sin = jnp.cos(ang), jnp.sin(ang)
    zeros = jnp.zeros((pos.shape[0], LANE - QK_ROPE), F32)
    return (jnp.concatenate([cos, cos, zeros], axis=1), jnp.concatenate([-sin, sin, zeros], axis=1))


def _pad_lanes(v, n=LANE):
    return jnp.pad(v, ((0, 0), (0, n - v.shape[1])))


def _prep_weights(w_in, g_cq, w_uq, g_ckv, w_ukv, g_q, g_k, w_o, w_mq, w_mo, w_ff1, w_ff2, w_mk, w_mv):
    d = w_in.shape[0]
    n_conv = 2 * GDN_HEADS * GDN_DK + GDN_HEADS * GDN_DV
    n_z = GDN_HEADS * GDN_DV
    o = 0
    w_cq = w_in[:, o:o + Q_RANK]; o += Q_RANK
    w_ckv = w_in[:, o:o + KV_RANK]; o += KV_RANK
    w_kpe = w_in[:, o:o + QK_ROPE]; o += QK_ROPE
    w_qkv = w_in[:, o:o + n_conv]; o += n_conv
    w_z = w_in[:, o:o + n_z]; o += n_z
    w_b = w_in[:, o:o + GDN_HEADS]; o += GDN_HEADS
    w_a = w_in[:, o:o + GDN_HEADS]
    half = QK_ROPE // 2
    swap = jnp.concatenate([jnp.arange(half, QK_ROPE), jnp.arange(0, half)])
    z64 = jnp.zeros((d, LANE - QK_ROPE), w_in.dtype)
    misc = jnp.concatenate([w_kpe, z64, w_kpe[:, swap], z64, w_b, w_a], axis=1)
    misc = _pad_lanes(misc, COL_TILE)
    w_in_p = jnp.concatenate([w_cq, w_ckv, w_qkv, w_z, misc], axis=1).astype(BF)
    wbat = jnp.concatenate([w_b, w_a], axis=1).T.astype(BF)

    r = w_uq.shape[0]
    wq3 = w_uq.reshape(r, MLA_HEADS, QK_HEAD)
    zq = jnp.zeros((r, MLA_HEADS, LANE - QK_ROPE), w_uq.dtype)
    w1 = jnp.concatenate([wq3, zq], axis=2).reshape(r, MLA_HEADS * QPAD).astype(BF)
    w2 = jnp.concatenate([wq3[:, :, QK_NOPE:][:, :, swap], zq], axis=2).reshape(r, MLA_HEADS * LANE).astype(BF)

    dmla = MLA_HEADS * V_DIM
    return dict(
        n_conv=n_conv, n_z=n_z, w_in_p=w_in_p, wbat=wbat, w1=w1, w2=w2,
        gcq=g_cq[None, :], gckv=g_ckv[None, :],
        gq_n=g_q[None, :QK_NOPE], gq_r=_pad_lanes(g_q[None, QK_NOPE:]),
        gk_n=g_k[None, :QK_NOPE], gk_r=_pad_lanes(g_k[None, QK_NOPE:]),
        w_ukv=w_ukv.astype(BF), wo1=w_o[:dmla].astype(BF), wo2=w_o[dmla:].astype(BF),
        w_mq=w_mq.astype(BF), w_mo=w_mo.astype(BF), w_ff1=w_ff1.astype(BF), w_ff2=w_ff2.astype(BF),
        w_mk=w_mk.astype(BF), w_mv=w_mv.astype(BF),
    )


def _gate_params(a_log, dt_bias):
    z8 = jnp.zeros((GDN_HEADS,), F32)
    al = jnp.concatenate([z8, a_log.astype(F32)])
    db = jnp.concatenate([z8, dt_bias.astype(F32)])
    return _pad_lanes(al[None, :]), _pad_lanes(db[None, :]), al[:, None], db[:, None]


def _layer(x, pos, past, conv_past, s0, mem_k, mem_v, wp, lw):
    nb, s_len, d = x.shape
    t = nb * s_len
    xf = x.reshape(t, d)
    ctab, stab = _rope_tables(pos)
    if ctab.shape[0] % 8 != 0 or (s_len < IN_TM and nb > 1):
        ctab, stab = jnp.tile(ctab, (nb, 1)), jnp.tile(stab, (nb, 1))

    cqn, ckv, kpe_pad, qkv, z, ba, bat = _in_proj(
        xf, lw["g_norm_mix"], wp["w_in_p"], wp["wbat"], wp["gcq"], wp["gckv"], ctab, stab,
        wp["n_conv"], wp["n_z"])
    q = _q_proj(cqn, wp["w1"], wp["w2"], ctab, stab, wp["gq_n"], wp["gq_r"])

    if past is None:
        ckv_all, kpe_all, q_off, t_valid = ckv, kpe_pad, 0, s_len
    else:
        ckv_past, kpe_past = past
        p_len = ckv_past.shape[1]
        t_valid = p_len + s_len
        t_pad = -(-t_valid // ATTN_TK) * ATTN_TK
        ckv_all = jnp.concatenate(
            [ckv_past.astype(F32), ckv.reshape(nb, s_len, KV_RANK),
             jnp.zeros((nb, t_pad - t_valid, KV_RANK), F32)], axis=1).reshape(nb * t_pad, KV_RANK)
        kpe_all = jnp.concatenate(
            [jnp.pad(kpe_past.astype(F32), ((0, 0), (0, 0), (0, LANE - QK_ROPE))),
             kpe_pad.reshape(nb, s_len, LANE),
             jnp.zeros((nb, t_pad - t_valid, LANE), F32)], axis=1).reshape(nb * t_pad, LANE)
        q_off = p_len
    k, v = _kv_proj(ckv_all, kpe_all, wp["w_ukv"], wp["gk_n"], wp["gk_r"])
    o_mla = _attention(q, k, v, lw["g_mla_out"], nb, q_off, t_valid)

    conv_past8 = jnp.pad(conv_past.astype(F32), ((0, 0), (HALO - (CONV_W - 1), 0), (0, 0)))
    conv_past8 = conv_past8.reshape(nb * HALO, -1)
    bat3 = bat.reshape(16, nb, s_len).transpose(1, 0, 2)
    alog_l, dtb_l, alog_c, dtb_c = _gate_params(lw["a_log"], lw["dt_bias"])
    o_gdn, s_new = _gdn(qkv, lw["conv_w"], conv_past8, ba, bat3, alog_l, dtb_l, alog_c, dtb_c,
                        z, s0.astype(F32), lw["g_gdn_out"], nb)

    h2 = _mix_mem(xf, o_mla, o_gdn, wp["wo1"], wp["wo2"], lw["g_norm_mem_q"], wp["w_mq"],
                  lw["g_mq"], mem_k, mem_v, wp["w_mo"], nb)
    y = _ffn(h2, lw["g_norm_ffn"], wp["w_ff1"], wp["w_ff2"])

    conv_in_tail = jnp.concatenate([conv_past.astype(F32), qkv.reshape(nb, s_len, -1)[:, -(CONV_W - 1):]], axis=1)
    conv_new = conv_in_tail[:, -(CONV_W - 1):]
    return (y.reshape(nb, s_len, d), ckv.reshape(nb, s_len, KV_RANK),
            kpe_pad[:, :QK_ROPE].reshape(nb, s_len, QK_ROPE), conv_new, s_new)


def kernel(x_prompt, x_sample, mem_prompt, cache_mla_ckv, cache_mla_kpe, cache_gdn_conv, state_gdn, cache_mem_k, cache_mem_v, g_norm_mix, w_in, g_cq, w_uq, g_ckv, w_ukv, g_q_mla, g_k_mla, g_mla_out, conv_w, a_log, dt_bias, g_gdn_out, w_o, g_norm_mem_q, g_norm_mem_kv, w_mq, w_mk, w_mv, g_mq, g_mk, w_mo, g_norm_ffn, w_ff1, w_ff2):
    depth = w_in.shape[0]
    nbp, sp, d = x_prompt.shape
    nbs, ss, _ = x_sample.shape
    n_mem = mem_prompt.shape[1]
    mw = MEM_HEADS * MEM_DIM
    n_conv = 2 * GDN_HEADS * GDN_DK + GDN_HEADS * GDN_DV
    pos_p = jnp.arange(sp)
    pos_s = cache_mla_ckv.shape[2] + jnp.arange(ss)
    zeros_conv = jnp.zeros((nbp, CONV_W - 1, n_conv), F32)
    zeros_state = jnp.zeros((nbp, GDN_HEADS, GDN_DK, GDN_DV), F32)
    hp, hs = x_prompt, x_sample
    outs_p = [[] for _ in range(6)]
    outs_s = [[] for _ in range(4)]
    for l in range(depth):
        wp = _prep_weights(w_in[l], g_cq[l], w_uq[l], g_ckv[l], w_ukv[l], g_q_mla[l], g_k_mla[l], w_o[l],
                           w_mq[l], w_mo[l], w_ff1[l], w_ff2[l], w_mk[l], w_mv[l])
        lw = dict(g_norm_mix=g_norm_mix[l][None, :], g_mla_out=g_mla_out[l][None, :], conv_w=conv_w[l],
                  a_log=a_log[l], dt_bias=dt_bias[l], g_gdn_out=g_gdn_out[l][None, :],
                  g_norm_mem_q=g_norm_mem_q[l][None, :], g_mq=g_mq[l][None, :],
                  g_norm_ffn=g_norm_ffn[l][None, :])
        mk, mv = _mem_kv(mem_prompt.reshape(nbp * n_mem, d), g_norm_mem_kv[l][None, :], wp["w_mk"], wp["w_mv"],
                         g_mk[l][None, :])
        mk3, mv3 = mk.reshape(nbp, n_mem, mw), mv.reshape(nbp, n_mem, mw)
        hp, c1, c2, c3, c4 = _layer(hp, pos_p, None, zeros_conv, zeros_state, mk3.astype(BF), mv3.astype(BF), wp, lw)
        for lst, val in zip(outs_p, (c1, c2, c3, c4, mk3.reshape(nbp, n_mem, MEM_HEADS, MEM_DIM),
                                     mv3.reshape(nbp, n_mem, MEM_HEADS, MEM_DIM))):
            lst.append(val)
        hs, d1, d2, d3, d4 = _layer(hs, pos_s, (cache_mla_ckv[l], cache_mla_kpe[l]), cache_gdn_conv[l], state_gdn[l],
                                    cache_mem_k[l].reshape(nbs, n_mem, mw).astype(BF),
                                    cache_mem_v[l].reshape(nbs, n_mem, mw).astype(BF), wp, lw)
        for lst, val in zip(outs_s, (d1, d2, d3, d4)):
            lst.append(val)
    return (hp, hs, *(jnp.stack(v) for v in outs_p), *(jnp.stack(v) for v in outs_s))
```

```python
import functools
import math

import jax
import jax.numpy as jnp
from jax import lax
from jax.experimental import pallas as pl
from jax.experimental.pallas import tpu as pltpu

F32 = jnp.float32
BF = jnp.bfloat16

EPS = 1e-6
CHUNK = 64
ROPE_THETA = 10000.0
MLA_HEADS = 8
QK_NOPE = 128
QK_ROPE = 64
QK_HEAD = QK_NOPE + QK_ROPE
V_DIM = 128
Q_RANK = 512
KV_RANK = 512
GDN_HEADS = 8
GDN_DK = 128
GDN_DV = 128
CONV_W = 4
MEM_HEADS = 4
MEM_DIM = 128
LANE = 128
QPAD = 2 * LANE
COL_TILE = 512
HALO = 8
NEG_BIG = -1e30
QSCALE = (QK_HEAD ** -0.5) * math.log2(math.e)

VMEM_LIMIT = 56 * 1024 * 1024

IN_TM = 1024
PROJ_TM = 512
ATTN_TQ = 1024
ATTN_TK = 1024
ATTN_WIDE = 2
GDN_TM = 256
GDN_WAVES = 2
MIX_TM = 512
FFN_TM = 1024
FFN_TF = 512


def _tile(n, pref, mult=8):
    if n <= pref:
        return n
    t = (pref // mult) * mult
    while t >= mult:
        if n % t == 0:
            return t
        t -= mult
    return n


def _nt(a, b):
    return lax.dot_general(a, b, (((1,), (1,)), ((), ())), preferred_element_type=F32)


def _mm(a, b):
    return jnp.dot(a, b, preferred_element_type=F32)


def _sigmoid(x):
    return 1.0 / (1.0 + jnp.exp(-x))


def _softplus(x):
    return jnp.maximum(x, 0.0) + jnp.log(1.0 + jnp.exp(-jnp.abs(x)))


def _rms(x, g, n=None):
    n = x.shape[-1] if n is None else n
    ms = jnp.sum(x * x, axis=-1, keepdims=True) * (1.0 / n)
    return (x * lax.rsqrt(ms + EPS)) * g


def _split3(x):
    hi = x.astype(BF)
    r1 = x - hi.astype(F32)
    mid = r1.astype(BF)
    lo = (r1 - mid.astype(F32)).astype(BF)
    return hi, mid, lo


def _interleave(lists):
    items = []
    for li, lst in enumerate(lists):
        items += [((i + 0.5) / len(lst), li, f) for i, f in enumerate(lst)]
    return [f for _, _, f in sorted(items, key=lambda it: (it[0], it[1]))]


def _params(sem):
    return pltpu.CompilerParams(dimension_semantics=sem, vmem_limit_bytes=VMEM_LIMIT)


def _in_proj_body(x_ref, g_ref, w_ref, wbat_ref, gcq_ref, gckv_ref, ctab_ref, stab_ref,
                  cqn_ref, ckv_ref, kpe_ref, qkv_ref, z_ref, ba_ref, bat_ref, xn_scr, *, nqkv, nz):
    j = pl.program_id(1)

    @pl.when(j == 0)
    def _():
        xn_scr[...] = _rms(x_ref[...], g_ref[...]).astype(BF)

    acc = _mm(xn_scr[...], w_ref[...])

    @pl.when(j == 0)
    def _():
        cqn_ref[...] = _rms(acc, gcq_ref[...]).astype(BF)

    @pl.when(j == 1)
    def _():
        ckv_ref[...] = _rms(acc, gckv_ref[...])

    @pl.when((j >= 2) & (j < 2 + nqkv))
    def _():
        qkv_ref[...] = acc

    @pl.when((j >= 2 + nqkv) & (j < 2 + nqkv + nz))
    def _():
        z_ref[...] = acc.astype(BF)

    @pl.when(j == 2 + nqkv + nz)
    def _():
        kpe_ref[...] = acc[:, 0:LANE] * ctab_ref[...] + acc[:, LANE:2 * LANE] * stab_ref[...]
        ba_ref[...] = acc[:, 2 * LANE:3 * LANE]
        bat_ref[...] = _nt(wbat_ref[...], xn_scr[...])


def _in_proj(x, g, w_p, wbat, gcq, gckv, ctab, stab, n_conv, n_z):
    t, d = x.shape
    tm = _tile(min(t, ctab.shape[0]), IN_TM)
    ntab = ctab.shape[0] // tm
    nqkv = n_conv // COL_TILE
    nz = n_z // COL_TILE
    ncol = 2 + nqkv + nz + 1
    assert w_p.shape[1] == ncol * COL_TILE
    row = lambda i, j: (i, 0)
    const = lambda i, j: (0, 0)
    body = functools.partial(_in_proj_body, nqkv=nqkv, nz=nz)
    return pl.pallas_call(
        body,
        grid=(t // tm, ncol),
        in_specs=[
            pl.BlockSpec((tm, d), row),
            pl.BlockSpec((1, d), const),
            pl.BlockSpec((d, COL_TILE), lambda i, j: (0, j)),
            pl.BlockSpec((16, d), const),
            pl.BlockSpec((1, Q_RANK), const),
            pl.BlockSpec((1, KV_RANK), const),
            pl.BlockSpec((tm, LANE), lambda i, j: (i % ntab, 0)),
            pl.BlockSpec((tm, LANE), lambda i, j: (i % ntab, 0)),
        ],
        out_specs=[
            pl.BlockSpec((tm, Q_RANK), row),
            pl.BlockSpec((tm, KV_RANK), row),
            pl.BlockSpec((tm, LANE), row),
            pl.BlockSpec((tm, COL_TILE), lambda i, j: (i, jnp.clip(j - 2, 0, nqkv - 1))),
            pl.BlockSpec((tm, COL_TILE), lambda i, j: (i, jnp.clip(j - 2 - nqkv, 0, nz - 1))),
            pl.BlockSpec((tm, LANE), row),
            pl.BlockSpec((16, tm), lambda i, j: (0, i)),
        ],
        out_shape=[
            jax.ShapeDtypeStruct((t, Q_RANK), BF),
            jax.ShapeDtypeStruct((t, KV_RANK), F32),
            jax.ShapeDtypeStruct((t, LANE), F32),
            jax.ShapeDtypeStruct((t, n_conv), F32),
            jax.ShapeDtypeStruct((t, n_z), BF),
            jax.ShapeDtypeStruct((t, LANE), F32),
            jax.ShapeDtypeStruct((16, t), F32),
        ],
        scratch_shapes=[pltpu.VMEM((tm, d), BF)],
        compiler_params=_params(("arbitrary", "arbitrary")),
        name="in_proj",
    )(x, g, w_p, wbat, gcq, gckv, ctab, stab)


def _q_proj_body(c_ref, w1_ref, w2_ref, ctab_ref, stab_ref, gn_ref, gr_ref, q_ref):
    c = c_ref[...]
    qf = _mm(c, w1_ref[...])
    qs = _mm(c, w2_ref[...])
    ct = ctab_ref[...]
    st = stab_ref[...]
    gn = gn_ref[...]
    gr = gr_ref[...]
    for h in range(MLA_HEADS):
        nope = qf[:, h * QPAD:h * QPAD + LANE]
        rot = qf[:, h * QPAD + LANE:(h + 1) * QPAD] * ct + qs[:, h * LANE:(h + 1) * LANE] * st
        ss = jnp.sum(nope * nope, axis=-1, keepdims=True) + jnp.sum(rot * rot, axis=-1, keepdims=True)
        rs = lax.rsqrt(ss * (1.0 / QK_HEAD) + EPS) * QSCALE
        q_ref[:, h * QPAD:h * QPAD + LANE] = (nope * rs * gn).astype(BF)
        q_ref[:, h * QPAD + LANE:(h + 1) * QPAD] = (rot * rs * gr).astype(BF)


def _q_proj(cqn, w1, w2, ctab, stab, gn, gr):
    t = cqn.shape[0]
    tm = _tile(min(t, ctab.shape[0]), PROJ_TM)
    ntab = ctab.shape[0] // tm
    const = lambda i: (0, 0)
    return pl.pallas_call(
        _q_proj_body,
        grid=(t // tm,),
        in_specs=[
            pl.BlockSpec((tm, Q_RANK), lambda i: (i, 0)),
            pl.BlockSpec(w1.shape, const),
            pl.BlockSpec(w2.shape, const),
            pl.BlockSpec((tm, LANE), lambda i: (i % ntab, 0)),
            pl.BlockSpec((tm, LANE), lambda i: (i % ntab, 0)),
            pl.BlockSpec((1, LANE), const),
            pl.BlockSpec((1, LANE), const),
        ],
        out_specs=pl.BlockSpec((tm, MLA_HEADS * QPAD), lambda i: (i, 0)),
        out_shape=jax.ShapeDtypeStruct((t, MLA_HEADS * QPAD), BF),
        compiler_params=_params(("arbitrary",)),
        name="q_proj",
    )(cqn, w1, w2, ctab, stab, gn, gr)


def _kv_proj_body(c_ref, kpe_ref, w_ref, gn_ref, gr_ref, k_ref, v_ref):
    kv = _mm(c_ref[...].astype(BF), w_ref[...])
    kp = kpe_ref[...]
    kps = jnp.sum(kp * kp, axis=-1, keepdims=True)
    gn = gn_ref[...]
    gr = gr_ref[...]
    for h in range(MLA_HEADS):
        kn = kv[:, h * 2 * LANE:h * 2 * LANE + LANE]
        rs = lax.rsqrt((jnp.sum(kn * kn, axis=-1, keepdims=True) + kps) * (1.0 / QK_HEAD) + EPS)
        k_ref[:, h * QPAD:h * QPAD + LANE] = (kn * rs * gn).astype(BF)
        k_ref[:, h * QPAD + LANE:(h + 1) * QPAD] = (kp * rs * gr).astype(BF)
        v_ref[:, h * V_DIM:(h + 1) * V_DIM] = kv[:, h * 2 * LANE + LANE:(h + 1) * 2 * LANE].astype(BF)


def _kv_proj(ckv, kpe_pad, w, gn, gr):
    t = ckv.shape[0]
    tm = _tile(t, PROJ_TM)
    const = lambda i: (0, 0)
    return pl.pallas_call(
        _kv_proj_body,
        grid=(t // tm,),
        in_specs=[
            pl.BlockSpec((tm, KV_RANK), lambda i: (i, 0)),
            pl.BlockSpec((tm, LANE), lambda i: (i, 0)),
            pl.BlockSpec(w.shape, const),
            pl.BlockSpec((1, LANE), const),
            pl.BlockSpec((1, LANE), const),
        ],
        out_specs=[
            pl.BlockSpec((tm, MLA_HEADS * QPAD), lambda i: (i, 0)),
            pl.BlockSpec((tm, MLA_HEADS * V_DIM), lambda i: (i, 0)),
        ],
        out_shape=[
            jax.ShapeDtypeStruct((t, MLA_HEADS * QPAD), BF),
            jax.ShapeDtypeStruct((t, MLA_HEADS * V_DIM), BF),
        ],
        compiler_params=_params(("arbitrary",)),
        name="kv_proj",
    )(ckv, kpe_pad, w, gn, gr)


def _attn_body(q_ref, k_ref, v_ref, g_ref, o_ref, m_scr, l_scr, acc_scr, *, tq, tk, wide, q_off, t_valid):
    qi = pl.program_id(2)
    qpos0 = q_off + qi * tq
    n_full = jnp.minimum((qpos0 // CHUNK * CHUNK + CHUNK) // tk, t_valid // tk)
    hi = jnp.minimum((qpos0 + tq - 1) // CHUNK * CHUNK + CHUNK, t_valid)
    n_total = (hi + tk - 1) // tk

    m_scr[...] = jnp.full(m_scr.shape, NEG_BIG, F32)
    l_scr[...] = jnp.zeros(l_scr.shape, F32)
    acc_scr[...] = jnp.zeros(acc_scr.shape, F32)
    def step(kc, width, masked):
        k0 = pl.multiple_of(kc * tk, tk)
        s = _nt(q_ref[...], k_ref[pl.ds(k0, width), :])
        if masked:
            qpos = qpos0 + lax.broadcasted_iota(jnp.int32, (tq, 1), 0)
            last = jnp.minimum(qpos | (CHUNK - 1), t_valid - 1) - k0
            s = jnp.where(lax.broadcasted_iota(jnp.int32, (tq, width), 1) <= last, s, NEG_BIG)
        m_prev = m_scr[...]
        m_new = jnp.maximum(m_prev, jnp.max(s, axis=-1, keepdims=True))
        alpha = jnp.exp2(m_prev - m_new)
        ps = [jnp.exp2(s[:, j * LANE:(j + 1) * LANE] - m_new) for j in range(width // LANE)]
        psum = ps[0]
        for pj in ps[1:]:
            psum = psum + pj
        l_scr[...] = alpha * l_scr[...] + psum
        p = jnp.concatenate(ps, axis=1).astype(BF) if len(ps) > 1 else ps[0].astype(BF)
        acc_scr[...] = acc_scr[...] * alpha + _mm(p, v_ref[pl.ds(k0, width), :])
        m_scr[...] = m_new

    def loop(lo, hi, fn):
        lax.fori_loop(lo, hi, lambda i, c: (fn(i), c)[1], 0)

    n_wide = n_full // wide
    loop(0, n_wide, lambda i: step(i * wide, wide * tk, False))
    loop(n_wide * wide, n_full, lambda kc: step(kc, tk, False))
    loop(n_full, n_total, lambda kc: step(kc, tk, True))
    o = acc_scr[...] / jnp.sum(l_scr[...], axis=-1, keepdims=True)
    o_ref[...] = _rms(o, g_ref[...]).astype(BF)


def _attention(q, k, v, g_out, nb, q_off, t_valid):
    tq_total = q.shape[0] // nb
    tk_total = k.shape[0] // nb
    tq = _tile(tq_total, ATTN_TQ)
    tk = _tile(tk_total, ATTN_TK)
    nq = tq_total // tq
    body = functools.partial(_attn_body, tq=tq, tk=tk, wide=ATTN_WIDE, q_off=q_off, t_valid=t_valid)
    return pl.pallas_call(
        body,
        grid=(nb, MLA_HEADS, nq),
        in_specs=[
            pl.BlockSpec((tq, QPAD), lambda b, h, i: (b * nq + i, h)),
            pl.BlockSpec((tk_total, QPAD), lambda b, h, i: (b, h)),
            pl.BlockSpec((tk_total, V_DIM), lambda b, h, i: (b, h)),
            pl.BlockSpec((1, V_DIM), lambda b, h, i: (0, 0)),
        ],
        out_specs=pl.BlockSpec((tq, V_DIM), lambda b, h, i: (b * nq + i, h)),
        out_shape=jax.ShapeDtypeStruct((q.shape[0], MLA_HEADS * V_DIM), BF),
        scratch_shapes=[
            pltpu.VMEM((tq, LANE), F32),
            pltpu.VMEM((tq, LANE), F32),
            pltpu.VMEM((tq, V_DIM), F32),
        ],
        compiler_params=_params(("arbitrary", "arbitrary", "arbitrary")),
        name="mla_attention",
    )(q, k, v, g_out)


def _gdn_body(q_ref, k_ref, v_ref, wq_ref, wk_ref, wv_ref, pq_ref, pk_ref, pv_ref,
              ba_ref, bat_ref, alog_l_ref, dtb_l_ref, alog_c_ref, dtb_c_ref,
              z_ref, s0_ref, gout_ref,
              o_ref, sout_ref,
              s_scr, halo_scr, vnew_scr, *, tm, L, hg):
    hb = pl.program_id(1)
    t = pl.program_id(2)
    nt = pl.num_programs(2)
    n_chunks = tm // L
    n_factors = int(math.log2(L)) - 1

    @pl.when(t == 0)
    def _():
        s_scr[...] = s0_ref[0]
        halo_scr[0] = pq_ref[...]
        halo_scr[1] = pk_ref[...]
        halo_scr[2] = pv_ref[...]
        vnew_scr[...] = jnp.zeros(vnew_scr.shape, BF)

    ba = ba_ref[...]
    beta_all = _sigmoid(ba)
    g_all = -jnp.exp(alog_l_ref[...]) * _softplus(ba + dtb_l_ref[...])
    gt_all = -jnp.exp(alog_c_ref[...]) * _softplus(bat_ref[0] + dtb_c_ref[...])

    r = lax.broadcasted_iota(jnp.int32, (tm, tm), 0)
    c = lax.broadcasted_iota(jnp.int32, (tm, tm), 1)
    same = (r // L) == (c // L)
    lower = same & (c <= r)
    strict = same & (c < r)
    cs = jnp.where(lower, 1.0, 0.0).astype(BF)
    bd = jnp.where(same, 1.0, 0.0).astype(BF)
    g3 = _split3(g_all)
    gcol_all = _mm(cs, g3[0]) + _mm(cs, g3[1]) + _mm(cs, g3[2])
    glast_all = _mm(bd, g3[0]) + _mm(bd, g3[1]) + _mm(bd, g3[2])
    gt3 = _split3(gt_all)
    grow_all = _nt(gt3[0], cs) + _nt(gt3[1], cs) + _nt(gt3[2], cs)

    lane = lax.broadcasted_iota(jnp.int32, (1, LANE), 1)
    sub = lax.broadcasted_iota(jnp.int32, (16, 1), 0)
    gout = gout_ref[...]
    eye_t = jnp.where(r == c, 1.0, 0.0)
    eye_k = jnp.where(lax.broadcasted_iota(jnp.int32, (GDN_DK, GDN_DK), 0)
                      == lax.broadcasted_iota(jnp.int32, (GDN_DK, GDN_DK), 1), 1.0, 0.0).astype(BF)
    col_chunk = lax.broadcasted_iota(jnp.int32, (1, tm), 1) // L

    def make_wave(heads):
        c0, c1 = heads[0] * GDN_DK, (heads[-1] + 1) * GDN_DK
        loc = lambda h: slice((h - heads[0]) * GDN_DK, (h - heads[0] + 1) * GDN_DK)
        w = {}

        def conv(idx, x_ref, w_ref):
            xs = jnp.concatenate([halo_scr[idx, :, c0:c1], x_ref[:, c0:c1]], axis=0)
            cw = w_ref[:, c0:c1]
            y = xs[HALO - 3:HALO - 3 + tm] * cw[0:1]
            for i in range(1, CONV_W):
                y = y + xs[HALO - 3 + i:HALO - 3 + i + tm] * cw[i:i + 1]
            halo_scr[idx, :, c0:c1] = xs[tm:tm + HALO]
            return y * _sigmoid(y)

        def p_q():
            w["yq"] = conv(0, q_ref, wq_ref)

        def p_k():
            w["yk"] = conv(1, k_ref, wk_ref)

        def p_v():
            w["yv"] = conv(2, v_ref, wv_ref)

        def p_gate():
            w["beta"], w["gc"], w["glast"], w["gr"] = {}, {}, {}, {}
            for h in heads:
                hglob = hb * hg + h
                sel_g = lane == GDN_HEADS + hglob
                w["beta"][h] = jnp.sum(jnp.where(lane == hglob, beta_all, 0.0), axis=-1, keepdims=True)
                w["gc"][h] = jnp.sum(jnp.where(sel_g, gcol_all, 0.0), axis=-1, keepdims=True)
                w["glast"][h] = jnp.sum(jnp.where(sel_g, glast_all, 0.0), axis=-1, keepdims=True)
                w["gr"][h] = jnp.sum(jnp.where(sub == GDN_HEADS + hglob, grow_all, 0.0), axis=0,
                                     keepdims=True)

        def p_norm():
            w["qn"], w["kn"] = {}, {}
            for h in heads:
                qh = w["yq"][:, loc(h)]
                kh = w["yk"][:, loc(h)]
                w["qn"][h] = qh * lax.rsqrt(jnp.sum(qh * qh, axis=-1, keepdims=True) + EPS) * (GDN_DK ** -0.5)
                w["kn"][h] = kh * lax.rsqrt(jnp.sum(kh * kh, axis=-1, keepdims=True) + EPS)
            kb = {h: w["kn"][h].astype(BF) for h in heads}
            w["kk"] = {h: _nt(kb[h], kb[h]) for h in heads}
            w["qk"] = {h: _nt(w["qn"][h].astype(BF), kb[h]) for h in heads}

        def p_mat():
            beta, gc = w["beta"], w["gc"]
            decay = {h: jnp.where(lower, jnp.exp(jnp.where(lower, gc[h] - w["gr"][h], 0.0)), 0.0) for h in heads}
            w["nm"] = {h: jnp.where(strict, beta[h] * w["kk"][h] * decay[h], 0.0) for h in heads}
            w["attn"] = {h: (w["qk"][h] * decay[h]).astype(BF) for h in heads}
            w["eg"] = {h: jnp.exp(gc[h]) for h in heads}
            w["rhs"] = {h: jnp.concatenate([w["yv"][:, loc(h)] * beta[h],
                                            w["kn"][h] * (beta[h] * w["eg"][h])], axis=1) for h in heads}
            w["nb"] = {h: w["nm"][h].astype(BF) for h in heads}

        def s_first():
            w["m"] = {h: _mm(w["nb"][h], w["nb"][h]) for h in heads}
            w["T"] = {h: eye_t - w["nm"][h] for h in heads}

        def s_level(s):
            xb = {h: jnp.concatenate([w["m"][h], w["T"][h]], axis=0).astype(BF) for h in heads}
            if s < n_factors - 1:
                prod = {h: _mm(xb[h], xb[h][:tm]) for h in heads}
                w["T"] = {h: w["T"][h] + prod[h][tm:] for h in heads}
                w["m"] = {h: prod[h][:tm] for h in heads}
            else:
                w["T"] = {h: w["T"][h] + _mm(xb[h][tm:], xb[h][:tm]) for h in heads}

        def s_y0():
            w["tb"] = {h: w["T"][h].astype(BF) for h in heads}
            w["y0"] = {h: _mm(w["tb"][h], w["rhs"][h].astype(BF)) for h in heads}

        def s_res():
            res = {}
            for h in heads:
                nbf = w["nb"][h].astype(F32)
                nx = jnp.concatenate([nbf, w["nm"][h] - nbf], axis=0).astype(BF)
                yh = w["y0"][h].astype(BF)
                yl = (w["y0"][h] - yh.astype(F32)).astype(BF)
                p1 = _mm(nx, yh)
                res[h] = w["rhs"][h] - w["y0"][h] - (p1[:tm] + p1[tm:] + _mm(nx[:tm], yl))
            w["res"] = res

        def s_fix():
            w["y"] = {h: w["y0"][h] + _mm(w["tb"][h], w["res"][h].astype(BF)) for h in heads}

        def c_prep():
            y = w["y"]
            w["u"] = {h: y[h][:, :GDN_DV] for h in heads}
            w["wq"] = {}
            for h in heads:
                qd = w["qn"][h] * w["eg"][h]
                parts = []
                for ci in range(n_chunks):
                    parts += [y[h][ci * L:(ci + 1) * L, GDN_DV:], qd[ci * L:(ci + 1) * L]]
                w["wq"][h] = jnp.concatenate(parts, axis=0).astype(BF)
            kdb = {h: (w["kn"][h] * jnp.exp(w["glast"][h] - w["gc"][h])).astype(BF) for h in heads}
            w["gtot"] = {h: jnp.exp(w["glast"][h]) for h in heads}
            kdt = {h: _nt(eye_k, kdb[h]) for h in heads}
            w["kdt"] = {h: [jnp.where(col_chunk == ci, kdt[h], 0.0).astype(BF) for ci in range(n_chunks)]
                        for h in heads}
            w["st"] = {h: s_scr[h] for h in heads}
            w["outs"] = {h: [] for h in heads}

        def c_chunk(ci):
            r0 = ci * L
            st = w["st"]
            ws = {h: _mm(w["wq"][h][2 * r0:2 * r0 + 2 * L], st[h].astype(BF)) for h in heads}
            for h in heads:
                vnew_scr[h, r0:r0 + L, :] = (w["u"][h][r0:r0 + L] - ws[h][:L]).astype(BF)
            for h in heads:
                w["outs"][h].append(ws[h][L:] + _mm(w["attn"][h][r0:r0 + L, :], vnew_scr[h]))
            w["st"] = {h: st[h] * w["gtot"][h][r0:r0 + 1, :] + _mm(w["kdt"][h][ci], vnew_scr[h]) for h in heads}

        def c_out():
            for h in heads:
                s_scr[h] = w["st"][h]
                o = w["outs"][h][0] if n_chunks == 1 else jnp.concatenate(w["outs"][h], axis=0)
                zz = z_ref[:, h * GDN_DV:(h + 1) * GDN_DV].astype(F32)
                o_ref[:, h * GDN_DV:(h + 1) * GDN_DV] = (_rms(o, gout) * (zz * _sigmoid(zz))).astype(BF)

        prep = [p_q, p_k, p_v, p_gate, p_norm, p_mat]
        solve = ([s_first] + [functools.partial(s_level, s) for s in range(n_factors)] + [s_y0, s_res, s_fix])
        scan = [c_prep] + [functools.partial(c_chunk, ci) for ci in range(n_chunks)] + [c_out]
        return [prep, solve, scan]

    nw = min(GDN_WAVES, hg)
    per = hg // nw
    waves = [make_wave(list(range(i * per, (i + 1) * per))) for i in range(nw)]
    for slot in range(nw + 2):
        active = [waves[i][slot - i] for i in range(nw) if 0 <= slot - i < 3]
        for stage in _interleave(active):
            stage()

    @pl.when(t == nt - 1)
    def _():
        sout_ref[0] = s_scr[...]


def _gdn(qkv, conv_w, conv_past8, ba, bat, alog_l, dtb_l, alog_c, dtb_c, z, s0, gout, nb):
    t = qkv.shape[0]
    s_len = t // nb
    L = min(s_len, CHUNK)
    tm = _tile(s_len, GDN_TM, mult=L)
    nt = s_len // tm
    hg = GDN_HEADS
    hw = hg * GDN_DK
    body = functools.partial(_gdn_body, tm=tm, L=L, hg=hg)
    rows = lambda off: (lambda b, h, i: (b * nt + i, off + h))
    wcol = lambda off: (lambda b, h, i: (0, off + h))
    pcol = lambda off: (lambda b, h, i: (b, off + h))
    const = lambda b, h, i: (0, 0)
    ng = GDN_HEADS // hg
    return pl.pallas_call(
        body,
        grid=(nb, ng, nt),
        in_specs=[
            pl.BlockSpec((tm, hw), rows(0)),
            pl.BlockSpec((tm, hw), rows(ng)),
            pl.BlockSpec((tm, hw), rows(2 * ng)),
            pl.BlockSpec((CONV_W, hw), wcol(0)),
            pl.BlockSpec((CONV_W, hw), wcol(ng)),
            pl.BlockSpec((CONV_W, hw), wcol(2 * ng)),
            pl.BlockSpec((HALO, hw), pcol(0)),
            pl.BlockSpec((HALO, hw), pcol(ng)),
            pl.BlockSpec((HALO, hw), pcol(2 * ng)),
            pl.BlockSpec((tm, LANE), lambda b, h, i: (b * nt + i, 0)),
            pl.BlockSpec((1, 16, tm), lambda b, h, i: (b, 0, i)),
            pl.BlockSpec((1, LANE), const),
            pl.BlockSpec((1, LANE), const),
            pl.BlockSpec((16, 1), const),
            pl.BlockSpec((16, 1), const),
            pl.BlockSpec((tm, hw), rows(0)),
            pl.BlockSpec((1, hg, GDN_DK, GDN_DV), lambda b, h, i: (b, h, 0, 0)),
            pl.BlockSpec((1, GDN_DV), const),
        ],
        out_specs=[
            pl.BlockSpec((tm, hw), rows(0)),
            pl.BlockSpec((1, hg, GDN_DK, GDN_DV), lambda b, h, i: (b, h, 0, 0)),
        ],
        out_shape=[
            jax.ShapeDtypeStruct((t, GDN_HEADS * GDN_DV), BF),
            jax.ShapeDtypeStruct((nb, GDN_HEADS, GDN_DK, GDN_DV), F32),
        ],
        scratch_shapes=[
            pltpu.VMEM((hg, GDN_DK, GDN_DV), F32),
            pltpu.VMEM((3, HALO, hw), F32),
            pltpu.VMEM((hg, tm, GDN_DV), BF),
        ],
        compiler_params=_params(("arbitrary", "arbitrary", "arbitrary")),
        name="gdn",
    )(qkv, qkv, qkv, conv_w, conv_w, conv_w, conv_past8, conv_past8, conv_past8,
      ba, bat, alog_l, dtb_l, alog_c, dtb_c, z, s0, gout)


def _mix_mem_body(x_ref, a1_ref, a2_ref, wo1_ref, wo2_ref, gq_ref, wmq_ref, gmq_ref,
                  mk_ref, mv_ref, wmo_ref, o_ref):
    h1 = x_ref[...] + _mm(a1_ref[...], wo1_ref[...]) + _mm(a2_ref[...], wo2_ref[...])
    hn = _rms(h1, gq_ref[...]).astype(BF)
    q = _mm(hn, wmq_ref[...])
    gmq = gmq_ref[...]
    mk = mk_ref[0]
    mv = mv_ref[0]
    outs = []
    for h in range(MEM_HEADS):
        sl = slice(h * MEM_DIM, (h + 1) * MEM_DIM)
        qh = _rms(q[:, sl], gmq).astype(BF)
        s = _nt(qh, mk[:, sl]) * (MEM_DIM ** -0.5)
        p = jnp.exp(s - jnp.max(s, axis=-1, keepdims=True))
        p = p / jnp.sum(p, axis=-1, keepdims=True)
        outs.append(_mm(p.astype(BF), mv[:, sl]))
    o = jnp.concatenate(outs, axis=1).astype(BF)
    o_ref[...] = h1 + _mm(o, wmo_ref[...])


def _mix_mem(x, a1, a2, wo1, wo2, gq, wmq, gmq, mk, mv, wmo, nb):
    t, d = x.shape
    s_len = t // nb
    tm = _tile(s_len, MIX_TM)
    nt = s_len // tm
    n_mem = mk.shape[1]
    mw = MEM_HEADS * MEM_DIM
    row = lambda i: (i, 0)
    const = lambda i: (0, 0)
    return pl.pallas_call(
        _mix_mem_body,
        grid=(t // tm,),
        in_specs=[
            pl.BlockSpec((tm, d), row),
            pl.BlockSpec((tm, a1.shape[1]), row),
            pl.BlockSpec((tm, a2.shape[1]), row),
            pl.BlockSpec(wo1.shape, const),
            pl.BlockSpec(wo2.shape, const),
            pl.BlockSpec((1, d), const),
            pl.BlockSpec(wmq.shape, const),
            pl.BlockSpec((1, MEM_DIM), const),
            pl.BlockSpec((1, n_mem, mw), lambda i: (i // nt, 0, 0)),
            pl.BlockSpec((1, n_mem, mw), lambda i: (i // nt, 0, 0)),
            pl.BlockSpec(wmo.shape, const),
        ],
        out_specs=pl.BlockSpec((tm, d), row),
        out_shape=jax.ShapeDtypeStruct((t, d), F32),
        compiler_params=_params(("arbitrary",)),
        name="mix_mem",
    )(x, a1, a2, wo1, wo2, gq, wmq, gmq, mk, mv, wmo)


def _ffn_body(x_ref, g_ref, w1_ref, w2_ref, o_ref, hn_scr):
    f = pl.program_id(1)

    @pl.when(f == 0)
    def _():
        x = x_ref[...]
        hn_scr[...] = _rms(x, g_ref[...]).astype(BF)
        o_ref[...] = x

    a = jnp.maximum(_mm(hn_scr[...], w1_ref[...]), 0.0)
    o_ref[...] += _mm((a * a).astype(BF), w2_ref[...])


def _ffn(x, g, w1, w2):
    t, d = x.shape
    dff = w1.shape[1]
    tm = _tile(t, FFN_TM)
    tf = _tile(dff, FFN_TF, mult=LANE)
    return pl.pallas_call(
        _ffn_body,
        grid=(t // tm, dff // tf),
        in_specs=[
            pl.BlockSpec((tm, d), lambda i, f: (i, 0)),
            pl.BlockSpec((1, d), lambda i, f: (0, 0)),
            pl.BlockSpec((d, tf), lambda i, f: (0, f)),
            pl.BlockSpec((tf, d), lambda i, f: (f, 0)),
        ],
        out_specs=pl.BlockSpec((tm, d), lambda i, f: (i, 0)),
        out_shape=jax.ShapeDtypeStruct((t, d), F32),
        scratch_shapes=[pltpu.VMEM((tm, d), BF)],
        compiler_params=_params(("arbitrary", "arbitrary")),
        name="ffn",
    )(x, g, w1, w2)


def _mem_kv_body(m_ref, g_ref, wk_ref, wv_ref, gk_ref, k_ref, v_ref):
    mn = _rms(m_ref[...], g_ref[...]).astype(BF)
    k = _mm(mn, wk_ref[...])
    gk = gk_ref[...]
    for h in range(MEM_HEADS):
        sl = slice(h * MEM_DIM, (h + 1) * MEM_DIM)
        k_ref[:, sl] = _rms(k[:, sl], gk)
    v_ref[...] = _mm(mn, wv_ref[...])


def _mem_kv(mem, g, wk, wv, gk):
    t, d = mem.shape
    tm = _tile(t, 256)
    mw = MEM_HEADS * MEM_DIM
    const = lambda i: (0, 0)
    return pl.pallas_call(
        _mem_kv_body,
        grid=(t // tm,),
        in_specs=[
            pl.BlockSpec((tm, d), lambda i: (i, 0)),
            pl.BlockSpec((1, d), const),
            pl.BlockSpec(wk.shape, const),
            pl.BlockSpec(wv.shape, const),
            pl.BlockSpec((1, MEM_DIM), const),
        ],
        out_specs=[pl.BlockSpec((tm, mw), lambda i: (i, 0)), pl.BlockSpec((tm, mw), lambda i: (i, 0))],
        out_shape=[jax.ShapeDtypeStruct((t, mw), F32), jax.ShapeDtypeStruct((t, mw), F32)],
        compiler_params=_params(("arbitrary",)),
        name="mem_kv",
    )(mem, g, wk, wv, gk)


def _rope_tables(pos):
    half = QK_ROPE // 2
    inv_freq = ROPE_THETA ** (-jnp.arange(half, dtype=F32) / half)
    ang = pos.astype(F32)[:, None] * inv_freq[None, :]
    cos, sin = jnp.cos(ang), jnp.sin(ang)
    zeros = jnp.zeros((pos.shape[0], LANE - QK_ROPE), F32)
    return (jnp.concatenate([cos, cos, zeros], axis=1), jnp.concatenate([-sin, sin, zeros], axis=1))


def _pad_lanes(v, n=LANE):
    return jnp.pad(v, ((0, 0), (0, n - v.shape[1])))


def _prep_weights(w_in, g_cq, w_uq, g_ckv, w_ukv, g_q, g_k, w_o, w_mq, w_mo, w_ff1, w_ff2, w_mk, w_mv):
    d = w_in.shape[0]
    n_conv = 2 * GDN_HEADS * GDN_DK + GDN_HEADS * GDN_DV
    n_z = GDN_HEADS * GDN_DV
    o = 0
    w_cq = w_in[:, o:o + Q_RANK]; o += Q_RANK
    w_ckv = w_in[:, o:o + KV_RANK]; o += KV_RANK
    w_kpe = w_in[:, o:o + QK_ROPE]; o += QK_ROPE
    w_qkv = w_in[:, o:o + n_conv]; o += n_conv
    w_z = w_in[:, o:o + n_z]; o += n_z
    w_b = w_in[:, o:o + GDN_HEADS]; o += GDN_HEADS
    w_a = w_in[:, o:o + GDN_HEADS]
    half = QK_ROPE // 2
    swap = jnp.concatenate([jnp.arange(half, QK_ROPE), jnp.arange(0, half)])
    z64 = jnp.zeros((d, LANE - QK_ROPE), w_in.dtype)
    misc = jnp.concatenate([w_kpe, z64, w_kpe[:, swap], z64, w_b, w_a], axis=1)
    misc = _pad_lanes(misc, COL_TILE)
    w_in_p = jnp.concatenate([w_cq, w_ckv, w_qkv, w_z, misc], axis=1).astype(BF)
    wbat = jnp.concatenate([w_b, w_a], axis=1).T.astype(BF)

    r = w_uq.shape[0]
    wq3 = w_uq.reshape(r, MLA_HEADS, QK_HEAD)
    zq = jnp.zeros((r, MLA_HEADS, LANE - QK_ROPE), w_uq.dtype)
    w1 = jnp.concatenate([wq3, zq], axis=2).reshape(r, MLA_HEADS * QPAD).astype(BF)
    w2 = jnp.concatenate([wq3[:, :, QK_NOPE:][:, :, swap], zq], axis=2).reshape(r, MLA_HEADS * LANE).astype(BF)

    dmla = MLA_HEADS * V_DIM
    return dict(
        n_conv=n_conv, n_z=n_z, w_in_p=w_in_p, wbat=wbat, w1=w1, w2=w2,
        gcq=g_cq[None, :], gckv=g_ckv[None, :],
        gq_n=g_q[None, :QK_NOPE], gq_r=_pad_lanes(g_q[None, QK_NOPE:]),
        gk_n=g_k[None, :QK_NOPE], gk_r=_pad_lanes(g_k[None, QK_NOPE:]),
        w_ukv=w_ukv.astype(BF), wo1=w_o[:dmla].astype(BF), wo2=w_o[dmla:].astype(BF),
        w_mq=w_mq.astype(BF), w_mo=w_mo.astype(BF), w_ff1=w_ff1.astype(BF), w_ff2=w_ff2.astype(BF),
        w_mk=w_mk.astype(BF), w_mv=w_mv.astype(BF),
    )


def _gate_params(a_log, dt_bias):
    z8 = jnp.zeros((GDN_HEADS,), F32)
    al = jnp.concatenate([z8, a_log.astype(F32)])
    db = jnp.concatenate([z8, dt_bias.astype(F32)])
    return _pad_lanes(al[None, :]), _pad_lanes(db[None, :]), al[:, None], db[:, None]


def _layer(x, pos, past, conv_past, s0, mem_k, mem_v, wp, lw):
    nb, s_len, d = x.shape
    t = nb * s_len
    xf = x.reshape(t, d)
    ctab, stab = _rope_tables(pos)
    if ctab.shape[0] % 8 != 0 or (s_len < IN_TM and nb > 1):
        ctab, stab = jnp.tile(ctab, (nb, 1)), jnp.tile(stab, (nb, 1))

    cqn, ckv, kpe_pad, qkv, z, ba, bat = _in_proj(
        xf, lw["g_norm_mix"], wp["w_in_p"], wp["wbat"], wp["gcq"], wp["gckv"], ctab, stab,
        wp["n_conv"], wp["n_z"])
    q = _q_proj(cqn, wp["w1"], wp["w2"], ctab, stab, wp["gq_n"], wp["gq_r"])

    if past is None:
        ckv_all, kpe_all, q_off, t_valid = ckv, kpe_pad, 0, s_len
    else:
        ckv_past, kpe_past = past
        p_len = ckv_past.shape[1]
        t_valid = p_len + s_len
        t_pad = -(-t_valid // ATTN_TK) * ATTN_TK
        ckv_all = jnp.concatenate(
            [ckv_past.astype(F32), ckv.reshape(nb, s_len, KV_RANK),
             jnp.zeros((nb, t_pad - t_valid, KV_RANK), F32)], axis=1).reshape(nb * t_pad, KV_RANK)
        kpe_all = jnp.concatenate(
            [jnp.pad(kpe_past.astype(F32), ((0, 0), (0, 0), (0, LANE - QK_ROPE))),
             kpe_pad.reshape(nb, s_len, LANE),
             jnp.zeros((nb, t_pad - t_valid, LANE), F32)], axis=1).reshape(nb * t_pad, LANE)
        q_off = p_len
    k, v = _kv_proj(ckv_all, kpe_all, wp["w_ukv"], wp["gk_n"], wp["gk_r"])
    o_mla = _attention(q, k, v, lw["g_mla_out"], nb, q_off, t_valid)

    conv_past8 = jnp.pad(conv_past.astype(F32), ((0, 0), (HALO - (CONV_W - 1), 0), (0, 0)))
    conv_past8 = conv_past8.reshape(nb * HALO, -1)
    bat3 = bat.reshape(16, nb, s_len).transpose(1, 0, 2)
    alog_l, dtb_l, alog_c, dtb_c = _gate_params(lw["a_log"], lw["dt_bias"])
    o_gdn, s_new = _gdn(qkv, lw["conv_w"], conv_past8, ba, bat3, alog_l, dtb_l, alog_c, dtb_c,
                        z, s0.astype(F32), lw["g_gdn_out"], nb)

    h2 = _mix_mem(xf, o_mla, o_gdn, wp["wo1"], wp["wo2"], lw["g_norm_mem_q"], wp["w_mq"],
                  lw["g_mq"], mem_k, mem_v, wp["w_mo"], nb)
    y = _ffn(h2, lw["g_norm_ffn"], wp["w_ff1"], wp["w_ff2"])

    conv_in_tail = jnp.concatenate([conv_past.astype(F32), qkv.reshape(nb, s_len, -1)[:, -(CONV_W - 1):]], axis=1)
    conv_new = conv_in_tail[:, -(CONV_W - 1):]
    return (y.reshape(nb, s_len, d), ckv.reshape(nb, s_len, KV_RANK),
            kpe_pad[:, :QK_ROPE].reshape(nb, s_len, QK_ROPE), conv_new, s_new)


def kernel(x_prompt, x_sample, mem_prompt, cache_mla_ckv, cache_mla_kpe, cache_gdn_conv, state_gdn, cache_mem_k, cache_mem_v, g_norm_mix, w_in, g_cq, w_uq, g_ckv, w_ukv, g_q_mla, g_k_mla, g_mla_out, conv_w, a_log, dt_bias, g_gdn_out, w_o, g_norm_mem_q, g_norm_mem_kv, w_mq, w_mk, w_mv, g_mq, g_mk, w_mo, g_norm_ffn, w_ff1, w_ff2):
    depth = w_in.shape[0]
    nbp, sp, d = x_prompt.shape
    nbs, ss, _ = x_sample.shape
    n_mem = mem_prompt.shape[1]
    mw = MEM_HEADS * MEM_DIM
    n_conv = 2 * GDN_HEADS * GDN_DK + GDN_HEADS * GDN_DV
    pos_p = jnp.arange(sp)
    pos_s = cache_mla_ckv.shape[2] + jnp.arange(ss)
    zeros_conv = jnp.zeros((nbp, CONV_W - 1, n_conv), F32)
    zeros_state = jnp.zeros((nbp, GDN_HEADS, GDN_DK, GDN_DV), F32)
    hp, hs = x_prompt, x_sample
    outs_p = [[] for _ in range(6)]
    outs_s = [[] for _ in range(4)]
    for l in range(depth):
        wp = _prep_weights(w_in[l], g_cq[l], w_uq[l], g_ckv[l], w_ukv[l], g_q_mla[l], g_k_mla[l], w_o[l],
                           w_mq[l], w_mo[l], w_ff1[l], w_ff2[l], w_mk[l], w_mv[l])
        lw = dict(g_norm_mix=g_norm_mix[l][None, :], g_mla_out=g_mla_out[l][None, :], conv_w=conv_w[l],
                  a_log=a_log[l], dt_bias=dt_bias[l], g_gdn_out=g_gdn_out[l][None, :],
                  g_norm_mem_q=g_norm_mem_q[l][None, :], g_mq=g_mq[l][None, :],
                  g_norm_ffn=g_norm_ffn[l][None, :])
        mk, mv = _mem_kv(mem_prompt.reshape(nbp * n_mem, d), g_norm_mem_kv[l][None, :], wp["w_mk"], wp["w_mv"],
                         g_mk[l][None, :])
        mk3, mv3 = mk.reshape(nbp, n_mem, mw), mv.reshape(nbp, n_mem, mw)
        hp, c1, c2, c3, c4 = _layer(hp, pos_p, None, zeros_conv, zeros_state, mk3.astype(BF), mv3.astype(BF), wp, lw)
        for lst, val in zip(outs_p, (c1, c2, c3, c4, mk3.reshape(nbp, n_mem, MEM_HEADS, MEM_DIM),
                                     mv3.reshape(nbp, n_mem, MEM_HEADS, MEM_DIM))):
            lst.append(val)
        hs, d1, d2, d3, d4 = _layer(hs, pos_s, (cache_mla_ckv[l], cache_mla_kpe[l]), cache_gdn_conv[l], state_gdn[l],
                                    cache_mem_k[l].reshape(nbs, n_mem, mw).astype(BF),
                                    cache_mem_v[l].reshape(nbs, n_mem, mw).astype(BF), wp, lw)
        for lst, val in zip(outs_s, (d1, d2, d3, d4)):
            lst.append(val)
    return (hp, hs, *(jnp.stack(v) for v in outs_p), *(jnp.stack(v) for v in outs_s))
```

```python
import functools
import math

import jax
import jax.numpy as jnp
from jax import lax
from jax.experimental import pallas as pl
from jax.experimental.pallas import tpu as pltpu

F32 = jnp.float32
BF = jnp.bfloat16

EPS = 1e-6
CHUNK = 64
ROPE_THETA = 10000.0
MLA_HEADS = 8
QK_NOPE = 128
QK_ROPE = 64
QK_HEAD = QK_NOPE + QK_ROPE
V_DIM = 128
Q_RANK = 512
KV_RANK = 512
GDN_HEADS = 8
GDN_DK = 128
GDN_DV = 128
CONV_W = 4
MEM_HEADS = 4
MEM_DIM = 128
LANE = 128
QPAD = 2 * LANE
COL_TILE = 512
HALO = 8
NEG_BIG = -1e30
QSCALE = (QK_HEAD ** -0.5) * math.log2(math.e)

VMEM_LIMIT = 56 * 1024 * 1024

IN_TM = 1024
PROJ_TM = 512
ATTN_TQ = 1024
ATTN_TK = 1024
ATTN_WIDE = 2
GDN_TM = 256
GDN_WAVES = 2
MIX_TM = 512
FFN_TM = 1024
FFN_TF = 1024


def _tile(n, pref, mult=8):
    if n <= pref:
        return n
    t = (pref // mult) * mult
    while t >= mult:
        if n % t == 0:
            return t
        t -= mult
    return n


def _nt(a, b):
    return lax.dot_general(a, b, (((1,), (1,)), ((), ())), preferred_element_type=F32)


def _mm(a, b):
    return jnp.dot(a, b, preferred_element_type=F32)


def _sigmoid(x):
    return 1.0 / (1.0 + jnp.exp(-x))


def _softplus(x):
    return jnp.maximum(x, 0.0) + jnp.log(1.0 + jnp.exp(-jnp.abs(x)))


def _rms(x, g, n=None):
    n = x.shape[-1] if n is None else n
    ms = jnp.sum(x * x, axis=-1, keepdims=True) * (1.0 / n)
    return (x * lax.rsqrt(ms + EPS)) * g


def _split3(x):
    hi = x.astype(BF)
    r1 = x - hi.astype(F32)
    mid = r1.astype(BF)
    lo = (r1 - mid.astype(F32)).astype(BF)
    return hi, mid, lo


def _interleave(lists):
    items = []
    for li, lst in enumerate(lists):
        items += [((i + 0.5) / len(lst), li, f) for i, f in enumerate(lst)]
    return [f for _, _, f in sorted(items, key=lambda it: (it[0], it[1]))]


def _params(sem):
    return pltpu.CompilerParams(dimension_semantics=sem, vmem_limit_bytes=VMEM_LIMIT)


def _in_proj_body(x_ref, g_ref, w_ref, wbat_ref, gcq_ref, gckv_ref, ctab_ref, stab_ref,
                  cqn_ref, ckv_ref, kpe_ref, qkv_ref, z_ref, ba_ref, bat_ref, xn_scr, *, nqkv, nz):
    j = pl.program_id(1)

    @pl.when(j == 0)
    def _():
        xn_scr[...] = _rms(x_ref[...], g_ref[...]).astype(BF)

    acc = _mm(xn_scr[...], w_ref[...])

    @pl.when(j == 0)
    def _():
        cqn_ref[...] = _rms(acc, gcq_ref[...]).astype(BF)

    @pl.when(j == 1)
    def _():
        ckv_ref[...] = _rms(acc, gckv_ref[...])

    @pl.when((j >= 2) & (j < 2 + nqkv))
    def _():
        qkv_ref[...] = acc

    @pl.when((j >= 2 + nqkv) & (j < 2 + nqkv + nz))
    def _():
        z_ref[...] = acc.astype(BF)

    @pl.when(j == 2 + nqkv + nz)
    def _():
        kpe_ref[...] = acc[:, 0:LANE] * ctab_ref[...] + acc[:, LANE:2 * LANE] * stab_ref[...]
        ba_ref[...] = acc[:, 2 * LANE:3 * LANE]
        bat_ref[...] = _nt(wbat_ref[...], xn_scr[...])


def _in_proj(x, g, w_p, wbat, gcq, gckv, ctab, stab, n_conv, n_z):
    t, d = x.shape
    tm = _tile(min(t, ctab.shape[0]), IN_TM)
    ntab = ctab.shape[0] // tm
    nqkv = n_conv // COL_TILE
    nz = n_z // COL_TILE
    ncol = 2 + nqkv + nz + 1
    assert w_p.shape[1] == ncol * COL_TILE
    row = lambda i, j: (i, 0)
    const = lambda i, j: (0, 0)
    body = functools.partial(_in_proj_body, nqkv=nqkv, nz=nz)
    return pl.pallas_call(
        body,
        grid=(t // tm, ncol),
        in_specs=[
            pl.BlockSpec((tm, d), row),
            pl.BlockSpec((1, d), const),
            pl.BlockSpec((d, COL_TILE), lambda i, j: (0, j)),
            pl.BlockSpec((16, d), const),
            pl.BlockSpec((1, Q_RANK), const),
            pl.BlockSpec((1, KV_RANK), const),
            pl.BlockSpec((tm, LANE), lambda i, j: (i % ntab, 0)),
            pl.BlockSpec((tm, LANE), lambda i, j: (i % ntab, 0)),
        ],
        out_specs=[
            pl.BlockSpec((tm, Q_RANK), row),
            pl.BlockSpec((tm, KV_RANK), row),
            pl.BlockSpec((tm, LANE), row),
            pl.BlockSpec((tm, COL_TILE), lambda i, j: (i, jnp.clip(j - 2, 0, nqkv - 1))),
            pl.BlockSpec((tm, COL_TILE), lambda i, j: (i, jnp.clip(j - 2 - nqkv, 0, nz - 1))),
            pl.BlockSpec((tm, LANE), row),
            pl.BlockSpec((16, tm), lambda i, j: (0, i)),
        ],
        out_shape=[
            jax.ShapeDtypeStruct((t, Q_RANK), BF),
            jax.ShapeDtypeStruct((t, KV_RANK), F32),
            jax.ShapeDtypeStruct((t, LANE), F32),
            jax.ShapeDtypeStruct((t, n_conv), F32),
            jax.ShapeDtypeStruct((t, n_z), BF),
            jax.ShapeDtypeStruct((t, LANE), F32),
            jax.ShapeDtypeStruct((16, t), F32),
        ],
        scratch_shapes=[pltpu.VMEM((tm, d), BF)],
        compiler_params=_params(("arbitrary", "arbitrary")),
        name="in_proj",
    )(x, g, w_p, wbat, gcq, gckv, ctab, stab)


def _q_proj_body(c_ref, w1_ref, w2_ref, ctab_ref, stab_ref, gn_ref, gr_ref, q_ref):
    c = c_ref[...]
    qf = _mm(c, w1_ref[...])
    qs = _mm(c, w2_ref[...])
    ct = ctab_ref[...]
    st = stab_ref[...]
    gn = gn_ref[...]
    gr = gr_ref[...]
    for h in range(MLA_HEADS):
        nope = qf[:, h * QPAD:h * QPAD + LANE]
        rot = qf[:, h * QPAD + LANE:(h + 1) * QPAD] * ct + qs[:, h * LANE:(h + 1) * LANE] * st
        ss = jnp.sum(nope * nope, axis=-1, keepdims=True) + jnp.sum(rot * rot, axis=-1, keepdims=True)
        rs = lax.rsqrt(ss * (1.0 / QK_HEAD) + EPS) * QSCALE
        q_ref[:, h * QPAD:h * QPAD + LANE] = (nope * rs * gn).astype(BF)
        q_ref[:, h * QPAD + LANE:(h + 1) * QPAD] = (rot * rs * gr).astype(BF)


def _q_proj(cqn, w1, w2, ctab, stab, gn, gr):
    t = cqn.shape[0]
    tm = _tile(min(t, ctab.shape[0]), PROJ_TM)
    ntab = ctab.shape[0] // tm
    const = lambda i: (0, 0)
    return pl.pallas_call(
        _q_proj_body,
        grid=(t // tm,),
        in_specs=[
            pl.BlockSpec((tm, Q_RANK), lambda i: (i, 0)),
            pl.BlockSpec(w1.shape, const),
            pl.BlockSpec(w2.shape, const),
            pl.BlockSpec((tm, LANE), lambda i: (i % ntab, 0)),
            pl.BlockSpec((tm, LANE), lambda i: (i % ntab, 0)),
            pl.BlockSpec((1, LANE), const),
            pl.BlockSpec((1, LANE), const),
        ],
        out_specs=pl.BlockSpec((tm, MLA_HEADS * QPAD), lambda i: (i, 0)),
        out_shape=jax.ShapeDtypeStruct((t, MLA_HEADS * QPAD), BF),
        compiler_params=_params(("arbitrary",)),
        name="q_proj",
    )(cqn, w1, w2, ctab, stab, gn, gr)


def _kv_proj_body(c_ref, kpe_ref, w_ref, gn_ref, gr_ref, k_ref, v_ref):
    kv = _mm(c_ref[...].astype(BF), w_ref[...])
    kp = kpe_ref[...]
    kps = jnp.sum(kp * kp, axis=-1, keepdims=True)
    gn = gn_ref[...]
    gr = gr_ref[...]
    for h in range(MLA_HEADS):
        kn = kv[:, h * 2 * LANE:h * 2 * LANE + LANE]
        rs = lax.rsqrt((jnp.sum(kn * kn, axis=-1, keepdims=True) + kps) * (1.0 / QK_HEAD) + EPS)
        k_ref[:, h * QPAD:h * QPAD + LANE] = (kn * rs * gn).astype(BF)
        k_ref[:, h * QPAD + LANE:(h + 1) * QPAD] = (kp * rs * gr).astype(BF)
        v_ref[:, h * V_DIM:(h + 1) * V_DIM] = kv[:, h * 2 * LANE + LANE:(h + 1) * 2 * LANE].astype(BF)


def _kv_proj(ckv, kpe_pad, w, gn, gr):
    t = ckv.shape[0]
    tm = _tile(t, PROJ_TM)
    const = lambda i: (0, 0)
    return pl.pallas_call(
        _kv_proj_body,
        grid=(t // tm,),
        in_specs=[
            pl.BlockSpec((tm, KV_RANK), lambda i: (i, 0)),
            pl.BlockSpec((tm, LANE), lambda i: (i, 0)),
            pl.BlockSpec(w.shape, const),
            pl.BlockSpec((1, LANE), const),
            pl.BlockSpec((1, LANE), const),
        ],
        out_specs=[
            pl.BlockSpec((tm, MLA_HEADS * QPAD), lambda i: (i, 0)),
            pl.BlockSpec((tm, MLA_HEADS * V_DIM), lambda i: (i, 0)),
        ],
        out_shape=[
            jax.ShapeDtypeStruct((t, MLA_HEADS * QPAD), BF),
            jax.ShapeDtypeStruct((t, MLA_HEADS * V_DIM), BF),
        ],
        compiler_params=_params(("arbitrary",)),
        name="kv_proj",
    )(ckv, kpe_pad, w, gn, gr)


def _attn_body(q_ref, k_ref, v_ref, g_ref, o_ref, m_scr, l_scr, acc_scr, *, tq, tk, wide, q_off, t_valid):
    qi = pl.program_id(2)
    qpos0 = q_off + qi * tq
    n_full = jnp.minimum((qpos0 // CHUNK * CHUNK + CHUNK) // tk, t_valid // tk)
    hi = jnp.minimum((qpos0 + tq - 1) // CHUNK * CHUNK + CHUNK, t_valid)
    n_total = (hi + tk - 1) // tk

    m_scr[...] = jnp.full(m_scr.shape, NEG_BIG, F32)
    l_scr[...] = jnp.zeros(l_scr.shape, F32)
    acc_scr[...] = jnp.zeros(acc_scr.shape, F32)
    def step(kc, width, masked):
        k0 = pl.multiple_of(kc * tk, tk)
        s = _nt(q_ref[...], k_ref[pl.ds(k0, width), :])
        if masked:
            qpos = qpos0 + lax.broadcasted_iota(jnp.int32, (tq, 1), 0)
            last = jnp.minimum(qpos | (CHUNK - 1), t_valid - 1) - k0
            s = jnp.where(lax.broadcasted_iota(jnp.int32, (tq, width), 1) <= last, s, NEG_BIG)
        m_prev = m_scr[...]
        m_new = jnp.maximum(m_prev, jnp.max(s, axis=-1, keepdims=True))
        alpha = jnp.exp2(m_prev - m_new)
        ps = [jnp.exp2(s[:, j * LANE:(j + 1) * LANE] - m_new) for j in range(width // LANE)]
        psum = ps[0]
        for pj in ps[1:]:
            psum = psum + pj
        l_scr[...] = alpha * l_scr[...] + psum
        p = jnp.concatenate(ps, axis=1).astype(BF) if len(ps) > 1 else ps[0].astype(BF)
        acc_scr[...] = acc_scr[...] * alpha + _mm(p, v_ref[pl.ds(k0, width), :])
        m_scr[...] = m_new

    def loop(lo, hi, fn):
        lax.fori_loop(lo, hi, lambda i, c: (fn(i), c)[1], 0)

    n_wide = n_full // wide
    loop(0, n_wide, lambda i: step(i * wide, wide * tk, False))
    loop(n_wide * wide, n_full, lambda kc: step(kc, tk, False))
    loop(n_full, n_total, lambda kc: step(kc, tk, True))
    o = acc_scr[...] / jnp.sum(l_scr[...], axis=-1, keepdims=True)
    o_ref[...] = _rms(o, g_ref[...]).astype(BF)


def _attention(q, k, v, g_out, nb, q_off, t_valid):
    tq_total = q.shape[0] // nb
    tk_total = k.shape[0] // nb
    tq = _tile(tq_total, ATTN_TQ)
    tk = _tile(tk_total, ATTN_TK)
    nq = tq_total // tq
    body = functools.partial(_attn_body, tq=tq, tk=tk, wide=ATTN_WIDE, q_off=q_off, t_valid=t_valid)
    return pl.pallas_call(
        body,
        grid=(nb, MLA_HEADS, nq),
        in_specs=[
            pl.BlockSpec((tq, QPAD), lambda b, h, i: (b * nq + i, h)),
            pl.BlockSpec((tk_total, QPAD), lambda b, h, i: (b, h)),
            pl.BlockSpec((tk_total, V_DIM), lambda b, h, i: (b, h)),
            pl.BlockSpec((1, V_DIM), lambda b, h, i: (0, 0)),
        ],
        out_specs=pl.BlockSpec((tq, V_DIM), lambda b, h, i: (b * nq + i, h)),
        out_shape=jax.ShapeDtypeStruct((q.shape[0], MLA_HEADS * V_DIM), BF),
        scratch_shapes=[
            pltpu.VMEM((tq, LANE), F32),
            pltpu.VMEM((tq, LANE), F32),
            pltpu.VMEM((tq, V_DIM), F32),
        ],
        compiler_params=_params(("arbitrary", "arbitrary", "arbitrary")),
        name="mla_attention",
    )(q, k, v, g_out)


def _gdn_body(q_ref, k_ref, v_ref, wq_ref, wk_ref, wv_ref, pq_ref, pk_ref, pv_ref,
              ba_ref, bat_ref, alog_l_ref, dtb_l_ref, alog_c_ref, dtb_c_ref,
              z_ref, s0_ref, gout_ref,
              o_ref, sout_ref,
              s_scr, halo_scr, vnew_scr, *, tm, L, hg):
    t = pl.program_id(2)
    nt = pl.num_programs(2)
    n_chunks = tm // L
    n_factors = int(math.log2(L)) - 1

    @pl.when(t == 0)
    def _():
        s_scr[...] = s0_ref[0]
        halo_scr[0] = pq_ref[...]
        halo_scr[1] = pk_ref[...]
        halo_scr[2] = pv_ref[...]
        vnew_scr[...] = jnp.zeros(vnew_scr.shape, BF)

    ba = ba_ref[...]
    beta_all = _sigmoid(ba)
    g_all = -jnp.exp(alog_l_ref[...]) * _softplus(ba + dtb_l_ref[...])
    gt_all = -jnp.exp(alog_c_ref[...]) * _softplus(bat_ref[0] + dtb_c_ref[...])

    r = lax.broadcasted_iota(jnp.int32, (tm, tm), 0)
    c = lax.broadcasted_iota(jnp.int32, (tm, tm), 1)
    same = (r // L) == (c // L)
    lower = same & (c <= r)
    strict = same & (c < r)
    cs = jnp.where(lower, 1.0, 0.0).astype(BF)
    bd = jnp.where(same, 1.0, 0.0).astype(BF)
    g3 = _split3(g_all)
    gcol_all = _mm(cs, g3[0]) + _mm(cs, g3[1]) + _mm(cs, g3[2])
    glast_all = _mm(bd, g3[0]) + _mm(bd, g3[1]) + _mm(bd, g3[2])
    gt3 = _split3(gt_all)
    grow_all = _nt(gt3[0], cs) + _nt(gt3[1], cs) + _nt(gt3[2], cs)

    gout = gout_ref[...]
    eye_t = jnp.where(r == c, 1.0, 0.0)
    eye_k = jnp.where(lax.broadcasted_iota(jnp.int32, (GDN_DK, GDN_DK), 0)
                      == lax.broadcasted_iota(jnp.int32, (GDN_DK, GDN_DK), 1), 1.0, 0.0).astype(BF)
    col_chunk = lax.broadcasted_iota(jnp.int32, (1, tm), 1) // L

    def make_wave(heads):
        c0, c1 = heads[0] * GDN_DK, (heads[-1] + 1) * GDN_DK
        loc = lambda h: slice((h - heads[0]) * GDN_DK, (h - heads[0] + 1) * GDN_DK)
        w = {}

        def conv(idx, x_ref, w_ref):
            xs = jnp.concatenate([halo_scr[idx, :, c0:c1], x_ref[:, c0:c1]], axis=0)
            cw = w_ref[:, c0:c1]
            acc = xs * cw[0:1]
            for i in range(1, CONV_W):
                acc = pltpu.roll(acc, 1, axis=0) + xs * cw[i:i + 1]
            halo_scr[idx, :, c0:c1] = xs[tm:tm + HALO]
            y = acc[HALO:HALO + tm]
            return y * _sigmoid(y)

        def p_q():
            w["yq"] = conv(0, q_ref, wq_ref)

        def p_k():
            w["yk"] = conv(1, k_ref, wk_ref)

        def p_v():
            w["yv"] = conv(2, v_ref, wv_ref)

        def p_gate():
            w["beta"], w["gc"], w["glast"], w["gr"] = {}, {}, {}, {}
            for h in heads:
                g = GDN_HEADS + h
                w["beta"][h] = beta_all[:, h:h + 1]
                w["gc"][h] = gcol_all[:, g:g + 1]
                w["glast"][h] = glast_all[:, g:g + 1]
                w["gr"][h] = grow_all[g:g + 1, :]

        def p_norm():
            w["qn"], w["kn"] = {}, {}
            for h in heads:
                qh = w["yq"][:, loc(h)]
                kh = w["yk"][:, loc(h)]
                w["qn"][h] = qh * lax.rsqrt(jnp.sum(qh * qh, axis=-1, keepdims=True) + EPS) * (GDN_DK ** -0.5)
                w["kn"][h] = kh * lax.rsqrt(jnp.sum(kh * kh, axis=-1, keepdims=True) + EPS)
            kb = {h: w["kn"][h].astype(BF) for h in heads}
            w["kk"] = {h: _nt(kb[h], kb[h]) for h in heads}
            w["qk"] = {h: _nt(w["qn"][h].astype(BF), kb[h]) for h in heads}

        def p_mat():
            beta, gc = w["beta"], w["gc"]
            decay = {h: jnp.where(lower, jnp.exp(gc[h] - w["gr"][h]), 0.0) for h in heads}
            w["nm"] = {h: jnp.where(strict, beta[h] * w["kk"][h] * decay[h], 0.0) for h in heads}
            w["attn"] = {h: (w["qk"][h] * decay[h]).astype(BF) for h in heads}
            w["eg"] = {h: jnp.exp(gc[h]) for h in heads}
            w["rhs"] = {h: jnp.concatenate([w["yv"][:, loc(h)] * beta[h],
                                            w["kn"][h] * (beta[h] * w["eg"][h])], axis=1) for h in heads}
            w["nb"] = {h: w["nm"][h].astype(BF) for h in heads}

        def s_first():
            w["m"] = {h: _mm(w["nb"][h], w["nb"][h]) for h in heads}
            w["T"] = {h: eye_t - w["nm"][h] for h in heads}

        def s_level(s):
            xb = {h: jnp.concatenate([w["m"][h], w["T"][h]], axis=0).astype(BF) for h in heads}
            if s < n_factors - 1:
                prod = {h: _mm(xb[h], xb[h][:tm]) for h in heads}
                w["T"] = {h: w["T"][h] + prod[h][tm:] for h in heads}
                w["m"] = {h: prod[h][:tm] for h in heads}
            else:
                w["T"] = {h: w["T"][h] + _mm(xb[h][tm:], xb[h][:tm]) for h in heads}

        def s_y0():
            w["tb"] = {h: w["T"][h].astype(BF) for h in heads}
            w["y0"] = {h: _mm(w["tb"][h], w["rhs"][h].astype(BF)) for h in heads}

        def s_res():
            res = {}
            for h in heads:
                nbf = w["nb"][h].astype(F32)
                nx = jnp.concatenate([nbf, w["nm"][h] - nbf], axis=0).astype(BF)
                yh = w["y0"][h].astype(BF)
                yl = (w["y0"][h] - yh.astype(F32)).astype(BF)
                p1 = _mm(nx, yh)
                res[h] = w["rhs"][h] - w["y0"][h] - (p1[:tm] + p1[tm:] + _mm(nx[:tm], yl))
            w["res"] = res

        def s_fix():
            w["y"] = {h: w["y0"][h] + _mm(w["tb"][h], w["res"][h].astype(BF)) for h in heads}

        def c_prep():
            y = w["y"]
            w["u"] = {h: y[h][:, :GDN_DV] for h in heads}
            w["wq"] = {}
            for h in heads:
                qd = w["qn"][h] * w["eg"][h]
                parts = []
                for ci in range(n_chunks):
                    parts += [y[h][ci * L:(ci + 1) * L, GDN_DV:], qd[ci * L:(ci + 1) * L]]
                w["wq"][h] = jnp.concatenate(parts, axis=0).astype(BF)
            kdb = {h: (w["kn"][h] * jnp.exp(w["glast"][h] - w["gc"][h])).astype(BF) for h in heads}
            w["gtot"] = {h: jnp.exp(w["glast"][h]) for h in heads}
            kdt = {h: _nt(eye_k, kdb[h]) for h in heads}
            w["kdt"] = {h: [jnp.where(col_chunk == ci, kdt[h], 0.0).astype(BF) for ci in range(n_chunks)]
                        for h in heads}
            w["st"] = {h: s_scr[h] for h in heads}
            w["outs"] = {h: [] for h in heads}

        def c_chunk(ci):
            r0 = ci * L
            st = w["st"]
            ws = {h: _mm(w["wq"][h][2 * r0:2 * r0 + 2 * L], st[h].astype(BF)) for h in heads}
            for h in heads:
                vnew_scr[h, r0:r0 + L, :] = (w["u"][h][r0:r0 + L] - ws[h][:L]).astype(BF)
            for h in heads:
                w["outs"][h].append(ws[h][L:] + _mm(w["attn"][h][r0:r0 + L, :], vnew_scr[h]))
            w["st"] = {h: st[h] * w["gtot"][h][r0:r0 + 1, :] + _mm(w["kdt"][h][ci], vnew_scr[h]) for h in heads}

        def c_out():
            for h in heads:
                s_scr[h] = w["st"][h]
                o = w["outs"][h][0] if n_chunks == 1 else jnp.concatenate(w["outs"][h], axis=0)
                zz = z_ref[:, h * GDN_DV:(h + 1) * GDN_DV].astype(F32)
                o_ref[:, h * GDN_DV:(h + 1) * GDN_DV] = (_rms(o, gout) * (zz * _sigmoid(zz))).astype(BF)

        prep = [p_q, p_k, p_v, p_gate, p_norm, p_mat]
        solve = ([s_first] + [functools.partial(s_level, s) for s in range(n_factors)] + [s_y0, s_res, s_fix])
        scan = [c_prep] + [functools.partial(c_chunk, ci) for ci in range(n_chunks)] + [c_out]
        return [prep, solve, scan]

    nw = min(GDN_WAVES, hg)
    per = hg // nw
    waves = [make_wave(list(range(i * per, (i + 1) * per))) for i in range(nw)]
    for slot in range(nw + 2):
        active = [waves[i][slot - i] for i in range(nw) if 0 <= slot - i < 3]
        for stage in _interleave(active):
            stage()

    @pl.when(t == nt - 1)
    def _():
        sout_ref[0] = s_scr[...]


def _gdn(qkv, conv_w, conv_past8, ba, bat, alog_l, dtb_l, alog_c, dtb_c, z, s0, gout, nb):
    t = qkv.shape[0]
    s_len = t // nb
    L = min(s_len, CHUNK)
    tm = _tile(s_len, GDN_TM, mult=L)
    nt = s_len // tm
    hg = GDN_HEADS
    hw = hg * GDN_DK
    body = functools.partial(_gdn_body, tm=tm, L=L, hg=hg)
    rows = lambda off: (lambda b, h, i: (b * nt + i, off + h))
    wcol = lambda off: (lambda b, h, i: (0, off + h))
    pcol = lambda off: (lambda b, h, i: (b, off + h))
    const = lambda b, h, i: (0, 0)
    ng = GDN_HEADS // hg
    return pl.pallas_call(
        body,
        grid=(nb, ng, nt),
        in_specs=[
            pl.BlockSpec((tm, hw), rows(0)),
            pl.BlockSpec((tm, hw), rows(ng)),
            pl.BlockSpec((tm, hw), rows(2 * ng)),
            pl.BlockSpec((CONV_W, hw), wcol(0)),
            pl.BlockSpec((CONV_W, hw), wcol(ng)),
            pl.BlockSpec((CONV_W, hw), wcol(2 * ng)),
            pl.BlockSpec((HALO, hw), pcol(0)),
            pl.BlockSpec((HALO, hw), pcol(ng)),
            pl.BlockSpec((HALO, hw), pcol(2 * ng)),
            pl.BlockSpec((tm, LANE), lambda b, h, i: (b * nt + i, 0)),
            pl.BlockSpec((1, 16, tm), lambda b, h, i: (b, 0, i)),
            pl.BlockSpec((1, LANE), const),
            pl.BlockSpec((1, LANE), const),
            pl.BlockSpec((16, 1), const),
            pl.BlockSpec((16, 1), const),
            pl.BlockSpec((tm, hw), rows(0)),
            pl.BlockSpec((1, hg, GDN_DK, GDN_DV), lambda b, h, i: (b, h, 0, 0)),
            pl.BlockSpec((1, GDN_DV), const),
        ],
        out_specs=[
            pl.BlockSpec((tm, hw), rows(0)),
            pl.BlockSpec((1, hg, GDN_DK, GDN_DV), lambda b, h, i: (b, h, 0, 0)),
        ],
        out_shape=[
            jax.ShapeDtypeStruct((t, GDN_HEADS * GDN_DV), BF),
            jax.ShapeDtypeStruct((nb, GDN_HEADS, GDN_DK, GDN_DV), F32),
        ],
        scratch_shapes=[
            pltpu.VMEM((hg, GDN_DK, GDN_DV), F32),
            pltpu.VMEM((3, HALO, hw), F32),
            pltpu.VMEM((hg, tm, GDN_DV), BF),
        ],
        compiler_params=_params(("arbitrary", "arbitrary", "arbitrary")),
        name="gdn",
    )(qkv, qkv, qkv, conv_w, conv_w, conv_w, conv_past8, conv_past8, conv_past8,
      ba, bat, alog_l, dtb_l, alog_c, dtb_c, z, s0, gout)


def _mix_mem_body(x_ref, a1_ref, a2_ref, wo1_ref, wo2_ref, gq_ref, wmq_ref, gmq_ref,
                  mk_ref, mv_ref, wmo_ref, o_ref):
    h1 = x_ref[...] + _mm(a1_ref[...], wo1_ref[...]) + _mm(a2_ref[...], wo2_ref[...])
    hn = _rms(h1, gq_ref[...]).astype(BF)
    q = _mm(hn, wmq_ref[...])
    gmq = gmq_ref[...]
    mk = mk_ref[0]
    mv = mv_ref[0]
    outs = []
    for h in range(MEM_HEADS):
        sl = slice(h * MEM_DIM, (h + 1) * MEM_DIM)
        qh = _rms(q[:, sl], gmq).astype(BF)
        s = _nt(qh, mk[:, sl]) * (MEM_DIM ** -0.5)
        p = jnp.exp(s - jnp.max(s, axis=-1, keepdims=True))
        p = p / jnp.sum(p, axis=-1, keepdims=True)
        outs.append(_mm(p.astype(BF), mv[:, sl]))
    o = jnp.concatenate(outs, axis=1).astype(BF)
    o_ref[...] = h1 + _mm(o, wmo_ref[...])


def _mix_mem(x, a1, a2, wo1, wo2, gq, wmq, gmq, mk, mv, wmo, nb):
    t, d = x.shape
    s_len = t // nb
    tm = _tile(s_len, MIX_TM)
    nt = s_len // tm
    n_mem = mk.shape[1]
    mw = MEM_HEADS * MEM_DIM
    row = lambda i: (i, 0)
    const = lambda i: (0, 0)
    return pl.pallas_call(
        _mix_mem_body,
        grid=(t // tm,),
        in_specs=[
            pl.BlockSpec((tm, d), row),
            pl.BlockSpec((tm, a1.shape[1]), row),
            pl.BlockSpec((tm, a2.shape[1]), row),
            pl.BlockSpec(wo1.shape, const),
            pl.BlockSpec(wo2.shape, const),
            pl.BlockSpec((1, d), const),
            pl.BlockSpec(wmq.shape, const),
            pl.BlockSpec((1, MEM_DIM), const),
            pl.BlockSpec((1, n_mem, mw), lambda i: (i // nt, 0, 0)),
            pl.BlockSpec((1, n_mem, mw), lambda i: (i // nt, 0, 0)),
            pl.BlockSpec(wmo.shape, const),
        ],
        out_specs=pl.BlockSpec((tm, d), row),
        out_shape=jax.ShapeDtypeStruct((t, d), F32),
        compiler_params=_params(("arbitrary",)),
        name="mix_mem",
    )(x, a1, a2, wo1, wo2, gq, wmq, gmq, mk, mv, wmo)


def _ffn_body(x_ref, g_ref, w1_ref, w2_ref, o_ref, hn_scr):
    f = pl.program_id(1)

    @pl.when(f == 0)
    def _():
        x = x_ref[...]
        hn_scr[...] = _rms(x, g_ref[...]).astype(BF)
        o_ref[...] = x

    a = jnp.maximum(_mm(hn_scr[...], w1_ref[...]), 0.0)
    o_ref[...] += _mm((a * a).astype(BF), w2_ref[...])


def _ffn(x, g, w1, w2):
    t, d = x.shape
    dff = w1.shape[1]
    tm = _tile(t, FFN_TM)
    tf = _tile(dff, FFN_TF, mult=LANE)
    return pl.pallas_call(
        _ffn_body,
        grid=(t // tm, dff // tf),
        in_specs=[
            pl.BlockSpec((tm, d), lambda i, f: (i, 0), pipeline_mode=pl.Buffered(1)),
            pl.BlockSpec((1, d), lambda i, f: (0, 0)),
            pl.BlockSpec((d, tf), lambda i, f: (0, f)),
            pl.BlockSpec((tf, d), lambda i, f: (f, 0)),
        ],
        out_specs=pl.BlockSpec((tm, d), lambda i, f: (i, 0)),
        out_shape=jax.ShapeDtypeStruct((t, d), F32),
        scratch_shapes=[pltpu.VMEM((tm, d), BF)],
        compiler_params=_params(("arbitrary", "arbitrary")),
        name="ffn",
    )(x, g, w1, w2)


def _mem_kv_body(m_ref, g_ref, wk_ref, wv_ref, gk_ref, k_ref, v_ref):
    mn = _rms(m_ref[...], g_ref[...]).astype(BF)
    k = _mm(mn, wk_ref[...])
    gk = gk_ref[...]
    for h in range(MEM_HEADS):
        sl = slice(h * MEM_DIM, (h + 1) * MEM_DIM)
        k_ref[:, sl] = _rms(k[:, sl], gk)
    v_ref[...] = _mm(mn, wv_ref[...])


def _mem_kv(mem, g, wk, wv, gk):
    t, d = mem.shape
    tm = _tile(t, 256)
    mw = MEM_HEADS * MEM_DIM
    const = lambda i: (0, 0)
    return pl.pallas_call(
        _mem_kv_body,
        grid=(t // tm,),
        in_specs=[
            pl.BlockSpec((tm, d), lambda i: (i, 0)),
            pl.BlockSpec((1, d), const),
            pl.BlockSpec(wk.shape, const),
            pl.BlockSpec(wv.shape, const),
            pl.BlockSpec((1, MEM_DIM), const),
        ],
        out_specs=[pl.BlockSpec((tm, mw), lambda i: (i, 0)), pl.BlockSpec((tm, mw), lambda i: (i, 0))],
        out_shape=[jax.ShapeDtypeStruct((t, mw), F32), jax.ShapeDtypeStruct((t, mw), F32)],
        compiler_params=_params(("arbitrary",)),
        name="mem_kv",
    )(mem, g, wk, wv, gk)


def _rope_tables(pos):
    half = QK_ROPE // 2
    inv_freq = ROPE_THETA ** (-jnp.arange(half, dtype=F32) / half)
    ang = pos.astype(F32)[:, None] * inv_freq[None, :]
    cos, sin = jnp.cos(ang), jnp.sin(ang)
    zeros = jnp.zeros((pos.shape[0], LANE - QK_ROPE), F32)
    return (jnp.concatenate([cos, cos, zeros], axis=1), jnp.concatenate([-sin, sin, zeros], axis=1))


def _pad_lanes(v, n=LANE):
    return jnp.pad(v, ((0, 0), (0, n - v.shape[1])))


def _prep_weights(w_in, g_cq, w_uq, g_ckv, w_ukv, g_q, g_k, w_o, w_mq, w_mo, w_ff1, w_ff2, w_mk, w_mv):
    d = w_in.shape[0]
    n_conv = 2 * GDN_HEADS * GDN_DK + GDN_HEADS * GDN_DV
    n_z = GDN_HEADS * GDN_DV
    o = 0
    w_cq = w_in[:, o:o + Q_RANK]; o += Q_RANK
    w_ckv = w_in[:, o:o + KV_RANK]; o += KV_RANK
    w_kpe = w_in[:, o:o + QK_ROPE]; o += QK_ROPE
    w_qkv = w_in[:, o:o + n_conv]; o += n_conv
    w_z = w_in[:, o:o + n_z]; o += n_z
    w_b = w_in[:, o:o + GDN_HEADS]; o += GDN_HEADS
    w_a = w_in[:, o:o + GDN_HEADS]
    half = QK_ROPE // 2
    swap = jnp.concatenate([jnp.arange(half, QK_ROPE), jnp.arange(0, half)])
    z64 = jnp.zeros((d, LANE - QK_ROPE), w_in.dtype)
    misc = jnp.concatenate([w_kpe, z64, w_kpe[:, swap], z64, w_b, w_a], axis=1)
    misc = _pad_lanes(misc, COL_TILE)
    w_in_p = jnp.concatenate([w_cq, w_ckv, w_qkv, w_z, misc], axis=1).astype(BF)
    wbat = jnp.concatenate([w_b, w_a], axis=1).T.astype(BF)

    r = w_uq.shape[0]
    wq3 = w_uq.reshape(r, MLA_HEADS, QK_HEAD)
    zq = jnp.zeros((r, MLA_HEADS, LANE - QK_ROPE), w_uq.dtype)
    w1 = jnp.concatenate([wq3, zq], axis=2).reshape(r, MLA_HEADS * QPAD).astype(BF)
    w2 = jnp.concatenate([wq3[:, :, QK_NOPE:][:, :, swap], zq], axis=2).reshape(r, MLA_HEADS * LANE).astype(BF)

    dmla = MLA_HEADS * V_DIM
    return dict(
        n_conv=n_conv, n_z=n_z, w_in_p=w_in_p, wbat=wbat, w1=w1, w2=w2,
        gcq=g_cq[None, :], gckv=g_ckv[None, :],
        gq_n=g_q[None, :QK_NOPE], gq_r=_pad_lanes(g_q[None, QK_NOPE:]),
        gk_n=g_k[None, :QK_NOPE], gk_r=_pad_lanes(g_k[None, QK_NOPE:]),
        w_ukv=w_ukv.astype(BF), wo1=w_o[:dmla].astype(BF), wo2=w_o[dmla:].astype(BF),
        w_mq=w_mq.astype(BF), w_mo=w_mo.astype(BF), w_ff1=w_ff1.astype(BF), w_ff2=w_ff2.astype(BF),
        w_mk=w_mk.astype(BF), w_mv=w_mv.astype(BF),
    )


def _gate_params(a_log, dt_bias):
    z8 = jnp.zeros((GDN_HEADS,), F32)
    al = jnp.concatenate([z8, a_log.astype(F32)])
    db = jnp.concatenate([z8, dt_bias.astype(F32)])
    return _pad_lanes(al[None, :]), _pad_lanes(db[None, :]), al[:, None], db[:, None]


def _layer(x, pos, past, conv_past, s0, mem_k, mem_v, wp, lw):
    nb, s_len, d = x.shape
    t = nb * s_len
    xf = x.reshape(t, d)
    ctab, stab = _rope_tables(pos)
    if ctab.shape[0] % 8 != 0 or (s_len < IN_TM and nb > 1):
        ctab, stab = jnp.tile(ctab, (nb, 1)), jnp.tile(stab, (nb, 1))

    cqn, ckv, kpe_pad, qkv, z, ba, bat = _in_proj(
        xf, lw["g_norm_mix"], wp["w_in_p"], wp["wbat"], wp["gcq"], wp["gckv"], ctab, stab,
        wp["n_conv"], wp["n_z"])
    q = _q_proj(cqn, wp["w1"], wp["w2"], ctab, stab, wp["gq_n"], wp["gq_r"])

    if past is None:
        ckv_all, kpe_all, q_off, t_valid = ckv, kpe_pad, 0, s_len
    else:
        ckv_past, kpe_past = past
        p_len = ckv_past.shape[1]
        t_valid = p_len + s_len
        t_pad = -(-t_valid // ATTN_TK) * ATTN_TK
        ckv_all = jnp.concatenate(
            [ckv_past.astype(F32), ckv.reshape(nb, s_len, KV_RANK),
             jnp.zeros((nb, t_pad - t_valid, KV_RANK), F32)], axis=1).reshape(nb * t_pad, KV_RANK)
        kpe_all = jnp.concatenate(
            [jnp.pad(kpe_past.astype(F32), ((0, 0), (0, 0), (0, LANE - QK_ROPE))),
             kpe_pad.reshape(nb, s_len, LANE),
             jnp.zeros((nb, t_pad - t_valid, LANE), F32)], axis=1).reshape(nb * t_pad, LANE)
        q_off = p_len
    k, v = _kv_proj(ckv_all, kpe_all, wp["w_ukv"], wp["gk_n"], wp["gk_r"])
    o_mla = _attention(q, k, v, lw["g_mla_out"], nb, q_off, t_valid)

    conv_past8 = jnp.pad(conv_past.astype(F32), ((0, 0), (HALO - (CONV_W - 1), 0), (0, 0)))
    conv_past8 = conv_past8.reshape(nb * HALO, -1)
    bat3 = bat.reshape(16, nb, s_len).transpose(1, 0, 2)
    alog_l, dtb_l, alog_c, dtb_c = _gate_params(lw["a_log"], lw["dt_bias"])
    o_gdn, s_new = _gdn(qkv, lw["conv_w"], conv_past8, ba, bat3, alog_l, dtb_l, alog_c, dtb_c,
                        z, s0.astype(F32), lw["g_gdn_out"], nb)

    h2 = _mix_mem(xf, o_mla, o_gdn, wp["wo1"], wp["wo2"], lw["g_norm_mem_q"], wp["w_mq"],
                  lw["g_mq"], mem_k, mem_v, wp["w_mo"], nb)
    y = _ffn(h2, lw["g_norm_ffn"], wp["w_ff1"], wp["w_ff2"])

    conv_in_tail = jnp.concatenate([conv_past.astype(F32), qkv.reshape(nb, s_len, -1)[:, -(CONV_W - 1):]], axis=1)
    conv_new = conv_in_tail[:, -(CONV_W - 1):]
    return (y.reshape(nb, s_len, d), ckv.reshape(nb, s_len, KV_RANK),
            kpe_pad[:, :QK_ROPE].reshape(nb, s_len, QK_ROPE), conv_new, s_new)


def kernel(x_prompt, x_sample, mem_prompt, cache_mla_ckv, cache_mla_kpe, cache_gdn_conv, state_gdn, cache_mem_k, cache_mem_v, g_norm_mix, w_in, g_cq, w_uq, g_ckv, w_ukv, g_q_mla, g_k_mla, g_mla_out, conv_w, a_log, dt_bias, g_gdn_out, w_o, g_norm_mem_q, g_norm_mem_kv, w_mq, w_mk, w_mv, g_mq, g_mk, w_mo, g_norm_ffn, w_ff1, w_ff2):
    depth = w_in.shape[0]
    nbp, sp, d = x_prompt.shape
    nbs, ss, _ = x_sample.shape
    n_mem = mem_prompt.shape[1]
    mw = MEM_HEADS * MEM_DIM
    n_conv = 2 * GDN_HEADS * GDN_DK + GDN_HEADS * GDN_DV
    pos_p = jnp.arange(sp)
    pos_s = cache_mla_ckv.shape[2] + jnp.arange(ss)
    zeros_conv = jnp.zeros((nbp, CONV_W - 1, n_conv), F32)
    zeros_state = jnp.zeros((nbp, GDN_HEADS, GDN_DK, GDN_DV), F32)
    hp, hs = x_prompt, x_sample
    outs_p = [[] for _ in range(6)]
    outs_s = [[] for _ in range(4)]
    for l in range(depth):
        wp = _prep_weights(w_in[l], g_cq[l], w_uq[l], g_ckv[l], w_ukv[l], g_q_mla[l], g_k_mla[l], w_o[l],
                           w_mq[l], w_mo[l], w_ff1[l], w_ff2[l], w_mk[l], w_mv[l])
        lw = dict(g_norm_mix=g_norm_mix[l][None, :], g_mla_out=g_mla_out[l][None, :], conv_w=conv_w[l],
                  a_log=a_log[l], dt_bias=dt_bias[l], g_gdn_out=g_gdn_out[l][None, :],
                  g_norm_mem_q=g_norm_mem_q[l][None, :], g_mq=g_mq[l][None, :],
                  g_norm_ffn=g_norm_ffn[l][None, :])
        mk, mv = _mem_kv(mem_prompt.reshape(nbp * n_mem, d), g_norm_mem_kv[l][None, :], wp["w_mk"], wp["w_mv"],
                         g_mk[l][None, :])
        mk3, mv3 = mk.reshape(nbp, n_mem, mw), mv.reshape(nbp, n_mem, mw)
        hp, c1, c2, c3, c4 = _layer(hp, pos_p, None, zeros_conv, zeros_state, mk3.astype(BF), mv3.astype(BF), wp, lw)
        for lst, val in zip(outs_p, (c1, c2, c3, c4, mk3.reshape(nbp, n_mem, MEM_HEADS, MEM_DIM),
                                     mv3.reshape(nbp, n_mem, MEM_HEADS, MEM_DIM))):
            lst.append(val)
        hs, d1, d2, d3, d4 = _layer(hs, pos_s, (cache_mla_ckv[l], cache_mla_kpe[l]), cache_gdn_conv[l], state_gdn[l],
                                    cache_mem_k[l].reshape(nbs, n_mem, mw).astype(BF),
                                    cache_mem_v[l].reshape(nbs, n_mem, mw).astype(BF), wp, lw)
        for lst, val in zip(outs_s, (d1, d2, d3, d4)):
            lst.append(val)
    return (hp, hs, *(jnp.stack(v) for v in outs_p), *(jnp.stack(v) for v in outs_s))
```

```python
import functools
import math

import jax
import jax.numpy as jnp
import numpy as np
from jax import lax
from jax.experimental import pallas as pl
from jax.experimental.pallas import tpu as pltpu

F32 = jnp.float32
BF = jnp.bfloat16

EPS = 1e-6
CHUNK = 64
ROPE_THETA = 10000.0
MLA_HEADS = 8
QK_NOPE = 128
QK_ROPE = 64
QK_HEAD = QK_NOPE + QK_ROPE
V_DIM = 128
Q_RANK = 512
KV_RANK = 512
GDN_HEADS = 8
GDN_DK = 128
GDN_DV = 128
CONV_W = 4
MEM_HEADS = 4
MEM_DIM = 128
LANE = 128
QPAD = 2 * LANE
COL_TILE = 512
HALO = 8
NEG_BIG = -1e30
QSCALE = (QK_HEAD ** -0.5) * math.log2(math.e)

VMEM_LIMIT = 56 * 1024 * 1024

IN_TM = 1024
PROJ_TM = 512
ATTN_TQ = 1024
ATTN_TK = 1024
ATTN_WIDE = 2
GDN_TM = 256
GDN_WAVES = 2
MIX_TM = 512
FFN_TM = 1024
FFN_TF = 512


def _tile(n, pref, mult=8):
    if n <= pref:
        return n
    t = (pref // mult) * mult
    while t >= mult:
        if n % t == 0:
            return t
        t -= mult
    return n


def _nt(a, b):
    return lax.dot_general(a, b, (((1,), (1,)), ((), ())), preferred_element_type=F32)


def _mm(a, b):
    return jnp.dot(a, b, preferred_element_type=F32)


def _sigmoid(x):
    return 1.0 / (1.0 + jnp.exp(-x))


def _softplus(x):
    return jnp.maximum(x, 0.0) + jnp.log(1.0 + jnp.exp(-jnp.abs(x)))


def _rms(x, g, n=None):
    n = x.shape[-1] if n is None else n
    ms = jnp.sum(x * x, axis=-1, keepdims=True) * (1.0 / n)
    return (x * lax.rsqrt(ms + EPS)) * g


def _split3(x):
    hi = x.astype(BF)
    r1 = x - hi.astype(F32)
    mid = r1.astype(BF)
    lo = (r1 - mid.astype(F32)).astype(BF)
    return hi, mid, lo


def _interleave(lists):
    items = []
    for li, lst in enumerate(lists):
        items += [((i + 0.5) / len(lst), li, f) for i, f in enumerate(lst)]
    return [f for _, _, f in sorted(items, key=lambda it: (it[0], it[1]))]


def _params(sem):
    return pltpu.CompilerParams(dimension_semantics=sem, vmem_limit_bytes=VMEM_LIMIT)


def _in_proj_body(x_ref, g_ref, w_ref, wbat_ref, gcq_ref, gckv_ref, ctab_ref, stab_ref,
                  cqn_ref, ckv_ref, kpe_ref, qkv_ref, z_ref, ba_ref, bat_ref, xn_scr, *, nqkv, nz):
    j = pl.program_id(1)

    @pl.when(j == 0)
    def _():
        xn_scr[...] = _rms(x_ref[...], g_ref[...]).astype(BF)

    acc = _mm(xn_scr[...], w_ref[...])

    @pl.when(j == 0)
    def _():
        cqn_ref[...] = _rms(acc, gcq_ref[...]).astype(BF)

    @pl.when(j == 1)
    def _():
        ckv_ref[...] = _rms(acc, gckv_ref[...])

    @pl.when((j >= 2) & (j < 2 + nqkv))
    def _():
        qkv_ref[...] = acc

    @pl.when((j >= 2 + nqkv) & (j < 2 + nqkv + nz))
    def _():
        z_ref[...] = acc.astype(BF)

    @pl.when(j == 2 + nqkv + nz)
    def _():
        kpe_ref[...] = acc[:, 0:LANE] * ctab_ref[...] + acc[:, LANE:2 * LANE] * stab_ref[...]
        ba_ref[...] = acc[:, 2 * LANE:3 * LANE]
        bat_ref[...] = _nt(wbat_ref[...], xn_scr[...])


def _in_proj(x, g, w_p, wbat, gcq, gckv, ctab, stab, n_conv, n_z):
    t, d = x.shape
    tm = _tile(min(t, ctab.shape[0]), IN_TM)
    ntab = ctab.shape[0] // tm
    nqkv = n_conv // COL_TILE
    nz = n_z // COL_TILE
    ncol = 2 + nqkv + nz + 1
    assert w_p.shape[1] == ncol * COL_TILE
    row = lambda i, j: (i, 0)
    const = lambda i, j: (0, 0)
    body = functools.partial(_in_proj_body, nqkv=nqkv, nz=nz)
    return pl.pallas_call(
        body,
        grid=(t // tm, ncol),
        in_specs=[
            pl.BlockSpec((tm, d), row),
            pl.BlockSpec((1, d), const),
            pl.BlockSpec((d, COL_TILE), lambda i, j: (0, j)),
            pl.BlockSpec((16, d), const),
            pl.BlockSpec((1, Q_RANK), const),
            pl.BlockSpec((1, KV_RANK), const),
            pl.BlockSpec((tm, LANE), lambda i, j: (i % ntab, 0)),
            pl.BlockSpec((tm, LANE), lambda i, j: (i % ntab, 0)),
        ],
        out_specs=[
            pl.BlockSpec((tm, Q_RANK), row),
            pl.BlockSpec((tm, KV_RANK), row),
            pl.BlockSpec((tm, LANE), row),
            pl.BlockSpec((tm, COL_TILE), lambda i, j: (i, jnp.clip(j - 2, 0, nqkv - 1))),
            pl.BlockSpec((tm, COL_TILE), lambda i, j: (i, jnp.clip(j - 2 - nqkv, 0, nz - 1))),
            pl.BlockSpec((tm, LANE), row),
            pl.BlockSpec((16, tm), lambda i, j: (0, i)),
        ],
        out_shape=[
            jax.ShapeDtypeStruct((t, Q_RANK), BF),
            jax.ShapeDtypeStruct((t, KV_RANK), F32),
            jax.ShapeDtypeStruct((t, LANE), F32),
            jax.ShapeDtypeStruct((t, n_conv), F32),
            jax.ShapeDtypeStruct((t, n_z), BF),
            jax.ShapeDtypeStruct((t, LANE), F32),
            jax.ShapeDtypeStruct((16, t), F32),
        ],
        scratch_shapes=[pltpu.VMEM((tm, d), BF)],
        compiler_params=_params(("arbitrary", "arbitrary")),
        name="in_proj",
    )(x, g, w_p, wbat, gcq, gckv, ctab, stab)


def _q_proj_body(c_ref, w1_ref, w2_ref, ctab_ref, stab_ref, gn_ref, gr_ref, q_ref):
    c = c_ref[...]
    qf = _mm(c, w1_ref[...])
    qs = _mm(c, w2_ref[...])
    ct = ctab_ref[...]
    st = stab_ref[...]
    gn = gn_ref[...]
    gr = gr_ref[...]
    for h in range(MLA_HEADS):
        nope = qf[:, h * QPAD:h * QPAD + LANE]
        rot = qf[:, h * QPAD + LANE:(h + 1) * QPAD] * ct + qs[:, h * LANE:(h + 1) * LANE] * st
        ss = jnp.sum(nope * nope, axis=-1, keepdims=True) + jnp.sum(rot * rot, axis=-1, keepdims=True)
        rs = lax.rsqrt(ss * (1.0 / QK_HEAD) + EPS) * QSCALE
        q_ref[:, h * QPAD:h * QPAD + LANE] = (nope * rs * gn).astype(BF)
        q_ref[:, h * QPAD + LANE:(h + 1) * QPAD] = (rot * rs * gr).astype(BF)


def _q_proj(cqn, w1, w2, ctab, stab, gn, gr):
    t = cqn.shape[0]
    tm = _tile(min(t, ctab.shape[0]), PROJ_TM)
    ntab = ctab.shape[0] // tm
    const = lambda i: (0, 0)
    return pl.pallas_call(
        _q_proj_body,
        grid=(t // tm,),
        in_specs=[
            pl.BlockSpec((tm, Q_RANK), lambda i: (i, 0)),
            pl.BlockSpec(w1.shape, const),
            pl.BlockSpec(w2.shape, const),
            pl.BlockSpec((tm, LANE), lambda i: (i % ntab, 0)),
            pl.BlockSpec((tm, LANE), lambda i: (i % ntab, 0)),
            pl.BlockSpec((1, LANE), const),
            pl.BlockSpec((1, LANE), const),
        ],
        out_specs=pl.BlockSpec((tm, MLA_HEADS * QPAD), lambda i: (i, 0)),
        out_shape=jax.ShapeDtypeStruct((t, MLA_HEADS * QPAD), BF),
        compiler_params=_params(("arbitrary",)),
        name="q_proj",
    )(cqn, w1, w2, ctab, stab, gn, gr)


def _kv_proj_body(c_ref, kpe_ref, w_ref, gn_ref, gr_ref, k_ref, v_ref):
    kv = _mm(c_ref[...].astype(BF), w_ref[...])
    kp = kpe_ref[...]
    kps = jnp.sum(kp * kp, axis=-1, keepdims=True)
    gn = gn_ref[...]
    gr = gr_ref[...]
    for h in range(MLA_HEADS):
        kn = kv[:, h * 2 * LANE:h * 2 * LANE + LANE]
        rs = lax.rsqrt((jnp.sum(kn * kn, axis=-1, keepdims=True) + kps) * (1.0 / QK_HEAD) + EPS)
        k_ref[:, h * QPAD:h * QPAD + LANE] = (kn * rs * gn).astype(BF)
        k_ref[:, h * QPAD + LANE:(h + 1) * QPAD] = (kp * rs * gr).astype(BF)
        v_ref[:, h * V_DIM:(h + 1) * V_DIM] = kv[:, h * 2 * LANE + LANE:(h + 1) * 2 * LANE].astype(BF)


def _kv_proj(ckv, kpe_pad, w, gn, gr):
    t = ckv.shape[0]
    tm = _tile(t, PROJ_TM)
    const = lambda i: (0, 0)
    return pl.pallas_call(
        _kv_proj_body,
        grid=(t // tm,),
        in_specs=[
            pl.BlockSpec((tm, KV_RANK), lambda i: (i, 0)),
            pl.BlockSpec((tm, LANE), lambda i: (i, 0)),
            pl.BlockSpec(w.shape, const),
            pl.BlockSpec((1, LANE), const),
            pl.BlockSpec((1, LANE), const),
        ],
        out_specs=[
            pl.BlockSpec((tm, MLA_HEADS * QPAD), lambda i: (i, 0)),
            pl.BlockSpec((tm, MLA_HEADS * V_DIM), lambda i: (i, 0)),
        ],
        out_shape=[
            jax.ShapeDtypeStruct((t, MLA_HEADS * QPAD), BF),
            jax.ShapeDtypeStruct((t, MLA_HEADS * V_DIM), BF),
        ],
        compiler_params=_params(("arbitrary",)),
        name="kv_proj",
    )(ckv, kpe_pad, w, gn, gr)


def _attn_body(q_ref, k_ref, v_ref, g_ref, o_ref, m_scr, l_scr, acc_scr, *, tq, tk, wide, q_off, t_valid):
    qi = pl.program_id(2)
    qpos0 = q_off + qi * tq
    n_full = jnp.minimum((qpos0 // CHUNK * CHUNK + CHUNK) // tk, t_valid // tk)
    hi = jnp.minimum((qpos0 + tq - 1) // CHUNK * CHUNK + CHUNK, t_valid)
    n_total = (hi + tk - 1) // tk

    m_scr[...] = jnp.full(m_scr.shape, NEG_BIG, F32)
    l_scr[...] = jnp.zeros(l_scr.shape, F32)
    acc_scr[...] = jnp.zeros(acc_scr.shape, F32)
    diag = q_off % tk == 0 and tq == tk and t_valid % tk == 0 and tk % (2 * LANE) == 0 and tq % 32 == 0

    def step(kc, width, masked, r0=0, nr=tq):
        rows = slice(r0, r0 + nr)
        k0 = pl.multiple_of(kc * tk, tk)
        s = _nt(q_ref[rows, :], k_ref[pl.ds(k0, width), :])
        if masked:
            qpos = qpos0 + r0 + lax.broadcasted_iota(jnp.int32, (nr, 1), 0)
            last = jnp.minimum(qpos | (CHUNK - 1), t_valid - 1) - k0
            s = jnp.where(lax.broadcasted_iota(jnp.int32, (nr, width), 1) <= last, s, NEG_BIG)
        m_prev = m_scr[rows, :]
        m_new = jnp.maximum(m_prev, jnp.max(s, axis=-1, keepdims=True))
        alpha = jnp.exp2(m_prev - m_new)
        ps = [jnp.exp2(s[:, j * LANE:(j + 1) * LANE] - m_new) for j in range(width // LANE)]
        psum = ps[0]
        for pj in ps[1:]:
            psum = psum + pj
        l_scr[rows, :] = alpha * l_scr[rows, :] + psum
        p = jnp.concatenate(ps, axis=1).astype(BF) if len(ps) > 1 else ps[0].astype(BF)
        acc_scr[rows, :] = acc_scr[rows, :] * alpha + _mm(p, v_ref[pl.ds(k0, width), :])
        m_scr[rows, :] = m_new

    def loop(lo, hi, fn):
        lax.fori_loop(lo, hi, lambda i, c: (fn(i), c)[1], 0)

    n_wide = n_full // wide
    loop(0, n_wide, lambda i: step(i * wide, wide * tk, False))
    loop(n_wide * wide, n_full, lambda kc: step(kc, tk, False))
    if diag:
        def masked(kc):
            step(kc, tk // 2, True, 0, tq // 2)
            step(kc, tk, True, tq // 2, tq // 2)
    else:
        def masked(kc):
            step(kc, tk, True)
    loop(n_full, n_total, masked)
    o = acc_scr[...] / jnp.sum(l_scr[...], axis=-1, keepdims=True)
    o_ref[...] = _rms(o, g_ref[...]).astype(BF)


def _attention(q, k, v, g_out, nb, q_off, t_valid):
    tq_total = q.shape[0] // nb
    tk_total = k.shape[0] // nb
    tq = _tile(tq_total, ATTN_TQ)
    tk = _tile(tk_total, ATTN_TK, mult=LANE)
    nq = tq_total // tq
    body = functools.partial(_attn_body, tq=tq, tk=tk, wide=ATTN_WIDE, q_off=q_off, t_valid=t_valid)
    return pl.pallas_call(
        body,
        grid=(nb, MLA_HEADS, nq),
        in_specs=[
            pl.BlockSpec((tq, QPAD), lambda b, h, i: (b * nq + i, h)),
            pl.BlockSpec((tk_total, QPAD), lambda b, h, i: (b, h)),
            pl.BlockSpec((tk_total, V_DIM), lambda b, h, i: (b, h)),
            pl.BlockSpec((1, V_DIM), lambda b, h, i: (0, 0)),
        ],
        out_specs=pl.BlockSpec((tq, V_DIM), lambda b, h, i: (b * nq + i, h)),
        out_shape=jax.ShapeDtypeStruct((q.shape[0], MLA_HEADS * V_DIM), BF),
        scratch_shapes=[
            pltpu.VMEM((tq, LANE), F32),
            pltpu.VMEM((tq, LANE), F32),
            pltpu.VMEM((tq, V_DIM), F32),
        ],
        compiler_params=_params(("arbitrary", "arbitrary", "arbitrary")),
        name="mla_attention",
    )(q, k, v, g_out)


def _gdn_body(q_ref, k_ref, v_ref, wq_ref, wk_ref, wv_ref, pq_ref, pk_ref, pv_ref,
              ba_ref, bat_ref, alog_l_ref, dtb_l_ref, alog_c_ref, dtb_c_ref,
              z_ref, s0_ref, gout_ref,
              o_ref, sout_ref,
              s_scr, halo_scr, vnew_scr, *, tm, L, hg):
    t = pl.program_id(2)
    nt = pl.num_programs(2)
    n_chunks = tm // L
    n_factors = int(math.log2(L)) - 1

    @pl.when(t == 0)
    def _():
        s_scr[...] = s0_ref[0]
        halo_scr[0] = pq_ref[...]
        halo_scr[1] = pk_ref[...]
        halo_scr[2] = pv_ref[...]
        vnew_scr[...] = jnp.zeros(vnew_scr.shape, BF)

    ba = ba_ref[...]
    beta_all = _sigmoid(ba)
    g_all = -jnp.exp(alog_l_ref[...]) * _softplus(ba + dtb_l_ref[...])
    gt_all = -jnp.exp(alog_c_ref[...]) * _softplus(bat_ref[0] + dtb_c_ref[...])

    r = lax.broadcasted_iota(jnp.int32, (tm, tm), 0)
    c = lax.broadcasted_iota(jnp.int32, (tm, tm), 1)
    same = (r // L) == (c // L)
    lower = same & (c <= r)
    strict = same & (c < r)
    cs = jnp.where(lower, 1.0, 0.0).astype(BF)
    bd = jnp.where(same, 1.0, 0.0).astype(BF)
    g3 = _split3(g_all)
    gcol_all = _mm(cs, g3[0]) + _mm(cs, g3[1]) + _mm(cs, g3[2])
    glast_all = _mm(bd, g3[0]) + _mm(bd, g3[1]) + _mm(bd, g3[2])
    gt3 = _split3(gt_all)
    grow_all = _nt(gt3[0], cs) + _nt(gt3[1], cs) + _nt(gt3[2], cs)

    gout = gout_ref[...]
    eye_t = jnp.where(r == c, 1.0, 0.0)
    eye_k = jnp.where(lax.broadcasted_iota(jnp.int32, (GDN_DK, GDN_DK), 0)
                      == lax.broadcasted_iota(jnp.int32, (GDN_DK, GDN_DK), 1), 1.0, 0.0).astype(BF)
    col_chunk = lax.broadcasted_iota(jnp.int32, (1, tm), 1) // L

    def make_wave(heads):
        c0, c1 = heads[0] * GDN_DK, (heads[-1] + 1) * GDN_DK
        loc = lambda h: slice((h - heads[0]) * GDN_DK, (h - heads[0] + 1) * GDN_DK)
        w = {}

        def conv(idx, x_ref, w_ref):
            xs = jnp.concatenate([halo_scr[idx, :, c0:c1], x_ref[:, c0:c1]], axis=0)
            cw = w_ref[:, c0:c1]
            acc = xs * cw[0:1]
            for i in range(1, CONV_W):
                acc = pltpu.roll(acc, 1, axis=0) + xs * cw[i:i + 1]
            halo_scr[idx, :, c0:c1] = xs[tm:tm + HALO]
            y = acc[HALO:HALO + tm]
            return y * _sigmoid(y)

        def p_q():
            w["yq"] = conv(0, q_ref, wq_ref)

        def p_k():
            w["yk"] = conv(1, k_ref, wk_ref)

        def p_v():
            w["yv"] = conv(2, v_ref, wv_ref)

        def p_gate():
            w["beta"], w["gc"], w["glast"], w["gr"] = {}, {}, {}, {}
            for h in heads:
                g = GDN_HEADS + h
                w["beta"][h] = beta_all[:, h:h + 1]
                w["gc"][h] = gcol_all[:, g:g + 1]
                w["glast"][h] = glast_all[:, g:g + 1]
                w["gr"][h] = grow_all[g:g + 1, :]

        def p_norm():
            w["qn"], w["kn"] = {}, {}
            for h in heads:
                qh = w["yq"][:, loc(h)]
                kh = w["yk"][:, loc(h)]
                w["qn"][h] = qh * lax.rsqrt(jnp.sum(qh * qh, axis=-1, keepdims=True) + EPS) * (GDN_DK ** -0.5)
                w["kn"][h] = kh * lax.rsqrt(jnp.sum(kh * kh, axis=-1, keepdims=True) + EPS)
            kb = {h: w["kn"][h].astype(BF) for h in heads}
            w["kk"] = {h: _nt(kb[h], kb[h]) for h in heads}
            w["qk"] = {h: _nt(w["qn"][h].astype(BF), kb[h]) for h in heads}

        def p_mat():
            beta, gc = w["beta"], w["gc"]
            decay = {h: jnp.where(lower, jnp.exp(gc[h] - w["gr"][h]), 0.0) for h in heads}
            w["nm"] = {h: jnp.where(strict, beta[h] * w["kk"][h] * decay[h], 0.0) for h in heads}
            w["attn"] = {h: (w["qk"][h] * decay[h]).astype(BF) for h in heads}
            w["eg"] = {h: jnp.exp(gc[h]) for h in heads}
            w["rhs"] = {h: jnp.concatenate([w["yv"][:, loc(h)] * beta[h],
                                            w["kn"][h] * (beta[h] * w["eg"][h])], axis=1) for h in heads}
            w["nb"] = {h: w["nm"][h].astype(BF) for h in heads}

        def s_first():
            w["m"] = {h: _mm(w["nb"][h], w["nb"][h]) for h in heads}
            w["T"] = {h: eye_t - w["nm"][h] for h in heads}

        def s_level(s):
            xb = {h: jnp.concatenate([w["m"][h], w["T"][h]], axis=0).astype(BF) for h in heads}
            if s < n_factors - 1:
                prod = {h: _mm(xb[h], xb[h][:tm]) for h in heads}
                w["T"] = {h: w["T"][h] + prod[h][tm:] for h in heads}
                w["m"] = {h: prod[h][:tm] for h in heads}
            else:
                w["T"] = {h: w["T"][h] + _mm(xb[h][tm:], xb[h][:tm]) for h in heads}

        def s_y0():
            w["tb"] = {h: w["T"][h].astype(BF) for h in heads}
            w["y0"] = {h: _mm(w["tb"][h], w["rhs"][h].astype(BF)) for h in heads}

        def s_res():
            res = {}
            for h in heads:
                nbf = w["nb"][h].astype(F32)
                nx = jnp.concatenate([nbf, w["nm"][h] - nbf], axis=0).astype(BF)
                yh = w["y0"][h].astype(BF)
                yl = (w["y0"][h] - yh.astype(F32)).astype(BF)
                p1 = _mm(nx, yh)
                res[h] = w["rhs"][h] - w["y0"][h] - (p1[:tm] + p1[tm:] + _mm(nx[:tm], yl))
            w["res"] = res

        def s_fix():
            w["y"] = {h: w["y0"][h] + _mm(w["tb"][h], w["res"][h].astype(BF)) for h in heads}

        def c_prep():
            y = w["y"]
            w["u"] = {h: y[h][:, :GDN_DV] for h in heads}
            w["wq"] = {}
            for h in heads:
                qd = w["qn"][h] * w["eg"][h]
                parts = []
                for ci in range(n_chunks):
                    parts += [y[h][ci * L:(ci + 1) * L, GDN_DV:], qd[ci * L:(ci + 1) * L]]
                w["wq"][h] = jnp.concatenate(parts, axis=0).astype(BF)
            kdb = {h: (w["kn"][h] * jnp.exp(w["glast"][h] - w["gc"][h])).astype(BF) for h in heads}
            w["gtot"] = {h: jnp.exp(w["glast"][h]) for h in heads}
            kdt = {h: _nt(eye_k, kdb[h]) for h in heads}
            w["kdt"] = {h: [jnp.where(col_chunk == ci, kdt[h], 0.0).astype(BF) for ci in range(n_chunks)]
                        for h in heads}
            w["st"] = {h: s_scr[h] for h in heads}
            w["outs"] = {h: [] for h in heads}

        def c_chunk(ci):
            r0 = ci * L
            st = w["st"]
            ws = {h: _mm(w["wq"][h][2 * r0:2 * r0 + 2 * L], st[h].astype(BF)) for h in heads}
            for h in heads:
                vnew_scr[h, r0:r0 + L, :] = (w["u"][h][r0:r0 + L] - ws[h][:L]).astype(BF)
            for h in heads:
                w["outs"][h].append(ws[h][L:] + _mm(w["attn"][h][r0:r0 + L, :], vnew_scr[h]))
            w["st"] = {h: st[h] * w["gtot"][h][r0:r0 + 1, :] + _mm(w["kdt"][h][ci], vnew_scr[h]) for h in heads}

        def c_out():
            for h in heads:
                s_scr[h] = w["st"][h]
                o = w["outs"][h][0] if n_chunks == 1 else jnp.concatenate(w["outs"][h], axis=0)
                zz = z_ref[:, h * GDN_DV:(h + 1) * GDN_DV].astype(F32)
                o_ref[:, h * GDN_DV:(h + 1) * GDN_DV] = (_rms(o, gout) * (zz * _sigmoid(zz))).astype(BF)

        prep = [p_q, p_k, p_v, p_gate, p_norm, p_mat]
        solve = ([s_first] + [functools.partial(s_level, s) for s in range(n_factors)] + [s_y0, s_res, s_fix])
        scan = [c_prep] + [functools.partial(c_chunk, ci) for ci in range(n_chunks)] + [c_out]
        return [prep, solve, scan]

    nw = min(GDN_WAVES, hg)
    per = hg // nw
    waves = [make_wave(list(range(i * per, (i + 1) * per))) for i in range(nw)]
    for slot in range(nw + 2):
        active = [waves[i][slot - i] for i in range(nw) if 0 <= slot - i < 3]
        for stage in _interleave(active):
            stage()

    @pl.when(t == nt - 1)
    def _():
        sout_ref[0] = s_scr[...]


def _gdn(qkv, conv_w, conv_past8, ba, bat, alog_l, dtb_l, alog_c, dtb_c, z, s0, gout, nb):
    t = qkv.shape[0]
    s_len = t // nb
    L = min(s_len, CHUNK)
    tm = _tile(s_len, GDN_TM, mult=L)
    nt = s_len // tm
    hg = GDN_HEADS
    hw = hg * GDN_DK
    body = functools.partial(_gdn_body, tm=tm, L=L, hg=hg)
    rows = lambda off: (lambda b, h, i: (b * nt + i, off + h))
    wcol = lambda off: (lambda b, h, i: (0, off + h))
    pcol = lambda off: (lambda b, h, i: (b, off + h))
    const = lambda b, h, i: (0, 0)
    ng = GDN_HEADS // hg
    return pl.pallas_call(
        body,
        grid=(nb, ng, nt),
        in_specs=[
            pl.BlockSpec((tm, hw), rows(0)),
            pl.BlockSpec((tm, hw), rows(ng)),
            pl.BlockSpec((tm, hw), rows(2 * ng)),
            pl.BlockSpec((CONV_W, hw), wcol(0)),
            pl.BlockSpec((CONV_W, hw), wcol(ng)),
            pl.BlockSpec((CONV_W, hw), wcol(2 * ng)),
            pl.BlockSpec((HALO, hw), pcol(0)),
            pl.BlockSpec((HALO, hw), pcol(ng)),
            pl.BlockSpec((HALO, hw), pcol(2 * ng)),
            pl.BlockSpec((tm, LANE), lambda b, h, i: (b * nt + i, 0)),
            pl.BlockSpec((1, 16, tm), lambda b, h, i: (b, 0, i)),
            pl.BlockSpec((1, LANE), const),
            pl.BlockSpec((1, LANE), const),
            pl.BlockSpec((16, 1), const),
            pl.BlockSpec((16, 1), const),
            pl.BlockSpec((tm, hw), rows(0)),
            pl.BlockSpec((1, hg, GDN_DK, GDN_DV), lambda b, h, i: (b, h, 0, 0)),
            pl.BlockSpec((1, GDN_DV), const),
        ],
        out_specs=[
            pl.BlockSpec((tm, hw), rows(0)),
            pl.BlockSpec((1, hg, GDN_DK, GDN_DV), lambda b, h, i: (b, h, 0, 0)),
        ],
        out_shape=[
            jax.ShapeDtypeStruct((t, GDN_HEADS * GDN_DV), BF),
            jax.ShapeDtypeStruct((nb, GDN_HEADS, GDN_DK, GDN_DV), F32),
        ],
        scratch_shapes=[
            pltpu.VMEM((hg, GDN_DK, GDN_DV), F32),
            pltpu.VMEM((3, HALO, hw), F32),
            pltpu.VMEM((hg, tm, GDN_DV), BF),
        ],
        compiler_params=_params(("arbitrary", "arbitrary", "arbitrary")),
        name="gdn",
    )(qkv, qkv, qkv, conv_w, conv_w, conv_w, conv_past8, conv_past8, conv_past8,
      ba, bat, alog_l, dtb_l, alog_c, dtb_c, z, s0, gout)


def _mix_mem_body(x_ref, a1_ref, a2_ref, wo1_ref, wo2_ref, gq_ref, wmq_ref, gmq_ref,
                  mk_ref, mv_ref, wmo_ref, o_ref):
    h1 = x_ref[...] + _mm(a1_ref[...], wo1_ref[...]) + _mm(a2_ref[...], wo2_ref[...])
    hn = _rms(h1, gq_ref[...]).astype(BF)
    q = _mm(hn, wmq_ref[...])
    gmq = gmq_ref[...]
    mk = mk_ref[0]
    mv = mv_ref[0]
    outs = []
    for h in range(MEM_HEADS):
        sl = slice(h * MEM_DIM, (h + 1) * MEM_DIM)
        qh = _rms(q[:, sl], gmq).astype(BF)
        s = _nt(qh, mk[:, sl]) * (MEM_DIM ** -0.5)
        p = jnp.exp(s - jnp.max(s, axis=-1, keepdims=True))
        p = p / jnp.sum(p, axis=-1, keepdims=True)
        outs.append(_mm(p.astype(BF), mv[:, sl]))
    o = jnp.concatenate(outs, axis=1).astype(BF)
    o_ref[...] = h1 + _mm(o, wmo_ref[...])


def _mix_mem(x, a1, a2, wo1, wo2, gq, wmq, gmq, mk, mv, wmo, nb):
    t, d = x.shape
    s_len = t // nb
    tm = _tile(s_len, MIX_TM)
    nt = s_len // tm
    n_mem = mk.shape[1]
    mw = MEM_HEADS * MEM_DIM
    row = lambda i: (i, 0)
    const = lambda i: (0, 0)
    return pl.pallas_call(
        _mix_mem_body,
        grid=(t // tm,),
        in_specs=[
            pl.BlockSpec((tm, d), row),
            pl.BlockSpec((tm, a1.shape[1]), row),
            pl.BlockSpec((tm, a2.shape[1]), row),
            pl.BlockSpec(wo1.shape, const),
            pl.BlockSpec(wo2.shape, const),
            pl.BlockSpec((1, d), const),
            pl.BlockSpec(wmq.shape, const),
            pl.BlockSpec((1, MEM_DIM), const),
            pl.BlockSpec((1, n_mem, mw), lambda i: (i // nt, 0, 0)),
            pl.BlockSpec((1, n_mem, mw), lambda i: (i // nt, 0, 0)),
            pl.BlockSpec(wmo.shape, const),
        ],
        out_specs=pl.BlockSpec((tm, d), row),
        out_shape=jax.ShapeDtypeStruct((t, d), F32),
        compiler_params=_params(("arbitrary",)),
        name="mix_mem",
    )(x, a1, a2, wo1, wo2, gq, wmq, gmq, mk, mv, wmo)


def _ffn_body(x_ref, g_ref, w1_ref, w2_ref, o_ref, hn_scr):
    f = pl.program_id(1)

    @pl.when(f == 0)
    def _():
        x = x_ref[...]
        hn_scr[...] = _rms(x, g_ref[...]).astype(BF)
        o_ref[...] = x

    a = jnp.maximum(_mm(hn_scr[...], w1_ref[...]), 0.0)
    o_ref[...] += _mm((a * a).astype(BF), w2_ref[...])


def _ffn(x, g, w1, w2):
    t, d = x.shape
    dff = w1.shape[1]
    tm = _tile(t, FFN_TM)
    tf = _tile(dff, FFN_TF, mult=LANE)
    return pl.pallas_call(
        _ffn_body,
        grid=(t // tm, dff // tf),
        in_specs=[
            pl.BlockSpec((tm, d), lambda i, f: (i, 0)),
            pl.BlockSpec((1, d), lambda i, f: (0, 0)),
            pl.BlockSpec((d, tf), lambda i, f: (0, f)),
            pl.BlockSpec((tf, d), lambda i, f: (f, 0)),
        ],
        out_specs=pl.BlockSpec((tm, d), lambda i, f: (i, 0)),
        out_shape=jax.ShapeDtypeStruct((t, d), F32),
        scratch_shapes=[pltpu.VMEM((tm, d), BF)],
        compiler_params=_params(("arbitrary", "arbitrary")),
        name="ffn",
    )(x, g, w1, w2)


def _mem_kv_body(m_ref, g_ref, wk_ref, wv_ref, gk_ref, k_ref, v_ref):
    mn = _rms(m_ref[...], g_ref[...]).astype(BF)
    k = _mm(mn, wk_ref[...])
    gk = gk_ref[...]
    for h in range(MEM_HEADS):
        sl = slice(h * MEM_DIM, (h + 1) * MEM_DIM)
        k_ref[:, sl] = _rms(k[:, sl], gk)
    v_ref[...] = _mm(mn, wv_ref[...])


def _mem_kv(mem, g, wk, wv, gk):
    t, d = mem.shape
    tm = _tile(t, 256)
    mw = MEM_HEADS * MEM_DIM
    const = lambda i: (0, 0)
    return pl.pallas_call(
        _mem_kv_body,
        grid=(t // tm,),
        in_specs=[
            pl.BlockSpec((tm, d), lambda i: (i, 0)),
            pl.BlockSpec((1, d), const),
            pl.BlockSpec(wk.shape, const),
            pl.BlockSpec(wv.shape, const),
            pl.BlockSpec((1, MEM_DIM), const),
        ],
        out_specs=[pl.BlockSpec((tm, mw), lambda i: (i, 0)), pl.BlockSpec((tm, mw), lambda i: (i, 0))],
        out_shape=[jax.ShapeDtypeStruct((t, mw), F32), jax.ShapeDtypeStruct((t, mw), F32)],
        compiler_params=_params(("arbitrary",)),
        name="mem_kv",
    )(mem, g, wk, wv, gk)


def _rope_tables(pos):
    half = QK_ROPE // 2
    inv_freq = ROPE_THETA ** (-np.arange(half, dtype=np.float64) / half)
    ang = np.asarray(pos, np.float64)[:, None] * inv_freq[None, :]
    cos, sin = np.cos(ang), np.sin(ang)
    zeros = np.zeros((ang.shape[0], LANE - QK_ROPE))
    return (jnp.asarray(np.concatenate([cos, cos, zeros], axis=1), F32),
            jnp.asarray(np.concatenate([-sin, sin, zeros], axis=1), F32))


def _pad_lanes(v, n=LANE):
    return jnp.pad(v, ((0, 0), (0, n - v.shape[1])))


def _prep_weights(w_in, g_cq, w_uq, g_ckv, w_ukv, g_q, g_k, w_o, w_mq, w_mo, w_ff1, w_ff2, w_mk, w_mv):
    d = w_in.shape[0]
    n_conv = 2 * GDN_HEADS * GDN_DK + GDN_HEADS * GDN_DV
    n_z = GDN_HEADS * GDN_DV
    o = 0
    w_cq = w_in[:, o:o + Q_RANK]; o += Q_RANK
    w_ckv = w_in[:, o:o + KV_RANK]; o += KV_RANK
    w_kpe = w_in[:, o:o + QK_ROPE]; o += QK_ROPE
    w_qkv = w_in[:, o:o + n_conv]; o += n_conv
    w_z = w_in[:, o:o + n_z]; o += n_z
    w_b = w_in[:, o:o + GDN_HEADS]; o += GDN_HEADS
    w_a = w_in[:, o:o + GDN_HEADS]
    half = QK_ROPE // 2
    swap = jnp.concatenate([jnp.arange(half, QK_ROPE), jnp.arange(0, half)])
    z64 = jnp.zeros((d, LANE - QK_ROPE), w_in.dtype)
    misc = jnp.concatenate([w_kpe, z64, w_kpe[:, swap], z64, w_b, w_a], axis=1)
    misc = _pad_lanes(misc, COL_TILE)
    w_in_p = jnp.concatenate([w_cq, w_ckv, w_qkv, w_z, misc], axis=1).astype(BF)
    wbat = jnp.concatenate([w_b, w_a], axis=1).T.astype(BF)

    r = w_uq.shape[0]
    wq3 = w_uq.reshape(r, MLA_HEADS, QK_HEAD)
    zq = jnp.zeros((r, MLA_HEADS, LANE - QK_ROPE), w_uq.dtype)
    w1 = jnp.concatenate([wq3, zq], axis=2).reshape(r, MLA_HEADS * QPAD).astype(BF)
    w2 = jnp.concatenate([wq3[:, :, QK_NOPE:][:, :, swap], zq], axis=2).reshape(r, MLA_HEADS * LANE).astype(BF)

    dmla = MLA_HEADS * V_DIM
    return dict(
        n_conv=n_conv, n_z=n_z, w_in_p=w_in_p, wbat=wbat, w1=w1, w2=w2,
        gcq=g_cq[None, :], gckv=g_ckv[None, :],
        gq_n=g_q[None, :QK_NOPE], gq_r=_pad_lanes(g_q[None, QK_NOPE:]),
        gk_n=g_k[None, :QK_NOPE], gk_r=_pad_lanes(g_k[None, QK_NOPE:]),
        w_ukv=w_ukv.astype(BF), wo1=w_o[:dmla].astype(BF), wo2=w_o[dmla:].astype(BF),
        w_mq=w_mq.astype(BF), w_mo=w_mo.astype(BF), w_ff1=w_ff1.astype(BF), w_ff2=w_ff2.astype(BF),
        w_mk=w_mk.astype(BF), w_mv=w_mv.astype(BF),
    )


def _gate_params(a_log, dt_bias):
    z8 = jnp.zeros((GDN_HEADS,), F32)
    al = jnp.concatenate([z8, a_log.astype(F32)])
    db = jnp.concatenate([z8, dt_bias.astype(F32)])
    return _pad_lanes(al[None, :]), _pad_lanes(db[None, :]), al[:, None], db[:, None]


def _layer(x, pos, past, conv_past, s0, mem_k, mem_v, wp, lw):
    nb, s_len, d = x.shape
    t = nb * s_len
    xf = x.reshape(t, d)
    ctab, stab = _rope_tables(pos)
    if ctab.shape[0] % 8 != 0 or (s_len < IN_TM and nb > 1):
        ctab, stab = jnp.tile(ctab, (nb, 1)), jnp.tile(stab, (nb, 1))

    cqn, ckv, kpe_pad, qkv, z, ba, bat = _in_proj(
        xf, lw["g_norm_mix"], wp["w_in_p"], wp["wbat"], wp["gcq"], wp["gckv"], ctab, stab,
        wp["n_conv"], wp["n_z"])
    q = _q_proj(cqn, wp["w1"], wp["w2"], ctab, stab, wp["gq_n"], wp["gq_r"])

    if past is None:
        ckv_all, kpe_all, q_off, t_valid = ckv, kpe_pad, 0, s_len
    else:
        ckv_past, kpe_past = past
        p_len = ckv_past.shape[1]
        t_valid = p_len + s_len
        t_pad = -(-t_valid // PROJ_TM) * PROJ_TM
        ckv_all = jnp.concatenate(
            [ckv_past.astype(F32), ckv.reshape(nb, s_len, KV_RANK),
             jnp.zeros((nb, t_pad - t_valid, KV_RANK), F32)], axis=1).reshape(nb * t_pad, KV_RANK)
        kpe_all = jnp.concatenate(
            [jnp.pad(kpe_past.astype(F32), ((0, 0), (0, 0), (0, LANE - QK_ROPE))),
             kpe_pad.reshape(nb, s_len, LANE),
             jnp.zeros((nb, t_pad - t_valid, LANE), F32)], axis=1).reshape(nb * t_pad, LANE)
        q_off = p_len
    k, v = _kv_proj(ckv_all, kpe_all, wp["w_ukv"], wp["gk_n"], wp["gk_r"])
    o_mla = _attention(q, k, v, lw["g_mla_out"], nb, q_off, t_valid)

    conv_past8 = jnp.pad(conv_past.astype(F32), ((0, 0), (HALO - (CONV_W - 1), 0), (0, 0)))
    conv_past8 = conv_past8.reshape(nb * HALO, -1)
    bat3 = bat.reshape(16, nb, s_len).transpose(1, 0, 2)
    alog_l, dtb_l, alog_c, dtb_c = _gate_params(lw["a_log"], lw["dt_bias"])
    o_gdn, s_new = _gdn(qkv, lw["conv_w"], conv_past8, ba, bat3, alog_l, dtb_l, alog_c, dtb_c,
                        z, s0.astype(F32), lw["g_gdn_out"], nb)

    h2 = _mix_mem(xf, o_mla, o_gdn, wp["wo1"], wp["wo2"], lw["g_norm_mem_q"], wp["w_mq"],
                  lw["g_mq"], mem_k, mem_v, wp["w_mo"], nb)
    y = _ffn(h2, lw["g_norm_ffn"], wp["w_ff1"], wp["w_ff2"])

    conv_in_tail = jnp.concatenate([conv_past.astype(F32), qkv.reshape(nb, s_len, -1)[:, -(CONV_W - 1):]], axis=1)
    conv_new = conv_in_tail[:, -(CONV_W - 1):]
    return (y.reshape(nb, s_len, d), ckv.reshape(nb, s_len, KV_RANK),
            kpe_pad[:, :QK_ROPE].reshape(nb, s_len, QK_ROPE), conv_new, s_new)


def kernel(x_prompt, x_sample, mem_prompt, cache_mla_ckv, cache_mla_kpe, cache_gdn_conv, state_gdn, cache_mem_k, cache_mem_v, g_norm_mix, w_in, g_cq, w_uq, g_ckv, w_ukv, g_q_mla, g_k_mla, g_mla_out, conv_w, a_log, dt_bias, g_gdn_out, w_o, g_norm_mem_q, g_norm_mem_kv, w_mq, w_mk, w_mv, g_mq, g_mk, w_mo, g_norm_ffn, w_ff1, w_ff2):
    depth = w_in.shape[0]
    nbp, sp, d = x_prompt.shape
    nbs, ss, _ = x_sample.shape
    n_mem = mem_prompt.shape[1]
    mw = MEM_HEADS * MEM_DIM
    n_conv = 2 * GDN_HEADS * GDN_DK + GDN_HEADS * GDN_DV
    pos_p = np.arange(sp)
    pos_s = cache_mla_ckv.shape[2] + np.arange(ss)
    zeros_conv = jnp.zeros((nbp, CONV_W - 1, n_conv), F32)
    zeros_state = jnp.zeros((nbp, GDN_HEADS, GDN_DK, GDN_DV), F32)
    hp, hs = x_prompt, x_sample
    outs_p = [[] for _ in range(6)]
    outs_s = [[] for _ in range(4)]
    for l in range(depth):
        wp = _prep_weights(w_in[l], g_cq[l], w_uq[l], g_ckv[l], w_ukv[l], g_q_mla[l], g_k_mla[l], w_o[l],
                           w_mq[l], w_mo[l], w_ff1[l], w_ff2[l], w_mk[l], w_mv[l])
        lw = dict(g_norm_mix=g_norm_mix[l][None, :], g_mla_out=g_mla_out[l][None, :], conv_w=conv_w[l],
                  a_log=a_log[l], dt_bias=dt_bias[l], g_gdn_out=g_gdn_out[l][None, :],
                  g_norm_mem_q=g_norm_mem_q[l][None, :], g_mq=g_mq[l][None, :],
                  g_norm_ffn=g_norm_ffn[l][None, :])
        mk, mv = _mem_kv(mem_prompt.reshape(nbp * n_mem, d), g_norm_mem_kv[l][None, :], wp["w_mk"], wp["w_mv"],
                         g_mk[l][None, :])
        mk3, mv3 = mk.reshape(nbp, n_mem, mw), mv.reshape(nbp, n_mem, mw)
        hp, c1, c2, c3, c4 = _layer(hp, pos_p, None, zeros_conv, zeros_state, mk3.astype(BF), mv3.astype(BF), wp, lw)
        for lst, val in zip(outs_p, (c1, c2, c3, c4, mk3.reshape(nbp, n_mem, MEM_HEADS, MEM_DIM),
                                     mv3.reshape(nbp, n_mem, MEM_HEADS, MEM_DIM))):
            lst.append(val)
        hs, d1, d2, d3, d4 = _layer(hs, pos_s, (cache_mla_ckv[l], cache_mla_kpe[l]), cache_gdn_conv[l], state_gdn[l],
                                    cache_mem_k[l].reshape(nbs, n_mem, mw).astype(BF),
                                    cache_mem_v[l].reshape(nbs, n_mem, mw).astype(BF), wp, lw)
        for lst, val in zip(outs_s, (d1, d2, d3, d4)):
            lst.append(val)
    return (hp, hs, *(jnp.stack(v) for v in outs_p), *(jnp.stack(v) for v in outs_s))
```

```python
import functools
import math

import jax
import jax.numpy as jnp
import numpy as np
from jax import lax
from jax.experimental import pallas as pl
from jax.experimental.pallas import tpu as pltpu

F32 = jnp.float32
BF = jnp.bfloat16

EPS = 1e-6
CHUNK = 64
ROPE_THETA = 10000.0
MLA_HEADS = 8
QK_NOPE = 128
QK_ROPE = 64
QK_HEAD = QK_NOPE + QK_ROPE
V_DIM = 128
Q_RANK = 512
KV_RANK = 512
GDN_HEADS = 8
GDN_DK = 128
GDN_DV = 128
CONV_W = 4
MEM_HEADS = 4
MEM_DIM = 128
LANE = 128
QPAD = 2 * LANE
COL_TILE = 512
HALO = 8
NEG_BIG = -1e30
QSCALE = (QK_HEAD ** -0.5) * math.log2(math.e)

VMEM_LIMIT = 56 * 1024 * 1024

IN_TM = 1024
PROJ_TM = 512
ATTN_TQ = 1024
ATTN_TK = 1024
ATTN_WIDE = 2
GDN_TM = 256
GDN_WAVES = 2
MIX_TM = 512
FFN_TM = 1024
FFN_TF = 512


def _tile(n, pref, mult=8):
    if n <= pref:
        return n
    t = (pref // mult) * mult
    while t >= mult:
        if n % t == 0:
            return t
        t -= mult
    return n


def _nt(a, b):
    return lax.dot_general(a, b, (((1,), (1,)), ((), ())), preferred_element_type=F32)


def _mm(a, b):
    return jnp.dot(a, b, preferred_element_type=F32)


def _sigmoid(x):
    return 1.0 / (1.0 + jnp.exp(-x))


def _softplus(x):
    return jnp.maximum(x, 0.0) + jnp.log(1.0 + jnp.exp(-jnp.abs(x)))


def _rms(x, g, n=None):
    n = x.shape[-1] if n is None else n
    ms = jnp.sum(x * x, axis=-1, keepdims=True) * (1.0 / n)
    return (x * lax.rsqrt(ms + EPS)) * g


def _split3(x):
    hi = x.astype(BF)
    r1 = x - hi.astype(F32)
    mid = r1.astype(BF)
    lo = (r1 - mid.astype(F32)).astype(BF)
    return hi, mid, lo


def _interleave(lists):
    items = []
    for li, lst in enumerate(lists):
        items += [((i + 0.5) / len(lst), li, f) for i, f in enumerate(lst)]
    return [f for _, _, f in sorted(items, key=lambda it: (it[0], it[1]))]


def _params(sem):
    return pltpu.CompilerParams(dimension_semantics=sem, vmem_limit_bytes=VMEM_LIMIT)


def _in_proj_body(x_ref, g_ref, w_ref, wbat_ref, gcq_ref, gckv_ref, ctab_ref, stab_ref,
                  cqn_ref, ckv_ref, kpe_ref, qkv_ref, z_ref, ba_ref, bat_ref, xn_scr, *, nqkv, nz):
    j = pl.program_id(1)

    @pl.when(j == 0)
    def _():
        xn_scr[...] = _rms(x_ref[...], g_ref[...]).astype(BF)

    def halves(store):
        hw = COL_TILE // 2
        for c0 in (0, hw):
            store(c0, hw, _mm(xn_scr[...], w_ref[:, c0:c0 + hw]))

    @pl.when(j == 0)
    def _():
        cqn_ref[...] = _rms(_mm(xn_scr[...], w_ref[...]), gcq_ref[...]).astype(BF)

    @pl.when(j == 1)
    def _():
        ckv_ref[...] = _rms(_mm(xn_scr[...], w_ref[...]), gckv_ref[...])

    @pl.when((j >= 2) & (j < 2 + nqkv))
    def _():
        def store(c0, n, a):
            qkv_ref[:, c0:c0 + n] = a
        halves(store)

    @pl.when((j >= 2 + nqkv) & (j < 2 + nqkv + nz))
    def _():
        def store(c0, n, a):
            z_ref[:, c0:c0 + n] = a.astype(BF)
        halves(store)

    @pl.when(j == 2 + nqkv + nz)
    def _():
        acc = _mm(xn_scr[...], w_ref[:, 0:3 * LANE])
        kpe_ref[...] = acc[:, 0:LANE] * ctab_ref[...] + acc[:, LANE:2 * LANE] * stab_ref[...]
        ba_ref[...] = acc[:, 2 * LANE:3 * LANE]
        bat_ref[...] = _nt(wbat_ref[...], xn_scr[...])


def _in_proj(x, g, w_p, wbat, gcq, gckv, ctab, stab, n_conv, n_z):
    t, d = x.shape
    tm = _tile(min(t, ctab.shape[0]), IN_TM)
    ntab = ctab.shape[0] // tm
    nqkv = n_conv // COL_TILE
    nz = n_z // COL_TILE
    ncol = 2 + nqkv + nz + 1
    assert w_p.shape[1] == ncol * COL_TILE
    row = lambda i, j: (i, 0)
    const = lambda i, j: (0, 0)
    body = functools.partial(_in_proj_body, nqkv=nqkv, nz=nz)
    return pl.pallas_call(
        body,
        grid=(t // tm, ncol),
        in_specs=[
            pl.BlockSpec((tm, d), row),
            pl.BlockSpec((1, d), const),
            pl.BlockSpec((d, COL_TILE), lambda i, j: (0, j)),
            pl.BlockSpec((16, d), const),
            pl.BlockSpec((1, Q_RANK), const),
            pl.BlockSpec((1, KV_RANK), const),
            pl.BlockSpec((tm, LANE), lambda i, j: (i % ntab, 0)),
            pl.BlockSpec((tm, LANE), lambda i, j: (i % ntab, 0)),
        ],
        out_specs=[
            pl.BlockSpec((tm, Q_RANK), row),
            pl.BlockSpec((tm, KV_RANK), row),
            pl.BlockSpec((tm, LANE), row),
            pl.BlockSpec((tm, COL_TILE), lambda i, j: (i, jnp.clip(j - 2, 0, nqkv - 1))),
            pl.BlockSpec((tm, COL_TILE), lambda i, j: (i, jnp.clip(j - 2 - nqkv, 0, nz - 1))),
            pl.BlockSpec((tm, LANE), row),
            pl.BlockSpec((16, tm), lambda i, j: (0, i)),
        ],
        out_shape=[
            jax.ShapeDtypeStruct((t, Q_RANK), BF),
            jax.ShapeDtypeStruct((t, KV_RANK), F32),
            jax.ShapeDtypeStruct((t, LANE), F32),
            jax.ShapeDtypeStruct((t, n_conv), F32),
            jax.ShapeDtypeStruct((t, n_z), BF),
            jax.ShapeDtypeStruct((t, LANE), F32),
            jax.ShapeDtypeStruct((16, t), F32),
        ],
        scratch_shapes=[pltpu.VMEM((tm, d), BF)],
        compiler_params=_params(("arbitrary", "arbitrary")),
        name="in_proj",
    )(x, g, w_p, wbat, gcq, gckv, ctab, stab)


def _q_proj_body(c_ref, w1_ref, w2_ref, ctab_ref, stab_ref, gn_ref, gr_ref, q_ref):
    c = c_ref[...]
    qf = _mm(c, w1_ref[...])
    qs = _mm(c, w2_ref[...])
    ct = ctab_ref[...]
    st = stab_ref[...]
    gn = gn_ref[...]
    gr = gr_ref[...]
    for h in range(MLA_HEADS):
        nope = qf[:, h * QPAD:h * QPAD + LANE]
        rot = qf[:, h * QPAD + LANE:(h + 1) * QPAD] * ct + qs[:, h * LANE:(h + 1) * LANE] * st
        ss = jnp.sum(nope * nope, axis=-1, keepdims=True) + jnp.sum(rot * rot, axis=-1, keepdims=True)
        rs = lax.rsqrt(ss * (1.0 / QK_HEAD) + EPS) * QSCALE
        q_ref[:, h * QPAD:h * QPAD + LANE] = (nope * rs * gn).astype(BF)
        q_ref[:, h * QPAD + LANE:(h + 1) * QPAD] = (rot * rs * gr).astype(BF)


def _q_proj(cqn, w1, w2, ctab, stab, gn, gr):
    t = cqn.shape[0]
    tm = _tile(min(t, ctab.shape[0]), PROJ_TM)
    ntab = ctab.shape[0] // tm
    const = lambda i: (0, 0)
    return pl.pallas_call(
        _q_proj_body,
        grid=(t // tm,),
        in_specs=[
            pl.BlockSpec((tm, Q_RANK), lambda i: (i, 0)),
            pl.BlockSpec(w1.shape, const),
            pl.BlockSpec(w2.shape, const),
            pl.BlockSpec((tm, LANE), lambda i: (i % ntab, 0)),
            pl.BlockSpec((tm, LANE), lambda i: (i % ntab, 0)),
            pl.BlockSpec((1, LANE), const),
            pl.BlockSpec((1, LANE), const),
        ],
        out_specs=pl.BlockSpec((tm, MLA_HEADS * QPAD), lambda i: (i, 0)),
        out_shape=jax.ShapeDtypeStruct((t, MLA_HEADS * QPAD), BF),
        compiler_params=_params(("arbitrary",)),
        name="q_proj",
    )(cqn, w1, w2, ctab, stab, gn, gr)


def _kv_proj_body(c_ref, kpe_ref, w_ref, gn_ref, gr_ref, k_ref, v_ref):
    kv = _mm(c_ref[...].astype(BF), w_ref[...])
    kp = kpe_ref[...]
    kps = jnp.sum(kp * kp, axis=-1, keepdims=True)
    gn = gn_ref[...]
    gr = gr_ref[...]
    for h in range(MLA_HEADS):
        kn = kv[:, h * 2 * LANE:h * 2 * LANE + LANE]
        rs = lax.rsqrt((jnp.sum(kn * kn, axis=-1, keepdims=True) + kps) * (1.0 / QK_HEAD) + EPS)
        k_ref[:, h * QPAD:h * QPAD + LANE] = (kn * rs * gn).astype(BF)
        k_ref[:, h * QPAD + LANE:(h + 1) * QPAD] = (kp * rs * gr).astype(BF)
        v_ref[:, h * V_DIM:(h + 1) * V_DIM] = kv[:, h * 2 * LANE + LANE:(h + 1) * 2 * LANE].astype(BF)


def _kv_proj(ckv, kpe_pad, w, gn, gr):
    t = ckv.shape[0]
    tm = _tile(t, PROJ_TM)
    const = lambda i: (0, 0)
    return pl.pallas_call(
        _kv_proj_body,
        grid=(t // tm,),
        in_specs=[
            pl.BlockSpec((tm, KV_RANK), lambda i: (i, 0)),
            pl.BlockSpec((tm, LANE), lambda i: (i, 0)),
            pl.BlockSpec(w.shape, const),
            pl.BlockSpec((1, LANE), const),
            pl.BlockSpec((1, LANE), const),
        ],
        out_specs=[
            pl.BlockSpec((tm, MLA_HEADS * QPAD), lambda i: (i, 0)),
            pl.BlockSpec((tm, MLA_HEADS * V_DIM), lambda i: (i, 0)),
        ],
        out_shape=[
            jax.ShapeDtypeStruct((t, MLA_HEADS * QPAD), BF),
            jax.ShapeDtypeStruct((t, MLA_HEADS * V_DIM), BF),
        ],
        compiler_params=_params(("arbitrary",)),
        name="kv_proj",
    )(ckv, kpe_pad, w, gn, gr)


def _attn_body(q_ref, k_ref, v_ref, g_ref, o_ref, m_scr, l_scr, acc_scr, *, tq, tk, wide, q_off, t_valid):
    qi = pl.program_id(2)
    qpos0 = q_off + qi * tq
    n_full = jnp.minimum((qpos0 // CHUNK * CHUNK + CHUNK) // tk, t_valid // tk)
    hi = jnp.minimum((qpos0 + tq - 1) // CHUNK * CHUNK + CHUNK, t_valid)
    n_total = (hi + tk - 1) // tk

    m_scr[...] = jnp.full(m_scr.shape, NEG_BIG, F32)
    l_scr[...] = jnp.zeros(l_scr.shape, F32)
    acc_scr[...] = jnp.zeros(acc_scr.shape, F32)
    diag = q_off % tk == 0 and tq == tk and t_valid % tk == 0 and tk % (2 * LANE) == 0 and tq % 32 == 0

    def step(kc, width, masked, r0=0, nr=tq):
        rows = slice(r0, r0 + nr)
        k0 = pl.multiple_of(kc * tk, tk)
        s = _nt(q_ref[rows, :], k_ref[pl.ds(k0, width), :])
        if masked:
            qpos = qpos0 + r0 + lax.broadcasted_iota(jnp.int32, (nr, 1), 0)
            last = jnp.minimum(qpos | (CHUNK - 1), t_valid - 1) - k0
            s = jnp.where(lax.broadcasted_iota(jnp.int32, (nr, width), 1) <= last, s, NEG_BIG)
        m_prev = m_scr[rows, :]
        m_new = jnp.maximum(m_prev, jnp.max(s, axis=-1, keepdims=True))
        alpha = jnp.exp2(m_prev - m_new)
        ps = [jnp.exp2(s[:, j * LANE:(j + 1) * LANE] - m_new) for j in range(width // LANE)]
        psum = ps[0]
        for pj in ps[1:]:
            psum = psum + pj
        l_scr[rows, :] = alpha * l_scr[rows, :] + psum
        p = jnp.concatenate(ps, axis=1).astype(BF) if len(ps) > 1 else ps[0].astype(BF)
        acc_scr[rows, :] = acc_scr[rows, :] * alpha + _mm(p, v_ref[pl.ds(k0, width), :])
        m_scr[rows, :] = m_new

    def loop(lo, hi, fn):
        lax.fori_loop(lo, hi, lambda i, c: (fn(i), c)[1], 0)

    n_wide = n_full // wide
    loop(0, n_wide, lambda i: step(i * wide, wide * tk, False))
    loop(n_wide * wide, n_full, lambda kc: step(kc, tk, False))
    if diag:
        def masked(kc):
            step(kc, tk // 2, True, 0, tq // 2)
            step(kc, tk, True, tq // 2, tq // 2)
    else:
        def masked(kc):
            step(kc, tk, True)
    loop(n_full, n_total, masked)
    o = acc_scr[...] / jnp.sum(l_scr[...], axis=-1, keepdims=True)
    o_ref[...] = _rms(o, g_ref[...]).astype(BF)


def _attention(q, k, v, g_out, nb, q_off, t_valid):
    tq_total = q.shape[0] // nb
    tk_total = k.shape[0] // nb
    tq = _tile(tq_total, ATTN_TQ)
    tk = _tile(tk_total, ATTN_TK, mult=LANE)
    nq = tq_total // tq
    body = functools.partial(_attn_body, tq=tq, tk=tk, wide=ATTN_WIDE, q_off=q_off, t_valid=t_valid)
    return pl.pallas_call(
        body,
        grid=(nb, MLA_HEADS, nq),
        in_specs=[
            pl.BlockSpec((tq, QPAD), lambda b, h, i: (b * nq + i, h)),
            pl.BlockSpec((tk_total, QPAD), lambda b, h, i: (b, h)),
            pl.BlockSpec((tk_total, V_DIM), lambda b, h, i: (b, h)),
            pl.BlockSpec((1, V_DIM), lambda b, h, i: (0, 0)),
        ],
        out_specs=pl.BlockSpec((tq, V_DIM), lambda b, h, i: (b * nq + i, h)),
        out_shape=jax.ShapeDtypeStruct((q.shape[0], MLA_HEADS * V_DIM), BF),
        scratch_shapes=[
            pltpu.VMEM((tq, LANE), F32),
            pltpu.VMEM((tq, LANE), F32),
            pltpu.VMEM((tq, V_DIM), F32),
        ],
        compiler_params=_params(("arbitrary", "arbitrary", "arbitrary")),
        name="mla_attention",
    )(q, k, v, g_out)


def _gdn_body(q_ref, k_ref, v_ref, wq_ref, wk_ref, wv_ref, pq_ref, pk_ref, pv_ref,
              ba_ref, bat_ref, alog_l_ref, dtb_l_ref, alog_c_ref, dtb_c_ref,
              z_ref, s0_ref, gout_ref,
              o_ref, sout_ref,
              s_scr, halo_scr, vnew_scr, *, tm, L, hg):
    t = pl.program_id(2)
    nt = pl.num_programs(2)
    n_chunks = tm // L
    n_factors = int(math.log2(L)) - 1

    @pl.when(t == 0)
    def _():
        s_scr[...] = s0_ref[0]
        halo_scr[0] = pq_ref[...]
        halo_scr[1] = pk_ref[...]
        halo_scr[2] = pv_ref[...]
        vnew_scr[...] = jnp.zeros(vnew_scr.shape, BF)

    ba = ba_ref[...]
    beta_all = _sigmoid(ba)
    g_all = -jnp.exp(alog_l_ref[...]) * _softplus(ba + dtb_l_ref[...])
    gt_all = -jnp.exp(alog_c_ref[...]) * _softplus(bat_ref[0] + dtb_c_ref[...])

    r = lax.broadcasted_iota(jnp.int32, (tm, tm), 0)
    c = lax.broadcasted_iota(jnp.int32, (tm, tm), 1)
    same = (r // L) == (c // L)
    lower = same & (c <= r)
    strict = same & (c < r)
    cs = jnp.where(lower, 1.0, 0.0).astype(BF)
    bd = jnp.where(same, 1.0, 0.0).astype(BF)
    g3 = _split3(g_all)
    gcol_all = _mm(cs, g3[0]) + _mm(cs, g3[1]) + _mm(cs, g3[2])
    glast_all = _mm(bd, g3[0]) + _mm(bd, g3[1]) + _mm(bd, g3[2])
    gt3 = _split3(gt_all)
    grow_all = _nt(gt3[0], cs) + _nt(gt3[1], cs) + _nt(gt3[2], cs)

    gout = gout_ref[...]
    eye_t = jnp.where(r == c, 1.0, 0.0)
    eye_k = jnp.where(lax.broadcasted_iota(jnp.int32, (GDN_DK, GDN_DK), 0)
                      == lax.broadcasted_iota(jnp.int32, (GDN_DK, GDN_DK), 1), 1.0, 0.0).astype(BF)
    col_chunk = lax.broadcasted_iota(jnp.int32, (1, tm), 1) // L

    def make_wave(heads):
        c0, c1 = heads[0] * GDN_DK, (heads[-1] + 1) * GDN_DK
        loc = lambda h: slice((h - heads[0]) * GDN_DK, (h - heads[0] + 1) * GDN_DK)
        w = {}

        def conv(idx, x_ref, w_ref):
            xs = jnp.concatenate([halo_scr[idx, :, c0:c1], x_ref[:, c0:c1]], axis=0)
            cw = w_ref[:, c0:c1]
            acc = xs * cw[0:1]
            for i in range(1, CONV_W):
                acc = pltpu.roll(acc, 1, axis=0) + xs * cw[i:i + 1]
            halo_scr[idx, :, c0:c1] = xs[tm:tm + HALO]
            y = acc[HALO:HALO + tm]
            return y * _sigmoid(y)

        def p_q():
            w["yq"] = conv(0, q_ref, wq_ref)

        def p_k():
            w["yk"] = conv(1, k_ref, wk_ref)

        def p_v():
            w["yv"] = conv(2, v_ref, wv_ref)

        def p_gate():
            w["beta"], w["gc"], w["glast"], w["gr"] = {}, {}, {}, {}
            for h in heads:
                g = GDN_HEADS + h
                w["beta"][h] = beta_all[:, h:h + 1]
                w["gc"][h] = gcol_all[:, g:g + 1]
                w["glast"][h] = glast_all[:, g:g + 1]
                w["gr"][h] = grow_all[g:g + 1, :]

        def p_norm():
            w["qn"], w["kn"] = {}, {}
            for h in heads:
                qh = w["yq"][:, loc(h)]
                kh = w["yk"][:, loc(h)]
                w["qn"][h] = qh * lax.rsqrt(jnp.sum(qh * qh, axis=-1, keepdims=True) + EPS) * (GDN_DK ** -0.5)
                w["kn"][h] = kh * lax.rsqrt(jnp.sum(kh * kh, axis=-1, keepdims=True) + EPS)
            kb = {h: w["kn"][h].astype(BF) for h in heads}
            w["kk"] = {h: _nt(kb[h], kb[h]) for h in heads}
            w["qk"] = {h: _nt(w["qn"][h].astype(BF), kb[h]) for h in heads}

        def p_mat():
            beta, gc = w["beta"], w["gc"]
            decay = {h: jnp.where(lower, jnp.exp(gc[h] - w["gr"][h]), 0.0) for h in heads}
            w["nm"] = {h: jnp.where(strict, beta[h] * w["kk"][h] * decay[h], 0.0) for h in heads}
            w["attn"] = {h: (w["qk"][h] * decay[h]).astype(BF) for h in heads}
            w["eg"] = {h: jnp.exp(gc[h]) for h in heads}
            w["rhs"] = {h: jnp.concatenate([w["yv"][:, loc(h)] * beta[h],
                                            w["kn"][h] * (beta[h] * w["eg"][h])], axis=1) for h in heads}
            w["nb"] = {h: w["nm"][h].astype(BF) for h in heads}

        def s_first():
            w["m"] = {h: _mm(w["nb"][h], w["nb"][h]) for h in heads}
            w["T"] = {h: eye_t - w["nm"][h] for h in heads}

        def s_level(s):
            xb = {h: jnp.concatenate([w["m"][h], w["T"][h]], axis=0).astype(BF) for h in heads}
            if s < n_factors - 1:
                prod = {h: _mm(xb[h], xb[h][:tm]) for h in heads}
                w["T"] = {h: w["T"][h] + prod[h][tm:] for h in heads}
                w["m"] = {h: prod[h][:tm] for h in heads}
            else:
                w["T"] = {h: w["T"][h] + _mm(xb[h][tm:], xb[h][:tm]) for h in heads}

        def s_y0():
            w["tb"] = {h: w["T"][h].astype(BF) for h in heads}
            w["y0"] = {h: _mm(w["tb"][h], w["rhs"][h].astype(BF)) for h in heads}

        def s_res():
            res = {}
            for h in heads:
                nbf = w["nb"][h].astype(F32)
                nx = jnp.concatenate([nbf, w["nm"][h] - nbf], axis=0).astype(BF)
                yh = w["y0"][h].astype(BF)
                yl = (w["y0"][h] - yh.astype(F32)).astype(BF)
                p1 = _mm(nx, yh)
                res[h] = w["rhs"][h] - w["y0"][h] - (p1[:tm] + p1[tm:] + _mm(nx[:tm], yl))
            w["res"] = res

        def s_fix():
            w["y"] = {h: w["y0"][h] + _mm(w["tb"][h], w["res"][h].astype(BF)) for h in heads}

        def c_prep():
            y = w["y"]
            w["u"] = {h: y[h][:, :GDN_DV] for h in heads}
            w["wq"] = {}
            for h in heads:
                qd = w["qn"][h] * w["eg"][h]
                parts = []
                for ci in range(n_chunks):
                    parts += [y[h][ci * L:(ci + 1) * L, GDN_DV:], qd[ci * L:(ci + 1) * L]]
                w["wq"][h] = jnp.concatenate(parts, axis=0).astype(BF)
            kdb = {h: (w["kn"][h] * jnp.exp(w["glast"][h] - w["gc"][h])).astype(BF) for h in heads}
            w["gtot"] = {h: jnp.exp(w["glast"][h]) for h in heads}
            kdt = {h: _nt(eye_k, kdb[h]) for h in heads}
            w["kdt"] = {h: [jnp.where(col_chunk == ci, kdt[h], 0.0).astype(BF) for ci in range(n_chunks)]
                        for h in heads}
            w["st"] = {h: s_scr[h] for h in heads}
            w["outs"] = {h: [] for h in heads}

        def c_chunk(ci):
            r0 = ci * L
            st = w["st"]
            ws = {h: _mm(w["wq"][h][2 * r0:2 * r0 + 2 * L], st[h].astype(BF)) for h in heads}
            for h in heads:
                vnew_scr[h, r0:r0 + L, :] = (w["u"][h][r0:r0 + L] - ws[h][:L]).astype(BF)
            for h in heads:
                w["outs"][h].append(ws[h][L:] + _mm(w["attn"][h][r0:r0 + L, :], vnew_scr[h]))
            w["st"] = {h: st[h] * w["gtot"][h][r0:r0 + 1, :] + _mm(w["kdt"][h][ci], vnew_scr[h]) for h in heads}

        def c_out():
            for h in heads:
                s_scr[h] = w["st"][h]
                o = w["outs"][h][0] if n_chunks == 1 else jnp.concatenate(w["outs"][h], axis=0)
                zz = z_ref[:, h * GDN_DV:(h + 1) * GDN_DV].astype(F32)
                o_ref[:, h * GDN_DV:(h + 1) * GDN_DV] = (_rms(o, gout) * (zz * _sigmoid(zz))).astype(BF)

        prep = [p_q, p_k, p_v, p_gate, p_norm, p_mat]
        solve = ([s_first] + [functools.partial(s_level, s) for s in range(n_factors)] + [s_y0, s_res, s_fix])
        scan = [c_prep] + [functools.partial(c_chunk, ci) for ci in range(n_chunks)] + [c_out]
        return [prep, solve, scan]

    nw = min(GDN_WAVES, hg)
    per = hg // nw
    waves = [make_wave(list(range(i * per, (i + 1) * per))) for i in range(nw)]
    for slot in range(nw + 2):
        active = [waves[i][slot - i] for i in range(nw) if 0 <= slot - i < 3]
        for stage in _interleave(active):
            stage()

    @pl.when(t == nt - 1)
    def _():
        sout_ref[0] = s_scr[...]


def _gdn(qkv, conv_w, conv_past8, ba, bat, alog_l, dtb_l, alog_c, dtb_c, z, s0, gout, nb):
    t = qkv.shape[0]
    s_len = t // nb
    L = min(s_len, CHUNK)
    tm = _tile(s_len, GDN_TM, mult=L)
    nt = s_len // tm
    hg = GDN_HEADS
    hw = hg * GDN_DK
    body = functools.partial(_gdn_body, tm=tm, L=L, hg=hg)
    rows = lambda off: (lambda b, h, i: (b * nt + i, off + h))
    wcol = lambda off: (lambda b, h, i: (0, off + h))
    pcol = lambda off: (lambda b, h, i: (b, off + h))
    const = lambda b, h, i: (0, 0)
    ng = GDN_HEADS // hg
    return pl.pallas_call(
        body,
        grid=(nb, ng, nt),
        in_specs=[
            pl.BlockSpec((tm, hw), rows(0)),
            pl.BlockSpec((tm, hw), rows(ng)),
            pl.BlockSpec((tm, hw), rows(2 * ng)),
            pl.BlockSpec((CONV_W, hw), wcol(0)),
            pl.BlockSpec((CONV_W, hw), wcol(ng)),
            pl.BlockSpec((CONV_W, hw), wcol(2 * ng)),
            pl.BlockSpec((HALO, hw), pcol(0)),
            pl.BlockSpec((HALO, hw), pcol(ng)),
            pl.BlockSpec((HALO, hw), pcol(2 * ng)),
            pl.BlockSpec((tm, LANE), lambda b, h, i: (b * nt + i, 0)),
            pl.BlockSpec((1, 16, tm), lambda b, h, i: (b, 0, i)),
            pl.BlockSpec((1, LANE), const),
            pl.BlockSpec((1, LANE), const),
            pl.BlockSpec((16, 1), const),
            pl.BlockSpec((16, 1), const),
            pl.BlockSpec((tm, hw), rows(0)),
            pl.BlockSpec((1, hg, GDN_DK, GDN_DV), lambda b, h, i: (b, h, 0, 0)),
            pl.BlockSpec((1, GDN_DV), const),
        ],
        out_specs=[
            pl.BlockSpec((tm, hw), rows(0)),
            pl.BlockSpec((1, hg, GDN_DK, GDN_DV), lambda b, h, i: (b, h, 0, 0)),
        ],
        out_shape=[
            jax.ShapeDtypeStruct((t, GDN_HEADS * GDN_DV), BF),
            jax.ShapeDtypeStruct((nb, GDN_HEADS, GDN_DK, GDN_DV), F32),
        ],
        scratch_shapes=[
            pltpu.VMEM((hg, GDN_DK, GDN_DV), F32),
            pltpu.VMEM((3, HALO, hw), F32),
            pltpu.VMEM((hg, tm, GDN_DV), BF),
        ],
        compiler_params=_params(("arbitrary", "arbitrary", "arbitrary")),
        name="gdn",
    )(qkv, qkv, qkv, conv_w, conv_w, conv_w, conv_past8, conv_past8, conv_past8,
      ba, bat, alog_l, dtb_l, alog_c, dtb_c, z, s0, gout)


def _mix_mem_body(x_ref, a1_ref, a2_ref, wo1_ref, wo2_ref, gq_ref, wmq_ref, gmq_ref,
                  mk_ref, mv_ref, wmo_ref, o_ref):
    h1 = x_ref[...] + _mm(a1_ref[...], wo1_ref[...]) + _mm(a2_ref[...], wo2_ref[...])
    hn = _rms(h1, gq_ref[...]).astype(BF)
    q = _mm(hn, wmq_ref[...])
    gmq = gmq_ref[...]
    mk = mk_ref[0]
    mv = mv_ref[0]
    outs = []
    for h in range(MEM_HEADS):
        sl = slice(h * MEM_DIM, (h + 1) * MEM_DIM)
        qh = _rms(q[:, sl], gmq).astype(BF)
        s = _nt(qh, mk[:, sl]) * (MEM_DIM ** -0.5)
        p = jnp.exp(s - jnp.max(s, axis=-1, keepdims=True))
        p = p / jnp.sum(p, axis=-1, keepdims=True)
        outs.append(_mm(p.astype(BF), mv[:, sl]))
    o = jnp.concatenate(outs, axis=1).astype(BF)
    o_ref[...] = h1 + _mm(o, wmo_ref[...])


def _mix_mem(x, a1, a2, wo1, wo2, gq, wmq, gmq, mk, mv, wmo, nb):
    t, d = x.shape
    s_len = t // nb
    tm = _tile(s_len, MIX_TM)
    nt = s_len // tm
    n_mem = mk.shape[1]
    mw = MEM_HEADS * MEM_DIM
    row = lambda i: (i, 0)
    const = lambda i: (0, 0)
    return pl.pallas_call(
        _mix_mem_body,
        grid=(t // tm,),
        in_specs=[
            pl.BlockSpec((tm, d), row),
            pl.BlockSpec((tm, a1.shape[1]), row),
            pl.BlockSpec((tm, a2.shape[1]), row),
            pl.BlockSpec(wo1.shape, const),
            pl.BlockSpec(wo2.shape, const),
            pl.BlockSpec((1, d), const),
            pl.BlockSpec(wmq.shape, const),
            pl.BlockSpec((1, MEM_DIM), const),
            pl.BlockSpec((1, n_mem, mw), lambda i: (i // nt, 0, 0)),
            pl.BlockSpec((1, n_mem, mw), lambda i: (i // nt, 0, 0)),
            pl.BlockSpec(wmo.shape, const),
        ],
        out_specs=pl.BlockSpec((tm, d), row),
        out_shape=jax.ShapeDtypeStruct((t, d), F32),
        compiler_params=_params(("arbitrary",)),
        name="mix_mem",
    )(x, a1, a2, wo1, wo2, gq, wmq, gmq, mk, mv, wmo)


def _ffn_body(x_ref, g_ref, w1_ref, w2_ref, o_ref, hn_scr):
    f = pl.program_id(1)

    @pl.when(f == 0)
    def _():
        x = x_ref[...]
        hn_scr[...] = _rms(x, g_ref[...]).astype(BF)
        o_ref[...] = x

    a = jnp.maximum(_mm(hn_scr[...], w1_ref[...]), 0.0)
    o_ref[...] += _mm((a * a).astype(BF), w2_ref[...])


def _ffn(x, g, w1, w2):
    t, d = x.shape
    dff = w1.shape[1]
    tm = _tile(t, FFN_TM)
    tf = _tile(dff, FFN_TF, mult=LANE)
    return pl.pallas_call(
        _ffn_body,
        grid=(t // tm, dff // tf),
        in_specs=[
            pl.BlockSpec((tm, d), lambda i, f: (i, 0)),
            pl.BlockSpec((1, d), lambda i, f: (0, 0)),
            pl.BlockSpec((d, tf), lambda i, f: (0, f)),
            pl.BlockSpec((tf, d), lambda i, f: (f, 0)),
        ],
        out_specs=pl.BlockSpec((tm, d), lambda i, f: (i, 0)),
        out_shape=jax.ShapeDtypeStruct((t, d), F32),
        scratch_shapes=[pltpu.VMEM((tm, d), BF)],
        compiler_params=_params(("arbitrary", "arbitrary")),
        name="ffn",
    )(x, g, w1, w2)


def _mem_kv_body(m_ref, g_ref, wk_ref, wv_ref, gk_ref, k_ref, v_ref):
    mn = _rms(m_ref[...], g_ref[...]).astype(BF)
    k = _mm(mn, wk_ref[...])
    gk = gk_ref[...]
    for h in range(MEM_HEADS):
        sl = slice(h * MEM_DIM, (h + 1) * MEM_DIM)
        k_ref[:, sl] = _rms(k[:, sl], gk)
    v_ref[...] = _mm(mn, wv_ref[...])


def _mem_kv(mem, g, wk, wv, gk):
    t, d = mem.shape
    tm = _tile(t, 256)
    mw = MEM_HEADS * MEM_DIM
    const = lambda i: (0, 0)
    return pl.pallas_call(
        _mem_kv_body,
        grid=(t // tm,),
        in_specs=[
            pl.BlockSpec((tm, d), lambda i: (i, 0)),
            pl.BlockSpec((1, d), const),
            pl.BlockSpec(wk.shape, const),
            pl.BlockSpec(wv.shape, const),
            pl.BlockSpec((1, MEM_DIM), const),
        ],
        out_specs=[pl.BlockSpec((tm, mw), lambda i: (i, 0)), pl.BlockSpec((tm, mw), lambda i: (i, 0))],
        out_shape=[jax.ShapeDtypeStruct((t, mw), F32), jax.ShapeDtypeStruct((t, mw), F32)],
        compiler_params=_params(("arbitrary",)),
        name="mem_kv",
    )(mem, g, wk, wv, gk)


def _rope_tables(pos):
    half = QK_ROPE // 2
    inv_freq = ROPE_THETA ** (-np.arange(half, dtype=np.float64) / half)
    ang = np.asarray(pos, np.float64)[:, None] * inv_freq[None, :]
    cos, sin = np.cos(ang), np.sin(ang)
    zeros = np.zeros((ang.shape[0], LANE - QK_ROPE))
    return (jnp.asarray(np.concatenate([cos, cos, zeros], axis=1), F32),
            jnp.asarray(np.concatenate([-sin, sin, zeros], axis=1), F32))


def _pad_lanes(v, n=LANE):
    return jnp.pad(v, ((0, 0), (0, n - v.shape[1])))


def _prep_weights(w_in, g_cq, w_uq, g_ckv, w_ukv, g_q, g_k, w_o, w_mq, w_mo, w_ff1, w_ff2, w_mk, w_mv):
    d = w_in.shape[0]
    n_conv = 2 * GDN_HEADS * GDN_DK + GDN_HEADS * GDN_DV
    n_z = GDN_HEADS * GDN_DV
    o = 0
    w_cq = w_in[:, o:o + Q_RANK]; o += Q_RANK
    w_ckv = w_in[:, o:o + KV_RANK]; o += KV_RANK
    w_kpe = w_in[:, o:o + QK_ROPE]; o += QK_ROPE
    w_qkv = w_in[:, o:o + n_conv]; o += n_conv
    w_z = w_in[:, o:o + n_z]; o += n_z
    w_b = w_in[:, o:o + GDN_HEADS]; o += GDN_HEADS
    w_a = w_in[:, o:o + GDN_HEADS]
    half = QK_ROPE // 2
    swap = jnp.concatenate([jnp.arange(half, QK_ROPE), jnp.arange(0, half)])
    z64 = jnp.zeros((d, LANE - QK_ROPE), w_in.dtype)
    misc = jnp.concatenate([w_kpe, z64, w_kpe[:, swap], z64, w_b, w_a], axis=1)
    misc = _pad_lanes(misc, COL_TILE)
    w_in_p = jnp.concatenate([w_cq, w_ckv, w_qkv, w_z, misc], axis=1).astype(BF)
    wbat = jnp.concatenate([w_b, w_a], axis=1).T.astype(BF)

    r = w_uq.shape[0]
    wq3 = w_uq.reshape(r, MLA_HEADS, QK_HEAD)
    zq = jnp.zeros((r, MLA_HEADS, LANE - QK_ROPE), w_uq.dtype)
    w1 = jnp.concatenate([wq3, zq], axis=2).reshape(r, MLA_HEADS * QPAD).astype(BF)
    w2 = jnp.concatenate([wq3[:, :, QK_NOPE:][:, :, swap], zq], axis=2).reshape(r, MLA_HEADS * LANE).astype(BF)

    dmla = MLA_HEADS * V_DIM
    return dict(
        n_conv=n_conv, n_z=n_z, w_in_p=w_in_p, wbat=wbat, w1=w1, w2=w2,
        gcq=g_cq[None, :], gckv=g_ckv[None, :],
        gq_n=g_q[None, :QK_NOPE], gq_r=_pad_lanes(g_q[None, QK_NOPE:]),
        gk_n=g_k[None, :QK_NOPE], gk_r=_pad_lanes(g_k[None, QK_NOPE:]),
        w_ukv=w_ukv.astype(BF), wo1=w_o[:dmla].astype(BF), wo2=w_o[dmla:].astype(BF),
        w_mq=w_mq.astype(BF), w_mo=w_mo.astype(BF), w_ff1=w_ff1.astype(BF), w_ff2=w_ff2.astype(BF),
        w_mk=w_mk.astype(BF), w_mv=w_mv.astype(BF),
    )


def _gate_params(a_log, dt_bias):
    z8 = jnp.zeros((GDN_HEADS,), F32)
    al = jnp.concatenate([z8, a_log.astype(F32)])
    db = jnp.concatenate([z8, dt_bias.astype(F32)])
    return _pad_lanes(al[None, :]), _pad_lanes(db[None, :]), al[:, None], db[:, None]


def _layer(x, pos, past, conv_past, s0, mem_k, mem_v, wp, lw):
    nb, s_len, d = x.shape
    t = nb * s_len
    xf = x.reshape(t, d)
    ctab, stab = _rope_tables(pos)
    if ctab.shape[0] % 8 != 0 or (s_len < IN_TM and nb > 1):
        ctab, stab = jnp.tile(ctab, (nb, 1)), jnp.tile(stab, (nb, 1))

    cqn, ckv, kpe_pad, qkv, z, ba, bat = _in_proj(
        xf, lw["g_norm_mix"], wp["w_in_p"], wp["wbat"], wp["gcq"], wp["gckv"], ctab, stab,
        wp["n_conv"], wp["n_z"])
    q = _q_proj(cqn, wp["w1"], wp["w2"], ctab, stab, wp["gq_n"], wp["gq_r"])

    if past is None:
        ckv_all, kpe_all, q_off, t_valid = ckv, kpe_pad, 0, s_len
    else:
        ckv_past, kpe_past = past
        p_len = ckv_past.shape[1]
        t_valid = p_len + s_len
        t_pad = -(-t_valid // PROJ_TM) * PROJ_TM
        ckv_all = jnp.concatenate(
            [ckv_past.astype(F32), ckv.reshape(nb, s_len, KV_RANK),
             jnp.zeros((nb, t_pad - t_valid, KV_RANK), F32)], axis=1).reshape(nb * t_pad, KV_RANK)
        kpe_all = jnp.concatenate(
            [jnp.pad(kpe_past.astype(F32), ((0, 0), (0, 0), (0, LANE - QK_ROPE))),
             kpe_pad.reshape(nb, s_len, LANE),
             jnp.zeros((nb, t_pad - t_valid, LANE), F32)], axis=1).reshape(nb * t_pad, LANE)
        q_off = p_len
    k, v = _kv_proj(ckv_all, kpe_all, wp["w_ukv"], wp["gk_n"], wp["gk_r"])
    o_mla = _attention(q, k, v, lw["g_mla_out"], nb, q_off, t_valid)

    conv_past8 = jnp.pad(conv_past.astype(F32), ((0, 0), (HALO - (CONV_W - 1), 0), (0, 0)))
    conv_past8 = conv_past8.reshape(nb * HALO, -1)
    bat3 = bat.reshape(16, nb, s_len).transpose(1, 0, 2)
    alog_l, dtb_l, alog_c, dtb_c = _gate_params(lw["a_log"], lw["dt_bias"])
    o_gdn, s_new = _gdn(qkv, lw["conv_w"], conv_past8, ba, bat3, alog_l, dtb_l, alog_c, dtb_c,
                        z, s0.astype(F32), lw["g_gdn_out"], nb)

    h2 = _mix_mem(xf, o_mla, o_gdn, wp["wo1"], wp["wo2"], lw["g_norm_mem_q"], wp["w_mq"],
                  lw["g_mq"], mem_k, mem_v, wp["w_mo"], nb)
    y = _ffn(h2, lw["g_norm_ffn"], wp["w_ff1"], wp["w_ff2"])

    conv_in_tail = jnp.concatenate([conv_past.astype(F32), qkv.reshape(nb, s_len, -1)[:, -(CONV_W - 1):]], axis=1)
    conv_new = conv_in_tail[:, -(CONV_W - 1):]
    return (y.reshape(nb, s_len, d), ckv.reshape(nb, s_len, KV_RANK),
            kpe_pad[:, :QK_ROPE].reshape(nb, s_len, QK_ROPE), conv_new, s_new)


def kernel(x_prompt, x_sample, mem_prompt, cache_mla_ckv, cache_mla_kpe, cache_gdn_conv, state_gdn, cache_mem_k, cache_mem_v, g_norm_mix, w_in, g_cq, w_uq, g_ckv, w_ukv, g_q_mla, g_k_mla, g_mla_out, conv_w, a_log, dt_bias, g_gdn_out, w_o, g_norm_mem_q, g_norm_mem_kv, w_mq, w_mk, w_mv, g_mq, g_mk, w_mo, g_norm_ffn, w_ff1, w_ff2):
    depth = w_in.shape[0]
    nbp, sp, d = x_prompt.shape
    nbs, ss, _ = x_sample.shape
    n_mem = mem_prompt.shape[1]
    mw = MEM_HEADS * MEM_DIM
    n_conv = 2 * GDN_HEADS * GDN_DK + GDN_HEADS * GDN_DV
    pos_p = np.arange(sp)
    pos_s = cache_mla_ckv.shape[2] + np.arange(ss)
    zeros_conv = jnp.zeros((nbp, CONV_W - 1, n_conv), F32)
    zeros_state = jnp.zeros((nbp, GDN_HEADS, GDN_DK, GDN_DV), F32)
    hp, hs = x_prompt, x_sample
    outs_p = [[] for _ in range(6)]
    outs_s = [[] for _ in range(4)]
    for l in range(depth):
        wp = _prep_weights(w_in[l], g_cq[l], w_uq[l], g_ckv[l], w_ukv[l], g_q_mla[l], g_k_mla[l], w_o[l],
                           w_mq[l], w_mo[l], w_ff1[l], w_ff2[l], w_mk[l], w_mv[l])
        lw = dict(g_norm_mix=g_norm_mix[l][None, :], g_mla_out=g_mla_out[l][None, :], conv_w=conv_w[l],
                  a_log=a_log[l], dt_bias=dt_bias[l], g_gdn_out=g_gdn_out[l][None, :],
                  g_norm_mem_q=g_norm_mem_q[l][None, :], g_mq=g_mq[l][None, :],
                  g_norm_ffn=g_norm_ffn[l][None, :])
        mk, mv = _mem_kv(mem_prompt.reshape(nbp * n_mem, d), g_norm_mem_kv[l][None, :], wp["w_mk"], wp["w_mv"],
                         g_mk[l][None, :])
        mk3, mv3 = mk.reshape(nbp, n_mem, mw), mv.reshape(nbp, n_mem, mw)
        hp, c1, c2, c3, c4 = _layer(hp, pos_p, None, zeros_conv, zeros_state, mk3.astype(BF), mv3.astype(BF), wp, lw)
        for lst, val in zip(outs_p, (c1, c2, c3, c4, mk3.reshape(nbp, n_mem, MEM_HEADS, MEM_DIM),
                                     mv3.reshape(nbp, n_mem, MEM_HEADS, MEM_DIM))):
            lst.append(val)
        hs, d1, d2, d3, d4 = _layer(hs, pos_s, (cache_mla_ckv[l], cache_mla_kpe[l]), cache_gdn_conv[l], state_gdn[l],
                                    cache_mem_k[l].reshape(nbs, n_mem, mw).astype(BF),
                                    cache_mem_v[l].reshape(nbs, n_mem, mw).astype(BF), wp, lw)
        for lst, val in zip(outs_s, (d1, d2, d3, d4)):
            lst.append(val)
    return (hp, hs, *(jnp.stack(v) for v in outs_p), *(jnp.stack(v) for v in outs_s))
```

```python
import functools
import math

import jax
import jax.numpy as jnp
import numpy as np
from jax import lax
from jax.experimental import pallas as pl
from jax.experimental.pallas import tpu as pltpu

F32 = jnp.float32
BF = jnp.bfloat16

EPS = 1e-6
CHUNK = 64
ROPE_THETA = 10000.0
MLA_HEADS = 8
QK_NOPE = 128
QK_ROPE = 64
QK_HEAD = QK_NOPE + QK_ROPE
V_DIM = 128
Q_RANK = 512
KV_RANK = 512
GDN_HEADS = 8
GDN_DK = 128
GDN_DV = 128
CONV_W = 4
MEM_HEADS = 4
MEM_DIM = 128
LANE = 128
QPAD = 2 * LANE
COL_TILE = 1024
HALO = 8
NEG_BIG = -1e30
QSCALE = (QK_HEAD ** -0.5) * math.log2(math.e)

VMEM_LIMIT = 56 * 1024 * 1024

IN_TM = 1024
PROJ_TM = 512
ATTN_TQ = 1024
ATTN_TK = 1024
ATTN_WIDE = 2
GDN_TM = 256
GDN_WAVES = 2
MIX_TM = 512
FFN_TM = 1024
FFN_TF = 512


def _tile(n, pref, mult=8):
    if n <= pref:
        return n
    t = (pref // mult) * mult
    while t >= mult:
        if n % t == 0:
            return t
        t -= mult
    return n


def _nt(a, b):
    return lax.dot_general(a, b, (((1,), (1,)), ((), ())), preferred_element_type=F32)


def _mm(a, b):
    return jnp.dot(a, b, preferred_element_type=F32)


def _sigmoid(x):
    return 1.0 / (1.0 + jnp.exp(-x))


def _softplus(x):
    return jnp.maximum(x, 0.0) + jnp.log(1.0 + jnp.exp(-jnp.abs(x)))


def _rms(x, g, n=None):
    n = x.shape[-1] if n is None else n
    ms = jnp.sum(x * x, axis=-1, keepdims=True) * (1.0 / n)
    return (x * lax.rsqrt(ms + EPS)) * g


def _split3(x):
    hi = x.astype(BF)
    r1 = x - hi.astype(F32)
    mid = r1.astype(BF)
    lo = (r1 - mid.astype(F32)).astype(BF)
    return hi, mid, lo


def _interleave(lists):
    items = []
    for li, lst in enumerate(lists):
        items += [((i + 0.5) / len(lst), li, f) for i, f in enumerate(lst)]
    return [f for _, _, f in sorted(items, key=lambda it: (it[0], it[1]))]


def _params(sem):
    return pltpu.CompilerParams(dimension_semantics=sem, vmem_limit_bytes=VMEM_LIMIT)


def _in_proj_body(x_ref, g_ref, w_ref, wbat_ref, gcq_ref, gckv_ref, ctab_ref, stab_ref,
                  cqn_ref, ckv_ref, kpe_ref, qkv_ref, z_ref, ba_ref, bat_ref, xn_scr, *, nqkv, nz):
    j = pl.program_id(1)

    @pl.when(j == 0)
    def _():
        xn_scr[...] = _rms(x_ref[...], g_ref[...]).astype(BF)

    def halves(store):
        hw = COL_TILE // 2
        for c0 in (0, hw):
            store(c0, hw, _mm(xn_scr[...], w_ref[:, c0:c0 + hw]))

    @pl.when(j == 0)
    def _():
        cqn_ref[...] = _rms(_mm(xn_scr[...], w_ref[:, 0:Q_RANK]), gcq_ref[...]).astype(BF)
        ckv_ref[...] = _rms(_mm(xn_scr[...], w_ref[:, Q_RANK:Q_RANK + KV_RANK]), gckv_ref[...])

    @pl.when((j >= 1) & (j < 1 + nqkv))
    def _():
        def store(c0, n, a):
            qkv_ref[:, c0:c0 + n] = a
        halves(store)

    @pl.when((j >= 1 + nqkv) & (j < 1 + nqkv + nz))
    def _():
        def store(c0, n, a):
            z_ref[:, c0:c0 + n] = a.astype(BF)
        halves(store)

    @pl.when(j == 1 + nqkv + nz)
    def _():
        acc = _mm(xn_scr[...], w_ref[:, 0:3 * LANE])
        kpe_ref[...] = acc[:, 0:LANE] * ctab_ref[...] + acc[:, LANE:2 * LANE] * stab_ref[...]
        ba_ref[...] = acc[:, 2 * LANE:3 * LANE]
        bat_ref[...] = _nt(wbat_ref[...], xn_scr[...])


def _in_proj(x, g, w_p, wbat, gcq, gckv, ctab, stab, n_conv, n_z):
    t, d = x.shape
    tm = _tile(min(t, ctab.shape[0]), IN_TM)
    ntab = ctab.shape[0] // tm
    nqkv = n_conv // COL_TILE
    nz = n_z // COL_TILE
    ncol = 1 + nqkv + nz + 1
    assert Q_RANK + KV_RANK == COL_TILE and w_p.shape[1] == ncol * COL_TILE
    row = lambda i, j: (i, 0)
    const = lambda i, j: (0, 0)
    body = functools.partial(_in_proj_body, nqkv=nqkv, nz=nz)
    return pl.pallas_call(
        body,
        grid=(t // tm, ncol),
        in_specs=[
            pl.BlockSpec((tm, d), row),
            pl.BlockSpec((1, d), const),
            pl.BlockSpec((d, COL_TILE), lambda i, j: (0, j)),
            pl.BlockSpec((16, d), const),
            pl.BlockSpec((1, Q_RANK), const),
            pl.BlockSpec((1, KV_RANK), const),
            pl.BlockSpec((tm, LANE), lambda i, j: (i % ntab, 0)),
            pl.BlockSpec((tm, LANE), lambda i, j: (i % ntab, 0)),
        ],
        out_specs=[
            pl.BlockSpec((tm, Q_RANK), row),
            pl.BlockSpec((tm, KV_RANK), row),
            pl.BlockSpec((tm, LANE), row),
            pl.BlockSpec((tm, COL_TILE), lambda i, j: (i, jnp.clip(j - 1, 0, nqkv - 1))),
            pl.BlockSpec((tm, COL_TILE), lambda i, j: (i, jnp.clip(j - 1 - nqkv, 0, nz - 1))),
            pl.BlockSpec((tm, LANE), row),
            pl.BlockSpec((16, tm), lambda i, j: (0, i)),
        ],
        out_shape=[
            jax.ShapeDtypeStruct((t, Q_RANK), BF),
            jax.ShapeDtypeStruct((t, KV_RANK), F32),
            jax.ShapeDtypeStruct((t, LANE), F32),
            jax.ShapeDtypeStruct((t, n_conv), F32),
            jax.ShapeDtypeStruct((t, n_z), BF),
            jax.ShapeDtypeStruct((t, LANE), F32),
            jax.ShapeDtypeStruct((16, t), F32),
        ],
        scratch_shapes=[pltpu.VMEM((tm, d), BF)],
        compiler_params=_params(("arbitrary", "arbitrary")),
        name="in_proj",
    )(x, g, w_p, wbat, gcq, gckv, ctab, stab)


def _q_proj_body(c_ref, w1_ref, w2_ref, ctab_ref, stab_ref, gn_ref, gr_ref, q_ref):
    c = c_ref[...]
    qf = _mm(c, w1_ref[...])
    qs = _mm(c, w2_ref[...])
    ct = ctab_ref[...]
    st = stab_ref[...]
    gn = gn_ref[...]
    gr = gr_ref[...]
    for h in range(MLA_HEADS):
        nope = qf[:, h * QPAD:h * QPAD + LANE]
        rot = qf[:, h * QPAD + LANE:(h + 1) * QPAD] * ct + qs[:, h * LANE:(h + 1) * LANE] * st
        ss = jnp.sum(nope * nope, axis=-1, keepdims=True) + jnp.sum(rot * rot, axis=-1, keepdims=True)
        rs = lax.rsqrt(ss * (1.0 / QK_HEAD) + EPS) * QSCALE
        q_ref[:, h * QPAD:h * QPAD + LANE] = (nope * rs * gn).astype(BF)
        q_ref[:, h * QPAD + LANE:(h + 1) * QPAD] = (rot * rs * gr).astype(BF)


def _q_proj(cqn, w1, w2, ctab, stab, gn, gr):
    t = cqn.shape[0]
    tm = _tile(min(t, ctab.shape[0]), PROJ_TM)
    ntab = ctab.shape[0] // tm
    const = lambda i: (0, 0)
    return pl.pallas_call(
        _q_proj_body,
        grid=(t // tm,),
        in_specs=[
            pl.BlockSpec((tm, Q_RANK), lambda i: (i, 0)),
            pl.BlockSpec(w1.shape, const),
            pl.BlockSpec(w2.shape, const),
            pl.BlockSpec((tm, LANE), lambda i: (i % ntab, 0)),
            pl.BlockSpec((tm, LANE), lambda i: (i % ntab, 0)),
            pl.BlockSpec((1, LANE), const),
            pl.BlockSpec((1, LANE), const),
        ],
        out_specs=pl.BlockSpec((tm, MLA_HEADS * QPAD), lambda i: (i, 0)),
        out_shape=jax.ShapeDtypeStruct((t, MLA_HEADS * QPAD), BF),
        compiler_params=_params(("arbitrary",)),
        name="q_proj",
    )(cqn, w1, w2, ctab, stab, gn, gr)


def _kv_proj_body(c_ref, kpe_ref, w_ref, gn_ref, gr_ref, k_ref, v_ref):
    kv = _mm(c_ref[...].astype(BF), w_ref[...])
    kp = kpe_ref[...]
    kps = jnp.sum(kp * kp, axis=-1, keepdims=True)
    gn = gn_ref[...]
    gr = gr_ref[...]
    for h in range(MLA_HEADS):
        kn = kv[:, h * 2 * LANE:h * 2 * LANE + LANE]
        rs = lax.rsqrt((jnp.sum(kn * kn, axis=-1, keepdims=True) + kps) * (1.0 / QK_HEAD) + EPS)
        k_ref[:, h * QPAD:h * QPAD + LANE] = (kn * rs * gn).astype(BF)
        k_ref[:, h * QPAD + LANE:(h + 1) * QPAD] = (kp * rs * gr).astype(BF)
        v_ref[:, h * V_DIM:(h + 1) * V_DIM] = kv[:, h * 2 * LANE + LANE:(h + 1) * 2 * LANE].astype(BF)


def _kv_proj(ckv, kpe_pad, w, gn, gr):
    t = ckv.shape[0]
    tm = _tile(t, PROJ_TM)
    const = lambda i: (0, 0)
    return pl.pallas_call(
        _kv_proj_body,
        grid=(t // tm,),
        in_specs=[
            pl.BlockSpec((tm, KV_RANK), lambda i: (i, 0)),
            pl.BlockSpec((tm, LANE), lambda i: (i, 0)),
            pl.BlockSpec(w.shape, const),
            pl.BlockSpec((1, LANE), const),
            pl.BlockSpec((1, LANE), const),
        ],
        out_specs=[
            pl.BlockSpec((tm, MLA_HEADS * QPAD), lambda i: (i, 0)),
            pl.BlockSpec((tm, MLA_HEADS * V_DIM), lambda i: (i, 0)),
        ],
        out_shape=[
            jax.ShapeDtypeStruct((t, MLA_HEADS * QPAD), BF),
            jax.ShapeDtypeStruct((t, MLA_HEADS * V_DIM), BF),
        ],
        compiler_params=_params(("arbitrary",)),
        name="kv_proj",
    )(ckv, kpe_pad, w, gn, gr)


def _attn_body(q_ref, k_ref, v_ref, g_ref, o_ref, m_scr, l_scr, acc_scr, *, tq, tk, wide, q_off, t_valid):
    qi = pl.program_id(2)
    qpos0 = q_off + qi * tq
    n_full = jnp.minimum((qpos0 // CHUNK * CHUNK + CHUNK) // tk, t_valid // tk)
    hi = jnp.minimum((qpos0 + tq - 1) // CHUNK * CHUNK + CHUNK, t_valid)
    n_total = (hi + tk - 1) // tk

    m_scr[...] = jnp.full(m_scr.shape, NEG_BIG, F32)
    l_scr[...] = jnp.zeros(l_scr.shape, F32)
    acc_scr[...] = jnp.zeros(acc_scr.shape, F32)
    diag = q_off % tk == 0 and tq == tk and t_valid % tk == 0 and tk % (2 * LANE) == 0 and tq % 32 == 0

    def step(kc, width, masked, r0=0, nr=tq):
        rows = slice(r0, r0 + nr)
        k0 = pl.multiple_of(kc * tk, tk)
        s = _nt(q_ref[rows, :], k_ref[pl.ds(k0, width), :])
        if masked:
            qpos = qpos0 + r0 + lax.broadcasted_iota(jnp.int32, (nr, 1), 0)
            last = jnp.minimum(qpos | (CHUNK - 1), t_valid - 1) - k0
            s = jnp.where(lax.broadcasted_iota(jnp.int32, (nr, width), 1) <= last, s, NEG_BIG)
        m_prev = m_scr[rows, :]
        m_new = jnp.maximum(m_prev, jnp.max(s, axis=-1, keepdims=True))
        alpha = jnp.exp2(m_prev - m_new)
        ps = [jnp.exp2(s[:, j * LANE:(j + 1) * LANE] - m_new) for j in range(width // LANE)]
        psum = ps[0]
        for pj in ps[1:]:
            psum = psum + pj
        l_scr[rows, :] = alpha * l_scr[rows, :] + psum
        p = jnp.concatenate(ps, axis=1).astype(BF) if len(ps) > 1 else ps[0].astype(BF)
        acc_scr[rows, :] = acc_scr[rows, :] * alpha + _mm(p, v_ref[pl.ds(k0, width), :])
        m_scr[rows, :] = m_new

    def loop(lo, hi, fn):
        lax.fori_loop(lo, hi, lambda i, c: (fn(i), c)[1], 0)

    n_wide = n_full // wide
    loop(0, n_wide, lambda i: step(i * wide, wide * tk, False))
    loop(n_wide * wide, n_full, lambda kc: step(kc, tk, False))
    if diag:
        def masked(kc):
            step(kc, tk // 2, True, 0, tq // 2)
            step(kc, tk, True, tq // 2, tq // 2)
    else:
        def masked(kc):
            step(kc, tk, True)
    loop(n_full, n_total, masked)
    o = acc_scr[...] / jnp.sum(l_scr[...], axis=-1, keepdims=True)
    o_ref[...] = _rms(o, g_ref[...]).astype(BF)


def _attention(q, k, v, g_out, nb, q_off, t_valid):
    tq_total = q.shape[0] // nb
    tk_total = k.shape[0] // nb
    tq = _tile(tq_total, ATTN_TQ)
    tk = _tile(tk_total, ATTN_TK, mult=LANE)
    nq = tq_total // tq
    body = functools.partial(_attn_body, tq=tq, tk=tk, wide=ATTN_WIDE, q_off=q_off, t_valid=t_valid)
    return pl.pallas_call(
        body,
        grid=(nb, MLA_HEADS, nq),
        in_specs=[
            pl.BlockSpec((tq, QPAD), lambda b, h, i: (b * nq + i, h)),
            pl.BlockSpec((tk_total, QPAD), lambda b, h, i: (b, h)),
            pl.BlockSpec((tk_total, V_DIM), lambda b, h, i: (b, h)),
            pl.BlockSpec((1, V_DIM), lambda b, h, i: (0, 0)),
        ],
        out_specs=pl.BlockSpec((tq, V_DIM), lambda b, h, i: (b * nq + i, h)),
        out_shape=jax.ShapeDtypeStruct((q.shape[0], MLA_HEADS * V_DIM), BF),
        scratch_shapes=[
            pltpu.VMEM((tq, LANE), F32),
            pltpu.VMEM((tq, LANE), F32),
            pltpu.VMEM((tq, V_DIM), F32),
        ],
        compiler_params=_params(("arbitrary", "arbitrary", "arbitrary")),
        name="mla_attention",
    )(q, k, v, g_out)


def _gdn_body(q_ref, k_ref, v_ref, wq_ref, wk_ref, wv_ref, pq_ref, pk_ref, pv_ref,
              ba_ref, bat_ref, alog_l_ref, dtb_l_ref, alog_c_ref, dtb_c_ref,
              z_ref, s0_ref, gout_ref,
              o_ref, sout_ref,
              s_scr, halo_scr, vnew_scr, *, tm, L, hg):
    t = pl.program_id(2)
    nt = pl.num_programs(2)
    n_chunks = tm // L
    n_factors = int(math.log2(L)) - 1

    @pl.when(t == 0)
    def _():
        s_scr[...] = s0_ref[0]
        halo_scr[0] = pq_ref[...]
        halo_scr[1] = pk_ref[...]
        halo_scr[2] = pv_ref[...]
        vnew_scr[...] = jnp.zeros(vnew_scr.shape, BF)

    ba = ba_ref[...]
    beta_all = _sigmoid(ba)
    g_all = -jnp.exp(alog_l_ref[...]) * _softplus(ba + dtb_l_ref[...])
    gt_all = -jnp.exp(alog_c_ref[...]) * _softplus(bat_ref[0] + dtb_c_ref[...])

    r = lax.broadcasted_iota(jnp.int32, (tm, tm), 0)
    c = lax.broadcasted_iota(jnp.int32, (tm, tm), 1)
    same = (r // L) == (c // L)
    lower = same & (c <= r)
    strict = same & (c < r)
    cs = jnp.where(lower, 1.0, 0.0).astype(BF)
    bd = jnp.where(same, 1.0, 0.0).astype(BF)
    g3 = _split3(g_all)
    gcol_all = _mm(cs, g3[0]) + _mm(cs, g3[1]) + _mm(cs, g3[2])
    glast_all = _mm(bd, g3[0]) + _mm(bd, g3[1]) + _mm(bd, g3[2])
    gt3 = _split3(gt_all)
    grow_all = _nt(gt3[0], cs) + _nt(gt3[1], cs) + _nt(gt3[2], cs)

    gout = gout_ref[...]
    eye_t = jnp.where(r == c, 1.0, 0.0)
    eye_k = jnp.where(lax.broadcasted_iota(jnp.int32, (GDN_DK, GDN_DK), 0)
                      == lax.broadcasted_iota(jnp.int32, (GDN_DK, GDN_DK), 1), 1.0, 0.0).astype(BF)
    col_chunk = lax.broadcasted_iota(jnp.int32, (1, tm), 1) // L

    def make_wave(heads):
        c0, c1 = heads[0] * GDN_DK, (heads[-1] + 1) * GDN_DK
        loc = lambda h: slice((h - heads[0]) * GDN_DK, (h - heads[0] + 1) * GDN_DK)
        w = {}

        def conv(idx, x_ref, w_ref):
            xs = jnp.concatenate([halo_scr[idx, :, c0:c1], x_ref[:, c0:c1]], axis=0)
            cw = w_ref[:, c0:c1]
            acc = xs * cw[0:1]
            for i in range(1, CONV_W):
                acc = pltpu.roll(acc, 1, axis=0) + xs * cw[i:i + 1]
            halo_scr[idx, :, c0:c1] = xs[tm:tm + HALO]
            y = acc[HALO:HALO + tm]
            return y * _sigmoid(y)

        def p_q():
            w["yq"] = conv(0, q_ref, wq_ref)

        def p_k():
            w["yk"] = conv(1, k_ref, wk_ref)

        def p_v():
            w["yv"] = conv(2, v_ref, wv_ref)

        def p_gate():
            w["beta"], w["gc"], w["glast"], w["gr"] = {}, {}, {}, {}
            for h in heads:
                g = GDN_HEADS + h
                w["beta"][h] = beta_all[:, h:h + 1]
                w["gc"][h] = gcol_all[:, g:g + 1]
                w["glast"][h] = glast_all[:, g:g + 1]
                w["gr"][h] = grow_all[g:g + 1, :]

        def p_norm():
            w["qn"], w["kn"] = {}, {}
            for h in heads:
                qh = w["yq"][:, loc(h)]
                kh = w["yk"][:, loc(h)]
                w["qn"][h] = qh * lax.rsqrt(jnp.sum(qh * qh, axis=-1, keepdims=True) + EPS) * (GDN_DK ** -0.5)
                w["kn"][h] = kh * lax.rsqrt(jnp.sum(kh * kh, axis=-1, keepdims=True) + EPS)
            kb = {h: w["kn"][h].astype(BF) for h in heads}
            w["kk"] = {h: _nt(kb[h], kb[h]) for h in heads}
            w["qk"] = {h: _nt(w["qn"][h].astype(BF), kb[h]) for h in heads}

        def p_mat():
            beta, gc = w["beta"], w["gc"]
            decay = {h: jnp.where(lower, jnp.exp(gc[h] - w["gr"][h]), 0.0) for h in heads}
            w["nm"] = {h: jnp.where(strict, beta[h] * w["kk"][h] * decay[h], 0.0) for h in heads}
            w["attn"] = {h: (w["qk"][h] * decay[h]).astype(BF) for h in heads}
            w["eg"] = {h: jnp.exp(gc[h]) for h in heads}
            w["rhs"] = {h: jnp.concatenate([w["yv"][:, loc(h)] * beta[h],
                                            w["kn"][h] * (beta[h] * w["eg"][h])], axis=1) for h in heads}
            w["nb"] = {h: w["nm"][h].astype(BF) for h in heads}

        def s_first():
            w["m"] = {h: _mm(w["nb"][h], w["nb"][h]) for h in heads}
            w["T"] = {h: eye_t - w["nm"][h] for h in heads}

        def s_level(s):
            xb = {h: jnp.concatenate([w["m"][h], w["T"][h]], axis=0).astype(BF) for h in heads}
            if s < n_factors - 1:
                prod = {h: _mm(xb[h], xb[h][:tm]) for h in heads}
                w["T"] = {h: w["T"][h] + prod[h][tm:] for h in heads}
                w["m"] = {h: prod[h][:tm] for h in heads}
            else:
                w["T"] = {h: w["T"][h] + _mm(xb[h][tm:], xb[h][:tm]) for h in heads}

        def s_y0():
            w["tb"] = {h: w["T"][h].astype(BF) for h in heads}
            w["y0"] = {h: _mm(w["tb"][h], w["rhs"][h].astype(BF)) for h in heads}

        def s_res():
            res = {}
            for h in heads:
                nbf = w["nb"][h].astype(F32)
                nx = jnp.concatenate([nbf, w["nm"][h] - nbf], axis=0).astype(BF)
                yh = w["y0"][h].astype(BF)
                yl = (w["y0"][h] - yh.astype(F32)).astype(BF)
                p1 = _mm(nx, yh)
                res[h] = w["rhs"][h] - w["y0"][h] - (p1[:tm] + p1[tm:] + _mm(nx[:tm], yl))
            w["res"] = res

        def s_fix():
            w["y"] = {h: w["y0"][h] + _mm(w["tb"][h], w["res"][h].astype(BF)) for h in heads}

        def c_prep():
            y = w["y"]
            w["u"] = {h: y[h][:, :GDN_DV] for h in heads}
            w["wq"] = {}
            for h in heads:
                qd = w["qn"][h] * w["eg"][h]
                parts = []
                for ci in range(n_chunks):
                    parts += [y[h][ci * L:(ci + 1) * L, GDN_DV:], qd[ci * L:(ci + 1) * L]]
                w["wq"][h] = jnp.concatenate(parts, axis=0).astype(BF)
            kdb = {h: (w["kn"][h] * jnp.exp(w["glast"][h] - w["gc"][h])).astype(BF) for h in heads}
            w["gtot"] = {h: jnp.exp(w["glast"][h]) for h in heads}
            kdt = {h: _nt(eye_k, kdb[h]) for h in heads}
            w["kdt"] = {h: [jnp.where(col_chunk == ci, kdt[h], 0.0).astype(BF) for ci in range(n_chunks)]
                        for h in heads}
            w["st"] = {h: s_scr[h] for h in heads}
            w["outs"] = {h: [] for h in heads}

        def c_chunk(ci):
            r0 = ci * L
            st = w["st"]
            ws = {h: _mm(w["wq"][h][2 * r0:2 * r0 + 2 * L], st[h].astype(BF)) for h in heads}
            for h in heads:
                vnew_scr[h, r0:r0 + L, :] = (w["u"][h][r0:r0 + L] - ws[h][:L]).astype(BF)
            for h in heads:
                w["outs"][h].append(ws[h][L:] + _mm(w["attn"][h][r0:r0 + L, :], vnew_scr[h]))
            w["st"] = {h: st[h] * w["gtot"][h][r0:r0 + 1, :] + _mm(w["kdt"][h][ci], vnew_scr[h]) for h in heads}

        def c_out():
            for h in heads:
                s_scr[h] = w["st"][h]
                o = w["outs"][h][0] if n_chunks == 1 else jnp.concatenate(w["outs"][h], axis=0)
                zz = z_ref[:, h * GDN_DV:(h + 1) * GDN_DV].astype(F32)
                o_ref[:, h * GDN_DV:(h + 1) * GDN_DV] = (_rms(o, gout) * (zz * _sigmoid(zz))).astype(BF)

        prep = [p_q, p_k, p_v, p_gate, p_norm, p_mat]
        solve = ([s_first] + [functools.partial(s_level, s) for s in range(n_factors)] + [s_y0, s_res, s_fix])
        scan = [c_prep] + [functools.partial(c_chunk, ci) for ci in range(n_chunks)] + [c_out]
        return [prep, solve, scan]

    nw = min(GDN_WAVES, hg)
    per = hg // nw
    waves = [make_wave(list(range(i * per, (i + 1) * per))) for i in range(nw)]
    for slot in range(nw + 2):
        active = [waves[i][slot - i] for i in range(nw) if 0 <= slot - i < 3]
        for stage in _interleave(active):
            stage()

    @pl.when(t == nt - 1)
    def _():
        sout_ref[0] = s_scr[...]


def _gdn(qkv, conv_w, conv_past8, ba, bat, alog_l, dtb_l, alog_c, dtb_c, z, s0, gout, nb):
    t = qkv.shape[0]
    s_len = t // nb
    L = min(s_len, CHUNK)
    tm = _tile(s_len, GDN_TM, mult=L)
    nt = s_len // tm
    hg = GDN_HEADS
    hw = hg * GDN_DK
    body = functools.partial(_gdn_body, tm=tm, L=L, hg=hg)
    rows = lambda off: (lambda b, h, i: (b * nt + i, off + h))
    wcol = lambda off: (lambda b, h, i: (0, off + h))
    pcol = lambda off: (lambda b, h, i: (b, off + h))
    const = lambda b, h, i: (0, 0)
    ng = GDN_HEADS // hg
    return pl.pallas_call(
        body,
        grid=(nb, ng, nt),
        in_specs=[
            pl.BlockSpec((tm, hw), rows(0)),
            pl.BlockSpec((tm, hw), rows(ng)),
            pl.BlockSpec((tm, hw), rows(2 * ng)),
            pl.BlockSpec((CONV_W, hw), wcol(0)),
            pl.BlockSpec((CONV_W, hw), wcol(ng)),
            pl.BlockSpec((CONV_W, hw), wcol(2 * ng)),
            pl.BlockSpec((HALO, hw), pcol(0)),
            pl.BlockSpec((HALO, hw), pcol(ng)),
            pl.BlockSpec((HALO, hw), pcol(2 * ng)),
            pl.BlockSpec((tm, LANE), lambda b, h, i: (b * nt + i, 0)),
            pl.BlockSpec((1, 16, tm), lambda b, h, i: (b, 0, i)),
            pl.BlockSpec((1, LANE), const),
            pl.BlockSpec((1, LANE), const),
            pl.BlockSpec((16, 1), const),
            pl.BlockSpec((16, 1), const),
            pl.BlockSpec((tm, hw), rows(0)),
            pl.BlockSpec((1, hg, GDN_DK, GDN_DV), lambda b, h, i: (b, h, 0, 0)),
            pl.BlockSpec((1, GDN_DV), const),
        ],
        out_specs=[
            pl.BlockSpec((tm, hw), rows(0)),
            pl.BlockSpec((1, hg, GDN_DK, GDN_DV), lambda b, h, i: (b, h, 0, 0)),
        ],
        out_shape=[
            jax.ShapeDtypeStruct((t, GDN_HEADS * GDN_DV), BF),
            jax.ShapeDtypeStruct((nb, GDN_HEADS, GDN_DK, GDN_DV), F32),
        ],
        scratch_shapes=[
            pltpu.VMEM((hg, GDN_DK, GDN_DV), F32),
            pltpu.VMEM((3, HALO, hw), F32),
            pltpu.VMEM((hg, tm, GDN_DV), BF),
        ],
        compiler_params=_params(("arbitrary", "arbitrary", "arbitrary")),
        name="gdn",
    )(qkv, qkv, qkv, conv_w, conv_w, conv_w, conv_past8, conv_past8, conv_past8,
      ba, bat, alog_l, dtb_l, alog_c, dtb_c, z, s0, gout)


def _mix_mem_body(x_ref, a1_ref, a2_ref, wo1_ref, wo2_ref, gq_ref, wmq_ref, gmq_ref,
                  mk_ref, mv_ref, wmo_ref, o_ref):
    h1 = x_ref[...] + _mm(a1_ref[...], wo1_ref[...]) + _mm(a2_ref[...], wo2_ref[...])
    hn = _rms(h1, gq_ref[...]).astype(BF)
    q = _mm(hn, wmq_ref[...])
    gmq = gmq_ref[...]
    mk = mk_ref[0]
    mv = mv_ref[0]
    outs = []
    for h in range(MEM_HEADS):
        sl = slice(h * MEM_DIM, (h + 1) * MEM_DIM)
        qh = _rms(q[:, sl], gmq).astype(BF)
        s = _nt(qh, mk[:, sl]) * (MEM_DIM ** -0.5)
        p = jnp.exp(s - jnp.max(s, axis=-1, keepdims=True))
        p = p / jnp.sum(p, axis=-1, keepdims=True)
        outs.append(_mm(p.astype(BF), mv[:, sl]))
    o = jnp.concatenate(outs, axis=1).astype(BF)
    o_ref[...] = h1 + _mm(o, wmo_ref[...])


def _mix_mem(x, a1, a2, wo1, wo2, gq, wmq, gmq, mk, mv, wmo, nb):
    t, d = x.shape
    s_len = t // nb
    tm = _tile(s_len, MIX_TM)
    nt = s_len // tm
    n_mem = mk.shape[1]
    mw = MEM_HEADS * MEM_DIM
    row = lambda i: (i, 0)
    const = lambda i: (0, 0)
    return pl.pallas_call(
        _mix_mem_body,
        grid=(t // tm,),
        in_specs=[
            pl.BlockSpec((tm, d), row),
            pl.BlockSpec((tm, a1.shape[1]), row),
            pl.BlockSpec((tm, a2.shape[1]), row),
            pl.BlockSpec(wo1.shape, const),
            pl.BlockSpec(wo2.shape, const),
            pl.BlockSpec((1, d), const),
            pl.BlockSpec(wmq.shape, const),
            pl.BlockSpec((1, MEM_DIM), const),
            pl.BlockSpec((1, n_mem, mw), lambda i: (i // nt, 0, 0)),
            pl.BlockSpec((1, n_mem, mw), lambda i: (i // nt, 0, 0)),
            pl.BlockSpec(wmo.shape, const),
        ],
        out_specs=pl.BlockSpec((tm, d), row),
        out_shape=jax.ShapeDtypeStruct((t, d), F32),
        compiler_params=_params(("arbitrary",)),
        name="mix_mem",
    )(x, a1, a2, wo1, wo2, gq, wmq, gmq, mk, mv, wmo)


def _ffn_body(x_ref, g_ref, w1_ref, w2_ref, o_ref, hn_scr):
    f = pl.program_id(1)

    @pl.when(f == 0)
    def _():
        x = x_ref[...]
        hn_scr[...] = _rms(x, g_ref[...]).astype(BF)
        o_ref[...] = x

    a = jnp.maximum(_mm(hn_scr[...], w1_ref[...]), 0.0)
    o_ref[...] += _mm((a * a).astype(BF), w2_ref[...])


def _ffn(x, g, w1, w2):
    t, d = x.shape
    dff = w1.shape[1]
    tm = _tile(t, FFN_TM)
    tf = _tile(dff, FFN_TF, mult=LANE)
    return pl.pallas_call(
        _ffn_body,
        grid=(t // tm, dff // tf),
        in_specs=[
            pl.BlockSpec((tm, d), lambda i, f: (i, 0)),
            pl.BlockSpec((1, d), lambda i, f: (0, 0)),
            pl.BlockSpec((d, tf), lambda i, f: (0, f)),
            pl.BlockSpec((tf, d), lambda i, f: (f, 0)),
        ],
        out_specs=pl.BlockSpec((tm, d), lambda i, f: (i, 0)),
        out_shape=jax.ShapeDtypeStruct((t, d), F32),
        scratch_shapes=[pltpu.VMEM((tm, d), BF)],
        compiler_params=_params(("arbitrary", "arbitrary")),
        name="ffn",
    )(x, g, w1, w2)


def _mem_kv_body(m_ref, g_ref, wk_ref, wv_ref, gk_ref, k_ref, v_ref):
    mn = _rms(m_ref[...], g_ref[...]).astype(BF)
    k = _mm(mn, wk_ref[...])
    gk = gk_ref[...]
    for h in range(MEM_HEADS):
        sl = slice(h * MEM_DIM, (h + 1) * MEM_DIM)
        k_ref[:, sl] = _rms(k[:, sl], gk)
    v_ref[...] = _mm(mn, wv_ref[...])


def _mem_kv(mem, g, wk, wv, gk):
    t, d = mem.shape
    tm = _tile(t, 256)
    mw = MEM_HEADS * MEM_DIM
    const = lambda i: (0, 0)
    return pl.pallas_call(
        _mem_kv_body,
        grid=(t // tm,),
        in_specs=[
            pl.BlockSpec((tm, d), lambda i: (i, 0)),
            pl.BlockSpec((1, d), const),
            pl.BlockSpec(wk.shape, const),
            pl.BlockSpec(wv.shape, const),
            pl.BlockSpec((1, MEM_DIM), const),
        ],
        out_specs=[pl.BlockSpec((tm, mw), lambda i: (i, 0)), pl.BlockSpec((tm, mw), lambda i: (i, 0))],
        out_shape=[jax.ShapeDtypeStruct((t, mw), F32), jax.ShapeDtypeStruct((t, mw), F32)],
        compiler_params=_params(("arbitrary",)),
        name="mem_kv",
    )(mem, g, wk, wv, gk)


def _rope_tables(pos):
    half = QK_ROPE // 2
    inv_freq = ROPE_THETA ** (-np.arange(half, dtype=np.float64) / half)
    ang = np.asarray(pos, np.float64)[:, None] * inv_freq[None, :]
    cos, sin = np.cos(ang), np.sin(ang)
    zeros = np.zeros((ang.shape[0], LANE - QK_ROPE))
    return (jnp.asarray(np.concatenate([cos, cos, zeros], axis=1), F32),
            jnp.asarray(np.concatenate([-sin, sin, zeros], axis=1), F32))


def _pad_lanes(v, n=LANE):
    return jnp.pad(v, ((0, 0), (0, n - v.shape[1])))


def _prep_weights(w_in, g_cq, w_uq, g_ckv, w_ukv, g_q, g_k, w_o, w_mq, w_mo, w_ff1, w_ff2, w_mk, w_mv):
    d = w_in.shape[0]
    n_conv = 2 * GDN_HEADS * GDN_DK + GDN_HEADS * GDN_DV
    n_z = GDN_HEADS * GDN_DV
    w_in = w_in.astype(BF)
    o = 0
    w_cq = w_in[:, o:o + Q_RANK]; o += Q_RANK
    w_ckv = w_in[:, o:o + KV_RANK]; o += KV_RANK
    w_kpe = w_in[:, o:o + QK_ROPE]; o += QK_ROPE
    w_qkv = w_in[:, o:o + n_conv]; o += n_conv
    w_z = w_in[:, o:o + n_z]; o += n_z
    w_b = w_in[:, o:o + GDN_HEADS]; o += GDN_HEADS
    w_a = w_in[:, o:o + GDN_HEADS]
    half = QK_ROPE // 2
    swap = jnp.concatenate([jnp.arange(half, QK_ROPE), jnp.arange(0, half)])
    z64 = jnp.zeros((d, LANE - QK_ROPE), w_in.dtype)
    misc = jnp.concatenate([w_kpe, z64, w_kpe[:, swap], z64, w_b, w_a], axis=1)
    misc = _pad_lanes(misc, COL_TILE)
    w_in_p = jnp.concatenate([w_cq, w_ckv, w_qkv, w_z, misc], axis=1).astype(BF)
    wbat = jnp.concatenate([w_b, w_a], axis=1).T.astype(BF)

    r = w_uq.shape[0]
    wq3 = w_uq.reshape(r, MLA_HEADS, QK_HEAD)
    zq = jnp.zeros((r, MLA_HEADS, LANE - QK_ROPE), w_uq.dtype)
    w1 = jnp.concatenate([wq3, zq], axis=2).reshape(r, MLA_HEADS * QPAD).astype(BF)
    w2 = jnp.concatenate([wq3[:, :, QK_NOPE:][:, :, swap], zq], axis=2).reshape(r, MLA_HEADS * LANE).astype(BF)

    dmla = MLA_HEADS * V_DIM
    return dict(
        n_conv=n_conv, n_z=n_z, w_in_p=w_in_p, wbat=wbat, w1=w1, w2=w2,
        gcq=g_cq[None, :], gckv=g_ckv[None, :],
        gq_n=g_q[None, :QK_NOPE], gq_r=_pad_lanes(g_q[None, QK_NOPE:]),
        gk_n=g_k[None, :QK_NOPE], gk_r=_pad_lanes(g_k[None, QK_NOPE:]),
        w_ukv=w_ukv.astype(BF), wo1=w_o[:dmla].astype(BF), wo2=w_o[dmla:].astype(BF),
        w_mq=w_mq.astype(BF), w_mo=w_mo.astype(BF), w_ff1=w_ff1.astype(BF), w_ff2=w_ff2.astype(BF),
        w_mk=w_mk.astype(BF), w_mv=w_mv.astype(BF),
    )


def _gate_params(a_log, dt_bias):
    z8 = jnp.zeros((GDN_HEADS,), F32)
    al = jnp.concatenate([z8, a_log.astype(F32)])
    db = jnp.concatenate([z8, dt_bias.astype(F32)])
    return _pad_lanes(al[None, :]), _pad_lanes(db[None, :]), al[:, None], db[:, None]


def _layer(x, pos, past, conv_past, s0, mem_k, mem_v, wp, lw):
    nb, s_len, d = x.shape
    t = nb * s_len
    xf = x.reshape(t, d)
    ctab, stab = _rope_tables(pos)
    if ctab.shape[0] % 8 != 0 or (s_len < IN_TM and nb > 1):
        ctab, stab = jnp.tile(ctab, (nb, 1)), jnp.tile(stab, (nb, 1))

    cqn, ckv, kpe_pad, qkv, z, ba, bat = _in_proj(
        xf, lw["g_norm_mix"], wp["w_in_p"], wp["wbat"], wp["gcq"], wp["gckv"], ctab, stab,
        wp["n_conv"], wp["n_z"])
    q = _q_proj(cqn, wp["w1"], wp["w2"], ctab, stab, wp["gq_n"], wp["gq_r"])

    if past is None:
        ckv_all, kpe_all, q_off, t_valid = ckv, kpe_pad, 0, s_len
    else:
        ckv_past, kpe_past = past
        p_len = ckv_past.shape[1]
        t_valid = p_len + s_len
        t_pad = -(-t_valid // PROJ_TM) * PROJ_TM
        ckv_all = jnp.concatenate(
            [ckv_past.astype(F32), ckv.reshape(nb, s_len, KV_RANK),
             jnp.zeros((nb, t_pad - t_valid, KV_RANK), F32)], axis=1).reshape(nb * t_pad, KV_RANK)
        kpe_all = jnp.concatenate(
            [jnp.pad(kpe_past.astype(F32), ((0, 0), (0, 0), (0, LANE - QK_ROPE))),
             kpe_pad.reshape(nb, s_len, LANE),
             jnp.zeros((nb, t_pad - t_valid, LANE), F32)], axis=1).reshape(nb * t_pad, LANE)
        q_off = p_len
    k, v = _kv_proj(ckv_all, kpe_all, wp["w_ukv"], wp["gk_n"], wp["gk_r"])
    o_mla = _attention(q, k, v, lw["g_mla_out"], nb, q_off, t_valid)

    conv_past8 = jnp.pad(conv_past.astype(F32), ((0, 0), (HALO - (CONV_W - 1), 0), (0, 0)))
    conv_past8 = conv_past8.reshape(nb * HALO, -1)
    bat3 = bat.reshape(16, nb, s_len).transpose(1, 0, 2)
    alog_l, dtb_l, alog_c, dtb_c = _gate_params(lw["a_log"], lw["dt_bias"])
    o_gdn, s_new = _gdn(qkv, lw["conv_w"], conv_past8, ba, bat3, alog_l, dtb_l, alog_c, dtb_c,
                        z, s0.astype(F32), lw["g_gdn_out"], nb)

    h2 = _mix_mem(xf, o_mla, o_gdn, wp["wo1"], wp["wo2"], lw["g_norm_mem_q"], wp["w_mq"],
                  lw["g_mq"], mem_k, mem_v, wp["w_mo"], nb)
    y = _ffn(h2, lw["g_norm_ffn"], wp["w_ff1"], wp["w_ff2"])

    conv_in_tail = jnp.concatenate([conv_past.astype(F32), qkv.reshape(nb, s_len, -1)[:, -(CONV_W - 1):]], axis=1)
    conv_new = conv_in_tail[:, -(CONV_W - 1):]
    return (y.reshape(nb, s_len, d), ckv.reshape(nb, s_len, KV_RANK),
            kpe_pad[:, :QK_ROPE].reshape(nb, s_len, QK_ROPE), conv_new, s_new)


def kernel(x_prompt, x_sample, mem_prompt, cache_mla_ckv, cache_mla_kpe, cache_gdn_conv, state_gdn, cache_mem_k, cache_mem_v, g_norm_mix, w_in, g_cq, w_uq, g_ckv, w_ukv, g_q_mla, g_k_mla, g_mla_out, conv_w, a_log, dt_bias, g_gdn_out, w_o, g_norm_mem_q, g_norm_mem_kv, w_mq, w_mk, w_mv, g_mq, g_mk, w_mo, g_norm_ffn, w_ff1, w_ff2):
    depth = w_in.shape[0]
    nbp, sp, d = x_prompt.shape
    nbs, ss, _ = x_sample.shape
    n_mem = mem_prompt.shape[1]
    mw = MEM_HEADS * MEM_DIM
    n_conv = 2 * GDN_HEADS * GDN_DK + GDN_HEADS * GDN_DV
    pos_p = np.arange(sp)
    pos_s = cache_mla_ckv.shape[2] + np.arange(ss)
    zeros_conv = jnp.zeros((nbp, CONV_W - 1, n_conv), F32)
    zeros_state = jnp.zeros((nbp, GDN_HEADS, GDN_DK, GDN_DV), F32)
    hp, hs = x_prompt, x_sample
    outs_p = [[] for _ in range(6)]
    outs_s = [[] for _ in range(4)]
    for l in range(depth):
        wp = _prep_weights(w_in[l], g_cq[l], w_uq[l], g_ckv[l], w_ukv[l], g_q_mla[l], g_k_mla[l], w_o[l],
                           w_mq[l], w_mo[l], w_ff1[l], w_ff2[l], w_mk[l], w_mv[l])
        lw = dict(g_norm_mix=g_norm_mix[l][None, :], g_mla_out=g_mla_out[l][None, :], conv_w=conv_w[l],
                  a_log=a_log[l], dt_bias=dt_bias[l], g_gdn_out=g_gdn_out[l][None, :],
                  g_norm_mem_q=g_norm_mem_q[l][None, :], g_mq=g_mq[l][None, :],
                  g_norm_ffn=g_norm_ffn[l][None, :])
        mk, mv = _mem_kv(mem_prompt.reshape(nbp * n_mem, d), g_norm_mem_kv[l][None, :], wp["w_mk"], wp["w_mv"],
                         g_mk[l][None, :])
        mk3, mv3 = mk.reshape(nbp, n_mem, mw), mv.reshape(nbp, n_mem, mw)
        hp, c1, c2, c3, c4 = _layer(hp, pos_p, None, zeros_conv, zeros_state, mk3.astype(BF), mv3.astype(BF), wp, lw)
        for lst, val in zip(outs_p, (c1, c2, c3, c4, mk3.reshape(nbp, n_mem, MEM_HEADS, MEM_DIM),
                                     mv3.reshape(nbp, n_mem, MEM_HEADS, MEM_DIM))):
            lst.append(val)
        hs, d1, d2, d3, d4 = _layer(hs, pos_s, (cache_mla_ckv[l], cache_mla_kpe[l]), cache_gdn_conv[l], state_gdn[l],
                                    cache_mem_k[l].reshape(nbs, n_mem, mw).astype(BF),
                                    cache_mem_v[l].reshape(nbs, n_mem, mw).astype(BF), wp, lw)
        for lst, val in zip(outs_s, (d1, d2, d3, d4)):
            lst.append(val)
    return (hp, hs, *(jnp.stack(v) for v in outs_p), *(jnp.stack(v) for v in outs_s))
```

```python
import functools
import math

import jax
import jax.numpy as jnp
import numpy as np
from jax import lax
from jax.experimental import pallas as pl
from jax.experimental.pallas import tpu as pltpu

F32 = jnp.float32
BF = jnp.bfloat16

EPS = 1e-6
CHUNK = 64
ROPE_THETA = 10000.0
MLA_HEADS = 8
QK_NOPE = 128
QK_ROPE = 64
QK_HEAD = QK_NOPE + QK_ROPE
V_DIM = 128
Q_RANK = 512
KV_RANK = 512
GDN_HEADS = 8
GDN_DK = 128
GDN_DV = 128
CONV_W = 4
MEM_HEADS = 4
MEM_DIM = 128
LANE = 128
QPAD = 2 * LANE
COL_TILE = 1024
HALO = 8
NEG_BIG = -1e30
QSCALE = (QK_HEAD ** -0.5) * math.log2(math.e)

VMEM_LIMIT = 56 * 1024 * 1024

IN_TM = 1024
PROJ_TM = 512
ATTN_TQ = 1024
ATTN_TK = 1024
ATTN_WIDE = 2
GDN_TM = 256
GDN_WAVES = 2
MIX_TM = 512
FFN_TM = 1024
FFN_TF = 512


def _tile(n, pref, mult=8):
    if n <= pref:
        return n
    t = (pref // mult) * mult
    while t >= mult:
        if n % t == 0:
            return t
        t -= mult
    return n


def _nt(a, b):
    return lax.dot_general(a, b, (((1,), (1,)), ((), ())), preferred_element_type=F32)


def _mm(a, b):
    return jnp.dot(a, b, preferred_element_type=F32)


def _sigmoid(x):
    return 1.0 / (1.0 + jnp.exp(-x))


def _softplus(x):
    return jnp.maximum(x, 0.0) + jnp.log(1.0 + jnp.exp(-jnp.abs(x)))


def _rms(x, g, n=None):
    n = x.shape[-1] if n is None else n
    ms = jnp.sum(x * x, axis=-1, keepdims=True) * (1.0 / n)
    return (x * lax.rsqrt(ms + EPS)) * g


def _split3(x):
    hi = x.astype(BF)
    r1 = x - hi.astype(F32)
    mid = r1.astype(BF)
    lo = (r1 - mid.astype(F32)).astype(BF)
    return hi, mid, lo


def _interleave(lists):
    items = []
    for li, lst in enumerate(lists):
        items += [((i + 0.5) / len(lst), li, f) for i, f in enumerate(lst)]
    return [f for _, _, f in sorted(items, key=lambda it: (it[0], it[1]))]


def _params(sem):
    return pltpu.CompilerParams(dimension_semantics=sem, vmem_limit_bytes=VMEM_LIMIT)


def _in_proj_body(x_ref, g_ref, w_ref, wbat_ref, gcq_ref, gckv_ref, ctab_ref, stab_ref,
                  cqn_ref, ckv_ref, kpe_ref, qkv_ref, z_ref, ba_ref, bat_ref, xn_scr, *, nqkv, nz):
    j = pl.program_id(1)

    @pl.when(j == 0)
    def _():
        xn_scr[...] = _rms(x_ref[...], g_ref[...]).astype(BF)

    def halves(store):
        hw = COL_TILE // 2
        for c0 in (0, hw):
            store(c0, hw, _nt(xn_scr[...], w_ref[c0:c0 + hw, :]))

    @pl.when(j == 0)
    def _():
        cqn_ref[...] = _rms(_nt(xn_scr[...], w_ref[0:Q_RANK, :]), gcq_ref[...]).astype(BF)
        ckv_ref[...] = _rms(_nt(xn_scr[...], w_ref[Q_RANK:Q_RANK + KV_RANK, :]), gckv_ref[...])

    @pl.when((j >= 1) & (j < 1 + nqkv))
    def _():
        def store(c0, n, a):
            qkv_ref[:, c0:c0 + n] = a
        halves(store)

    @pl.when((j >= 1 + nqkv) & (j < 1 + nqkv + nz))
    def _():
        def store(c0, n, a):
            z_ref[:, c0:c0 + n] = a.astype(BF)
        halves(store)

    @pl.when(j == 1 + nqkv + nz)
    def _():
        acc = _nt(xn_scr[...], w_ref[0:3 * LANE, :])
        kpe_ref[...] = acc[:, 0:LANE] * ctab_ref[...] + acc[:, LANE:2 * LANE] * stab_ref[...]
        ba_ref[...] = acc[:, 2 * LANE:3 * LANE]
        bat_ref[...] = _nt(wbat_ref[...], xn_scr[...])


def _in_proj(x, g, w_p, wbat, gcq, gckv, ctab, stab, n_conv, n_z):
    t, d = x.shape
    tm = _tile(min(t, ctab.shape[0]), IN_TM)
    ntab = ctab.shape[0] // tm
    nqkv = n_conv // COL_TILE
    nz = n_z // COL_TILE
    ncol = 1 + nqkv + nz + 1
    assert Q_RANK + KV_RANK == COL_TILE and w_p.shape[0] == ncol * COL_TILE
    row = lambda i, j: (i, 0)
    const = lambda i, j: (0, 0)
    body = functools.partial(_in_proj_body, nqkv=nqkv, nz=nz)
    return pl.pallas_call(
        body,
        grid=(t // tm, ncol),
        in_specs=[
            pl.BlockSpec((tm, d), row),
            pl.BlockSpec((1, d), const),
            pl.BlockSpec((COL_TILE, d), lambda i, j: (j, 0)),
            pl.BlockSpec((16, d), const),
            pl.BlockSpec((1, Q_RANK), const),
            pl.BlockSpec((1, KV_RANK), const),
            pl.BlockSpec((tm, LANE), lambda i, j: (i % ntab, 0)),
            pl.BlockSpec((tm, LANE), lambda i, j: (i % ntab, 0)),
        ],
        out_specs=[
            pl.BlockSpec((tm, Q_RANK), row),
            pl.BlockSpec((tm, KV_RANK), row),
            pl.BlockSpec((tm, LANE), row),
            pl.BlockSpec((tm, COL_TILE), lambda i, j: (i, jnp.clip(j - 1, 0, nqkv - 1))),
            pl.BlockSpec((tm, COL_TILE), lambda i, j: (i, jnp.clip(j - 1 - nqkv, 0, nz - 1))),
            pl.BlockSpec((tm, LANE), row),
            pl.BlockSpec((16, tm), lambda i, j: (0, i)),
        ],
        out_shape=[
            jax.ShapeDtypeStruct((t, Q_RANK), BF),
            jax.ShapeDtypeStruct((t, KV_RANK), F32),
            jax.ShapeDtypeStruct((t, LANE), F32),
            jax.ShapeDtypeStruct((t, n_conv), F32),
            jax.ShapeDtypeStruct((t, n_z), BF),
            jax.ShapeDtypeStruct((t, LANE), F32),
            jax.ShapeDtypeStruct((16, t), F32),
        ],
        scratch_shapes=[pltpu.VMEM((tm, d), BF)],
        compiler_params=_params(("arbitrary", "arbitrary")),
        name="in_proj",
    )(x, g, w_p, wbat, gcq, gckv, ctab, stab)


def _q_proj_body(c_ref, w1_ref, w2_ref, ctab_ref, stab_ref, gn_ref, gr_ref, q_ref):
    c = c_ref[...]
    qf = _mm(c, w1_ref[...])
    qs = _mm(c, w2_ref[...])
    ct = ctab_ref[...]
    st = stab_ref[...]
    gn = gn_ref[...]
    gr = gr_ref[...]
    for h in range(MLA_HEADS):
        nope = qf[:, h * QPAD:h * QPAD + LANE]
        rot = qf[:, h * QPAD + LANE:(h + 1) * QPAD] * ct + qs[:, h * LANE:(h + 1) * LANE] * st
        ss = jnp.sum(nope * nope, axis=-1, keepdims=True) + jnp.sum(rot * rot, axis=-1, keepdims=True)
        rs = lax.rsqrt(ss * (1.0 / QK_HEAD) + EPS) * QSCALE
        q_ref[:, h * QPAD:h * QPAD + LANE] = (nope * rs * gn).astype(BF)
        q_ref[:, h * QPAD + LANE:(h + 1) * QPAD] = (rot * rs * gr).astype(BF)


def _q_proj(cqn, w1, w2, ctab, stab, gn, gr):
    t = cqn.shape[0]
    tm = _tile(min(t, ctab.shape[0]), PROJ_TM)
    ntab = ctab.shape[0] // tm
    const = lambda i: (0, 0)
    return pl.pallas_call(
        _q_proj_body,
        grid=(t // tm,),
        in_specs=[
            pl.BlockSpec((tm, Q_RANK), lambda i: (i, 0)),
            pl.BlockSpec(w1.shape, const),
            pl.BlockSpec(w2.shape, const),
            pl.BlockSpec((tm, LANE), lambda i: (i % ntab, 0)),
            pl.BlockSpec((tm, LANE), lambda i: (i % ntab, 0)),
            pl.BlockSpec((1, LANE), const),
            pl.BlockSpec((1, LANE), const),
        ],
        out_specs=pl.BlockSpec((tm, MLA_HEADS * QPAD), lambda i: (i, 0)),
        out_shape=jax.ShapeDtypeStruct((t, MLA_HEADS * QPAD), BF),
        compiler_params=_params(("arbitrary",)),
        name="q_proj",
    )(cqn, w1, w2, ctab, stab, gn, gr)


def _kv_proj_body(c_ref, kpe_ref, w_ref, gn_ref, gr_ref, k_ref, v_ref):
    kv = _mm(c_ref[...].astype(BF), w_ref[...])
    kp = kpe_ref[...]
    kps = jnp.sum(kp * kp, axis=-1, keepdims=True)
    gn = gn_ref[...]
    gr = gr_ref[...]
    for h in range(MLA_HEADS):
        kn = kv[:, h * 2 * LANE:h * 2 * LANE + LANE]
        rs = lax.rsqrt((jnp.sum(kn * kn, axis=-1, keepdims=True) + kps) * (1.0 / QK_HEAD) + EPS)
        k_ref[:, h * QPAD:h * QPAD + LANE] = (kn * rs * gn).astype(BF)
        k_ref[:, h * QPAD + LANE:(h + 1) * QPAD] = (kp * rs * gr).astype(BF)
        v_ref[:, h * V_DIM:(h + 1) * V_DIM] = kv[:, h * 2 * LANE + LANE:(h + 1) * 2 * LANE].astype(BF)


def _kv_proj(ckv, kpe_pad, w, gn, gr):
    t = ckv.shape[0]
    tm = _tile(t, PROJ_TM)
    const = lambda i: (0, 0)
    return pl.pallas_call(
        _kv_proj_body,
        grid=(t // tm,),
        in_specs=[
            pl.BlockSpec((tm, KV_RANK), lambda i: (i, 0)),
            pl.BlockSpec((tm, LANE), lambda i: (i, 0)),
            pl.BlockSpec(w.shape, const),
            pl.BlockSpec((1, LANE), const),
            pl.BlockSpec((1, LANE), const),
        ],
        out_specs=[
            pl.BlockSpec((tm, MLA_HEADS * QPAD), lambda i: (i, 0)),
            pl.BlockSpec((tm, MLA_HEADS * V_DIM), lambda i: (i, 0)),
        ],
        out_shape=[
            jax.ShapeDtypeStruct((t, MLA_HEADS * QPAD), BF),
            jax.ShapeDtypeStruct((t, MLA_HEADS * V_DIM), BF),
        ],
        compiler_params=_params(("arbitrary",)),
        name="kv_proj",
    )(ckv, kpe_pad, w, gn, gr)


def _attn_body(q_ref, k_ref, v_ref, g_ref, o_ref, m_scr, l_scr, acc_scr, *, tq, tk, wide, q_off, t_valid):
    qi = pl.program_id(2)
    qpos0 = q_off + qi * tq
    n_full = jnp.minimum((qpos0 // CHUNK * CHUNK + CHUNK) // tk, t_valid // tk)
    hi = jnp.minimum((qpos0 + tq - 1) // CHUNK * CHUNK + CHUNK, t_valid)
    n_total = (hi + tk - 1) // tk

    m_scr[...] = jnp.full(m_scr.shape, NEG_BIG, F32)
    l_scr[...] = jnp.zeros(l_scr.shape, F32)
    acc_scr[...] = jnp.zeros(acc_scr.shape, F32)
    diag = q_off % tk == 0 and tq == tk and t_valid % tk == 0 and tk % (2 * LANE) == 0 and tq % 32 == 0

    def step(kc, width, masked, r0=0, nr=tq):
        rows = slice(r0, r0 + nr)
        k0 = pl.multiple_of(kc * tk, tk)
        s = _nt(q_ref[rows, :], k_ref[pl.ds(k0, width), :])
        if masked:
            qpos = qpos0 + r0 + lax.broadcasted_iota(jnp.int32, (nr, 1), 0)
            last = jnp.minimum(qpos | (CHUNK - 1), t_valid - 1) - k0
            s = jnp.where(lax.broadcasted_iota(jnp.int32, (nr, width), 1) <= last, s, NEG_BIG)
        m_prev = m_scr[rows, :]
        m_new = jnp.maximum(m_prev, jnp.max(s, axis=-1, keepdims=True))
        alpha = jnp.exp2(m_prev - m_new)
        ps = [jnp.exp2(s[:, j * LANE:(j + 1) * LANE] - m_new) for j in range(width // LANE)]
        psum = ps[0]
        for pj in ps[1:]:
            psum = psum + pj
        l_scr[rows, :] = alpha * l_scr[rows, :] + psum
        p = jnp.concatenate(ps, axis=1).astype(BF) if len(ps) > 1 else ps[0].astype(BF)
        acc_scr[rows, :] = acc_scr[rows, :] * alpha + _mm(p, v_ref[pl.ds(k0, width), :])
        m_scr[rows, :] = m_new

    def loop(lo, hi, fn):
        lax.fori_loop(lo, hi, lambda i, c: (fn(i), c)[1], 0)

    n_wide = n_full // wide
    loop(0, n_wide, lambda i: step(i * wide, wide * tk, False))
    loop(n_wide * wide, n_full, lambda kc: step(kc, tk, False))
    if diag:
        def masked(kc):
            step(kc, tk // 2, True, 0, tq // 2)
            step(kc, tk, True, tq // 2, tq // 2)
    else:
        def masked(kc):
            step(kc, tk, True)
    loop(n_full, n_total, masked)
    o = acc_scr[...] / jnp.sum(l_scr[...], axis=-1, keepdims=True)
    o_ref[...] = _rms(o, g_ref[...]).astype(BF)


def _attention(q, k, v, g_out, nb, q_off, t_valid):
    tq_total = q.shape[0] // nb
    tk_total = k.shape[0] // nb
    tq = _tile(tq_total, ATTN_TQ)
    tk = _tile(tk_total, ATTN_TK, mult=LANE)
    nq = tq_total // tq
    body = functools.partial(_attn_body, tq=tq, tk=tk, wide=ATTN_WIDE, q_off=q_off, t_valid=t_valid)
    return pl.pallas_call(
        body,
        grid=(nb, MLA_HEADS, nq),
        in_specs=[
            pl.BlockSpec((tq, QPAD), lambda b, h, i: (b * nq + i, h)),
            pl.BlockSpec((tk_total, QPAD), lambda b, h, i: (b, h)),
            pl.BlockSpec((tk_total, V_DIM), lambda b, h, i: (b, h)),
            pl.BlockSpec((1, V_DIM), lambda b, h, i: (0, 0)),
        ],
        out_specs=pl.BlockSpec((tq, V_DIM), lambda b, h, i: (b * nq + i, h)),
        out_shape=jax.ShapeDtypeStruct((q.shape[0], MLA_HEADS * V_DIM), BF),
        scratch_shapes=[
            pltpu.VMEM((tq, LANE), F32),
            pltpu.VMEM((tq, LANE), F32),
            pltpu.VMEM((tq, V_DIM), F32),
        ],
        compiler_params=_params(("arbitrary", "arbitrary", "arbitrary")),
        name="mla_attention",
    )(q, k, v, g_out)


def _gdn_body(q_ref, k_ref, v_ref, wq_ref, wk_ref, wv_ref, pq_ref, pk_ref, pv_ref,
              ba_ref, bat_ref, alog_l_ref, dtb_l_ref, alog_c_ref, dtb_c_ref,
              z_ref, s0_ref, gout_ref,
              o_ref, sout_ref,
              s_scr, halo_scr, vnew_scr, *, tm, L, hg):
    t = pl.program_id(2)
    nt = pl.num_programs(2)
    n_chunks = tm // L
    n_factors = int(math.log2(L)) - 1

    @pl.when(t == 0)
    def _():
        s_scr[...] = s0_ref[0]
        halo_scr[0] = pq_ref[...]
        halo_scr[1] = pk_ref[...]
        halo_scr[2] = pv_ref[...]
        vnew_scr[...] = jnp.zeros(vnew_scr.shape, BF)

    ba = ba_ref[...]
    beta_all = _sigmoid(ba)
    g_all = -jnp.exp(alog_l_ref[...]) * _softplus(ba + dtb_l_ref[...])
    gt_all = -jnp.exp(alog_c_ref[...]) * _softplus(bat_ref[0] + dtb_c_ref[...])

    r = lax.broadcasted_iota(jnp.int32, (tm, tm), 0)
    c = lax.broadcasted_iota(jnp.int32, (tm, tm), 1)
    same = (r // L) == (c // L)
    lower = same & (c <= r)
    strict = same & (c < r)
    cs = jnp.where(lower, 1.0, 0.0).astype(BF)
    bd = jnp.where(same, 1.0, 0.0).astype(BF)
    g3 = _split3(g_all)
    gcol_all = _mm(cs, g3[0]) + _mm(cs, g3[1]) + _mm(cs, g3[2])
    glast_all = _mm(bd, g3[0]) + _mm(bd, g3[1]) + _mm(bd, g3[2])
    gt3 = _split3(gt_all)
    grow_all = _nt(gt3[0], cs) + _nt(gt3[1], cs) + _nt(gt3[2], cs)

    gout = gout_ref[...]
    eye_t = jnp.where(r == c, 1.0, 0.0)
    eye_k = jnp.where(lax.broadcasted_iota(jnp.int32, (GDN_DK, GDN_DK), 0)
                      == lax.broadcasted_iota(jnp.int32, (GDN_DK, GDN_DK), 1), 1.0, 0.0).astype(BF)
    col_chunk = lax.broadcasted_iota(jnp.int32, (1, tm), 1) // L

    def make_wave(heads):
        c0, c1 = heads[0] * GDN_DK, (heads[-1] + 1) * GDN_DK
        loc = lambda h: slice((h - heads[0]) * GDN_DK, (h - heads[0] + 1) * GDN_DK)
        w = {}

        def conv(idx, x_ref, w_ref):
            xs = jnp.concatenate([halo_scr[idx, :, c0:c1], x_ref[:, c0:c1]], axis=0)
            cw = w_ref[:, c0:c1]
            acc = xs * cw[0:1]
            for i in range(1, CONV_W):
                acc = pltpu.roll(acc, 1, axis=0) + xs * cw[i:i + 1]
            halo_scr[idx, :, c0:c1] = xs[tm:tm + HALO]
            y = acc[HALO:HALO + tm]
            return y * _sigmoid(y)

        def p_q():
            w["yq"] = conv(0, q_ref, wq_ref)

        def p_k():
            w["yk"] = conv(1, k_ref, wk_ref)

        def p_v():
            w["yv"] = conv(2, v_ref, wv_ref)

        def p_gate():
            w["beta"], w["gc"], w["glast"], w["gr"] = {}, {}, {}, {}
            for h in heads:
                g = GDN_HEADS + h
                w["beta"][h] = beta_all[:, h:h + 1]
                w["gc"][h] = gcol_all[:, g:g + 1]
                w["glast"][h] = glast_all[:, g:g + 1]
                w["gr"][h] = grow_all[g:g + 1, :]

        def p_norm():
            w["qn"], w["kn"] = {}, {}
            for h in heads:
                qh = w["yq"][:, loc(h)]
                kh = w["yk"][:, loc(h)]
                w["qn"][h] = qh * lax.rsqrt(jnp.sum(qh * qh, axis=-1, keepdims=True) + EPS) * (GDN_DK ** -0.5)
                w["kn"][h] = kh * lax.rsqrt(jnp.sum(kh * kh, axis=-1, keepdims=True) + EPS)
            kb = {h: w["kn"][h].astype(BF) for h in heads}
            w["kk"] = {h: _nt(kb[h], kb[h]) for h in heads}
            w["qk"] = {h: _nt(w["qn"][h].astype(BF), kb[h]) for h in heads}

        def p_mat():
            beta, gc = w["beta"], w["gc"]
            decay = {h: jnp.where(lower, jnp.exp(gc[h] - w["gr"][h]), 0.0) for h in heads}
            w["nm"] = {h: jnp.where(strict, beta[h] * w["kk"][h] * decay[h], 0.0) for h in heads}
            w["attn"] = {h: (w["qk"][h] * decay[h]).astype(BF) for h in heads}
            w["eg"] = {h: jnp.exp(gc[h]) for h in heads}
            w["rhs"] = {h: jnp.concatenate([w["yv"][:, loc(h)] * beta[h],
                                            w["kn"][h] * (beta[h] * w["eg"][h])], axis=1) for h in heads}
            w["nb"] = {h: w["nm"][h].astype(BF) for h in heads}

        def s_first():
            w["m"] = {h: _mm(w["nb"][h], w["nb"][h]) for h in heads}
            w["T"] = {h: eye_t - w["nm"][h] for h in heads}

        def s_level(s):
            xb = {h: jnp.concatenate([w["m"][h], w["T"][h]], axis=0).astype(BF) for h in heads}
            if s < n_factors - 1:
                prod = {h: _mm(xb[h], xb[h][:tm]) for h in heads}
                w["T"] = {h: w["T"][h] + prod[h][tm:] for h in heads}
                w["m"] = {h: prod[h][:tm] for h in heads}
            else:
                w["T"] = {h: w["T"][h] + _mm(xb[h][tm:], xb[h][:tm]) for h in heads}

        def s_y0():
            w["tb"] = {h: w["T"][h].astype(BF) for h in heads}
            w["y0"] = {h: _mm(w["tb"][h], w["rhs"][h].astype(BF)) for h in heads}

        def s_res():
            res = {}
            for h in heads:
                nbf = w["nb"][h].astype(F32)
                nx = jnp.concatenate([nbf, w["nm"][h] - nbf], axis=0).astype(BF)
                yh = w["y0"][h].astype(BF)
                yl = (w["y0"][h] - yh.astype(F32)).astype(BF)
                p1 = _mm(nx, yh)
                res[h] = w["rhs"][h] - w["y0"][h] - (p1[:tm] + p1[tm:] + _mm(nx[:tm], yl))
            w["res"] = res

        def s_fix():
            w["y"] = {h: w["y0"][h] + _mm(w["tb"][h], w["res"][h].astype(BF)) for h in heads}

        def c_prep():
            y = w["y"]
            w["u"] = {h: y[h][:, :GDN_DV] for h in heads}
            w["wq"] = {}
            for h in heads:
                qd = w["qn"][h] * w["eg"][h]
                parts = []
                for ci in range(n_chunks):
                    parts += [y[h][ci * L:(ci + 1) * L, GDN_DV:], qd[ci * L:(ci + 1) * L]]
                w["wq"][h] = jnp.concatenate(parts, axis=0).astype(BF)
            kdb = {h: (w["kn"][h] * jnp.exp(w["glast"][h] - w["gc"][h])).astype(BF) for h in heads}
            w["gtot"] = {h: jnp.exp(w["glast"][h]) for h in heads}
            kdt = {h: _nt(eye_k, kdb[h]) for h in heads}
            w["kdt"] = {h: [jnp.where(col_chunk == ci, kdt[h], 0.0).astype(BF) for ci in range(n_chunks)]
                        for h in heads}
            w["st"] = {h: s_scr[h] for h in heads}
            w["outs"] = {h: [] for h in heads}

        def c_chunk(ci):
            r0 = ci * L
            st = w["st"]
            ws = {h: _mm(w["wq"][h][2 * r0:2 * r0 + 2 * L], st[h].astype(BF)) for h in heads}
            for h in heads:
                vnew_scr[h, r0:r0 + L, :] = (w["u"][h][r0:r0 + L] - ws[h][:L]).astype(BF)
            for h in heads:
                w["outs"][h].append(ws[h][L:] + _mm(w["attn"][h][r0:r0 + L, :], vnew_scr[h]))
            w["st"] = {h: st[h] * w["gtot"][h][r0:r0 + 1, :] + _mm(w["kdt"][h][ci], vnew_scr[h]) for h in heads}

        def c_out():
            for h in heads:
                s_scr[h] = w["st"][h]
                o = w["outs"][h][0] if n_chunks == 1 else jnp.concatenate(w["outs"][h], axis=0)
                zz = z_ref[:, h * GDN_DV:(h + 1) * GDN_DV].astype(F32)
                o_ref[:, h * GDN_DV:(h + 1) * GDN_DV] = (_rms(o, gout) * (zz * _sigmoid(zz))).astype(BF)

        prep = [p_q, p_k, p_v, p_gate, p_norm, p_mat]
        solve = ([s_first] + [functools.partial(s_level, s) for s in range(n_factors)] + [s_y0, s_res, s_fix])
        scan = [c_prep] + [functools.partial(c_chunk, ci) for ci in range(n_chunks)] + [c_out]
        return [prep, solve, scan]

    nw = min(GDN_WAVES, hg)
    per = hg // nw
    waves = [make_wave(list(range(i * per, (i + 1) * per))) for i in range(nw)]
    for slot in range(nw + 2):
        active = [waves[i][slot - i] for i in range(nw) if 0 <= slot - i < 3]
        for stage in _interleave(active):
            stage()

    @pl.when(t == nt - 1)
    def _():
        sout_ref[0] = s_scr[...]


def _gdn(qkv, conv_w, conv_past8, ba, bat, alog_l, dtb_l, alog_c, dtb_c, z, s0, gout, nb):
    t = qkv.shape[0]
    s_len = t // nb
    L = min(s_len, CHUNK)
    tm = _tile(s_len, GDN_TM, mult=L)
    nt = s_len // tm
    hg = GDN_HEADS
    hw = hg * GDN_DK
    body = functools.partial(_gdn_body, tm=tm, L=L, hg=hg)
    rows = lambda off: (lambda b, h, i: (b * nt + i, off + h))
    wcol = lambda off: (lambda b, h, i: (0, off + h))
    pcol = lambda off: (lambda b, h, i: (b, off + h))
    const = lambda b, h, i: (0, 0)
    ng = GDN_HEADS // hg
    return pl.pallas_call(
        body,
        grid=(nb, ng, nt),
        in_specs=[
            pl.BlockSpec((tm, hw), rows(0)),
            pl.BlockSpec((tm, hw), rows(ng)),
            pl.BlockSpec((tm, hw), rows(2 * ng)),
            pl.BlockSpec((CONV_W, hw), wcol(0)),
            pl.BlockSpec((CONV_W, hw), wcol(ng)),
            pl.BlockSpec((CONV_W, hw), wcol(2 * ng)),
            pl.BlockSpec((HALO, hw), pcol(0)),
            pl.BlockSpec((HALO, hw), pcol(ng)),
            pl.BlockSpec((HALO, hw), pcol(2 * ng)),
            pl.BlockSpec((tm, LANE), lambda b, h, i: (b * nt + i, 0)),
            pl.BlockSpec((1, 16, tm), lambda b, h, i: (b, 0, i)),
            pl.BlockSpec((1, LANE), const),
            pl.BlockSpec((1, LANE), const),
            pl.BlockSpec((16, 1), const),
            pl.BlockSpec((16, 1), const),
            pl.BlockSpec((tm, hw), rows(0)),
            pl.BlockSpec((1, hg, GDN_DK, GDN_DV), lambda b, h, i: (b, h, 0, 0)),
            pl.BlockSpec((1, GDN_DV), const),
        ],
        out_specs=[
            pl.BlockSpec((tm, hw), rows(0)),
            pl.BlockSpec((1, hg, GDN_DK, GDN_DV), lambda b, h, i: (b, h, 0, 0)),
        ],
        out_shape=[
            jax.ShapeDtypeStruct((t, GDN_HEADS * GDN_DV), BF),
            jax.ShapeDtypeStruct((nb, GDN_HEADS, GDN_DK, GDN_DV), F32),
        ],
        scratch_shapes=[
            pltpu.VMEM((hg, GDN_DK, GDN_DV), F32),
            pltpu.VMEM((3, HALO, hw), F32),
            pltpu.VMEM((hg, tm, GDN_DV), BF),
        ],
        compiler_params=_params(("arbitrary", "arbitrary", "arbitrary")),
        name="gdn",
    )(qkv, qkv, qkv, conv_w, conv_w, conv_w, conv_past8, conv_past8, conv_past8,
      ba, bat, alog_l, dtb_l, alog_c, dtb_c, z, s0, gout)


def _mix_mem_body(x_ref, a1_ref, a2_ref, wo1_ref, wo2_ref, gq_ref, wmq_ref, gmq_ref,
                  mk_ref, mv_ref, wmo_ref, o_ref):
    h1 = x_ref[...] + _mm(a1_ref[...], wo1_ref[...]) + _mm(a2_ref[...], wo2_ref[...])
    hn = _rms(h1, gq_ref[...]).astype(BF)
    q = _mm(hn, wmq_ref[...])
    gmq = gmq_ref[...]
    mk = mk_ref[0]
    mv = mv_ref[0]
    outs = []
    for h in range(MEM_HEADS):
        sl = slice(h * MEM_DIM, (h + 1) * MEM_DIM)
        qh = _rms(q[:, sl], gmq).astype(BF)
        s = _nt(qh, mk[:, sl]) * (MEM_DIM ** -0.5)
        p = jnp.exp(s - jnp.max(s, axis=-1, keepdims=True))
        p = p / jnp.sum(p, axis=-1, keepdims=True)
        outs.append(_mm(p.astype(BF), mv[:, sl]))
    o = jnp.concatenate(outs, axis=1).astype(BF)
    o_ref[...] = h1 + _mm(o, wmo_ref[...])


def _mix_mem(x, a1, a2, wo1, wo2, gq, wmq, gmq, mk, mv, wmo, nb):
    t, d = x.shape
    s_len = t // nb
    tm = _tile(s_len, MIX_TM)
    nt = s_len // tm
    n_mem = mk.shape[1]
    mw = MEM_HEADS * MEM_DIM
    row = lambda i: (i, 0)
    const = lambda i: (0, 0)
    return pl.pallas_call(
        _mix_mem_body,
        grid=(t // tm,),
        in_specs=[
            pl.BlockSpec((tm, d), row),
            pl.BlockSpec((tm, a1.shape[1]), row),
            pl.BlockSpec((tm, a2.shape[1]), row),
            pl.BlockSpec(wo1.shape, const),
            pl.BlockSpec(wo2.shape, const),
            pl.BlockSpec((1, d), const),
            pl.BlockSpec(wmq.shape, const),
            pl.BlockSpec((1, MEM_DIM), const),
            pl.BlockSpec((1, n_mem, mw), lambda i: (i // nt, 0, 0)),
            pl.BlockSpec((1, n_mem, mw), lambda i: (i // nt, 0, 0)),
            pl.BlockSpec(wmo.shape, const),
        ],
        out_specs=pl.BlockSpec((tm, d), row),
        out_shape=jax.ShapeDtypeStruct((t, d), F32),
        compiler_params=_params(("arbitrary",)),
        name="mix_mem",
    )(x, a1, a2, wo1, wo2, gq, wmq, gmq, mk, mv, wmo)


def _ffn_body(x_ref, g_ref, w1_ref, w2_ref, o_ref, hn_scr):
    f = pl.program_id(1)

    @pl.when(f == 0)
    def _():
        x = x_ref[...]
        hn_scr[...] = _rms(x, g_ref[...]).astype(BF)
        o_ref[...] = x

    a = jnp.maximum(_mm(hn_scr[...], w1_ref[...]), 0.0)
    o_ref[...] += _mm((a * a).astype(BF), w2_ref[...])


def _ffn(x, g, w1, w2):
    t, d = x.shape
    dff = w1.shape[1]
    tm = _tile(t, FFN_TM)
    tf = _tile(dff, FFN_TF, mult=LANE)
    return pl.pallas_call(
        _ffn_body,
        grid=(t // tm, dff // tf),
        in_specs=[
            pl.BlockSpec((tm, d), lambda i, f: (i, 0)),
            pl.BlockSpec((1, d), lambda i, f: (0, 0)),
            pl.BlockSpec((d, tf), lambda i, f: (0, f)),
            pl.BlockSpec((tf, d), lambda i, f: (f, 0)),
        ],
        out_specs=pl.BlockSpec((tm, d), lambda i, f: (i, 0)),
        out_shape=jax.ShapeDtypeStruct((t, d), F32),
        scratch_shapes=[pltpu.VMEM((tm, d), BF)],
        compiler_params=_params(("arbitrary", "arbitrary")),
        name="ffn",
    )(x, g, w1, w2)


def _mem_kv_body(m_ref, g_ref, wk_ref, wv_ref, gk_ref, k_ref, v_ref):
    mn = _rms(m_ref[...], g_ref[...]).astype(BF)
    k = _mm(mn, wk_ref[...])
    gk = gk_ref[...]
    for h in range(MEM_HEADS):
        sl = slice(h * MEM_DIM, (h + 1) * MEM_DIM)
        k_ref[:, sl] = _rms(k[:, sl], gk)
    v_ref[...] = _mm(mn, wv_ref[...])


def _mem_kv(mem, g, wk, wv, gk):
    t, d = mem.shape
    tm = _tile(t, 256)
    mw = MEM_HEADS * MEM_DIM
    const = lambda i: (0, 0)
    return pl.pallas_call(
        _mem_kv_body,
        grid=(t // tm,),
        in_specs=[
            pl.BlockSpec((tm, d), lambda i: (i, 0)),
            pl.BlockSpec((1, d), const),
            pl.BlockSpec(wk.shape, const),
            pl.BlockSpec(wv.shape, const),
            pl.BlockSpec((1, MEM_DIM), const),
        ],
        out_specs=[pl.BlockSpec((tm, mw), lambda i: (i, 0)), pl.BlockSpec((tm, mw), lambda i: (i, 0))],
        out_shape=[jax.ShapeDtypeStruct((t, mw), F32), jax.ShapeDtypeStruct((t, mw), F32)],
        compiler_params=_params(("arbitrary",)),
        name="mem_kv",
    )(mem, g, wk, wv, gk)


def _rope_tables(pos):
    half = QK_ROPE // 2
    inv_freq = ROPE_THETA ** (-np.arange(half, dtype=np.float64) / half)
    ang = np.asarray(pos, np.float64)[:, None] * inv_freq[None, :]
    cos, sin = np.cos(ang), np.sin(ang)
    zeros = np.zeros((ang.shape[0], LANE - QK_ROPE))
    return (jnp.asarray(np.concatenate([cos, cos, zeros], axis=1), F32),
            jnp.asarray(np.concatenate([-sin, sin, zeros], axis=1), F32))


def _pad_lanes(v, n=LANE):
    return jnp.pad(v, ((0, 0), (0, n - v.shape[1])))


def _prep_weights(w_in, g_cq, w_uq, g_ckv, w_ukv, g_q, g_k, w_o, w_mq, w_mo, w_ff1, w_ff2, w_mk, w_mv):
    d = w_in.shape[0]
    n_conv = 2 * GDN_HEADS * GDN_DK + GDN_HEADS * GDN_DV
    n_z = GDN_HEADS * GDN_DV
    wt = w_in.T.astype(BF)
    o = 0
    w_cq = wt[o:o + Q_RANK]; o += Q_RANK
    w_ckv = wt[o:o + KV_RANK]; o += KV_RANK
    w_kpe = wt[o:o + QK_ROPE]; o += QK_ROPE
    w_qkv = wt[o:o + n_conv]; o += n_conv
    w_z = wt[o:o + n_z]; o += n_z
    w_b = wt[o:o + GDN_HEADS]; o += GDN_HEADS
    w_a = wt[o:o + GDN_HEADS]
    half = QK_ROPE // 2
    swap = jnp.concatenate([jnp.arange(half, QK_ROPE), jnp.arange(0, half)])
    z64 = jnp.zeros((LANE - QK_ROPE, d), BF)
    misc = jnp.concatenate([w_kpe, z64, w_kpe[swap], z64, w_b, w_a], axis=0)
    misc = jnp.pad(misc, ((0, COL_TILE - misc.shape[0]), (0, 0)))
    w_in_p = jnp.concatenate([w_cq, w_ckv, w_qkv, w_z, misc], axis=0)
    wbat = jnp.concatenate([w_b, w_a], axis=0)

    r = w_uq.shape[0]
    wq3 = w_uq.reshape(r, MLA_HEADS, QK_HEAD)
    zq = jnp.zeros((r, MLA_HEADS, LANE - QK_ROPE), w_uq.dtype)
    w1 = jnp.concatenate([wq3, zq], axis=2).reshape(r, MLA_HEADS * QPAD).astype(BF)
    w2 = jnp.concatenate([wq3[:, :, QK_NOPE:][:, :, swap], zq], axis=2).reshape(r, MLA_HEADS * LANE).astype(BF)

    dmla = MLA_HEADS * V_DIM
    return dict(
        n_conv=n_conv, n_z=n_z, w_in_p=w_in_p, wbat=wbat, w1=w1, w2=w2,
        gcq=g_cq[None, :], gckv=g_ckv[None, :],
        gq_n=g_q[None, :QK_NOPE], gq_r=_pad_lanes(g_q[None, QK_NOPE:]),
        gk_n=g_k[None, :QK_NOPE], gk_r=_pad_lanes(g_k[None, QK_NOPE:]),
        w_ukv=w_ukv.astype(BF), wo1=w_o[:dmla].astype(BF), wo2=w_o[dmla:].astype(BF),
        w_mq=w_mq.astype(BF), w_mo=w_mo.astype(BF), w_ff1=w_ff1.astype(BF), w_ff2=w_ff2.astype(BF),
        w_mk=w_mk.astype(BF), w_mv=w_mv.astype(BF),
    )


def _gate_params(a_log, dt_bias):
    z8 = jnp.zeros((GDN_HEADS,), F32)
    al = jnp.concatenate([z8, a_log.astype(F32)])
    db = jnp.concatenate([z8, dt_bias.astype(F32)])
    return _pad_lanes(al[None, :]), _pad_lanes(db[None, :]), al[:, None], db[:, None]


def _layer(x, pos, past, conv_past, s0, mem_k, mem_v, wp, lw):
    nb, s_len, d = x.shape
    t = nb * s_len
    xf = x.reshape(t, d)
    ctab, stab = _rope_tables(pos)
    if ctab.shape[0] % 8 != 0 or (s_len < IN_TM and nb > 1):
        ctab, stab = jnp.tile(ctab, (nb, 1)), jnp.tile(stab, (nb, 1))

    cqn, ckv, kpe_pad, qkv, z, ba, bat = _in_proj(
        xf, lw["g_norm_mix"], wp["w_in_p"], wp["wbat"], wp["gcq"], wp["gckv"], ctab, stab,
        wp["n_conv"], wp["n_z"])
    q = _q_proj(cqn, wp["w1"], wp["w2"], ctab, stab, wp["gq_n"], wp["gq_r"])

    if past is None:
        ckv_all, kpe_all, q_off, t_valid = ckv, kpe_pad, 0, s_len
    else:
        ckv_past, kpe_past = past
        p_len = ckv_past.shape[1]
        t_valid = p_len + s_len
        t_pad = -(-t_valid // PROJ_TM) * PROJ_TM
        ckv_all = jnp.concatenate(
            [ckv_past.astype(F32), ckv.reshape(nb, s_len, KV_RANK),
             jnp.zeros((nb, t_pad - t_valid, KV_RANK), F32)], axis=1).reshape(nb * t_pad, KV_RANK)
        kpe_all = jnp.concatenate(
            [jnp.pad(kpe_past.astype(F32), ((0, 0), (0, 0), (0, LANE - QK_ROPE))),
             kpe_pad.reshape(nb, s_len, LANE),
             jnp.zeros((nb, t_pad - t_valid, LANE), F32)], axis=1).reshape(nb * t_pad, LANE)
        q_off = p_len
    k, v = _kv_proj(ckv_all, kpe_all, wp["w_ukv"], wp["gk_n"], wp["gk_r"])
    o_mla = _attention(q, k, v, lw["g_mla_out"], nb, q_off, t_valid)

    conv_past8 = jnp.pad(conv_past.astype(F32), ((0, 0), (HALO - (CONV_W - 1), 0), (0, 0)))
    conv_past8 = conv_past8.reshape(nb * HALO, -1)
    bat3 = bat.reshape(16, nb, s_len).transpose(1, 0, 2)
    alog_l, dtb_l, alog_c, dtb_c = _gate_params(lw["a_log"], lw["dt_bias"])
    o_gdn, s_new = _gdn(qkv, lw["conv_w"], conv_past8, ba, bat3, alog_l, dtb_l, alog_c, dtb_c,
                        z, s0.astype(F32), lw["g_gdn_out"], nb)

    h2 = _mix_mem(xf, o_mla, o_gdn, wp["wo1"], wp["wo2"], lw["g_norm_mem_q"], wp["w_mq"],
                  lw["g_mq"], mem_k, mem_v, wp["w_mo"], nb)
    y = _ffn(h2, lw["g_norm_ffn"], wp["w_ff1"], wp["w_ff2"])

    conv_in_tail = jnp.concatenate([conv_past.astype(F32), qkv.reshape(nb, s_len, -1)[:, -(CONV_W - 1):]], axis=1)
    conv_new = conv_in_tail[:, -(CONV_W - 1):]
    return (y.reshape(nb, s_len, d), ckv.reshape(nb, s_len, KV_RANK),
            kpe_pad[:, :QK_ROPE].reshape(nb, s_len, QK_ROPE), conv_new, s_new)


def kernel(x_prompt, x_sample, mem_prompt, cache_mla_ckv, cache_mla_kpe, cache_gdn_conv, state_gdn, cache_mem_k, cache_mem_v, g_norm_mix, w_in, g_cq, w_uq, g_ckv, w_ukv, g_q_mla, g_k_mla, g_mla_out, conv_w, a_log, dt_bias, g_gdn_out, w_o, g_norm_mem_q, g_norm_mem_kv, w_mq, w_mk, w_mv, g_mq, g_mk, w_mo, g_norm_ffn, w_ff1, w_ff2):
    depth = w_in.shape[0]
    nbp, sp, d = x_prompt.shape
    nbs, ss, _ = x_sample.shape
    n_mem = mem_prompt.shape[1]
    mw = MEM_HEADS * MEM_DIM
    n_conv = 2 * GDN_HEADS * GDN_DK + GDN_HEADS * GDN_DV
    pos_p = np.arange(sp)
    pos_s = cache_mla_ckv.shape[2] + np.arange(ss)
    zeros_conv = jnp.zeros((nbp, CONV_W - 1, n_conv), F32)
    zeros_state = jnp.zeros((nbp, GDN_HEADS, GDN_DK, GDN_DV), F32)
    hp, hs = x_prompt, x_sample
    outs_p = [[] for _ in range(6)]
    outs_s = [[] for _ in range(4)]
    for l in range(depth):
        wp = _prep_weights(w_in[l], g_cq[l], w_uq[l], g_ckv[l], w_ukv[l], g_q_mla[l], g_k_mla[l], w_o[l],
                           w_mq[l], w_mo[l], w_ff1[l], w_ff2[l], w_mk[l], w_mv[l])
        lw = dict(g_norm_mix=g_norm_mix[l][None, :], g_mla_out=g_mla_out[l][None, :], conv_w=conv_w[l],
                  a_log=a_log[l], dt_bias=dt_bias[l], g_gdn_out=g_gdn_out[l][None, :],
                  g_norm_mem_q=g_norm_mem_q[l][None, :], g_mq=g_mq[l][None, :],
                  g_norm_ffn=g_norm_ffn[l][None, :])
        mk, mv = _mem_kv(mem_prompt.reshape(nbp * n_mem, d), g_norm_mem_kv[l][None, :], wp["w_mk"], wp["w_mv"],
                         g_mk[l][None, :])
        mk3, mv3 = mk.reshape(nbp, n_mem, mw), mv.reshape(nbp, n_mem, mw)
        hp, c1, c2, c3, c4 = _layer(hp, pos_p, None, zeros_conv, zeros_state, mk3.astype(BF), mv3.astype(BF), wp, lw)
        for lst, val in zip(outs_p, (c1, c2, c3, c4, mk3.reshape(nbp, n_mem, MEM_HEADS, MEM_DIM),
                                     mv3.reshape(nbp, n_mem, MEM_HEADS, MEM_DIM))):
            lst.append(val)
        hs, d1, d2, d3, d4 = _layer(hs, pos_s, (cache_mla_ckv[l], cache_mla_kpe[l]), cache_gdn_conv[l], state_gdn[l],
                                    cache_mem_k[l].reshape(nbs, n_mem, mw).astype(BF),
                                    cache_mem_v[l].reshape(nbs, n_mem, mw).astype(BF), wp, lw)
        for lst, val in zip(outs_s, (d1, d2, d3, d4)):
            lst.append(val)
    return (hp, hs, *(jnp.stack(v) for v in outs_p), *(jnp.stack(v) for v in outs_s))
```

```python
import functools
import math

import jax
import jax.numpy as jnp
import numpy as np
from jax import lax
from jax.experimental import pallas as pl
from jax.experimental.pallas import tpu as pltpu

F32 = jnp.float32
BF = jnp.bfloat16

EPS = 1e-6
CHUNK = 64
ROPE_THETA = 10000.0
MLA_HEADS = 8
QK_NOPE = 128
QK_ROPE = 64
QK_HEAD = QK_NOPE + QK_ROPE
V_DIM = 128
Q_RANK = 512
KV_RANK = 512
GDN_HEADS = 8
GDN_DK = 128
GDN_DV = 128
CONV_W = 4
MEM_HEADS = 4
MEM_DIM = 128
LANE = 128
QPAD = 2 * LANE
COL_TILE = 1024
HALO = 8
NEG_BIG = -1e30
QSCALE = (QK_HEAD ** -0.5) * math.log2(math.e)

VMEM_LIMIT = 56 * 1024 * 1024

IN_TM = 1024
PROJ_TM = 512
ATTN_TQ = 1024
ATTN_TK = 1024
ATTN_WIDE = 3
GDN_TM = 256
GDN_WAVES = 2
MIX_TM = 512
FFN_TM = 1024
FFN_TF = 512


def _tile(n, pref, mult=8):
    if n <= pref:
        return n
    t = (pref // mult) * mult
    while t >= mult:
        if n % t == 0:
            return t
        t -= mult
    return n


def _nt(a, b):
    return lax.dot_general(a, b, (((1,), (1,)), ((), ())), preferred_element_type=F32)


def _mm(a, b):
    return jnp.dot(a, b, preferred_element_type=F32)


def _sigmoid(x):
    return 1.0 / (1.0 + jnp.exp(-x))


def _softplus(x):
    return jnp.maximum(x, 0.0) + jnp.log(1.0 + jnp.exp(-jnp.abs(x)))


def _rms(x, g, n=None):
    n = x.shape[-1] if n is None else n
    ms = jnp.sum(x * x, axis=-1, keepdims=True) * (1.0 / n)
    return (x * lax.rsqrt(ms + EPS)) * g


def _split3(x):
    hi = x.astype(BF)
    r1 = x - hi.astype(F32)
    mid = r1.astype(BF)
    lo = (r1 - mid.astype(F32)).astype(BF)
    return hi, mid, lo


def _interleave(lists):
    items = []
    for li, lst in enumerate(lists):
        items += [((i + 0.5) / len(lst), li, f) for i, f in enumerate(lst)]
    return [f for _, _, f in sorted(items, key=lambda it: (it[0], it[1]))]


def _params(sem):
    return pltpu.CompilerParams(dimension_semantics=sem, vmem_limit_bytes=VMEM_LIMIT)


def _in_proj_body(x_ref, g_ref, w_ref, wbat_ref, gcq_ref, gckv_ref, ctab_ref, stab_ref,
                  cqn_ref, ckv_ref, kpe_ref, qkv_ref, z_ref, ba_ref, bat_ref, xn_scr, *, nqkv, nz):
    j = pl.program_id(1)

    @pl.when(j == 0)
    def _():
        xn_scr[...] = _rms(x_ref[...], g_ref[...]).astype(BF)

    def halves(store):
        hw = COL_TILE // 2
        for c0 in (0, hw):
            store(c0, hw, _nt(xn_scr[...], w_ref[c0:c0 + hw, :]))

    @pl.when(j == 0)
    def _():
        cqn_ref[...] = _rms(_nt(xn_scr[...], w_ref[0:Q_RANK, :]), gcq_ref[...]).astype(BF)
        ckv_ref[...] = _rms(_nt(xn_scr[...], w_ref[Q_RANK:Q_RANK + KV_RANK, :]), gckv_ref[...])

    @pl.when((j >= 1) & (j < 1 + nqkv))
    def _():
        def store(c0, n, a):
            qkv_ref[:, c0:c0 + n] = a
        halves(store)

    @pl.when((j >= 1 + nqkv) & (j < 1 + nqkv + nz))
    def _():
        def store(c0, n, a):
            z_ref[:, c0:c0 + n] = a.astype(BF)
        halves(store)

    @pl.when(j == 1 + nqkv + nz)
    def _():
        acc = _nt(xn_scr[...], w_ref[0:3 * LANE, :])
        kpe_ref[...] = acc[:, 0:LANE] * ctab_ref[...] + acc[:, LANE:2 * LANE] * stab_ref[...]
        ba_ref[...] = acc[:, 2 * LANE:3 * LANE]
        bat_ref[...] = _nt(wbat_ref[...], xn_scr[...])


def _in_proj(x, g, w_p, wbat, gcq, gckv, ctab, stab, n_conv, n_z):
    t, d = x.shape
    tm = _tile(min(t, ctab.shape[0]), IN_TM)
    ntab = ctab.shape[0] // tm
    nqkv = n_conv // COL_TILE
    nz = n_z // COL_TILE
    ncol = 1 + nqkv + nz + 1
    assert Q_RANK + KV_RANK == COL_TILE and w_p.shape[0] == ncol * COL_TILE
    row = lambda i, j: (i, 0)
    const = lambda i, j: (0, 0)
    body = functools.partial(_in_proj_body, nqkv=nqkv, nz=nz)
    return pl.pallas_call(
        body,
        grid=(t // tm, ncol),
        in_specs=[
            pl.BlockSpec((tm, d), row),
            pl.BlockSpec((1, d), const),
            pl.BlockSpec((COL_TILE, d), lambda i, j: (j, 0)),
            pl.BlockSpec((16, d), const),
            pl.BlockSpec((1, Q_RANK), const),
            pl.BlockSpec((1, KV_RANK), const),
            pl.BlockSpec((tm, LANE), lambda i, j: (i % ntab, 0)),
            pl.BlockSpec((tm, LANE), lambda i, j: (i % ntab, 0)),
        ],
        out_specs=[
            pl.BlockSpec((tm, Q_RANK), row),
            pl.BlockSpec((tm, KV_RANK), row),
            pl.BlockSpec((tm, LANE), row),
            pl.BlockSpec((tm, COL_TILE), lambda i, j: (i, jnp.clip(j - 1, 0, nqkv - 1))),
            pl.BlockSpec((tm, COL_TILE), lambda i, j: (i, jnp.clip(j - 1 - nqkv, 0, nz - 1))),
            pl.BlockSpec((tm, LANE), row),
            pl.BlockSpec((16, tm), lambda i, j: (0, i)),
        ],
        out_shape=[
            jax.ShapeDtypeStruct((t, Q_RANK), BF),
            jax.ShapeDtypeStruct((t, KV_RANK), F32),
            jax.ShapeDtypeStruct((t, LANE), F32),
            jax.ShapeDtypeStruct((t, n_conv), F32),
            jax.ShapeDtypeStruct((t, n_z), BF),
            jax.ShapeDtypeStruct((t, LANE), F32),
            jax.ShapeDtypeStruct((16, t), F32),
        ],
        scratch_shapes=[pltpu.VMEM((tm, d), BF)],
        compiler_params=_params(("arbitrary", "arbitrary")),
        name="in_proj",
    )(x, g, w_p, wbat, gcq, gckv, ctab, stab)


def _q_proj_body(c_ref, w1_ref, w2_ref, ctab_ref, stab_ref, gn_ref, gr_ref, q_ref):
    c = c_ref[...]
    qf = _mm(c, w1_ref[...])
    qs = _mm(c, w2_ref[...])
    ct = ctab_ref[...]
    st = stab_ref[...]
    gn = gn_ref[...]
    gr = gr_ref[...]
    for h in range(MLA_HEADS):
        nope = qf[:, h * QPAD:h * QPAD + LANE]
        rot = qf[:, h * QPAD + LANE:(h + 1) * QPAD] * ct + qs[:, h * LANE:(h + 1) * LANE] * st
        ss = jnp.sum(nope * nope, axis=-1, keepdims=True) + jnp.sum(rot * rot, axis=-1, keepdims=True)
        rs = lax.rsqrt(ss * (1.0 / QK_HEAD) + EPS) * QSCALE
        q_ref[:, h * QPAD:h * QPAD + LANE] = (nope * rs * gn).astype(BF)
        q_ref[:, h * QPAD + LANE:(h + 1) * QPAD] = (rot * rs * gr).astype(BF)


def _q_proj(cqn, w1, w2, ctab, stab, gn, gr):
    t = cqn.shape[0]
    tm = _tile(min(t, ctab.shape[0]), PROJ_TM)
    ntab = ctab.shape[0] // tm
    const = lambda i: (0, 0)
    return pl.pallas_call(
        _q_proj_body,
        grid=(t // tm,),
        in_specs=[
            pl.BlockSpec((tm, Q_RANK), lambda i: (i, 0)),
            pl.BlockSpec(w1.shape, const),
            pl.BlockSpec(w2.shape, const),
            pl.BlockSpec((tm, LANE), lambda i: (i % ntab, 0)),
            pl.BlockSpec((tm, LANE), lambda i: (i % ntab, 0)),
            pl.BlockSpec((1, LANE), const),
            pl.BlockSpec((1, LANE), const),
        ],
        out_specs=pl.BlockSpec((tm, MLA_HEADS * QPAD), lambda i: (i, 0)),
        out_shape=jax.ShapeDtypeStruct((t, MLA_HEADS * QPAD), BF),
        compiler_params=_params(("arbitrary",)),
        name="q_proj",
    )(cqn, w1, w2, ctab, stab, gn, gr)


def _kv_proj_body(c_ref, kpe_ref, w_ref, gn_ref, gr_ref, k_ref, v_ref):
    kv = _mm(c_ref[...].astype(BF), w_ref[...])
    kp = kpe_ref[...]
    kps = jnp.sum(kp * kp, axis=-1, keepdims=True)
    gn = gn_ref[...]
    gr = gr_ref[...]
    for h in range(MLA_HEADS):
        kn = kv[:, h * 2 * LANE:h * 2 * LANE + LANE]
        rs = lax.rsqrt((jnp.sum(kn * kn, axis=-1, keepdims=True) + kps) * (1.0 / QK_HEAD) + EPS)
        k_ref[:, h * QPAD:h * QPAD + LANE] = (kn * rs * gn).astype(BF)
        k_ref[:, h * QPAD + LANE:(h + 1) * QPAD] = (kp * rs * gr).astype(BF)
        v_ref[:, h * V_DIM:(h + 1) * V_DIM] = kv[:, h * 2 * LANE + LANE:(h + 1) * 2 * LANE].astype(BF)


def _kv_proj(ckv, kpe_pad, w, gn, gr):
    t = ckv.shape[0]
    tm = _tile(t, PROJ_TM)
    const = lambda i: (0, 0)
    return pl.pallas_call(
        _kv_proj_body,
        grid=(t // tm,),
        in_specs=[
            pl.BlockSpec((tm, KV_RANK), lambda i: (i, 0)),
            pl.BlockSpec((tm, LANE), lambda i: (i, 0)),
            pl.BlockSpec(w.shape, const),
            pl.BlockSpec((1, LANE), const),
            pl.BlockSpec((1, LANE), const),
        ],
        out_specs=[
            pl.BlockSpec((tm, MLA_HEADS * QPAD), lambda i: (i, 0)),
            pl.BlockSpec((tm, MLA_HEADS * V_DIM), lambda i: (i, 0)),
        ],
        out_shape=[
            jax.ShapeDtypeStruct((t, MLA_HEADS * QPAD), BF),
            jax.ShapeDtypeStruct((t, MLA_HEADS * V_DIM), BF),
        ],
        compiler_params=_params(("arbitrary",)),
        name="kv_proj",
    )(ckv, kpe_pad, w, gn, gr)


def _attn_body(q_ref, k_ref, v_ref, g_ref, o_ref, m_scr, l_scr, acc_scr, *, tq, tk, wide, q_off, t_valid):
    qi = pl.program_id(2)
    qpos0 = q_off + qi * tq
    n_full = jnp.minimum((qpos0 // CHUNK * CHUNK + CHUNK) // tk, t_valid // tk)
    hi = jnp.minimum((qpos0 + tq - 1) // CHUNK * CHUNK + CHUNK, t_valid)
    n_total = (hi + tk - 1) // tk

    m_scr[...] = jnp.full(m_scr.shape, NEG_BIG, F32)
    l_scr[...] = jnp.zeros(l_scr.shape, F32)
    acc_scr[...] = jnp.zeros(acc_scr.shape, F32)
    diag = q_off % tk == 0 and tq == tk and t_valid % tk == 0 and tk % (2 * LANE) == 0 and tq % 32 == 0

    def step(kc, width, masked, r0=0, nr=tq):
        rows = slice(r0, r0 + nr)
        k0 = pl.multiple_of(kc * tk, tk)
        s = _nt(q_ref[rows, :], k_ref[pl.ds(k0, width), :])
        if masked:
            qpos = qpos0 + r0 + lax.broadcasted_iota(jnp.int32, (nr, 1), 0)
            last = jnp.minimum(qpos | (CHUNK - 1), t_valid - 1) - k0
            s = jnp.where(lax.broadcasted_iota(jnp.int32, (nr, width), 1) <= last, s, NEG_BIG)
        m_prev = m_scr[rows, :]
        m_new = jnp.maximum(m_prev, jnp.max(s, axis=-1, keepdims=True))
        alpha = jnp.exp2(m_prev - m_new)
        ps = [jnp.exp2(s[:, j * LANE:(j + 1) * LANE] - m_new) for j in range(width // LANE)]
        psum = ps[0]
        for pj in ps[1:]:
            psum = psum + pj
        l_scr[rows, :] = alpha * l_scr[rows, :] + psum
        p = jnp.concatenate(ps, axis=1).astype(BF) if len(ps) > 1 else ps[0].astype(BF)
        acc_scr[rows, :] = acc_scr[rows, :] * alpha + _mm(p, v_ref[pl.ds(k0, width), :])
        m_scr[rows, :] = m_new

    def loop(lo, hi, fn):
        lax.fori_loop(lo, hi, lambda i, c: (fn(i), c)[1], 0)

    done = 0
    for w in range(wide, 0, -1):
        cnt = (n_full - done) // w
        loop(0, cnt, lambda i, w=w, done=done: step(done + i * w, w * tk, False))
        done = done + cnt * w
    if diag:
        def masked(kc):
            step(kc, tk // 2, True, 0, tq // 2)
            step(kc, tk, True, tq // 2, tq // 2)
    else:
        def masked(kc):
            step(kc, tk, True)
    loop(n_full, n_total, masked)
    o = acc_scr[...] / jnp.sum(l_scr[...], axis=-1, keepdims=True)
    o_ref[...] = _rms(o, g_ref[...]).astype(BF)


def _attention(q, k, v, g_out, nb, q_off, t_valid):
    tq_total = q.shape[0] // nb
    tk_total = k.shape[0] // nb
    tq = _tile(tq_total, ATTN_TQ)
    tk = _tile(tk_total, ATTN_TK, mult=LANE)
    nq = tq_total // tq
    body = functools.partial(_attn_body, tq=tq, tk=tk, wide=ATTN_WIDE, q_off=q_off, t_valid=t_valid)
    return pl.pallas_call(
        body,
        grid=(nb, MLA_HEADS, nq),
        in_specs=[
            pl.BlockSpec((tq, QPAD), lambda b, h, i: (b * nq + i, h)),
            pl.BlockSpec((tk_total, QPAD), lambda b, h, i: (b, h)),
            pl.BlockSpec((tk_total, V_DIM), lambda b, h, i: (b, h)),
            pl.BlockSpec((1, V_DIM), lambda b, h, i: (0, 0)),
        ],
        out_specs=pl.BlockSpec((tq, V_DIM), lambda b, h, i: (b * nq + i, h)),
        out_shape=jax.ShapeDtypeStruct((q.shape[0], MLA_HEADS * V_DIM), BF),
        scratch_shapes=[
            pltpu.VMEM((tq, LANE), F32),
            pltpu.VMEM((tq, LANE), F32),
            pltpu.VMEM((tq, V_DIM), F32),
        ],
        compiler_params=_params(("arbitrary", "arbitrary", "arbitrary")),
        name="mla_attention",
    )(q, k, v, g_out)


def _gdn_body(q_ref, k_ref, v_ref, wq_ref, wk_ref, wv_ref, pq_ref, pk_ref, pv_ref,
              ba_ref, bat_ref, alog_l_ref, dtb_l_ref, alog_c_ref, dtb_c_ref,
              z_ref, s0_ref, gout_ref,
              o_ref, sout_ref,
              s_scr, halo_scr, vnew_scr, *, tm, L, hg):
    t = pl.program_id(2)
    nt = pl.num_programs(2)
    n_chunks = tm // L
    n_factors = int(math.log2(L)) - 1

    @pl.when(t == 0)
    def _():
        s_scr[...] = s0_ref[0]
        halo_scr[0] = pq_ref[...]
        halo_scr[1] = pk_ref[...]
        halo_scr[2] = pv_ref[...]
        vnew_scr[...] = jnp.zeros(vnew_scr.shape, BF)

    ba = ba_ref[...]
    beta_all = _sigmoid(ba)
    g_all = -jnp.exp(alog_l_ref[...]) * _softplus(ba + dtb_l_ref[...])
    gt_all = -jnp.exp(alog_c_ref[...]) * _softplus(bat_ref[0] + dtb_c_ref[...])

    r = lax.broadcasted_iota(jnp.int32, (tm, tm), 0)
    c = lax.broadcasted_iota(jnp.int32, (tm, tm), 1)
    same = (r // L) == (c // L)
    lower = same & (c <= r)
    strict = same & (c < r)
    cs = jnp.where(lower, 1.0, 0.0).astype(BF)
    bd = jnp.where(same, 1.0, 0.0).astype(BF)
    g3 = _split3(g_all)
    gcol_all = _mm(cs, g3[0]) + _mm(cs, g3[1]) + _mm(cs, g3[2])
    glast_all = _mm(bd, g3[0]) + _mm(bd, g3[1]) + _mm(bd, g3[2])
    gt3 = _split3(gt_all)
    grow_all = _nt(gt3[0], cs) + _nt(gt3[1], cs) + _nt(gt3[2], cs)

    gout = gout_ref[...]
    eye_t = jnp.where(r == c, 1.0, 0.0)
    eye_k = jnp.where(lax.broadcasted_iota(jnp.int32, (GDN_DK, GDN_DK), 0)
                      == lax.broadcasted_iota(jnp.int32, (GDN_DK, GDN_DK), 1), 1.0, 0.0).astype(BF)
    col_chunk = lax.broadcasted_iota(jnp.int32, (1, tm), 1) // L

    def make_wave(heads):
        c0, c1 = heads[0] * GDN_DK, (heads[-1] + 1) * GDN_DK
        loc = lambda h: slice((h - heads[0]) * GDN_DK, (h - heads[0] + 1) * GDN_DK)
        w = {}

        def conv(idx, x_ref, w_ref):
            xs = jnp.concatenate([halo_scr[idx, :, c0:c1], x_ref[:, c0:c1]], axis=0)
            cw = w_ref[:, c0:c1]
            acc = xs * cw[0:1]
            for i in range(1, CONV_W):
                acc = pltpu.roll(acc, 1, axis=0) + xs * cw[i:i + 1]
            halo_scr[idx, :, c0:c1] = xs[tm:tm + HALO]
            y = acc[HALO:HALO + tm]
            return y * _sigmoid(y)

        def p_q():
            w["yq"] = conv(0, q_ref, wq_ref)

        def p_k():
            w["yk"] = conv(1, k_ref, wk_ref)

        def p_v():
            w["yv"] = conv(2, v_ref, wv_ref)

        def p_gate():
            w["beta"], w["gc"], w["glast"], w["gr"] = {}, {}, {}, {}
            for h in heads:
                g = GDN_HEADS + h
                w["beta"][h] = beta_all[:, h:h + 1]
                w["gc"][h] = gcol_all[:, g:g + 1]
                w["glast"][h] = glast_all[:, g:g + 1]
                w["gr"][h] = grow_all[g:g + 1, :]

        def p_norm():
            w["qn"], w["kn"] = {}, {}
            for h in heads:
                qh = w["yq"][:, loc(h)]
                kh = w["yk"][:, loc(h)]
                w["qn"][h] = qh * lax.rsqrt(jnp.sum(qh * qh, axis=-1, keepdims=True) + EPS) * (GDN_DK ** -0.5)
                w["kn"][h] = kh * lax.rsqrt(jnp.sum(kh * kh, axis=-1, keepdims=True) + EPS)
            kb = {h: w["kn"][h].astype(BF) for h in heads}
            w["kk"] = {h: _nt(kb[h], kb[h]) for h in heads}
            w["qk"] = {h: _nt(w["qn"][h].astype(BF), kb[h]) for h in heads}

        def p_mat():
            beta, gc = w["beta"], w["gc"]
            decay = {h: jnp.where(lower, jnp.exp(gc[h] - w["gr"][h]), 0.0) for h in heads}
            w["nm"] = {h: jnp.where(strict, beta[h] * w["kk"][h] * decay[h], 0.0) for h in heads}
            w["attn"] = {h: (w["qk"][h] * decay[h]).astype(BF) for h in heads}
            w["eg"] = {h: jnp.exp(gc[h]) for h in heads}
            w["rhs"] = {h: jnp.concatenate([w["yv"][:, loc(h)] * beta[h],
                                            w["kn"][h] * (beta[h] * w["eg"][h])], axis=1) for h in heads}
            w["nb"] = {h: w["nm"][h].astype(BF) for h in heads}

        def s_first():
            w["m"] = {h: _mm(w["nb"][h], w["nb"][h]) for h in heads}
            w["T"] = {h: eye_t - w["nm"][h] for h in heads}

        def s_level(s):
            xb = {h: jnp.concatenate([w["m"][h], w["T"][h]], axis=0).astype(BF) for h in heads}
            if s < n_factors - 1:
                prod = {h: _mm(xb[h], xb[h][:tm]) for h in heads}
                w["T"] = {h: w["T"][h] + prod[h][tm:] for h in heads}
                w["m"] = {h: prod[h][:tm] for h in heads}
            else:
                w["T"] = {h: w["T"][h] + _mm(xb[h][tm:], xb[h][:tm]) for h in heads}

        def s_y0():
            w["tb"] = {h: w["T"][h].astype(BF) for h in heads}
            w["y0"] = {h: _mm(w["tb"][h], w["rhs"][h].astype(BF)) for h in heads}

        def s_res():
            res = {}
            for h in heads:
                nbf = w["nb"][h].astype(F32)
                nx = jnp.concatenate([nbf, w["nm"][h] - nbf], axis=0).astype(BF)
                yh = w["y0"][h].astype(BF)
                yl = (w["y0"][h] - yh.astype(F32)).astype(BF)
                p1 = _mm(nx, yh)
                res[h] = w["rhs"][h] - w["y0"][h] - (p1[:tm] + p1[tm:] + _mm(nx[:tm], yl))
            w["res"] = res

        def s_fix():
            w["y"] = {h: w["y0"][h] + _mm(w["tb"][h], w["res"][h].astype(BF)) for h in heads}

        def c_prep():
            y = w["y"]
            w["u"] = {h: y[h][:, :GDN_DV] for h in heads}
            w["wq"] = {}
            for h in heads:
                qd = w["qn"][h] * w["eg"][h]
                parts = []
                for ci in range(n_chunks):
                    parts += [y[h][ci * L:(ci + 1) * L, GDN_DV:], qd[ci * L:(ci + 1) * L]]
                w["wq"][h] = jnp.concatenate(parts, axis=0).astype(BF)
            kdb = {h: (w["kn"][h] * jnp.exp(w["glast"][h] - w["gc"][h])).astype(BF) for h in heads}
            w["gtot"] = {h: jnp.exp(w["glast"][h]) for h in heads}
            kdt = {h: _nt(eye_k, kdb[h]) for h in heads}
            w["kdt"] = {h: [jnp.where(col_chunk == ci, kdt[h], 0.0).astype(BF) for ci in range(n_chunks)]
                        for h in heads}
            w["st"] = {h: s_scr[h] for h in heads}
            w["outs"] = {h: [] for h in heads}

        def c_chunk(ci):
            r0 = ci * L
            st = w["st"]
            ws = {h: _mm(w["wq"][h][2 * r0:2 * r0 + 2 * L], st[h].astype(BF)) for h in heads}
            for h in heads:
                vnew_scr[h, r0:r0 + L, :] = (w["u"][h][r0:r0 + L] - ws[h][:L]).astype(BF)
            for h in heads:
                w["outs"][h].append(ws[h][L:] + _mm(w["attn"][h][r0:r0 + L, :], vnew_scr[h]))
            w["st"] = {h: st[h] * w["gtot"][h][r0:r0 + 1, :] + _mm(w["kdt"][h][ci], vnew_scr[h]) for h in heads}

        def c_out():
            for h in heads:
                s_scr[h] = w["st"][h]
                o = w["outs"][h][0] if n_chunks == 1 else jnp.concatenate(w["outs"][h], axis=0)
                zz = z_ref[:, h * GDN_DV:(h + 1) * GDN_DV].astype(F32)
                o_ref[:, h * GDN_DV:(h + 1) * GDN_DV] = (_rms(o, gout) * (zz * _sigmoid(zz))).astype(BF)

        prep = [p_q, p_k, p_v, p_gate, p_norm, p_mat]
        solve = ([s_first] + [functools.partial(s_level, s) for s in range(n_factors)] + [s_y0, s_res, s_fix])
        scan = [c_prep] + [functools.partial(c_chunk, ci) for ci in range(n_chunks)] + [c_out]
        return [prep, solve, scan]

    nw = min(GDN_WAVES, hg)
    per = hg // nw
    waves = [make_wave(list(range(i * per, (i + 1) * per))) for i in range(nw)]
    for slot in range(nw + 2):
        active = [waves[i][slot - i] for i in range(nw) if 0 <= slot - i < 3]
        for stage in _interleave(active):
            stage()

    @pl.when(t == nt - 1)
    def _():
        sout_ref[0] = s_scr[...]


def _gdn(qkv, conv_w, conv_past8, ba, bat, alog_l, dtb_l, alog_c, dtb_c, z, s0, gout, nb):
    t = qkv.shape[0]
    s_len = t // nb
    L = min(s_len, CHUNK)
    tm = _tile(s_len, GDN_TM, mult=L)
    nt = s_len // tm
    hg = GDN_HEADS
    hw = hg * GDN_DK
    body = functools.partial(_gdn_body, tm=tm, L=L, hg=hg)
    rows = lambda off: (lambda b, h, i: (b * nt + i, off + h))
    wcol = lambda off: (lambda b, h, i: (0, off + h))
    pcol = lambda off: (lambda b, h, i: (b, off + h))
    const = lambda b, h, i: (0, 0)
    ng = GDN_HEADS // hg
    return pl.pallas_call(
        body,
        grid=(nb, ng, nt),
        in_specs=[
            pl.BlockSpec((tm, hw), rows(0)),
            pl.BlockSpec((tm, hw), rows(ng)),
            pl.BlockSpec((tm, hw), rows(2 * ng)),
            pl.BlockSpec((CONV_W, hw), wcol(0)),
            pl.BlockSpec((CONV_W, hw), wcol(ng)),
            pl.BlockSpec((CONV_W, hw), wcol(2 * ng)),
            pl.BlockSpec((HALO, hw), pcol(0)),
            pl.BlockSpec((HALO, hw), pcol(ng)),
            pl.BlockSpec((HALO, hw), pcol(2 * ng)),
            pl.BlockSpec((tm, LANE), lambda b, h, i: (b * nt + i, 0)),
            pl.BlockSpec((1, 16, tm), lambda b, h, i: (b, 0, i)),
            pl.BlockSpec((1, LANE), const),
            pl.BlockSpec((1, LANE), const),
            pl.BlockSpec((16, 1), const),
            pl.BlockSpec((16, 1), const),
            pl.BlockSpec((tm, hw), rows(0)),
            pl.BlockSpec((1, hg, GDN_DK, GDN_DV), lambda b, h, i: (b, h, 0, 0)),
            pl.BlockSpec((1, GDN_DV), const),
        ],
        out_specs=[
            pl.BlockSpec((tm, hw), rows(0)),
            pl.BlockSpec((1, hg, GDN_DK, GDN_DV), lambda b, h, i: (b, h, 0, 0)),
        ],
        out_shape=[
            jax.ShapeDtypeStruct((t, GDN_HEADS * GDN_DV), BF),
            jax.ShapeDtypeStruct((nb, GDN_HEADS, GDN_DK, GDN_DV), F32),
        ],
        scratch_shapes=[
            pltpu.VMEM((hg, GDN_DK, GDN_DV), F32),
            pltpu.VMEM((3, HALO, hw), F32),
            pltpu.VMEM((hg, tm, GDN_DV), BF),
        ],
        compiler_params=_params(("arbitrary", "arbitrary", "arbitrary")),
        name="gdn",
    )(qkv, qkv, qkv, conv_w, conv_w, conv_w, conv_past8, conv_past8, conv_past8,
      ba, bat, alog_l, dtb_l, alog_c, dtb_c, z, s0, gout)


def _mix_mem_body(x_ref, a1_ref, a2_ref, wo1_ref, wo2_ref, gq_ref, wmq_ref, gmq_ref,
                  mk_ref, mv_ref, wmo_ref, o_ref):
    h1 = x_ref[...] + _mm(a1_ref[...], wo1_ref[...]) + _mm(a2_ref[...], wo2_ref[...])
    hn = _rms(h1, gq_ref[...]).astype(BF)
    q = _mm(hn, wmq_ref[...])
    gmq = gmq_ref[...]
    mk = mk_ref[0]
    mv = mv_ref[0]
    outs = []
    for h in range(MEM_HEADS):
        sl = slice(h * MEM_DIM, (h + 1) * MEM_DIM)
        qh = _rms(q[:, sl], gmq).astype(BF)
        s = _nt(qh, mk[:, sl]) * (MEM_DIM ** -0.5)
        p = jnp.exp(s - jnp.max(s, axis=-1, keepdims=True))
        p = p / jnp.sum(p, axis=-1, keepdims=True)
        outs.append(_mm(p.astype(BF), mv[:, sl]))
    o = jnp.concatenate(outs, axis=1).astype(BF)
    o_ref[...] = h1 + _mm(o, wmo_ref[...])


def _mix_mem(x, a1, a2, wo1, wo2, gq, wmq, gmq, mk, mv, wmo, nb):
    t, d = x.shape
    s_len = t // nb
    tm = _tile(s_len, MIX_TM)
    nt = s_len // tm
    n_mem = mk.shape[1]
    mw = MEM_HEADS * MEM_DIM
    row = lambda i: (i, 0)
    const = lambda i: (0, 0)
    return pl.pallas_call(
        _mix_mem_body,
        grid=(t // tm,),
        in_specs=[
            pl.BlockSpec((tm, d), row),
            pl.BlockSpec((tm, a1.shape[1]), row),
            pl.BlockSpec((tm, a2.shape[1]), row),
            pl.BlockSpec(wo1.shape, const),
            pl.BlockSpec(wo2.shape, const),
            pl.BlockSpec((1, d), const),
            pl.BlockSpec(wmq.shape, const),
            pl.BlockSpec((1, MEM_DIM), const),
            pl.BlockSpec((1, n_mem, mw), lambda i: (i // nt, 0, 0)),
            pl.BlockSpec((1, n_mem, mw), lambda i: (i // nt, 0, 0)),
            pl.BlockSpec(wmo.shape, const),
        ],
        out_specs=pl.BlockSpec((tm, d), row),
        out_shape=jax.ShapeDtypeStruct((t, d), F32),
        compiler_params=_params(("arbitrary",)),
        name="mix_mem",
    )(x, a1, a2, wo1, wo2, gq, wmq, gmq, mk, mv, wmo)


def _ffn_body(x_ref, g_ref, w1_ref, w2_ref, o_ref, hn_scr):
    f = pl.program_id(1)

    @pl.when(f == 0)
    def _():
        x = x_ref[...]
        hn_scr[...] = _rms(x, g_ref[...]).astype(BF)
        o_ref[...] = x

    a = jnp.maximum(_mm(hn_scr[...], w1_ref[...]), 0.0)
    o_ref[...] += _mm((a * a).astype(BF), w2_ref[...])


def _ffn(x, g, w1, w2):
    t, d = x.shape
    dff = w1.shape[1]
    tm = _tile(t, FFN_TM)
    tf = _tile(dff, FFN_TF, mult=LANE)
    return pl.pallas_call(
        _ffn_body,
        grid=(t // tm, dff // tf),
        in_specs=[
            pl.BlockSpec((tm, d), lambda i, f: (i, 0)),
            pl.BlockSpec((1, d), lambda i, f: (0, 0)),
            pl.BlockSpec((d, tf), lambda i, f: (0, f)),
            pl.BlockSpec((tf, d), lambda i, f: (f, 0)),
        ],
        out_specs=pl.BlockSpec((tm, d), lambda i, f: (i, 0)),
        out_shape=jax.ShapeDtypeStruct((t, d), F32),
        scratch_shapes=[pltpu.VMEM((tm, d), BF)],
        compiler_params=_params(("arbitrary", "arbitrary")),
        name="ffn",
    )(x, g, w1, w2)


def _mem_kv_body(m_ref, g_ref, wk_ref, wv_ref, gk_ref, k_ref, v_ref):
    mn = _rms(m_ref[...], g_ref[...]).astype(BF)
    k = _mm(mn, wk_ref[...])
    gk = gk_ref[...]
    for h in range(MEM_HEADS):
        sl = slice(h * MEM_DIM, (h + 1) * MEM_DIM)
        k_ref[:, sl] = _rms(k[:, sl], gk)
    v_ref[...] = _mm(mn, wv_ref[...])


def _mem_kv(mem, g, wk, wv, gk):
    t, d = mem.shape
    tm = _tile(t, 256)
    mw = MEM_HEADS * MEM_DIM
    const = lambda i: (0, 0)
    return pl.pallas_call(
        _mem_kv_body,
        grid=(t // tm,),
        in_specs=[
            pl.BlockSpec((tm, d), lambda i: (i, 0)),
            pl.BlockSpec((1, d), const),
            pl.BlockSpec(wk.shape, const),
            pl.BlockSpec(wv.shape, const),
            pl.BlockSpec((1, MEM_DIM), const),
        ],
        out_specs=[pl.BlockSpec((tm, mw), lambda i: (i, 0)), pl.BlockSpec((tm, mw), lambda i: (i, 0))],
        out_shape=[jax.ShapeDtypeStruct((t, mw), F32), jax.ShapeDtypeStruct((t, mw), F32)],
        compiler_params=_params(("arbitrary",)),
        name="mem_kv",
    )(mem, g, wk, wv, gk)


def _rope_tables(pos):
    half = QK_ROPE // 2
    inv_freq = ROPE_THETA ** (-np.arange(half, dtype=np.float64) / half)
    ang = np.asarray(pos, np.float64)[:, None] * inv_freq[None, :]
    cos, sin = np.cos(ang), np.sin(ang)
    zeros = np.zeros((ang.shape[0], LANE - QK_ROPE))
    return (jnp.asarray(np.concatenate([cos, cos, zeros], axis=1), F32),
            jnp.asarray(np.concatenate([-sin, sin, zeros], axis=1), F32))


def _pad_lanes(v, n=LANE):
    return jnp.pad(v, ((0, 0), (0, n - v.shape[1])))


def _prep_weights(w_in, g_cq, w_uq, g_ckv, w_ukv, g_q, g_k, w_o, w_mq, w_mo, w_ff1, w_ff2, w_mk, w_mv):
    d = w_in.shape[0]
    n_conv = 2 * GDN_HEADS * GDN_DK + GDN_HEADS * GDN_DV
    n_z = GDN_HEADS * GDN_DV
    wt = w_in.T.astype(BF)
    o = 0
    w_cq = wt[o:o + Q_RANK]; o += Q_RANK
    w_ckv = wt[o:o + KV_RANK]; o += KV_RANK
    w_kpe = wt[o:o + QK_ROPE]; o += QK_ROPE
    w_qkv = wt[o:o + n_conv]; o += n_conv
    w_z = wt[o:o + n_z]; o += n_z
    w_b = wt[o:o + GDN_HEADS]; o += GDN_HEADS
    w_a = wt[o:o + GDN_HEADS]
    half = QK_ROPE // 2
    swap = jnp.concatenate([jnp.arange(half, QK_ROPE), jnp.arange(0, half)])
    z64 = jnp.zeros((LANE - QK_ROPE, d), BF)
    misc = jnp.concatenate([w_kpe, z64, w_kpe[swap], z64, w_b, w_a], axis=0)
    misc = jnp.pad(misc, ((0, COL_TILE - misc.shape[0]), (0, 0)))
    w_in_p = jnp.concatenate([w_cq, w_ckv, w_qkv, w_z, misc], axis=0)
    wbat = jnp.concatenate([w_b, w_a], axis=0)

    r = w_uq.shape[0]
    wq3 = w_uq.reshape(r, MLA_HEADS, QK_HEAD)
    zq = jnp.zeros((r, MLA_HEADS, LANE - QK_ROPE), w_uq.dtype)
    w1 = jnp.concatenate([wq3, zq], axis=2).reshape(r, MLA_HEADS * QPAD).astype(BF)
    w2 = jnp.concatenate([wq3[:, :, QK_NOPE:][:, :, swap], zq], axis=2).reshape(r, MLA_HEADS * LANE).astype(BF)

    dmla = MLA_HEADS * V_DIM
    return dict(
        n_conv=n_conv, n_z=n_z, w_in_p=w_in_p, wbat=wbat, w1=w1, w2=w2,
        gcq=g_cq[None, :], gckv=g_ckv[None, :],
        gq_n=g_q[None, :QK_NOPE], gq_r=_pad_lanes(g_q[None, QK_NOPE:]),
        gk_n=g_k[None, :QK_NOPE], gk_r=_pad_lanes(g_k[None, QK_NOPE:]),
        w_ukv=w_ukv.astype(BF), wo1=w_o[:dmla].astype(BF), wo2=w_o[dmla:].astype(BF),
        w_mq=w_mq.astype(BF), w_mo=w_mo.astype(BF), w_ff1=w_ff1.astype(BF), w_ff2=w_ff2.astype(BF),
        w_mk=w_mk.astype(BF), w_mv=w_mv.astype(BF),
    )


def _gate_params(a_log, dt_bias):
    z8 = jnp.zeros((GDN_HEADS,), F32)
    al = jnp.concatenate([z8, a_log.astype(F32)])
    db = jnp.concatenate([z8, dt_bias.astype(F32)])
    return _pad_lanes(al[None, :]), _pad_lanes(db[None, :]), al[:, None], db[:, None]


def _layer(x, pos, past, conv_past, s0, mem_k, mem_v, wp, lw):
    nb, s_len, d = x.shape
    t = nb * s_len
    xf = x.reshape(t, d)
    ctab, stab = _rope_tables(pos)
    if ctab.shape[0] % 8 != 0 or (s_len < IN_TM and nb > 1):
        ctab, stab = jnp.tile(ctab, (nb, 1)), jnp.tile(stab, (nb, 1))

    cqn, ckv, kpe_pad, qkv, z, ba, bat = _in_proj(
        xf, lw["g_norm_mix"], wp["w_in_p"], wp["wbat"], wp["gcq"], wp["gckv"], ctab, stab,
        wp["n_conv"], wp["n_z"])
    q = _q_proj(cqn, wp["w1"], wp["w2"], ctab, stab, wp["gq_n"], wp["gq_r"])

    if past is None:
        ckv_all, kpe_all, q_off, t_valid = ckv, kpe_pad, 0, s_len
    else:
        ckv_past, kpe_past = past
        p_len = ckv_past.shape[1]
        t_valid = p_len + s_len
        t_pad = -(-t_valid // PROJ_TM) * PROJ_TM
        ckv_all = jnp.concatenate(
            [ckv_past.astype(F32), ckv.reshape(nb, s_len, KV_RANK),
             jnp.zeros((nb, t_pad - t_valid, KV_RANK), F32)], axis=1).reshape(nb * t_pad, KV_RANK)
        kpe_all = jnp.concatenate(
            [jnp.pad(kpe_past.astype(F32), ((0, 0), (0, 0), (0, LANE - QK_ROPE))),
             kpe_pad.reshape(nb, s_len, LANE),
             jnp.zeros((nb, t_pad - t_valid, LANE), F32)], axis=1).reshape(nb * t_pad, LANE)
        q_off = p_len
    k, v = _kv_proj(ckv_all, kpe_all, wp["w_ukv"], wp["gk_n"], wp["gk_r"])
    o_mla = _attention(q, k, v, lw["g_mla_out"], nb, q_off, t_valid)

    conv_past8 = jnp.pad(conv_past.astype(F32), ((0, 0), (HALO - (CONV_W - 1), 0), (0, 0)))
    conv_past8 = conv_past8.reshape(nb * HALO, -1)
    bat3 = bat.reshape(16, nb, s_len).transpose(1, 0, 2)
    alog_l, dtb_l, alog_c, dtb_c = _gate_params(lw["a_log"], lw["dt_bias"])
    o_gdn, s_new = _gdn(qkv, lw["conv_w"], conv_past8, ba, bat3, alog_l, dtb_l, alog_c, dtb_c,
                        z, s0.astype(F32), lw["g_gdn_out"], nb)

    h2 = _mix_mem(xf, o_mla, o_gdn, wp["wo1"], wp["wo2"], lw["g_norm_mem_q"], wp["w_mq"],
                  lw["g_mq"], mem_k, mem_v, wp["w_mo"], nb)
    y = _ffn(h2, lw["g_norm_ffn"], wp["w_ff1"], wp["w_ff2"])

    conv_in_tail = jnp.concatenate([conv_past.astype(F32), qkv.reshape(nb, s_len, -1)[:, -(CONV_W - 1):]], axis=1)
    conv_new = conv_in_tail[:, -(CONV_W - 1):]
    return (y.reshape(nb, s_len, d), ckv.reshape(nb, s_len, KV_RANK),
            kpe_pad[:, :QK_ROPE].reshape(nb, s_len, QK_ROPE), conv_new, s_new)


def kernel(x_prompt, x_sample, mem_prompt, cache_mla_ckv, cache_mla_kpe, cache_gdn_conv, state_gdn, cache_mem_k, cache_mem_v, g_norm_mix, w_in, g_cq, w_uq, g_ckv, w_ukv, g_q_mla, g_k_mla, g_mla_out, conv_w, a_log, dt_bias, g_gdn_out, w_o, g_norm_mem_q, g_norm_mem_kv, w_mq, w_mk, w_mv, g_mq, g_mk, w_mo, g_norm_ffn, w_ff1, w_ff2):
    depth = w_in.shape[0]
    nbp, sp, d = x_prompt.shape
    nbs, ss, _ = x_sample.shape
    n_mem = mem_prompt.shape[1]
    mw = MEM_HEADS * MEM_DIM
    n_conv = 2 * GDN_HEADS * GDN_DK + GDN_HEADS * GDN_DV
    pos_p = np.arange(sp)
    pos_s = cache_mla_ckv.shape[2] + np.arange(ss)
    zeros_conv = jnp.zeros((nbp, CONV_W - 1, n_conv), F32)
    zeros_state = jnp.zeros((nbp, GDN_HEADS, GDN_DK, GDN_DV), F32)
    hp, hs = x_prompt, x_sample
    outs_p = [[] for _ in range(6)]
    outs_s = [[] for _ in range(4)]
    for l in range(depth):
        wp = _prep_weights(w_in[l], g_cq[l], w_uq[l], g_ckv[l], w_ukv[l], g_q_mla[l], g_k_mla[l], w_o[l],
                           w_mq[l], w_mo[l], w_ff1[l], w_ff2[l], w_mk[l], w_mv[l])
        lw = dict(g_norm_mix=g_norm_mix[l][None, :], g_mla_out=g_mla_out[l][None, :], conv_w=conv_w[l],
                  a_log=a_log[l], dt_bias=dt_bias[l], g_gdn_out=g_gdn_out[l][None, :],
                  g_norm_mem_q=g_norm_mem_q[l][None, :], g_mq=g_mq[l][None, :],
                  g_norm_ffn=g_norm_ffn[l][None, :])
        mk, mv = _mem_kv(mem_prompt.reshape(nbp * n_mem, d), g_norm_mem_kv[l][None, :], wp["w_mk"], wp["w_mv"],
                         g_mk[l][None, :])
        mk3, mv3 = mk.reshape(nbp, n_mem, mw), mv.reshape(nbp, n_mem, mw)
        hp, c1, c2, c3, c4 = _layer(hp, pos_p, None, zeros_conv, zeros_state, mk3.astype(BF), mv3.astype(BF), wp, lw)
        for lst, val in zip(outs_p, (c1, c2, c3, c4, mk3.reshape(nbp, n_mem, MEM_HEADS, MEM_DIM),
                                     mv3.reshape(nbp, n_mem, MEM_HEADS, MEM_DIM))):
            lst.append(val)
        hs, d1, d2, d3, d4 = _layer(hs, pos_s, (cache_mla_ckv[l], cache_mla_kpe[l]), cache_gdn_conv[l], state_gdn[l],
                                    cache_mem_k[l].reshape(nbs, n_mem, mw).astype(BF),
                                    cache_mem_v[l].reshape(nbs, n_mem, mw).astype(BF), wp, lw)
        for lst, val in zip(outs_s, (d1, d2, d3, d4)):
            lst.append(val)
    return (hp, hs, *(jnp.stack(v) for v in outs_p), *(jnp.stack(v) for v in outs_s))
```

```python
import functools
import math

import jax
import jax.numpy as jnp
import numpy as np
from jax import lax
from jax.experimental import pallas as pl
from jax.experimental.pallas import tpu as pltpu

F32 = jnp.float32
BF = jnp.bfloat16

EPS = 1e-6
CHUNK = 64
ROPE_THETA = 10000.0
MLA_HEADS = 8
QK_NOPE = 128
QK_ROPE = 64
QK_HEAD = QK_NOPE + QK_ROPE
V_DIM = 128
Q_RANK = 512
KV_RANK = 512
GDN_HEADS = 8
GDN_DK = 128
GDN_DV = 128
CONV_W = 4
MEM_HEADS = 4
MEM_DIM = 128
LANE = 128
QPAD = 2 * LANE
COL_TILE = 512
MISC_ROWS = 3 * LANE
HALO = 8
NEG_BIG = -1e30
QSCALE = (QK_HEAD ** -0.5) * math.log2(math.e)

VMEM_LIMIT = 56 * 1024 * 1024

IN_TM = 512
PROJ_TM = 512
ATTN_TQ = 1024
ATTN_TK = 1024
ATTN_WIDE = 3
GDN_TM = 256
GDN_WAVES = 2
MIX_TM = 512
FFN_TM = 1024
FFN_TF = 512


def _tile(n, pref, mult=8):
    if n <= pref:
        return n
    t = (pref // mult) * mult
    while t >= mult:
        if n % t == 0:
            return t
        t -= mult
    return n


def _nt(a, b):
    return lax.dot_general(a, b, (((1,), (1,)), ((), ())), preferred_element_type=F32)


def _mm(a, b):
    return jnp.dot(a, b, preferred_element_type=F32)


def _sigmoid(x):
    return 1.0 / (1.0 + jnp.exp(-x))


def _softplus(x):
    return jnp.maximum(x, 0.0) + jnp.log(1.0 + jnp.exp(-jnp.abs(x)))


def _rms(x, g, n=None):
    n = x.shape[-1] if n is None else n
    ms = jnp.sum(x * x, axis=-1, keepdims=True) * (1.0 / n)
    return (x * lax.rsqrt(ms + EPS)) * g


def _split3(x):
    hi = x.astype(BF)
    r1 = x - hi.astype(F32)
    mid = r1.astype(BF)
    lo = (r1 - mid.astype(F32)).astype(BF)
    return hi, mid, lo


def _interleave(lists):
    items = []
    for li, lst in enumerate(lists):
        items += [((i + 0.5) / len(lst), li, f) for i, f in enumerate(lst)]
    return [f for _, _, f in sorted(items, key=lambda it: (it[0], it[1]))]


def _params(sem):
    return pltpu.CompilerParams(dimension_semantics=sem, vmem_limit_bytes=VMEM_LIMIT)


def _in_proj_body(x_ref, g_ref, w_ref, wbat_ref, gcq_ref, gckv_ref, ctab_ref, stab_ref,
                  cqn_ref, ckv_ref, kpe_ref, qkv_ref, z_ref, ba_ref, bat_ref, xn_scr, *, n_conv, n_z):
    xn_scr[...] = _rms(x_ref[...], g_ref[...]).astype(BF)

    def prod(r0, n):
        return _nt(xn_scr[...], w_ref[r0:r0 + n, :])

    cqn_ref[...] = _rms(prod(0, Q_RANK), gcq_ref[...]).astype(BF)
    ckv_ref[...] = _rms(prod(Q_RANK, KV_RANK), gckv_ref[...])
    r0 = Q_RANK + KV_RANK
    for c0 in range(0, n_conv, COL_TILE):
        qkv_ref[:, c0:c0 + COL_TILE] = prod(r0 + c0, COL_TILE)
    r0 += n_conv
    for c0 in range(0, n_z, COL_TILE):
        z_ref[:, c0:c0 + COL_TILE] = prod(r0 + c0, COL_TILE).astype(BF)
    r0 += n_z
    acc = prod(r0, MISC_ROWS)
    kpe_ref[...] = acc[:, 0:LANE] * ctab_ref[...] + acc[:, LANE:2 * LANE] * stab_ref[...]
    ba_ref[...] = acc[:, 2 * LANE:3 * LANE]
    bat_ref[...] = _nt(wbat_ref[...], xn_scr[...])


def _in_proj(x, g, w_p, wbat, gcq, gckv, ctab, stab, n_conv, n_z):
    t, d = x.shape
    tm = _tile(min(t, ctab.shape[0]), IN_TM)
    ntab = ctab.shape[0] // tm
    assert w_p.shape[0] == Q_RANK + KV_RANK + n_conv + n_z + MISC_ROWS
    assert n_conv % COL_TILE == 0 and n_z % COL_TILE == 0
    row = lambda i: (i, 0)
    const = lambda i: (0, 0)
    body = functools.partial(_in_proj_body, n_conv=n_conv, n_z=n_z)
    return pl.pallas_call(
        body,
        grid=(t // tm,),
        in_specs=[
            pl.BlockSpec((tm, d), row),
            pl.BlockSpec((1, d), const),
            pl.BlockSpec(w_p.shape, const, pipeline_mode=pl.Buffered(1)),
            pl.BlockSpec((16, d), const),
            pl.BlockSpec((1, Q_RANK), const),
            pl.BlockSpec((1, KV_RANK), const),
            pl.BlockSpec((tm, LANE), lambda i: (i % ntab, 0)),
            pl.BlockSpec((tm, LANE), lambda i: (i % ntab, 0)),
        ],
        out_specs=[
            pl.BlockSpec((tm, Q_RANK), row),
            pl.BlockSpec((tm, KV_RANK), row),
            pl.BlockSpec((tm, LANE), row),
            pl.BlockSpec((tm, n_conv), row),
            pl.BlockSpec((tm, n_z), row),
            pl.BlockSpec((tm, LANE), row),
            pl.BlockSpec((16, tm), lambda i: (0, i)),
        ],
        out_shape=[
            jax.ShapeDtypeStruct((t, Q_RANK), BF),
            jax.ShapeDtypeStruct((t, KV_RANK), F32),
            jax.ShapeDtypeStruct((t, LANE), F32),
            jax.ShapeDtypeStruct((t, n_conv), F32),
            jax.ShapeDtypeStruct((t, n_z), BF),
            jax.ShapeDtypeStruct((t, LANE), F32),
            jax.ShapeDtypeStruct((16, t), F32),
        ],
        scratch_shapes=[pltpu.VMEM((tm, d), BF)],
        compiler_params=_params(("arbitrary",)),
        name="in_proj",
    )(x, g, w_p, wbat, gcq, gckv, ctab, stab)


def _q_proj_body(c_ref, w1_ref, w2_ref, ctab_ref, stab_ref, gn_ref, gr_ref, q_ref):
    c = c_ref[...]
    qf = _mm(c, w1_ref[...])
    qs = _mm(c, w2_ref[...])
    ct = ctab_ref[...]
    st = stab_ref[...]
    gn = gn_ref[...]
    gr = gr_ref[...]
    for h in range(MLA_HEADS):
        nope = qf[:, h * QPAD:h * QPAD + LANE]
        rot = qf[:, h * QPAD + LANE:(h + 1) * QPAD] * ct + qs[:, h * LANE:(h + 1) * LANE] * st
        ss = jnp.sum(nope * nope, axis=-1, keepdims=True) + jnp.sum(rot * rot, axis=-1, keepdims=True)
        rs = lax.rsqrt(ss * (1.0 / QK_HEAD) + EPS) * QSCALE
        q_ref[:, h * QPAD:h * QPAD + LANE] = (nope * rs * gn).astype(BF)
        q_ref[:, h * QPAD + LANE:(h + 1) * QPAD] = (rot * rs * gr).astype(BF)


def _q_proj(cqn, w1, w2, ctab, stab, gn, gr):
    t = cqn.shape[0]
    tm = _tile(min(t, ctab.shape[0]), PROJ_TM)
    ntab = ctab.shape[0] // tm
    const = lambda i: (0, 0)
    return pl.pallas_call(
        _q_proj_body,
        grid=(t // tm,),
        in_specs=[
            pl.BlockSpec((tm, Q_RANK), lambda i: (i, 0)),
            pl.BlockSpec(w1.shape, const),
            pl.BlockSpec(w2.shape, const),
            pl.BlockSpec((tm, LANE), lambda i: (i % ntab, 0)),
            pl.BlockSpec((tm, LANE), lambda i: (i % ntab, 0)),
            pl.BlockSpec((1, LANE), const),
            pl.BlockSpec((1, LANE), const),
        ],
        out_specs=pl.BlockSpec((tm, MLA_HEADS * QPAD), lambda i: (i, 0)),
        out_shape=jax.ShapeDtypeStruct((t, MLA_HEADS * QPAD), BF),
        compiler_params=_params(("arbitrary",)),
        name="q_proj",
    )(cqn, w1, w2, ctab, stab, gn, gr)


def _kv_proj_body(c_ref, kpe_ref, w_ref, gn_ref, gr_ref, k_ref, v_ref):
    kv = _mm(c_ref[...].astype(BF), w_ref[...])
    kp = kpe_ref[...]
    kps = jnp.sum(kp * kp, axis=-1, keepdims=True)
    gn = gn_ref[...]
    gr = gr_ref[...]
    for h in range(MLA_HEADS):
        kn = kv[:, h * 2 * LANE:h * 2 * LANE + LANE]
        rs = lax.rsqrt((jnp.sum(kn * kn, axis=-1, keepdims=True) + kps) * (1.0 / QK_HEAD) + EPS)
        k_ref[:, h * QPAD:h * QPAD + LANE] = (kn * rs * gn).astype(BF)
        k_ref[:, h * QPAD + LANE:(h + 1) * QPAD] = (kp * rs * gr).astype(BF)
        v_ref[:, h * V_DIM:(h + 1) * V_DIM] = kv[:, h * 2 * LANE + LANE:(h + 1) * 2 * LANE].astype(BF)


def _kv_proj(ckv, kpe_pad, w, gn, gr):
    t = ckv.shape[0]
    tm = _tile(t, PROJ_TM)
    const = lambda i: (0, 0)
    return pl.pallas_call(
        _kv_proj_body,
        grid=(t // tm,),
        in_specs=[
            pl.BlockSpec((tm, KV_RANK), lambda i: (i, 0)),
            pl.BlockSpec((tm, LANE), lambda i: (i, 0)),
            pl.BlockSpec(w.shape, const),
            pl.BlockSpec((1, LANE), const),
            pl.BlockSpec((1, LANE), const),
        ],
        out_specs=[
            pl.BlockSpec((tm, MLA_HEADS * QPAD), lambda i: (i, 0)),
            pl.BlockSpec((tm, MLA_HEADS * V_DIM), lambda i: (i, 0)),
        ],
        out_shape=[
            jax.ShapeDtypeStruct((t, MLA_HEADS * QPAD), BF),
            jax.ShapeDtypeStruct((t, MLA_HEADS * V_DIM), BF),
        ],
        compiler_params=_params(("arbitrary",)),
        name="kv_proj",
    )(ckv, kpe_pad, w, gn, gr)


def _attn_body(q_ref, k_ref, v_ref, g_ref, o_ref, m_scr, l_scr, acc_scr, *, tq, tk, wide, q_off, t_valid):
    qi = pl.program_id(2)
    qpos0 = q_off + qi * tq
    n_full = jnp.minimum((qpos0 // CHUNK * CHUNK + CHUNK) // tk, t_valid // tk)
    hi = jnp.minimum((qpos0 + tq - 1) // CHUNK * CHUNK + CHUNK, t_valid)
    n_total = (hi + tk - 1) // tk

    m_scr[...] = jnp.full(m_scr.shape, NEG_BIG, F32)
    l_scr[...] = jnp.zeros(l_scr.shape, F32)
    acc_scr[...] = jnp.zeros(acc_scr.shape, F32)
    diag = q_off % tk == 0 and tq == tk and t_valid % tk == 0 and tk % (2 * LANE) == 0 and tq % 32 == 0

    def step(kc, width, masked, r0=0, nr=tq):
        rows = slice(r0, r0 + nr)
        k0 = pl.multiple_of(kc * tk, tk)
        s = _nt(q_ref[rows, :], k_ref[pl.ds(k0, width), :])
        if masked:
            qpos = qpos0 + r0 + lax.broadcasted_iota(jnp.int32, (nr, 1), 0)
            last = jnp.minimum(qpos | (CHUNK - 1), t_valid - 1) - k0
            s = jnp.where(lax.broadcasted_iota(jnp.int32, (nr, width), 1) <= last, s, NEG_BIG)
        m_prev = m_scr[rows, :]
        m_new = jnp.maximum(m_prev, jnp.max(s, axis=-1, keepdims=True))
        alpha = jnp.exp2(m_prev - m_new)
        ps = [jnp.exp2(s[:, j * LANE:(j + 1) * LANE] - m_new) for j in range(width // LANE)]
        psum = ps[0]
        for pj in ps[1:]:
            psum = psum + pj
        l_scr[rows, :] = alpha * l_scr[rows, :] + psum
        p = jnp.concatenate(ps, axis=1).astype(BF) if len(ps) > 1 else ps[0].astype(BF)
        acc_scr[rows, :] = acc_scr[rows, :] * alpha + _mm(p, v_ref[pl.ds(k0, width), :])
        m_scr[rows, :] = m_new

    def loop(lo, hi, fn):
        lax.fori_loop(lo, hi, lambda i, c: (fn(i), c)[1], 0)

    done = 0
    for w in range(wide, 0, -1):
        cnt = (n_full - done) // w
        loop(0, cnt, lambda i, w=w, done=done: step(done + i * w, w * tk, False))
        done = done + cnt * w
    if diag:
        def masked(kc):
            step(kc, tk // 2, True, 0, tq // 2)
            step(kc, tk, True, tq // 2, tq // 2)
    else:
        def masked(kc):
            step(kc, tk, True)
    loop(n_full, n_total, masked)
    o = acc_scr[...] / jnp.sum(l_scr[...], axis=-1, keepdims=True)
    o_ref[...] = _rms(o, g_ref[...]).astype(BF)


def _attention(q, k, v, g_out, nb, q_off, t_valid):
    tq_total = q.shape[0] // nb
    tk_total = k.shape[0] // nb
    tq = _tile(tq_total, ATTN_TQ)
    tk = _tile(tk_total, ATTN_TK, mult=LANE)
    nq = tq_total // tq
    body = functools.partial(_attn_body, tq=tq, tk=tk, wide=ATTN_WIDE, q_off=q_off, t_valid=t_valid)
    return pl.pallas_call(
        body,
        grid=(nb, MLA_HEADS, nq),
        in_specs=[
            pl.BlockSpec((tq, QPAD), lambda b, h, i: (b * nq + i, h)),
            pl.BlockSpec((tk_total, QPAD), lambda b, h, i: (b, h)),
            pl.BlockSpec((tk_total, V_DIM), lambda b, h, i: (b, h)),
            pl.BlockSpec((1, V_DIM), lambda b, h, i: (0, 0)),
        ],
        out_specs=pl.BlockSpec((tq, V_DIM), lambda b, h, i: (b * nq + i, h)),
        out_shape=jax.ShapeDtypeStruct((q.shape[0], MLA_HEADS * V_DIM), BF),
        scratch_shapes=[
            pltpu.VMEM((tq, LANE), F32),
            pltpu.VMEM((tq, LANE), F32),
            pltpu.VMEM((tq, V_DIM), F32),
        ],
        compiler_params=_params(("arbitrary", "arbitrary", "arbitrary")),
        name="mla_attention",
    )(q, k, v, g_out)


def _gdn_body(q_ref, k_ref, v_ref, wq_ref, wk_ref, wv_ref, pq_ref, pk_ref, pv_ref,
              ba_ref, bat_ref, alog_l_ref, dtb_l_ref, alog_c_ref, dtb_c_ref,
              z_ref, s0_ref, gout_ref,
              o_ref, sout_ref,
              s_scr, halo_scr, vnew_scr, *, tm, L, hg):
    t = pl.program_id(2)
    nt = pl.num_programs(2)
    n_chunks = tm // L
    n_factors = int(math.log2(L)) - 1

    @pl.when(t == 0)
    def _():
        s_scr[...] = s0_ref[0]
        halo_scr[0] = pq_ref[...]
        halo_scr[1] = pk_ref[...]
        halo_scr[2] = pv_ref[...]
        vnew_scr[...] = jnp.zeros(vnew_scr.shape, BF)

    ba = ba_ref[...]
    beta_all = _sigmoid(ba)
    g_all = -jnp.exp(alog_l_ref[...]) * _softplus(ba + dtb_l_ref[...])
    gt_all = -jnp.exp(alog_c_ref[...]) * _softplus(bat_ref[0] + dtb_c_ref[...])

    r = lax.broadcasted_iota(jnp.int32, (tm, tm), 0)
    c = lax.broadcasted_iota(jnp.int32, (tm, tm), 1)
    same = (r // L) == (c // L)
    lower = same & (c <= r)
    strict = same & (c < r)
    cs = jnp.where(lower, 1.0, 0.0).astype(BF)
    bd = jnp.where(same, 1.0, 0.0).astype(BF)
    g3 = _split3(g_all)
    gcol_all = _mm(cs, g3[0]) + _mm(cs, g3[1]) + _mm(cs, g3[2])
    glast_all = _mm(bd, g3[0]) + _mm(bd, g3[1]) + _mm(bd, g3[2])
    gt3 = _split3(gt_all)
    grow_all = _nt(gt3[0], cs) + _nt(gt3[1], cs) + _nt(gt3[2], cs)

    gout = gout_ref[...]
    eye_t = jnp.where(r == c, 1.0, 0.0)
    eye_k = jnp.where(lax.broadcasted_iota(jnp.int32, (GDN_DK, GDN_DK), 0)
                      == lax.broadcasted_iota(jnp.int32, (GDN_DK, GDN_DK), 1), 1.0, 0.0).astype(BF)
    col_chunk = lax.broadcasted_iota(jnp.int32, (1, tm), 1) // L

    def make_wave(heads):
        c0, c1 = heads[0] * GDN_DK, (heads[-1] + 1) * GDN_DK
        loc = lambda h: slice((h - heads[0]) * GDN_DK, (h - heads[0] + 1) * GDN_DK)
        w = {}

        def conv(idx, x_ref, w_ref):
            xs = jnp.concatenate([halo_scr[idx, :, c0:c1], x_ref[:, c0:c1]], axis=0)
            cw = w_ref[:, c0:c1]
            acc = xs * cw[0:1]
            for i in range(1, CONV_W):
                acc = pltpu.roll(acc, 1, axis=0) + xs * cw[i:i + 1]
            halo_scr[idx, :, c0:c1] = xs[tm:tm + HALO]
            y = acc[HALO:HALO + tm]
            return y * _sigmoid(y)

        def p_q():
            w["yq"] = conv(0, q_ref, wq_ref)

        def p_k():
            w["yk"] = conv(1, k_ref, wk_ref)

        def p_v():
            w["yv"] = conv(2, v_ref, wv_ref)

        def p_gate():
            w["beta"], w["gc"], w["glast"], w["gr"] = {}, {}, {}, {}
            for h in heads:
                g = GDN_HEADS + h
                w["beta"][h] = beta_all[:, h:h + 1]
                w["gc"][h] = gcol_all[:, g:g + 1]
                w["glast"][h] = glast_all[:, g:g + 1]
                w["gr"][h] = grow_all[g:g + 1, :]

        def p_norm():
            w["qn"], w["kn"] = {}, {}
            for h in heads:
                qh = w["yq"][:, loc(h)]
                kh = w["yk"][:, loc(h)]
                w["qn"][h] = qh * lax.rsqrt(jnp.sum(qh * qh, axis=-1, keepdims=True) + EPS) * (GDN_DK ** -0.5)
                w["kn"][h] = kh * lax.rsqrt(jnp.sum(kh * kh, axis=-1, keepdims=True) + EPS)
            kb = {h: w["kn"][h].astype(BF) for h in heads}
            w["kk"] = {h: _nt(kb[h], kb[h]) for h in heads}
            w["qk"] = {h: _nt(w["qn"][h].astype(BF), kb[h]) for h in heads}

        def p_mat():
            beta, gc = w["beta"], w["gc"]
            decay = {h: jnp.where(lower, jnp.exp(gc[h] - w["gr"][h]), 0.0) for h in heads}
            w["nm"] = {h: jnp.where(strict, beta[h] * w["kk"][h] * decay[h], 0.0) for h in heads}
            w["attn"] = {h: (w["qk"][h] * decay[h]).astype(BF) for h in heads}
            w["eg"] = {h: jnp.exp(gc[h]) for h in heads}
            w["rhs"] = {h: jnp.concatenate([w["yv"][:, loc(h)] * beta[h],
                                            w["kn"][h] * (beta[h] * w["eg"][h])], axis=1) for h in heads}
            w["nb"] = {h: w["nm"][h].astype(BF) for h in heads}

        def s_first():
            w["m"] = {h: _mm(w["nb"][h], w["nb"][h]) for h in heads}
            w["T"] = {h: eye_t - w["nm"][h] for h in heads}

        def s_level(s):
            xb = {h: jnp.concatenate([w["m"][h], w["T"][h]], axis=0).astype(BF) for h in heads}
            if s < n_factors - 1:
                prod = {h: _mm(xb[h], xb[h][:tm]) for h in heads}
                w["T"] = {h: w["T"][h] + prod[h][tm:] for h in heads}
                w["m"] = {h: prod[h][:tm] for h in heads}
            else:
                w["T"] = {h: w["T"][h] + _mm(xb[h][tm:], xb[h][:tm]) for h in heads}

        def s_y0():
            w["tb"] = {h: w["T"][h].astype(BF) for h in heads}
            w["y0"] = {h: _mm(w["tb"][h], w["rhs"][h].astype(BF)) for h in heads}

        def s_res():
            res = {}
            for h in heads:
                nbf = w["nb"][h].astype(F32)
                nx = jnp.concatenate([nbf, w["nm"][h] - nbf], axis=0).astype(BF)
                yh = w["y0"][h].astype(BF)
                yl = (w["y0"][h] - yh.astype(F32)).astype(BF)
                p1 = _mm(nx, yh)
                res[h] = w["rhs"][h] - w["y0"][h] - (p1[:tm] + p1[tm:] + _mm(nx[:tm], yl))
            w["res"] = res

        def s_fix():
            w["y"] = {h: w["y0"][h] + _mm(w["tb"][h], w["res"][h].astype(BF)) for h in heads}

        def c_prep():
            y = w["y"]
            w["u"] = {h: y[h][:, :GDN_DV] for h in heads}
            w["wq"] = {}
            for h in heads:
                qd = w["qn"][h] * w["eg"][h]
                parts = []
                for ci in range(n_chunks):
                    parts += [y[h][ci * L:(ci + 1) * L, GDN_DV:], qd[ci * L:(ci + 1) * L]]
                w["wq"][h] = jnp.concatenate(parts, axis=0).astype(BF)
            kdb = {h: (w["kn"][h] * jnp.exp(w["glast"][h] - w["gc"][h])).astype(BF) for h in heads}
            w["gtot"] = {h: jnp.exp(w["glast"][h]) for h in heads}
            kdt = {h: _nt(eye_k, kdb[h]) for h in heads}
            w["kdt"] = {h: [jnp.where(col_chunk == ci, kdt[h], 0.0).astype(BF) for ci in range(n_chunks)]
                        for h in heads}
            w["st"] = {h: s_scr[h] for h in heads}
            w["outs"] = {h: [] for h in heads}

        def c_chunk(ci):
            r0 = ci * L
            st = w["st"]
            ws = {h: _mm(w["wq"][h][2 * r0:2 * r0 + 2 * L], st[h].astype(BF)) for h in heads}
            for h in heads:
                vnew_scr[h, r0:r0 + L, :] = (w["u"][h][r0:r0 + L] - ws[h][:L]).astype(BF)
            for h in heads:
                w["outs"][h].append(ws[h][L:] + _mm(w["attn"][h][r0:r0 + L, :], vnew_scr[h]))
            w["st"] = {h: st[h] * w["gtot"][h][r0:r0 + 1, :] + _mm(w["kdt"][h][ci], vnew_scr[h]) for h in heads}

        def c_out():
            for h in heads:
                s_scr[h] = w["st"][h]
                o = w["outs"][h][0] if n_chunks == 1 else jnp.concatenate(w["outs"][h], axis=0)
                zz = z_ref[:, h * GDN_DV:(h + 1) * GDN_DV].astype(F32)
                o_ref[:, h * GDN_DV:(h + 1) * GDN_DV] = (_rms(o, gout) * (zz * _sigmoid(zz))).astype(BF)

        prep = [p_q, p_k, p_v, p_gate, p_norm, p_mat]
        solve = ([s_first] + [functools.partial(s_level, s) for s in range(n_factors)] + [s_y0, s_res, s_fix])
        scan = [c_prep] + [functools.partial(c_chunk, ci) for ci in range(n_chunks)] + [c_out]
        return [prep, solve, scan]

    nw = min(GDN_WAVES, hg)
    per = hg // nw
    waves = [make_wave(list(range(i * per, (i + 1) * per))) for i in range(nw)]
    for slot in range(nw + 2):
        active = [waves[i][slot - i] for i in range(nw) if 0 <= slot - i < 3]
        for stage in _interleave(active):
            stage()

    @pl.when(t == nt - 1)
    def _():
        sout_ref[0] = s_scr[...]


def _gdn(qkv, conv_w, conv_past8, ba, bat, alog_l, dtb_l, alog_c, dtb_c, z, s0, gout, nb):
    t = qkv.shape[0]
    s_len = t // nb
    L = min(s_len, CHUNK)
    tm = _tile(s_len, GDN_TM, mult=L)
    nt = s_len // tm
    hg = GDN_HEADS
    hw = hg * GDN_DK
    body = functools.partial(_gdn_body, tm=tm, L=L, hg=hg)
    rows = lambda off: (lambda b, h, i: (b * nt + i, off + h))
    wcol = lambda off: (lambda b, h, i: (0, off + h))
    pcol = lambda off: (lambda b, h, i: (b, off + h))
    const = lambda b, h, i: (0, 0)
    ng = GDN_HEADS // hg
    return pl.pallas_call(
        body,
        grid=(nb, ng, nt),
        in_specs=[
            pl.BlockSpec((tm, hw), rows(0)),
            pl.BlockSpec((tm, hw), rows(ng)),
            pl.BlockSpec((tm, hw), rows(2 * ng)),
            pl.BlockSpec((CONV_W, hw), wcol(0)),
            pl.BlockSpec((CONV_W, hw), wcol(ng)),
            pl.BlockSpec((CONV_W, hw), wcol(2 * ng)),
            pl.BlockSpec((HALO, hw), pcol(0)),
            pl.BlockSpec((HALO, hw), pcol(ng)),
            pl.BlockSpec((HALO, hw), pcol(2 * ng)),
            pl.BlockSpec((tm, LANE), lambda b, h, i: (b * nt + i, 0)),
            pl.BlockSpec((1, 16, tm), lambda b, h, i: (b, 0, i)),
            pl.BlockSpec((1, LANE), const),
            pl.BlockSpec((1, LANE), const),
            pl.BlockSpec((16, 1), const),
            pl.BlockSpec((16, 1), const),
            pl.BlockSpec((tm, hw), rows(0)),
            pl.BlockSpec((1, hg, GDN_DK, GDN_DV), lambda b, h, i: (b, h, 0, 0)),
            pl.BlockSpec((1, GDN_DV), const),
        ],
        out_specs=[
            pl.BlockSpec((tm, hw), rows(0)),
            pl.BlockSpec((1, hg, GDN_DK, GDN_DV), lambda b, h, i: (b, h, 0, 0)),
        ],
        out_shape=[
            jax.ShapeDtypeStruct((t, GDN_HEADS * GDN_DV), BF),
            jax.ShapeDtypeStruct((nb, GDN_HEADS, GDN_DK, GDN_DV), F32),
        ],
        scratch_shapes=[
            pltpu.VMEM((hg, GDN_DK, GDN_DV), F32),
            pltpu.VMEM((3, HALO, hw), F32),
            pltpu.VMEM((hg, tm, GDN_DV), BF),
        ],
        compiler_params=_params(("arbitrary", "arbitrary", "arbitrary")),
        name="gdn",
    )(qkv, qkv, qkv, conv_w, conv_w, conv_w, conv_past8, conv_past8, conv_past8,
      ba, bat, alog_l, dtb_l, alog_c, dtb_c, z, s0, gout)


def _mix_mem_body(x_ref, a1_ref, a2_ref, wo1_ref, wo2_ref, gq_ref, wmq_ref, gmq_ref,
                  mk_ref, mv_ref, wmo_ref, o_ref):
    h1 = x_ref[...] + _mm(a1_ref[...], wo1_ref[...]) + _mm(a2_ref[...], wo2_ref[...])
    hn = _rms(h1, gq_ref[...]).astype(BF)
    q = _mm(hn, wmq_ref[...])
    gmq = gmq_ref[...]
    mk = mk_ref[0]
    mv = mv_ref[0]
    outs = []
    for h in range(MEM_HEADS):
        sl = slice(h * MEM_DIM, (h + 1) * MEM_DIM)
        qh = _rms(q[:, sl], gmq).astype(BF)
        s = _nt(qh, mk[:, sl]) * (MEM_DIM ** -0.5)
        p = jnp.exp(s - jnp.max(s, axis=-1, keepdims=True))
        p = p / jnp.sum(p, axis=-1, keepdims=True)
        outs.append(_mm(p.astype(BF), mv[:, sl]))
    o = jnp.concatenate(outs, axis=1).astype(BF)
    o_ref[...] = h1 + _mm(o, wmo_ref[...])


def _mix_mem(x, a1, a2, wo1, wo2, gq, wmq, gmq, mk, mv, wmo, nb):
    t, d = x.shape
    s_len = t // nb
    tm = _tile(s_len, MIX_TM)
    nt = s_len // tm
    n_mem = mk.shape[1]
    mw = MEM_HEADS * MEM_DIM
    row = lambda i: (i, 0)
    const = lambda i: (0, 0)
    return pl.pallas_call(
        _mix_mem_body,
        grid=(t // tm,),
        in_specs=[
            pl.BlockSpec((tm, d), row),
            pl.BlockSpec((tm, a1.shape[1]), row),
            pl.BlockSpec((tm, a2.shape[1]), row),
            pl.BlockSpec(wo1.shape, const),
            pl.BlockSpec(wo2.shape, const),
            pl.BlockSpec((1, d), const),
            pl.BlockSpec(wmq.shape, const),
            pl.BlockSpec((1, MEM_DIM), const),
            pl.BlockSpec((1, n_mem, mw), lambda i: (i // nt, 0, 0)),
            pl.BlockSpec((1, n_mem, mw), lambda i: (i // nt, 0, 0)),
            pl.BlockSpec(wmo.shape, const),
        ],
        out_specs=pl.BlockSpec((tm, d), row),
        out_shape=jax.ShapeDtypeStruct((t, d), F32),
        compiler_params=_params(("arbitrary",)),
        name="mix_mem",
    )(x, a1, a2, wo1, wo2, gq, wmq, gmq, mk, mv, wmo)


def _ffn_body(x_ref, g_ref, w1_ref, w2_ref, o_ref, hn_scr):
    f = pl.program_id(1)

    @pl.when(f == 0)
    def _():
        x = x_ref[...]
        hn_scr[...] = _rms(x, g_ref[...]).astype(BF)
        o_ref[...] = x

    a = jnp.maximum(_mm(hn_scr[...], w1_ref[...]), 0.0)
    o_ref[...] += _mm((a * a).astype(BF), w2_ref[...])


def _ffn(x, g, w1, w2):
    t, d = x.shape
    dff = w1.shape[1]
    tm = _tile(t, FFN_TM)
    tf = _tile(dff, FFN_TF, mult=LANE)
    return pl.pallas_call(
        _ffn_body,
        grid=(t // tm, dff // tf),
        in_specs=[
            pl.BlockSpec((tm, d), lambda i, f: (i, 0)),
            pl.BlockSpec((1, d), lambda i, f: (0, 0)),
            pl.BlockSpec((d, tf), lambda i, f: (0, f)),
            pl.BlockSpec((tf, d), lambda i, f: (f, 0)),
        ],
        out_specs=pl.BlockSpec((tm, d), lambda i, f: (i, 0)),
        out_shape=jax.ShapeDtypeStruct((t, d), F32),
        scratch_shapes=[pltpu.VMEM((tm, d), BF)],
        compiler_params=_params(("arbitrary", "arbitrary")),
        name="ffn",
    )(x, g, w1, w2)


def _mem_kv_body(m_ref, g_ref, wk_ref, wv_ref, gk_ref, k_ref, v_ref):
    mn = _rms(m_ref[...], g_ref[...]).astype(BF)
    k = _mm(mn, wk_ref[...])
    gk = gk_ref[...]
    for h in range(MEM_HEADS):
        sl = slice(h * MEM_DIM, (h + 1) * MEM_DIM)
        k_ref[:, sl] = _rms(k[:, sl], gk)
    v_ref[...] = _mm(mn, wv_ref[...])


def _mem_kv(mem, g, wk, wv, gk):
    t, d = mem.shape
    tm = _tile(t, 256)
    mw = MEM_HEADS * MEM_DIM
    const = lambda i: (0, 0)
    return pl.pallas_call(
        _mem_kv_body,
        grid=(t // tm,),
        in_specs=[
            pl.BlockSpec((tm, d), lambda i: (i, 0)),
            pl.BlockSpec((1, d), const),
            pl.BlockSpec(wk.shape, const),
            pl.BlockSpec(wv.shape, const),
            pl.BlockSpec((1, MEM_DIM), const),
        ],
        out_specs=[pl.BlockSpec((tm, mw), lambda i: (i, 0)), pl.BlockSpec((tm, mw), lambda i: (i, 0))],
        out_shape=[jax.ShapeDtypeStruct((t, mw), F32), jax.ShapeDtypeStruct((t, mw), F32)],
        compiler_params=_params(("arbitrary",)),
        name="mem_kv",
    )(mem, g, wk, wv, gk)


def _rope_tables(pos):
    half = QK_ROPE // 2
    inv_freq = ROPE_THETA ** (-np.arange(half, dtype=np.float64) / half)
    ang = np.asarray(pos, np.float64)[:, None] * inv_freq[None, :]
    cos, sin = np.cos(ang), np.sin(ang)
    zeros = np.zeros((ang.shape[0], LANE - QK_ROPE))
    return (jnp.asarray(np.concatenate([cos, cos, zeros], axis=1), F32),
            jnp.asarray(np.concatenate([-sin, sin, zeros], axis=1), F32))


def _pad_lanes(v, n=LANE):
    return jnp.pad(v, ((0, 0), (0, n - v.shape[1])))


def _prep_weights(w_in, g_cq, w_uq, g_ckv, w_ukv, g_q, g_k, w_o, w_mq, w_mo, w_ff1, w_ff2, w_mk, w_mv):
    d = w_in.shape[0]
    n_conv = 2 * GDN_HEADS * GDN_DK + GDN_HEADS * GDN_DV
    n_z = GDN_HEADS * GDN_DV
    wt = w_in.T.astype(BF)
    o = 0
    w_cq = wt[o:o + Q_RANK]; o += Q_RANK
    w_ckv = wt[o:o + KV_RANK]; o += KV_RANK
    w_kpe = wt[o:o + QK_ROPE]; o += QK_ROPE
    w_qkv = wt[o:o + n_conv]; o += n_conv
    w_z = wt[o:o + n_z]; o += n_z
    w_b = wt[o:o + GDN_HEADS]; o += GDN_HEADS
    w_a = wt[o:o + GDN_HEADS]
    half = QK_ROPE // 2
    swap = jnp.concatenate([jnp.arange(half, QK_ROPE), jnp.arange(0, half)])
    z64 = jnp.zeros((LANE - QK_ROPE, d), BF)
    misc = jnp.concatenate([w_kpe, z64, w_kpe[swap], z64, w_b, w_a], axis=0)
    misc = jnp.pad(misc, ((0, MISC_ROWS - misc.shape[0]), (0, 0)))
    w_in_p = jnp.concatenate([w_cq, w_ckv, w_qkv, w_z, misc], axis=0)
    wbat = jnp.concatenate([w_b, w_a], axis=0)

    r = w_uq.shape[0]
    wq3 = w_uq.reshape(r, MLA_HEADS, QK_HEAD)
    zq = jnp.zeros((r, MLA_HEADS, LANE - QK_ROPE), w_uq.dtype)
    w1 = jnp.concatenate([wq3, zq], axis=2).reshape(r, MLA_HEADS * QPAD).astype(BF)
    w2 = jnp.concatenate([wq3[:, :, QK_NOPE:][:, :, swap], zq], axis=2).reshape(r, MLA_HEADS * LANE).astype(BF)

    dmla = MLA_HEADS * V_DIM
    return dict(
        n_conv=n_conv, n_z=n_z, w_in_p=w_in_p, wbat=wbat, w1=w1, w2=w2,
        gcq=g_cq[None, :], gckv=g_ckv[None, :],
        gq_n=g_q[None, :QK_NOPE], gq_r=_pad_lanes(g_q[None, QK_NOPE:]),
        gk_n=g_k[None, :QK_NOPE], gk_r=_pad_lanes(g_k[None, QK_NOPE:]),
        w_ukv=w_ukv.astype(BF), wo1=w_o[:dmla].astype(BF), wo2=w_o[dmla:].astype(BF),
        w_mq=w_mq.astype(BF), w_mo=w_mo.astype(BF), w_ff1=w_ff1.astype(BF), w_ff2=w_ff2.astype(BF),
        w_mk=w_mk.astype(BF), w_mv=w_mv.astype(BF),
    )


def _gate_params(a_log, dt_bias):
    z8 = jnp.zeros((GDN_HEADS,), F32)
    al = jnp.concatenate([z8, a_log.astype(F32)])
    db = jnp.concatenate([z8, dt_bias.astype(F32)])
    return _pad_lanes(al[None, :]), _pad_lanes(db[None, :]), al[:, None], db[:, None]


def _layer(x, pos, past, conv_past, s0, mem_k, mem_v, wp, lw):
    nb, s_len, d = x.shape
    t = nb * s_len
    xf = x.reshape(t, d)
    ctab, stab = _rope_tables(pos)
    if ctab.shape[0] % 8 != 0 or (s_len < IN_TM and nb > 1):
        ctab, stab = jnp.tile(ctab, (nb, 1)), jnp.tile(stab, (nb, 1))

    cqn, ckv, kpe_pad, qkv, z, ba, bat = _in_proj(
        xf, lw["g_norm_mix"], wp["w_in_p"], wp["wbat"], wp["gcq"], wp["gckv"], ctab, stab,
        wp["n_conv"], wp["n_z"])
    q = _q_proj(cqn, wp["w1"], wp["w2"], ctab, stab, wp["gq_n"], wp["gq_r"])

    if past is None:
        ckv_all, kpe_all, q_off, t_valid = ckv, kpe_pad, 0, s_len
    else:
        ckv_past, kpe_past = past
        p_len = ckv_past.shape[1]
        t_valid = p_len + s_len
        t_pad = -(-t_valid // PROJ_TM) * PROJ_TM
        ckv_all = jnp.concatenate(
            [ckv_past.astype(F32), ckv.reshape(nb, s_len, KV_RANK),
             jnp.zeros((nb, t_pad - t_valid, KV_RANK), F32)], axis=1).reshape(nb * t_pad, KV_RANK)
        kpe_all = jnp.concatenate(
            [jnp.pad(kpe_past.astype(F32), ((0, 0), (0, 0), (0, LANE - QK_ROPE))),
             kpe_pad.reshape(nb, s_len, LANE),
             jnp.zeros((nb, t_pad - t_valid, LANE), F32)], axis=1).reshape(nb * t_pad, LANE)
        q_off = p_len
    k, v = _kv_proj(ckv_all, kpe_all, wp["w_ukv"], wp["gk_n"], wp["gk_r"])
    o_mla = _attention(q, k, v, lw["g_mla_out"], nb, q_off, t_valid)

    conv_past8 = jnp.pad(conv_past.astype(F32), ((0, 0), (HALO - (CONV_W - 1), 0), (0, 0)))
    conv_past8 = conv_past8.reshape(nb * HALO, -1)
    bat3 = bat.reshape(16, nb, s_len).transpose(1, 0, 2)
    alog_l, dtb_l, alog_c, dtb_c = _gate_params(lw["a_log"], lw["dt_bias"])
    o_gdn, s_new = _gdn(qkv, lw["conv_w"], conv_past8, ba, bat3, alog_l, dtb_l, alog_c, dtb_c,
                        z, s0.astype(F32), lw["g_gdn_out"], nb)

    h2 = _mix_mem(xf, o_mla, o_gdn, wp["wo1"], wp["wo2"], lw["g_norm_mem_q"], wp["w_mq"],
                  lw["g_mq"], mem_k, mem_v, wp["w_mo"], nb)
    y = _ffn(h2, lw["g_norm_ffn"], wp["w_ff1"], wp["w_ff2"])

    conv_in_tail = jnp.concatenate([conv_past.astype(F32), qkv.reshape(nb, s_len, -1)[:, -(CONV_W - 1):]], axis=1)
    conv_new = conv_in_tail[:, -(CONV_W - 1):]
    return (y.reshape(nb, s_len, d), ckv.reshape(nb, s_len, KV_RANK),
            kpe_pad[:, :QK_ROPE].reshape(nb, s_len, QK_ROPE), conv_new, s_new)


def kernel(x_prompt, x_sample, mem_prompt, cache_mla_ckv, cache_mla_kpe, cache_gdn_conv, state_gdn, cache_mem_k, cache_mem_v, g_norm_mix, w_in, g_cq, w_uq, g_ckv, w_ukv, g_q_mla, g_k_mla, g_mla_out, conv_w, a_log, dt_bias, g_gdn_out, w_o, g_norm_mem_q, g_norm_mem_kv, w_mq, w_mk, w_mv, g_mq, g_mk, w_mo, g_norm_ffn, w_ff1, w_ff2):
    depth = w_in.shape[0]
    nbp, sp, d = x_prompt.shape
    nbs, ss, _ = x_sample.shape
    n_mem = mem_prompt.shape[1]
    mw = MEM_HEADS * MEM_DIM
    n_conv = 2 * GDN_HEADS * GDN_DK + GDN_HEADS * GDN_DV
    pos_p = np.arange(sp)
    pos_s = cache_mla_ckv.shape[2] + np.arange(ss)
    zeros_conv = jnp.zeros((nbp, CONV_W - 1, n_conv), F32)
    zeros_state = jnp.zeros((nbp, GDN_HEADS, GDN_DK, GDN_DV), F32)
    hp, hs = x_prompt, x_sample
    outs_p = [[] for _ in range(6)]
    outs_s = [[] for _ in range(4)]
    for l in range(depth):
        wp = _prep_weights(w_in[l], g_cq[l], w_uq[l], g_ckv[l], w_ukv[l], g_q_mla[l], g_k_mla[l], w_o[l],
                           w_mq[l], w_mo[l], w_ff1[l], w_ff2[l], w_mk[l], w_mv[l])
        lw = dict(g_norm_mix=g_norm_mix[l][None, :], g_mla_out=g_mla_out[l][None, :], conv_w=conv_w[l],
                  a_log=a_log[l], dt_bias=dt_bias[l], g_gdn_out=g_gdn_out[l][None, :],
                  g_norm_mem_q=g_norm_mem_q[l][None, :], g_mq=g_mq[l][None, :],
                  g_norm_ffn=g_norm_ffn[l][None, :])
        mk, mv = _mem_kv(mem_prompt.reshape(nbp * n_mem, d), g_norm_mem_kv[l][None, :], wp["w_mk"], wp["w_mv"],
                         g_mk[l][None, :])
        mk3, mv3 = mk.reshape(nbp, n_mem, mw), mv.reshape(nbp, n_mem, mw)
        hp, c1, c2, c3, c4 = _layer(hp, pos_p, None, zeros_conv, zeros_state, mk3.astype(BF), mv3.astype(BF), wp, lw)
        for lst, val in zip(outs_p, (c1, c2, c3, c4, mk3.reshape(nbp, n_mem, MEM_HEADS, MEM_DIM),
                                     mv3.reshape(nbp, n_mem, MEM_HEADS, MEM_DIM))):
            lst.append(val)
        hs, d1, d2, d3, d4 = _layer(hs, pos_s, (cache_mla_ckv[l], cache_mla_kpe[l]), cache_gdn_conv[l], state_gdn[l],
                                    cache_mem_k[l].reshape(nbs, n_mem, mw).astype(BF),
                                    cache_mem_v[l].reshape(nbs, n_mem, mw).astype(BF), wp, lw)
        for lst, val in zip(outs_s, (d1, d2, d3, d4)):
            lst.append(val)
    return (hp, hs, *(jnp.stack(v) for v in outs_p), *(jnp.stack(v) for v in outs_s))
```

```python
import functools
import math

import jax
import jax.numpy as jnp
import numpy as np
from jax import lax
from jax.experimental import pallas as pl
from jax.experimental.pallas import tpu as pltpu

F32 = jnp.float32
BF = jnp.bfloat16

EPS = 1e-6
CHUNK = 64
ROPE_THETA = 10000.0
MLA_HEADS = 8
QK_NOPE = 128
QK_ROPE = 64
QK_HEAD = QK_NOPE + QK_ROPE
V_DIM = 128
Q_RANK = 512
KV_RANK = 512
GDN_HEADS = 8
GDN_DK = 128
GDN_DV = 128
CONV_W = 4
MEM_HEADS = 4
MEM_DIM = 128
LANE = 128
QPAD = 2 * LANE
COL_TILE = 512
MISC_ROWS = 3 * LANE
HALO = 8
NEG_BIG = -1e30
QSCALE = (QK_HEAD ** -0.5) * math.log2(math.e)

VMEM_LIMIT = 56 * 1024 * 1024

IN_TM = 512
PROJ_TM = 512
ATTN_TQ = 1024
ATTN_TK = 1024
ATTN_WIDE = 3
GDN_TM = 256
GDN_WAVES = 2
MIX_TM = 512
FFN_TM = 1024
FFN_TF = 512


def _tile(n, pref, mult=8):
    if n <= pref:
        return n
    t = (pref // mult) * mult
    while t >= mult:
        if n % t == 0:
            return t
        t -= mult
    return n


def _nt(a, b):
    return lax.dot_general(a, b, (((1,), (1,)), ((), ())), preferred_element_type=F32)


def _mm(a, b):
    return jnp.dot(a, b, preferred_element_type=F32)


def _sigmoid(x):
    return 1.0 / (1.0 + jnp.exp(-x))


def _softplus(x):
    return jnp.maximum(x, 0.0) + jnp.log(1.0 + jnp.exp(-jnp.abs(x)))


def _rms(x, g, n=None):
    n = x.shape[-1] if n is None else n
    ms = jnp.sum(x * x, axis=-1, keepdims=True) * (1.0 / n)
    return (x * lax.rsqrt(ms + EPS)) * g


def _split3(x):
    hi = x.astype(BF)
    r1 = x - hi.astype(F32)
    mid = r1.astype(BF)
    lo = (r1 - mid.astype(F32)).astype(BF)
    return hi, mid, lo


def _interleave(lists):
    items = []
    for li, lst in enumerate(lists):
        items += [((i + 0.5) / len(lst), li, f) for i, f in enumerate(lst)]
    return [f for _, _, f in sorted(items, key=lambda it: (it[0], it[1]))]


def _params(sem):
    return pltpu.CompilerParams(dimension_semantics=sem, vmem_limit_bytes=VMEM_LIMIT)


def _in_proj_body(x_ref, g_ref, w_ref, wbat_ref, gcq_ref, gckv_ref, ctab_ref, stab_ref,
                  cqn_ref, ckv_ref, kpe_ref, qkv_ref, z_ref, ba_ref, bat_ref, xn_scr, *, n_conv, n_z):
    xn_scr[...] = _rms(x_ref[...], g_ref[...]).astype(BF)

    def prod(r0, n):
        return _nt(xn_scr[...], w_ref[r0:r0 + n, :])

    cqn_ref[...] = _rms(prod(0, Q_RANK), gcq_ref[...]).astype(BF)
    ckv_ref[...] = _rms(prod(Q_RANK, KV_RANK), gckv_ref[...])
    r0 = Q_RANK + KV_RANK
    for c0 in range(0, n_conv, COL_TILE):
        qkv_ref[:, c0:c0 + COL_TILE] = prod(r0 + c0, COL_TILE)
    r0 += n_conv
    for c0 in range(0, n_z, COL_TILE):
        z_ref[:, c0:c0 + COL_TILE] = prod(r0 + c0, COL_TILE).astype(BF)
    r0 += n_z
    acc = prod(r0, MISC_ROWS)
    kpe_ref[...] = acc[:, 0:LANE] * ctab_ref[...] + acc[:, LANE:2 * LANE] * stab_ref[...]
    ba_ref[...] = acc[:, 2 * LANE:3 * LANE]
    bat_ref[...] = _nt(wbat_ref[...], xn_scr[...])


def _in_proj(x, g, w_p, wbat, gcq, gckv, ctab, stab, n_conv, n_z):
    t, d = x.shape
    tm = _tile(min(t, ctab.shape[0]), IN_TM)
    ntab = ctab.shape[0] // tm
    assert w_p.shape[0] == Q_RANK + KV_RANK + n_conv + n_z + MISC_ROWS
    assert n_conv % COL_TILE == 0 and n_z % COL_TILE == 0
    row = lambda i: (i, 0)
    const = lambda i: (0, 0)
    body = functools.partial(_in_proj_body, n_conv=n_conv, n_z=n_z)
    return pl.pallas_call(
        body,
        grid=(t // tm,),
        in_specs=[
            pl.BlockSpec((tm, d), row),
            pl.BlockSpec((1, d), const),
            pl.BlockSpec(w_p.shape, const, pipeline_mode=pl.Buffered(1)),
            pl.BlockSpec((16, d), const),
            pl.BlockSpec((1, Q_RANK), const),
            pl.BlockSpec((1, KV_RANK), const),
            pl.BlockSpec((tm, LANE), lambda i: (i % ntab, 0)),
            pl.BlockSpec((tm, LANE), lambda i: (i % ntab, 0)),
        ],
        out_specs=[
            pl.BlockSpec((tm, Q_RANK), row),
            pl.BlockSpec((tm, KV_RANK), row),
            pl.BlockSpec((tm, LANE), row),
            pl.BlockSpec((tm, n_conv), row),
            pl.BlockSpec((tm, n_z), row),
            pl.BlockSpec((tm, LANE), row),
            pl.BlockSpec((16, tm), lambda i: (0, i)),
        ],
        out_shape=[
            jax.ShapeDtypeStruct((t, Q_RANK), BF),
            jax.ShapeDtypeStruct((t, KV_RANK), F32),
            jax.ShapeDtypeStruct((t, LANE), F32),
            jax.ShapeDtypeStruct((t, n_conv), F32),
            jax.ShapeDtypeStruct((t, n_z), BF),
            jax.ShapeDtypeStruct((t, LANE), F32),
            jax.ShapeDtypeStruct((16, t), F32),
        ],
        scratch_shapes=[pltpu.VMEM((tm, d), BF)],
        compiler_params=_params(("arbitrary",)),
        name="in_proj",
    )(x, g, w_p, wbat, gcq, gckv, ctab, stab)


def _q_proj_body(c_ref, w1_ref, w2_ref, ctab_ref, stab_ref, gn_ref, gr_ref, q_ref):
    c = c_ref[...]
    qf = _mm(c, w1_ref[...])
    qs = _mm(c, w2_ref[...])
    ct = ctab_ref[...]
    st = stab_ref[...]
    gn = gn_ref[...]
    gr = gr_ref[...]
    for h in range(MLA_HEADS):
        nope = qf[:, h * QPAD:h * QPAD + LANE]
        rot = qf[:, h * QPAD + LANE:(h + 1) * QPAD] * ct + qs[:, h * LANE:(h + 1) * LANE] * st
        ss = jnp.sum(nope * nope, axis=-1, keepdims=True) + jnp.sum(rot * rot, axis=-1, keepdims=True)
        rs = lax.rsqrt(ss * (1.0 / QK_HEAD) + EPS) * QSCALE
        q_ref[:, h * QPAD:h * QPAD + LANE] = (nope * rs * gn).astype(BF)
        q_ref[:, h * QPAD + LANE:(h + 1) * QPAD] = (rot * rs * gr).astype(BF)


def _q_proj(cqn, w1, w2, ctab, stab, gn, gr):
    t = cqn.shape[0]
    tm = _tile(min(t, ctab.shape[0]), PROJ_TM)
    ntab = ctab.shape[0] // tm
    const = lambda i: (0, 0)
    return pl.pallas_call(
        _q_proj_body,
        grid=(t // tm,),
        in_specs=[
            pl.BlockSpec((tm, Q_RANK), lambda i: (i, 0)),
            pl.BlockSpec(w1.shape, const),
            pl.BlockSpec(w2.shape, const),
            pl.BlockSpec((tm, LANE), lambda i: (i % ntab, 0)),
            pl.BlockSpec((tm, LANE), lambda i: (i % ntab, 0)),
            pl.BlockSpec((1, LANE), const),
            pl.BlockSpec((1, LANE), const),
        ],
        out_specs=pl.BlockSpec((tm, MLA_HEADS * QPAD), lambda i: (i, 0)),
        out_shape=jax.ShapeDtypeStruct((t, MLA_HEADS * QPAD), BF),
        compiler_params=_params(("arbitrary",)),
        name="q_proj",
    )(cqn, w1, w2, ctab, stab, gn, gr)


def _kv_proj_body(c_ref, kpe_ref, w_ref, gn_ref, gr_ref, k_ref, v_ref):
    kv = _mm(c_ref[...].astype(BF), w_ref[...])
    kp = kpe_ref[...]
    kps = jnp.sum(kp * kp, axis=-1, keepdims=True)
    gn = gn_ref[...]
    gr = gr_ref[...]
    for h in range(MLA_HEADS):
        kn = kv[:, h * 2 * LANE:h * 2 * LANE + LANE]
        rs = lax.rsqrt((jnp.sum(kn * kn, axis=-1, keepdims=True) + kps) * (1.0 / QK_HEAD) + EPS)
        k_ref[:, h * QPAD:h * QPAD + LANE] = (kn * rs * gn).astype(BF)
        k_ref[:, h * QPAD + LANE:(h + 1) * QPAD] = (kp * rs * gr).astype(BF)
        v_ref[:, h * V_DIM:(h + 1) * V_DIM] = kv[:, h * 2 * LANE + LANE:(h + 1) * 2 * LANE].astype(BF)


def _kv_proj(ckv, kpe_pad, w, gn, gr):
    t = ckv.shape[0]
    tm = _tile(t, PROJ_TM)
    const = lambda i: (0, 0)
    return pl.pallas_call(
        _kv_proj_body,
        grid=(t // tm,),
        in_specs=[
            pl.BlockSpec((tm, KV_RANK), lambda i: (i, 0)),
            pl.BlockSpec((tm, LANE), lambda i: (i, 0)),
            pl.BlockSpec(w.shape, const),
            pl.BlockSpec((1, LANE), const),
            pl.BlockSpec((1, LANE), const),
        ],
        out_specs=[
            pl.BlockSpec((tm, MLA_HEADS * QPAD), lambda i: (i, 0)),
            pl.BlockSpec((tm, MLA_HEADS * V_DIM), lambda i: (i, 0)),
        ],
        out_shape=[
            jax.ShapeDtypeStruct((t, MLA_HEADS * QPAD), BF),
            jax.ShapeDtypeStruct((t, MLA_HEADS * V_DIM), BF),
        ],
        compiler_params=_params(("arbitrary",)),
        name="kv_proj",
    )(ckv, kpe_pad, w, gn, gr)


def _attn_body(q_ref, k_ref, v_ref, g_ref, o_ref, m_scr, l_scr, acc_scr, *, tq, tk, wide, q_off, t_valid):
    qi = pl.program_id(2)
    qpos0 = q_off + qi * tq
    n_full = jnp.minimum((qpos0 // CHUNK * CHUNK + CHUNK) // tk, t_valid // tk)
    hi = jnp.minimum((qpos0 + tq - 1) // CHUNK * CHUNK + CHUNK, t_valid)
    n_total = (hi + tk - 1) // tk

    m_scr[...] = jnp.full(m_scr.shape, NEG_BIG, F32)
    l_scr[...] = jnp.zeros(l_scr.shape, F32)
    acc_scr[...] = jnp.zeros(acc_scr.shape, F32)
    diag = q_off % tk == 0 and tq == tk and t_valid % tk == 0 and tk % (2 * LANE) == 0 and tq % 32 == 0

    def step(kc, width, masked, r0=0, nr=tq):
        rows = slice(r0, r0 + nr)
        k0 = pl.multiple_of(kc * tk, tk)
        s = _nt(q_ref[rows, :], k_ref[pl.ds(k0, width), :])
        if masked:
            qpos = qpos0 + r0 + lax.broadcasted_iota(jnp.int32, (nr, 1), 0)
            last = jnp.minimum(qpos | (CHUNK - 1), t_valid - 1) - k0
            s = jnp.where(lax.broadcasted_iota(jnp.int32, (nr, width), 1) <= last, s, NEG_BIG)
        m_prev = m_scr[rows, :]
        m_new = jnp.maximum(m_prev, jnp.max(s, axis=-1, keepdims=True))
        alpha = jnp.exp2(m_prev - m_new)
        ps = [jnp.exp2(s[:, j * LANE:(j + 1) * LANE] - m_new) for j in range(width // LANE)]
        psum = ps[0]
        for pj in ps[1:]:
            psum = psum + pj
        l_scr[rows, :] = alpha * l_scr[rows, :] + psum
        p = jnp.concatenate(ps, axis=1).astype(BF) if len(ps) > 1 else ps[0].astype(BF)
        acc_scr[rows, :] = acc_scr[rows, :] * alpha + _mm(p, v_ref[pl.ds(k0, width), :])
        m_scr[rows, :] = m_new

    def loop(lo, hi, fn):
        lax.fori_loop(lo, hi, lambda i, c: (fn(i), c)[1], 0)

    done = 0
    for w in range(wide, 0, -1):
        cnt = (n_full - done) // w
        loop(0, cnt, lambda i, w=w, done=done: step(done + i * w, w * tk, False))
        done = done + cnt * w
    if diag:
        def masked(kc):
            step(kc, tk // 2, True, 0, tq // 2)
            step(kc, tk, True, tq // 2, tq // 2)
    else:
        def masked(kc):
            step(kc, tk, True)
    loop(n_full, n_total, masked)
    o = acc_scr[...] / jnp.sum(l_scr[...], axis=-1, keepdims=True)
    o_ref[...] = _rms(o, g_ref[...]).astype(BF)


def _attention(q, k, v, g_out, nb, q_off, t_valid):
    tq_total = q.shape[0] // nb
    tk_total = k.shape[0] // nb
    tq = _tile(tq_total, ATTN_TQ)
    tk = _tile(tk_total, ATTN_TK, mult=LANE)
    nq = tq_total // tq
    body = functools.partial(_attn_body, tq=tq, tk=tk, wide=ATTN_WIDE, q_off=q_off, t_valid=t_valid)
    return pl.pallas_call(
        body,
        grid=(nb, MLA_HEADS, nq),
        in_specs=[
            pl.BlockSpec((tq, QPAD), lambda b, h, i: (b * nq + i, h)),
            pl.BlockSpec((tk_total, QPAD), lambda b, h, i: (b, h)),
            pl.BlockSpec((tk_total, V_DIM), lambda b, h, i: (b, h)),
            pl.BlockSpec((1, V_DIM), lambda b, h, i: (0, 0)),
        ],
        out_specs=pl.BlockSpec((tq, V_DIM), lambda b, h, i: (b * nq + i, h)),
        out_shape=jax.ShapeDtypeStruct((q.shape[0], MLA_HEADS * V_DIM), BF),
        scratch_shapes=[
            pltpu.VMEM((tq, LANE), F32),
            pltpu.VMEM((tq, LANE), F32),
            pltpu.VMEM((tq, V_DIM), F32),
        ],
        compiler_params=_params(("arbitrary", "arbitrary", "arbitrary")),
        name="mla_attention",
    )(q, k, v, g_out)


def _cached_attn_body(q_ref, cp_ref, kp_ref, cn_ref, kn_ref, w_ref, gn_ref, gr_ref, g_ref, o_ref, *, p_len, s_len):
    cp = cp_ref[0].astype(BF)
    cn = cn_ref[...].astype(BF)
    kpp = kp_ref[0]
    kpn = kn_ref[:, :QK_ROPE]
    kps_p = jnp.sum(kpp * kpp, axis=-1, keepdims=True)
    kps_n = jnp.sum(kpn * kpn, axis=-1, keepdims=True)
    gn = gn_ref[...]
    gr = gr_ref[:, :QK_ROPE]
    g_out = g_ref[...]
    qpos = p_len + lax.broadcasted_iota(jnp.int32, (s_len, 1), 0)
    kpos = p_len + lax.broadcasted_iota(jnp.int32, (1, s_len), 1)
    ok_new = kpos <= (qpos | (CHUNK - 1))

    def keys_values(kv, kp, kps):
        kn = kv[:, :QK_NOPE]
        rs = lax.rsqrt((jnp.sum(kn * kn, axis=-1, keepdims=True) + kps) * (1.0 / QK_HEAD) + EPS)
        return (kn * rs * gn).astype(BF), (kp * rs * gr).astype(BF), kv[:, QK_NOPE:].astype(BF)

    hw = 2 * LANE
    for h in range(MLA_HEADS):
        if h % 2 == 0:
            kv2p = _mm(cp, w_ref[:, h * hw:(h + 2) * hw])
            kv2n = _mm(cn, w_ref[:, h * hw:(h + 2) * hw])
        kb, krb, vb = keys_values(kv2p[:, (h % 2) * hw:(h % 2 + 1) * hw], kpp, kps_p)
        kbn, krbn, vbn = keys_values(kv2n[:, (h % 2) * hw:(h % 2 + 1) * hw], kpn, kps_n)
        qn = q_ref[:, h * QPAD:h * QPAD + QK_NOPE]
        qr = q_ref[:, h * QPAD + QK_NOPE:h * QPAD + QK_HEAD]
        s_p = _nt(qn, kb) + _nt(qr, krb)
        s_n = jnp.where(ok_new, _nt(qn, kbn) + _nt(qr, krbn), NEG_BIG)
        m = jnp.maximum(jnp.max(s_p, axis=-1, keepdims=True), jnp.max(s_n, axis=-1, keepdims=True))
        pp = jnp.exp2(s_p - m)
        pn = jnp.exp2(s_n - m)
        l = jnp.sum(pp, axis=-1, keepdims=True) + jnp.sum(pn, axis=-1, keepdims=True)
        o = (_mm(pp.astype(BF), vb) + _mm(pn.astype(BF), vbn)) / l
        o_ref[:, h * V_DIM:(h + 1) * V_DIM] = _rms(o, g_out).astype(BF)


def _cached_attention(q, ckv_past, kpe_past, ckv_new, kpe_new, w, gn, gr, g_out, nb):
    s_len = q.shape[0] // nb
    p_len = ckv_past.shape[1]
    assert s_len % 16 == 0 or nb == 1
    body = functools.partial(_cached_attn_body, p_len=p_len, s_len=s_len)
    const = lambda b: (0, 0)
    return pl.pallas_call(
        body,
        grid=(nb,),
        in_specs=[
            pl.BlockSpec((s_len, MLA_HEADS * QPAD), lambda b: (b, 0)),
            pl.BlockSpec((1, p_len, KV_RANK), lambda b: (b, 0, 0)),
            pl.BlockSpec((1, p_len, QK_ROPE), lambda b: (b, 0, 0)),
            pl.BlockSpec((s_len, KV_RANK), lambda b: (b, 0)),
            pl.BlockSpec((s_len, LANE), lambda b: (b, 0)),
            pl.BlockSpec(w.shape, const),
            pl.BlockSpec((1, LANE), const),
            pl.BlockSpec((1, LANE), const),
            pl.BlockSpec((1, V_DIM), const),
        ],
        out_specs=pl.BlockSpec((s_len, MLA_HEADS * V_DIM), lambda b: (b, 0)),
        out_shape=jax.ShapeDtypeStruct((q.shape[0], MLA_HEADS * V_DIM), BF),
        compiler_params=_params(("arbitrary",)),
        name="cached_attention",
    )(q, ckv_past, kpe_past, ckv_new, kpe_new, w, gn, gr, g_out)


def _gdn_body(q_ref, k_ref, v_ref, wq_ref, wk_ref, wv_ref, pq_ref, pk_ref, pv_ref,
              ba_ref, bat_ref, alog_l_ref, dtb_l_ref, alog_c_ref, dtb_c_ref,
              z_ref, s0_ref, gout_ref,
              o_ref, sout_ref,
              s_scr, halo_scr, vnew_scr, *, tm, L, hg):
    t = pl.program_id(2)
    nt = pl.num_programs(2)
    n_chunks = tm // L
    n_factors = int(math.log2(L)) - 1

    @pl.when(t == 0)
    def _():
        s_scr[...] = s0_ref[0]
        halo_scr[0] = pq_ref[...]
        halo_scr[1] = pk_ref[...]
        halo_scr[2] = pv_ref[...]
        vnew_scr[...] = jnp.zeros(vnew_scr.shape, BF)

    ba = ba_ref[...]
    beta_all = _sigmoid(ba)
    g_all = -jnp.exp(alog_l_ref[...]) * _softplus(ba + dtb_l_ref[...])
    gt_all = -jnp.exp(alog_c_ref[...]) * _softplus(bat_ref[0] + dtb_c_ref[...])

    r = lax.broadcasted_iota(jnp.int32, (tm, tm), 0)
    c = lax.broadcasted_iota(jnp.int32, (tm, tm), 1)
    same = (r // L) == (c // L)
    lower = same & (c <= r)
    strict = same & (c < r)
    cs = jnp.where(lower, 1.0, 0.0).astype(BF)
    bd = jnp.where(same, 1.0, 0.0).astype(BF)
    g3 = _split3(g_all)
    gcol_all = _mm(cs, g3[0]) + _mm(cs, g3[1]) + _mm(cs, g3[2])
    glast_all = _mm(bd, g3[0]) + _mm(bd, g3[1]) + _mm(bd, g3[2])
    gt3 = _split3(gt_all)
    grow_all = _nt(gt3[0], cs) + _nt(gt3[1], cs) + _nt(gt3[2], cs)

    gout = gout_ref[...]
    eye_t = jnp.where(r == c, 1.0, 0.0)
    eye_k = jnp.where(lax.broadcasted_iota(jnp.int32, (GDN_DK, GDN_DK), 0)
                      == lax.broadcasted_iota(jnp.int32, (GDN_DK, GDN_DK), 1), 1.0, 0.0).astype(BF)
    col_chunk = lax.broadcasted_iota(jnp.int32, (1, tm), 1) // L

    def make_wave(heads):
        c0, c1 = heads[0] * GDN_DK, (heads[-1] + 1) * GDN_DK
        loc = lambda h: slice((h - heads[0]) * GDN_DK, (h - heads[0] + 1) * GDN_DK)
        w = {}

        def conv(idx, x_ref, w_ref):
            xs = jnp.concatenate([halo_scr[idx, :, c0:c1], x_ref[:, c0:c1]], axis=0)
            cw = w_ref[:, c0:c1]
            acc = xs * cw[0:1]
            for i in range(1, CONV_W):
                acc = pltpu.roll(acc, 1, axis=0) + xs * cw[i:i + 1]
            halo_scr[idx, :, c0:c1] = xs[tm:tm + HALO]
            y = acc[HALO:HALO + tm]
            return y * _sigmoid(y)

        def p_q():
            w["yq"] = conv(0, q_ref, wq_ref)

        def p_k():
            w["yk"] = conv(1, k_ref, wk_ref)

        def p_v():
            w["yv"] = conv(2, v_ref, wv_ref)

        def p_gate():
            w["beta"], w["gc"], w["glast"], w["gr"] = {}, {}, {}, {}
            for h in heads:
                g = GDN_HEADS + h
                w["beta"][h] = beta_all[:, h:h + 1]
                w["gc"][h] = gcol_all[:, g:g + 1]
                w["glast"][h] = glast_all[:, g:g + 1]
                w["gr"][h] = grow_all[g:g + 1, :]

        def p_norm():
            w["qn"], w["kn"] = {}, {}
            for h in heads:
                qh = w["yq"][:, loc(h)]
                kh = w["yk"][:, loc(h)]
                w["qn"][h] = qh * lax.rsqrt(jnp.sum(qh * qh, axis=-1, keepdims=True) + EPS) * (GDN_DK ** -0.5)
                w["kn"][h] = kh * lax.rsqrt(jnp.sum(kh * kh, axis=-1, keepdims=True) + EPS)
            kb = {h: w["kn"][h].astype(BF) for h in heads}
            w["kk"] = {h: _nt(kb[h], kb[h]) for h in heads}
            w["qk"] = {h: _nt(w["qn"][h].astype(BF), kb[h]) for h in heads}

        def p_mat():
            beta, gc = w["beta"], w["gc"]
            decay = {h: jnp.where(lower, jnp.exp(gc[h] - w["gr"][h]), 0.0) for h in heads}
            w["nm"] = {h: jnp.where(strict, beta[h] * w["kk"][h] * decay[h], 0.0) for h in heads}
            w["attn"] = {h: (w["qk"][h] * decay[h]).astype(BF) for h in heads}
            w["eg"] = {h: jnp.exp(gc[h]) for h in heads}
            w["rhs"] = {h: jnp.concatenate([w["yv"][:, loc(h)] * beta[h],
                                            w["kn"][h] * (beta[h] * w["eg"][h])], axis=1) for h in heads}
            w["nb"] = {h: w["nm"][h].astype(BF) for h in heads}

        def s_first():
            w["m"] = {h: _mm(w["nb"][h], w["nb"][h]) for h in heads}
            w["T"] = {h: eye_t - w["nm"][h] for h in heads}

        def s_level(s):
            xb = {h: jnp.concatenate([w["m"][h], w["T"][h]], axis=0).astype(BF) for h in heads}
            if s < n_factors - 1:
                prod = {h: _mm(xb[h], xb[h][:tm]) for h in heads}
                w["T"] = {h: w["T"][h] + prod[h][tm:] for h in heads}
                w["m"] = {h: prod[h][:tm] for h in heads}
            else:
                w["T"] = {h: w["T"][h] + _mm(xb[h][tm:], xb[h][:tm]) for h in heads}

        def s_y0():
            w["tb"] = {h: w["T"][h].astype(BF) for h in heads}
            w["y0"] = {h: _mm(w["tb"][h], w["rhs"][h].astype(BF)) for h in heads}

        def s_res():
            res = {}
            for h in heads:
                nbf = w["nb"][h].astype(F32)
                nx = jnp.concatenate([nbf, w["nm"][h] - nbf], axis=0).astype(BF)
                yh = w["y0"][h].astype(BF)
                yl = (w["y0"][h] - yh.astype(F32)).astype(BF)
                p1 = _mm(nx, yh)
                res[h] = w["rhs"][h] - w["y0"][h] - (p1[:tm] + p1[tm:] + _mm(nx[:tm], yl))
            w["res"] = res

        def s_fix():
            w["y"] = {h: w["y0"][h] + _mm(w["tb"][h], w["res"][h].astype(BF)) for h in heads}

        def c_prep():
            y = w["y"]
            w["u"] = {h: y[h][:, :GDN_DV] for h in heads}
            w["wq"] = {}
            for h in heads:
                qd = w["qn"][h] * w["eg"][h]
                parts = []
                for ci in range(n_chunks):
                    parts += [y[h][ci * L:(ci + 1) * L, GDN_DV:], qd[ci * L:(ci + 1) * L]]
                w["wq"][h] = jnp.concatenate(parts, axis=0).astype(BF)
            kdb = {h: (w["kn"][h] * jnp.exp(w["glast"][h] - w["gc"][h])).astype(BF) for h in heads}
            w["gtot"] = {h: jnp.exp(w["glast"][h]) for h in heads}
            kdt = {h: _nt(eye_k, kdb[h]) for h in heads}
            w["kdt"] = {h: [jnp.where(col_chunk == ci, kdt[h], 0.0).astype(BF) for ci in range(n_chunks)]
                        for h in heads}
            w["st"] = {h: s_scr[h] for h in heads}
            w["outs"] = {h: [] for h in heads}

        def c_chunk(ci):
            r0 = ci * L
            st = w["st"]
            ws = {h: _mm(w["wq"][h][2 * r0:2 * r0 + 2 * L], st[h].astype(BF)) for h in heads}
            for h in heads:
                vnew_scr[h, r0:r0 + L, :] = (w["u"][h][r0:r0 + L] - ws[h][:L]).astype(BF)
            for h in heads:
                w["outs"][h].append(ws[h][L:] + _mm(w["attn"][h][r0:r0 + L, :], vnew_scr[h]))
            w["st"] = {h: st[h] * w["gtot"][h][r0:r0 + 1, :] + _mm(w["kdt"][h][ci], vnew_scr[h]) for h in heads}

        def c_out():
            for h in heads:
                s_scr[h] = w["st"][h]
                o = w["outs"][h][0] if n_chunks == 1 else jnp.concatenate(w["outs"][h], axis=0)
                zz = z_ref[:, h * GDN_DV:(h + 1) * GDN_DV].astype(F32)
                o_ref[:, h * GDN_DV:(h + 1) * GDN_DV] = (_rms(o, gout) * (zz * _sigmoid(zz))).astype(BF)

        prep = [p_q, p_k, p_v, p_gate, p_norm, p_mat]
        solve = ([s_first] + [functools.partial(s_level, s) for s in range(n_factors)] + [s_y0, s_res, s_fix])
        scan = [c_prep] + [functools.partial(c_chunk, ci) for ci in range(n_chunks)] + [c_out]
        return [prep, solve, scan]

    nw = min(GDN_WAVES, hg)
    per = hg // nw
    waves = [make_wave(list(range(i * per, (i + 1) * per))) for i in range(nw)]
    for slot in range(nw + 2):
        active = [waves[i][slot - i] for i in range(nw) if 0 <= slot - i < 3]
        for stage in _interleave(active):
            stage()

    @pl.when(t == nt - 1)
    def _():
        sout_ref[0] = s_scr[...]


def _gdn(qkv, conv_w, conv_past8, ba, bat, alog_l, dtb_l, alog_c, dtb_c, z, s0, gout, nb):
    t = qkv.shape[0]
    s_len = t // nb
    L = min(s_len, CHUNK)
    tm = _tile(s_len, GDN_TM, mult=L)
    nt = s_len // tm
    hg = GDN_HEADS
    hw = hg * GDN_DK
    body = functools.partial(_gdn_body, tm=tm, L=L, hg=hg)
    rows = lambda off: (lambda b, h, i: (b * nt + i, off + h))
    wcol = lambda off: (lambda b, h, i: (0, off + h))
    pcol = lambda off: (lambda b, h, i: (b, off + h))
    const = lambda b, h, i: (0, 0)
    ng = GDN_HEADS // hg
    return pl.pallas_call(
        body,
        grid=(nb, ng, nt),
        in_specs=[
            pl.BlockSpec((tm, hw), rows(0)),
            pl.BlockSpec((tm, hw), rows(ng)),
            pl.BlockSpec((tm, hw), rows(2 * ng)),
            pl.BlockSpec((CONV_W, hw), wcol(0)),
            pl.BlockSpec((CONV_W, hw), wcol(ng)),
            pl.BlockSpec((CONV_W, hw), wcol(2 * ng)),
            pl.BlockSpec((HALO, hw), pcol(0)),
            pl.BlockSpec((HALO, hw), pcol(ng)),
            pl.BlockSpec((HALO, hw), pcol(2 * ng)),
            pl.BlockSpec((tm, LANE), lambda b, h, i: (b * nt + i, 0)),
            pl.BlockSpec((1, 16, tm), lambda b, h, i: (b, 0, i)),
            pl.BlockSpec((1, LANE), const),
            pl.BlockSpec((1, LANE), const),
            pl.BlockSpec((16, 1), const),
            pl.BlockSpec((16, 1), const),
            pl.BlockSpec((tm, hw), rows(0)),
            pl.BlockSpec((1, hg, GDN_DK, GDN_DV), lambda b, h, i: (b, h, 0, 0)),
            pl.BlockSpec((1, GDN_DV), const),
        ],
        out_specs=[
            pl.BlockSpec((tm, hw), rows(0)),
            pl.BlockSpec((1, hg, GDN_DK, GDN_DV), lambda b, h, i: (b, h, 0, 0)),
        ],
        out_shape=[
            jax.ShapeDtypeStruct((t, GDN_HEADS * GDN_DV), BF),
            jax.ShapeDtypeStruct((nb, GDN_HEADS, GDN_DK, GDN_DV), F32),
        ],
        scratch_shapes=[
            pltpu.VMEM((hg, GDN_DK, GDN_DV), F32),
            pltpu.VMEM((3, HALO, hw), F32),
            pltpu.VMEM((hg, tm, GDN_DV), BF),
        ],
        compiler_params=_params(("arbitrary", "arbitrary", "arbitrary")),
        name="gdn",
    )(qkv, qkv, qkv, conv_w, conv_w, conv_w, conv_past8, conv_past8, conv_past8,
      ba, bat, alog_l, dtb_l, alog_c, dtb_c, z, s0, gout)


def _mix_mem_body(x_ref, a1_ref, a2_ref, wo1_ref, wo2_ref, gq_ref, wmq_ref, gmq_ref,
                  mk_ref, mv_ref, wmo_ref, o_ref):
    h1 = x_ref[...] + _mm(a1_ref[...], wo1_ref[...]) + _mm(a2_ref[...], wo2_ref[...])
    hn = _rms(h1, gq_ref[...]).astype(BF)
    q = _mm(hn, wmq_ref[...])
    gmq = gmq_ref[...]
    mk = mk_ref[0]
    mv = mv_ref[0]
    outs = []
    for h in range(MEM_HEADS):
        sl = slice(h * MEM_DIM, (h + 1) * MEM_DIM)
        qh = _rms(q[:, sl], gmq).astype(BF)
        s = _nt(qh, mk[:, sl]) * (MEM_DIM ** -0.5)
        p = jnp.exp(s - jnp.max(s, axis=-1, keepdims=True))
        p = p / jnp.sum(p, axis=-1, keepdims=True)
        outs.append(_mm(p.astype(BF), mv[:, sl]))
    o = jnp.concatenate(outs, axis=1).astype(BF)
    o_ref[...] = h1 + _mm(o, wmo_ref[...])


def _mix_mem(x, a1, a2, wo1, wo2, gq, wmq, gmq, mk, mv, wmo, nb):
    t, d = x.shape
    s_len = t // nb
    tm = _tile(s_len, MIX_TM)
    nt = s_len // tm
    n_mem = mk.shape[1]
    mw = MEM_HEADS * MEM_DIM
    row = lambda i: (i, 0)
    const = lambda i: (0, 0)
    return pl.pallas_call(
        _mix_mem_body,
        grid=(t // tm,),
        in_specs=[
            pl.BlockSpec((tm, d), row),
            pl.BlockSpec((tm, a1.shape[1]), row),
            pl.BlockSpec((tm, a2.shape[1]), row),
            pl.BlockSpec(wo1.shape, const),
            pl.BlockSpec(wo2.shape, const),
            pl.BlockSpec((1, d), const),
            pl.BlockSpec(wmq.shape, const),
            pl.BlockSpec((1, MEM_DIM), const),
            pl.BlockSpec((1, n_mem, mw), lambda i: (i // nt, 0, 0)),
            pl.BlockSpec((1, n_mem, mw), lambda i: (i // nt, 0, 0)),
            pl.BlockSpec(wmo.shape, const),
        ],
        out_specs=pl.BlockSpec((tm, d), row),
        out_shape=jax.ShapeDtypeStruct((t, d), F32),
        compiler_params=_params(("arbitrary",)),
        name="mix_mem",
    )(x, a1, a2, wo1, wo2, gq, wmq, gmq, mk, mv, wmo)


def _ffn_body(x_ref, g_ref, w1_ref, w2_ref, o_ref, hn_scr):
    f = pl.program_id(1)

    @pl.when(f == 0)
    def _():
        x = x_ref[...]
        hn_scr[...] = _rms(x, g_ref[...]).astype(BF)
        o_ref[...] = x

    a = jnp.maximum(_mm(hn_scr[...], w1_ref[...]), 0.0)
    o_ref[...] += _mm((a * a).astype(BF), w2_ref[...])


def _ffn(x, g, w1, w2):
    t, d = x.shape
    dff = w1.shape[1]
    tm = _tile(t, FFN_TM)
    tf = _tile(dff, FFN_TF, mult=LANE)
    return pl.pallas_call(
        _ffn_body,
        grid=(t // tm, dff // tf),
        in_specs=[
            pl.BlockSpec((tm, d), lambda i, f: (i, 0)),
            pl.BlockSpec((1, d), lambda i, f: (0, 0)),
            pl.BlockSpec((d, tf), lambda i, f: (0, f)),
            pl.BlockSpec((tf, d), lambda i, f: (f, 0)),
        ],
        out_specs=pl.BlockSpec((tm, d), lambda i, f: (i, 0)),
        out_shape=jax.ShapeDtypeStruct((t, d), F32),
        scratch_shapes=[pltpu.VMEM((tm, d), BF)],
        compiler_params=_params(("arbitrary", "arbitrary")),
        name="ffn",
    )(x, g, w1, w2)


def _mem_kv_body(m_ref, g_ref, wk_ref, wv_ref, gk_ref, k_ref, v_ref):
    mn = _rms(m_ref[...], g_ref[...]).astype(BF)
    k = _mm(mn, wk_ref[...])
    gk = gk_ref[...]
    for h in range(MEM_HEADS):
        sl = slice(h * MEM_DIM, (h + 1) * MEM_DIM)
        k_ref[:, sl] = _rms(k[:, sl], gk)
    v_ref[...] = _mm(mn, wv_ref[...])


def _mem_kv(mem, g, wk, wv, gk):
    t, d = mem.shape
    tm = _tile(t, 256)
    mw = MEM_HEADS * MEM_DIM
    const = lambda i: (0, 0)
    return pl.pallas_call(
        _mem_kv_body,
        grid=(t // tm,),
        in_specs=[
            pl.BlockSpec((tm, d), lambda i: (i, 0)),
            pl.BlockSpec((1, d), const),
            pl.BlockSpec(wk.shape, const),
            pl.BlockSpec(wv.shape, const),
            pl.BlockSpec((1, MEM_DIM), const),
        ],
        out_specs=[pl.BlockSpec((tm, mw), lambda i: (i, 0)), pl.BlockSpec((tm, mw), lambda i: (i, 0))],
        out_shape=[jax.ShapeDtypeStruct((t, mw), F32), jax.ShapeDtypeStruct((t, mw), F32)],
        compiler_params=_params(("arbitrary",)),
        name="mem_kv",
    )(mem, g, wk, wv, gk)


def _rope_tables(pos):
    half = QK_ROPE // 2
    inv_freq = ROPE_THETA ** (-np.arange(half, dtype=np.float64) / half)
    ang = np.asarray(pos, np.float64)[:, None] * inv_freq[None, :]
    cos, sin = np.cos(ang), np.sin(ang)
    zeros = np.zeros((ang.shape[0], LANE - QK_ROPE))
    return (jnp.asarray(np.concatenate([cos, cos, zeros], axis=1), F32),
            jnp.asarray(np.concatenate([-sin, sin, zeros], axis=1), F32))


def _pad_lanes(v, n=LANE):
    return jnp.pad(v, ((0, 0), (0, n - v.shape[1])))


def _prep_weights(w_in, g_cq, w_uq, g_ckv, w_ukv, g_q, g_k, w_o, w_mq, w_mo, w_ff1, w_ff2, w_mk, w_mv):
    d = w_in.shape[0]
    n_conv = 2 * GDN_HEADS * GDN_DK + GDN_HEADS * GDN_DV
    n_z = GDN_HEADS * GDN_DV
    wt = w_in.T.astype(BF)
    o = 0
    w_cq = wt[o:o + Q_RANK]; o += Q_RANK
    w_ckv = wt[o:o + KV_RANK]; o += KV_RANK
    w_kpe = wt[o:o + QK_ROPE]; o += QK_ROPE
    w_qkv = wt[o:o + n_conv]; o += n_conv
    w_z = wt[o:o + n_z]; o += n_z
    w_b = wt[o:o + GDN_HEADS]; o += GDN_HEADS
    w_a = wt[o:o + GDN_HEADS]
    half = QK_ROPE // 2
    swap = jnp.concatenate([jnp.arange(half, QK_ROPE), jnp.arange(0, half)])
    z64 = jnp.zeros((LANE - QK_ROPE, d), BF)
    misc = jnp.concatenate([w_kpe, z64, w_kpe[swap], z64, w_b, w_a], axis=0)
    misc = jnp.pad(misc, ((0, MISC_ROWS - misc.shape[0]), (0, 0)))
    w_in_p = jnp.concatenate([w_cq, w_ckv, w_qkv, w_z, misc], axis=0)
    wbat = jnp.concatenate([w_b, w_a], axis=0)

    r = w_uq.shape[0]
    wq3 = w_uq.reshape(r, MLA_HEADS, QK_HEAD)
    zq = jnp.zeros((r, MLA_HEADS, LANE - QK_ROPE), w_uq.dtype)
    w1 = jnp.concatenate([wq3, zq], axis=2).reshape(r, MLA_HEADS * QPAD).astype(BF)
    w2 = jnp.concatenate([wq3[:, :, QK_NOPE:][:, :, swap], zq], axis=2).reshape(r, MLA_HEADS * LANE).astype(BF)

    dmla = MLA_HEADS * V_DIM
    return dict(
        n_conv=n_conv, n_z=n_z, w_in_p=w_in_p, wbat=wbat, w1=w1, w2=w2,
        gcq=g_cq[None, :], gckv=g_ckv[None, :],
        gq_n=g_q[None, :QK_NOPE], gq_r=_pad_lanes(g_q[None, QK_NOPE:]),
        gk_n=g_k[None, :QK_NOPE], gk_r=_pad_lanes(g_k[None, QK_NOPE:]),
        w_ukv=w_ukv.astype(BF), wo1=w_o[:dmla].astype(BF), wo2=w_o[dmla:].astype(BF),
        w_mq=w_mq.astype(BF), w_mo=w_mo.astype(BF), w_ff1=w_ff1.astype(BF), w_ff2=w_ff2.astype(BF),
        w_mk=w_mk.astype(BF), w_mv=w_mv.astype(BF),
    )


def _gate_params(a_log, dt_bias):
    z8 = jnp.zeros((GDN_HEADS,), F32)
    al = jnp.concatenate([z8, a_log.astype(F32)])
    db = jnp.concatenate([z8, dt_bias.astype(F32)])
    return _pad_lanes(al[None, :]), _pad_lanes(db[None, :]), al[:, None], db[:, None]


def _layer(x, pos, past, conv_past, s0, mem_k, mem_v, wp, lw):
    nb, s_len, d = x.shape
    t = nb * s_len
    xf = x.reshape(t, d)
    ctab, stab = _rope_tables(pos)
    if ctab.shape[0] % 8 != 0 or (s_len < IN_TM and nb > 1):
        ctab, stab = jnp.tile(ctab, (nb, 1)), jnp.tile(stab, (nb, 1))

    cqn, ckv, kpe_pad, qkv, z, ba, bat = _in_proj(
        xf, lw["g_norm_mix"], wp["w_in_p"], wp["wbat"], wp["gcq"], wp["gckv"], ctab, stab,
        wp["n_conv"], wp["n_z"])
    q = _q_proj(cqn, wp["w1"], wp["w2"], ctab, stab, wp["gq_n"], wp["gq_r"])

    if past is None:
        k, v = _kv_proj(ckv, kpe_pad, wp["w_ukv"], wp["gk_n"], wp["gk_r"])
        o_mla = _attention(q, k, v, lw["g_mla_out"], nb, 0, s_len)
    else:
        ckv_past, kpe_past = past
        o_mla = _cached_attention(q, ckv_past.astype(F32), kpe_past.astype(F32), ckv, kpe_pad, wp["w_ukv"],
                                  wp["gk_n"], wp["gk_r"], lw["g_mla_out"], nb)

    conv_past8 = jnp.pad(conv_past.astype(F32), ((0, 0), (HALO - (CONV_W - 1), 0), (0, 0)))
    conv_past8 = conv_past8.reshape(nb * HALO, -1)
    bat3 = bat.reshape(16, nb, s_len).transpose(1, 0, 2)
    alog_l, dtb_l, alog_c, dtb_c = _gate_params(lw["a_log"], lw["dt_bias"])
    o_gdn, s_new = _gdn(qkv, lw["conv_w"], conv_past8, ba, bat3, alog_l, dtb_l, alog_c, dtb_c,
                        z, s0.astype(F32), lw["g_gdn_out"], nb)

    h2 = _mix_mem(xf, o_mla, o_gdn, wp["wo1"], wp["wo2"], lw["g_norm_mem_q"], wp["w_mq"],
                  lw["g_mq"], mem_k, mem_v, wp["w_mo"], nb)
    y = _ffn(h2, lw["g_norm_ffn"], wp["w_ff1"], wp["w_ff2"])

    conv_in_tail = jnp.concatenate([conv_past.astype(F32), qkv.reshape(nb, s_len, -1)[:, -(CONV_W - 1):]], axis=1)
    conv_new = conv_in_tail[:, -(CONV_W - 1):]
    return (y.reshape(nb, s_len, d), ckv.reshape(nb, s_len, KV_RANK),
            kpe_pad[:, :QK_ROPE].reshape(nb, s_len, QK_ROPE), conv_new, s_new)


def kernel(x_prompt, x_sample, mem_prompt, cache_mla_ckv, cache_mla_kpe, cache_gdn_conv, state_gdn, cache_mem_k, cache_mem_v, g_norm_mix, w_in, g_cq, w_uq, g_ckv, w_ukv, g_q_mla, g_k_mla, g_mla_out, conv_w, a_log, dt_bias, g_gdn_out, w_o, g_norm_mem_q, g_norm_mem_kv, w_mq, w_mk, w_mv, g_mq, g_mk, w_mo, g_norm_ffn, w_ff1, w_ff2):
    depth = w_in.shape[0]
    nbp, sp, d = x_prompt.shape
    nbs, ss, _ = x_sample.shape
    n_mem = mem_prompt.shape[1]
    mw = MEM_HEADS * MEM_DIM
    n_conv = 2 * GDN_HEADS * GDN_DK + GDN_HEADS * GDN_DV
    pos_p = np.arange(sp)
    pos_s = cache_mla_ckv.shape[2] + np.arange(ss)
    zeros_conv = jnp.zeros((nbp, CONV_W - 1, n_conv), F32)
    zeros_state = jnp.zeros((nbp, GDN_HEADS, GDN_DK, GDN_DV), F32)
    hp, hs = x_prompt, x_sample
    outs_p = [[] for _ in range(6)]
    outs_s = [[] for _ in range(4)]
    for l in range(depth):
        wp = _prep_weights(w_in[l], g_cq[l], w_uq[l], g_ckv[l], w_ukv[l], g_q_mla[l], g_k_mla[l], w_o[l],
                           w_mq[l], w_mo[l], w_ff1[l], w_ff2[l], w_mk[l], w_mv[l])
        lw = dict(g_norm_mix=g_norm_mix[l][None, :], g_mla_out=g_mla_out[l][None, :], conv_w=conv_w[l],
                  a_log=a_log[l], dt_bias=dt_bias[l], g_gdn_out=g_gdn_out[l][None, :],
                  g_norm_mem_q=g_norm_mem_q[l][None, :], g_mq=g_mq[l][None, :],
                  g_norm_ffn=g_norm_ffn[l][None, :])
        mk, mv = _mem_kv(mem_prompt.reshape(nbp * n_mem, d), g_norm_mem_kv[l][None, :], wp["w_mk"], wp["w_mv"],
                         g_mk[l][None, :])
        mk3, mv3 = mk.reshape(nbp, n_mem, mw), mv.reshape(nbp, n_mem, mw)
        hp, c1, c2, c3, c4 = _layer(hp, pos_p, None, zeros_conv, zeros_state, mk3.astype(BF), mv3.astype(BF), wp, lw)
        for lst, val in zip(outs_p, (c1, c2, c3, c4, mk3.reshape(nbp, n_mem, MEM_HEADS, MEM_DIM),
                                     mv3.reshape(nbp, n_mem, MEM_HEADS, MEM_DIM))):
            lst.append(val)
        hs, d1, d2, d3, d4 = _layer(hs, pos_s, (cache_mla_ckv[l], cache_mla_kpe[l]), cache_gdn_conv[l], state_gdn[l],
                                    cache_mem_k[l].reshape(nbs, n_mem, mw).astype(BF),
                                    cache_mem_v[l].reshape(nbs, n_mem, mw).astype(BF), wp, lw)
        for lst, val in zip(outs_s, (d1, d2, d3, d4)):
            lst.append(val)
    return (hp, hs, *(jnp.stack(v) for v in outs_p), *(jnp.stack(v) for v in outs_s))
```

```python
import functools
import math

import jax
import jax.numpy as jnp
import numpy as np
from jax import lax
from jax.experimental import pallas as pl
from jax.experimental.pallas import tpu as pltpu

F32 = jnp.float32
BF = jnp.bfloat16

EPS = 1e-6
CHUNK = 64
ROPE_THETA = 10000.0
MLA_HEADS = 8
QK_NOPE = 128
QK_ROPE = 64
QK_HEAD = QK_NOPE + QK_ROPE
V_DIM = 128
Q_RANK = 512
KV_RANK = 512
GDN_HEADS = 8
GDN_DK = 128
GDN_DV = 128
CONV_W = 4
MEM_HEADS = 4
MEM_DIM = 128
LANE = 128
QPAD = 2 * LANE
COL_TILE = 512
MISC_ROWS = 3 * LANE
HALO = 8
NEG_BIG = -1e30
QSCALE = (QK_HEAD ** -0.5) * math.log2(math.e)

VMEM_LIMIT = 56 * 1024 * 1024

IN_TM = 512
PROJ_TM = 1024
ATTN_TQ = 1024
ATTN_TK = 1024
ATTN_WIDE = 3
GDN_TM = 256
GDN_WAVES = 2
MIX_TM = 512
FFN_TM = 1024
FFN_TF = 512


def _tile(n, pref, mult=8):
    if n <= pref:
        return n
    t = (pref // mult) * mult
    while t >= mult:
        if n % t == 0:
            return t
        t -= mult
    return n


def _nt(a, b):
    return lax.dot_general(a, b, (((1,), (1,)), ((), ())), preferred_element_type=F32)


def _mm(a, b):
    return jnp.dot(a, b, preferred_element_type=F32)


def _sigmoid(x):
    return 1.0 / (1.0 + jnp.exp(-x))


def _softplus(x):
    return jnp.maximum(x, 0.0) + jnp.log(1.0 + jnp.exp(-jnp.abs(x)))


def _rms(x, g, n=None):
    n = x.shape[-1] if n is None else n
    ms = jnp.sum(x * x, axis=-1, keepdims=True) * (1.0 / n)
    return (x * lax.rsqrt(ms + EPS)) * g


def _split3(x):
    hi = x.astype(BF)
    r1 = x - hi.astype(F32)
    mid = r1.astype(BF)
    lo = (r1 - mid.astype(F32)).astype(BF)
    return hi, mid, lo


def _interleave(lists):
    items = []
    for li, lst in enumerate(lists):
        items += [((i + 0.5) / len(lst), li, f) for i, f in enumerate(lst)]
    return [f for _, _, f in sorted(items, key=lambda it: (it[0], it[1]))]


def _params(sem):
    return pltpu.CompilerParams(dimension_semantics=sem, vmem_limit_bytes=VMEM_LIMIT)


def _in_proj_body(x_ref, g_ref, w_ref, wbat_ref, gcq_ref, gckv_ref, ctab_ref, stab_ref,
                  cqn_ref, ckv_ref, kpe_ref, qkv_ref, z_ref, ba_ref, bat_ref, xn_scr, *, n_conv, n_z):
    xn_scr[...] = _rms(x_ref[...], g_ref[...]).astype(BF)

    def prod(r0, n):
        return _nt(xn_scr[...], w_ref[r0:r0 + n, :])

    cqn_ref[...] = _rms(prod(0, Q_RANK), gcq_ref[...]).astype(BF)
    ckv_ref[...] = _rms(prod(Q_RANK, KV_RANK), gckv_ref[...])
    r0 = Q_RANK + KV_RANK
    for c0 in range(0, n_conv, COL_TILE):
        qkv_ref[:, c0:c0 + COL_TILE] = prod(r0 + c0, COL_TILE)
    r0 += n_conv
    for c0 in range(0, n_z, COL_TILE):
        z_ref[:, c0:c0 + COL_TILE] = prod(r0 + c0, COL_TILE).astype(BF)
    r0 += n_z
    acc = prod(r0, MISC_ROWS)
    kpe_ref[...] = acc[:, 0:LANE] * ctab_ref[...] + acc[:, LANE:2 * LANE] * stab_ref[...]
    ba_ref[...] = acc[:, 2 * LANE:3 * LANE]
    bat_ref[...] = _nt(wbat_ref[...], xn_scr[...])


def _in_proj(x, g, w_p, wbat, gcq, gckv, ctab, stab, n_conv, n_z):
    t, d = x.shape
    tm = _tile(min(t, ctab.shape[0]), IN_TM)
    ntab = ctab.shape[0] // tm
    assert w_p.shape[0] == Q_RANK + KV_RANK + n_conv + n_z + MISC_ROWS
    assert n_conv % COL_TILE == 0 and n_z % COL_TILE == 0
    row = lambda i: (i, 0)
    const = lambda i: (0, 0)
    body = functools.partial(_in_proj_body, n_conv=n_conv, n_z=n_z)
    return pl.pallas_call(
        body,
        grid=(t // tm,),
        in_specs=[
            pl.BlockSpec((tm, d), row),
            pl.BlockSpec((1, d), const),
            pl.BlockSpec(w_p.shape, const, pipeline_mode=pl.Buffered(1)),
            pl.BlockSpec((16, d), const),
            pl.BlockSpec((1, Q_RANK), const),
            pl.BlockSpec((1, KV_RANK), const),
            pl.BlockSpec((tm, LANE), lambda i: (i % ntab, 0)),
            pl.BlockSpec((tm, LANE), lambda i: (i % ntab, 0)),
        ],
        out_specs=[
            pl.BlockSpec((tm, Q_RANK), row),
            pl.BlockSpec((tm, KV_RANK), row),
            pl.BlockSpec((tm, LANE), row),
            pl.BlockSpec((tm, n_conv), row),
            pl.BlockSpec((tm, n_z), row),
            pl.BlockSpec((tm, LANE), row),
            pl.BlockSpec((16, tm), lambda i: (0, i)),
        ],
        out_shape=[
            jax.ShapeDtypeStruct((t, Q_RANK), BF),
            jax.ShapeDtypeStruct((t, KV_RANK), F32),
            jax.ShapeDtypeStruct((t, LANE), F32),
            jax.ShapeDtypeStruct((t, n_conv), F32),
            jax.ShapeDtypeStruct((t, n_z), BF),
            jax.ShapeDtypeStruct((t, LANE), F32),
            jax.ShapeDtypeStruct((16, t), F32),
        ],
        scratch_shapes=[pltpu.VMEM((tm, d), BF)],
        compiler_params=_params(("arbitrary",)),
        name="in_proj",
    )(x, g, w_p, wbat, gcq, gckv, ctab, stab)


def _q_proj_body(c_ref, w1_ref, w2_ref, ctab_ref, stab_ref, gn_ref, gr_ref, q_ref):
    c = c_ref[...]
    qf = _mm(c, w1_ref[...])
    qs = _mm(c, w2_ref[...])
    ct = ctab_ref[...]
    st = stab_ref[...]
    gn = gn_ref[...]
    gr = gr_ref[...]
    for h in range(MLA_HEADS):
        nope = qf[:, h * QPAD:h * QPAD + LANE]
        rot = qf[:, h * QPAD + LANE:(h + 1) * QPAD] * ct + qs[:, h * LANE:(h + 1) * LANE] * st
        ss = jnp.sum(nope * nope, axis=-1, keepdims=True) + jnp.sum(rot * rot, axis=-1, keepdims=True)
        rs = lax.rsqrt(ss * (1.0 / QK_HEAD) + EPS) * QSCALE
        q_ref[:, h * QPAD:h * QPAD + LANE] = (nope * rs * gn).astype(BF)
        q_ref[:, h * QPAD + LANE:(h + 1) * QPAD] = (rot * rs * gr).astype(BF)


def _q_proj(cqn, w1, w2, ctab, stab, gn, gr):
    t = cqn.shape[0]
    tm = _tile(min(t, ctab.shape[0]), PROJ_TM)
    ntab = ctab.shape[0] // tm
    const = lambda i: (0, 0)
    return pl.pallas_call(
        _q_proj_body,
        grid=(t // tm,),
        in_specs=[
            pl.BlockSpec((tm, Q_RANK), lambda i: (i, 0)),
            pl.BlockSpec(w1.shape, const),
            pl.BlockSpec(w2.shape, const),
            pl.BlockSpec((tm, LANE), lambda i: (i % ntab, 0)),
            pl.BlockSpec((tm, LANE), lambda i: (i % ntab, 0)),
            pl.BlockSpec((1, LANE), const),
            pl.BlockSpec((1, LANE), const),
        ],
        out_specs=pl.BlockSpec((tm, MLA_HEADS * QPAD), lambda i: (i, 0)),
        out_shape=jax.ShapeDtypeStruct((t, MLA_HEADS * QPAD), BF),
        compiler_params=_params(("arbitrary",)),
        name="q_proj",
    )(cqn, w1, w2, ctab, stab, gn, gr)


def _kv_proj_body(c_ref, kpe_ref, w_ref, gn_ref, gr_ref, k_ref, v_ref):
    kv = _mm(c_ref[...].astype(BF), w_ref[...])
    kp = kpe_ref[...]
    kps = jnp.sum(kp * kp, axis=-1, keepdims=True)
    gn = gn_ref[...]
    gr = gr_ref[...]
    for h in range(MLA_HEADS):
        kn = kv[:, h * 2 * LANE:h * 2 * LANE + LANE]
        rs = lax.rsqrt((jnp.sum(kn * kn, axis=-1, keepdims=True) + kps) * (1.0 / QK_HEAD) + EPS)
        k_ref[:, h * QPAD:h * QPAD + LANE] = (kn * rs * gn).astype(BF)
        k_ref[:, h * QPAD + LANE:(h + 1) * QPAD] = (kp * rs * gr).astype(BF)
        v_ref[:, h * V_DIM:(h + 1) * V_DIM] = kv[:, h * 2 * LANE + LANE:(h + 1) * 2 * LANE].astype(BF)


def _kv_proj(ckv, kpe_pad, w, gn, gr):
    t = ckv.shape[0]
    tm = _tile(t, PROJ_TM)
    const = lambda i: (0, 0)
    return pl.pallas_call(
        _kv_proj_body,
        grid=(t // tm,),
        in_specs=[
            pl.BlockSpec((tm, KV_RANK), lambda i: (i, 0)),
            pl.BlockSpec((tm, LANE), lambda i: (i, 0)),
            pl.BlockSpec(w.shape, const),
            pl.BlockSpec((1, LANE), const),
            pl.BlockSpec((1, LANE), const),
        ],
        out_specs=[
            pl.BlockSpec((tm, MLA_HEADS * QPAD), lambda i: (i, 0)),
            pl.BlockSpec((tm, MLA_HEADS * V_DIM), lambda i: (i, 0)),
        ],
        out_shape=[
            jax.ShapeDtypeStruct((t, MLA_HEADS * QPAD), BF),
            jax.ShapeDtypeStruct((t, MLA_HEADS * V_DIM), BF),
        ],
        compiler_params=_params(("arbitrary",)),
        name="kv_proj",
    )(ckv, kpe_pad, w, gn, gr)


def _attn_body(q_ref, k_ref, v_ref, g_ref, o_ref, m_scr, l_scr, acc_scr, *, tq, tk, wide, t_valid):
    qi = pl.program_id(2)
    qpos0 = qi * tq
    n_full = jnp.minimum((qpos0 // CHUNK * CHUNK + CHUNK) // tk, t_valid // tk)
    hi = jnp.minimum((qpos0 + tq - 1) // CHUNK * CHUNK + CHUNK, t_valid)
    n_total = (hi + tk - 1) // tk

    m_scr[...] = jnp.full(m_scr.shape, NEG_BIG, F32)
    l_scr[...] = jnp.zeros(l_scr.shape, F32)
    acc_scr[...] = jnp.zeros(acc_scr.shape, F32)
    diag = tq == tk and t_valid % tk == 0 and tk % (2 * LANE) == 0 and tq % 32 == 0

    def step(kc, width, masked, r0=0, nr=tq):
        rows = slice(r0, r0 + nr)
        k0 = pl.multiple_of(kc * tk, tk)
        s = _nt(q_ref[rows, :], k_ref[pl.ds(k0, width), :])
        if masked:
            qpos = qpos0 + r0 + lax.broadcasted_iota(jnp.int32, (nr, 1), 0)
            last = jnp.minimum(qpos | (CHUNK - 1), t_valid - 1) - k0
            s = jnp.where(lax.broadcasted_iota(jnp.int32, (nr, width), 1) <= last, s, NEG_BIG)
        m_prev = m_scr[rows, :]
        m_new = jnp.maximum(m_prev, jnp.max(s, axis=-1, keepdims=True))
        alpha = jnp.exp2(m_prev - m_new)
        ps = [jnp.exp2(s[:, j * LANE:(j + 1) * LANE] - m_new) for j in range(width // LANE)]
        psum = ps[0]
        for pj in ps[1:]:
            psum = psum + pj
        l_scr[rows, :] = alpha * l_scr[rows, :] + psum
        p = jnp.concatenate(ps, axis=1).astype(BF) if len(ps) > 1 else ps[0].astype(BF)
        acc_scr[rows, :] = acc_scr[rows, :] * alpha + _mm(p, v_ref[pl.ds(k0, width), :])
        m_scr[rows, :] = m_new

    def loop(lo, hi, fn):
        lax.fori_loop(lo, hi, lambda i, c: (fn(i), c)[1], 0)

    done = 0
    for w in range(wide, 0, -1):
        cnt = (n_full - done) // w
        loop(0, cnt, lambda i, w=w, done=done: step(done + i * w, w * tk, False))
        done = done + cnt * w
    if diag:
        def masked(kc):
            step(kc, tk // 2, True, 0, tq // 2)
            step(kc, tk, True, tq // 2, tq // 2)
    else:
        def masked(kc):
            step(kc, tk, True)
    loop(n_full, n_total, masked)
    o = acc_scr[...] / jnp.sum(l_scr[...], axis=-1, keepdims=True)
    o_ref[...] = _rms(o, g_ref[...]).astype(BF)


def _attention(q, k, v, g_out, nb):
    tq_total = q.shape[0] // nb
    tk_total = k.shape[0] // nb
    assert tq_total == tk_total
    tq = _tile(tq_total, ATTN_TQ)
    tk = _tile(tk_total, ATTN_TK, mult=LANE)
    nq = tq_total // tq
    body = functools.partial(_attn_body, tq=tq, tk=tk, wide=ATTN_WIDE, t_valid=tk_total)
    return pl.pallas_call(
        body,
        grid=(nb, MLA_HEADS, nq),
        in_specs=[
            pl.BlockSpec((tq, QPAD), lambda b, h, i: (b * nq + i, h)),
            pl.BlockSpec((tk_total, QPAD), lambda b, h, i: (b, h)),
            pl.BlockSpec((tk_total, V_DIM), lambda b, h, i: (b, h)),
            pl.BlockSpec((1, V_DIM), lambda b, h, i: (0, 0)),
        ],
        out_specs=pl.BlockSpec((tq, V_DIM), lambda b, h, i: (b * nq + i, h)),
        out_shape=jax.ShapeDtypeStruct((q.shape[0], MLA_HEADS * V_DIM), BF),
        scratch_shapes=[
            pltpu.VMEM((tq, LANE), F32),
            pltpu.VMEM((tq, LANE), F32),
            pltpu.VMEM((tq, V_DIM), F32),
        ],
        compiler_params=_params(("arbitrary", "arbitrary", "arbitrary")),
        name="mla_attention",
    )(q, k, v, g_out)


def _cached_attn_body(q_ref, cp_ref, kp_ref, cn_ref, kn_ref, w_ref, gn_ref, gr_ref, g_ref, o_ref, *, p_len, s_len):
    cp = cp_ref[0].astype(BF)
    cn = cn_ref[...].astype(BF)
    kpp = kp_ref[0]
    kpn = kn_ref[:, :QK_ROPE]
    kps_p = jnp.sum(kpp * kpp, axis=-1, keepdims=True)
    kps_n = jnp.sum(kpn * kpn, axis=-1, keepdims=True)
    gn = gn_ref[...]
    gr = gr_ref[:, :QK_ROPE]
    g_out = g_ref[...]
    qpos = p_len + lax.broadcasted_iota(jnp.int32, (s_len, 1), 0)
    kpos = p_len + lax.broadcasted_iota(jnp.int32, (1, s_len), 1)
    ok_new = kpos <= (qpos | (CHUNK - 1))

    def keys_values(kv, kp, kps):
        kn = kv[:, :QK_NOPE]
        rs = lax.rsqrt((jnp.sum(kn * kn, axis=-1, keepdims=True) + kps) * (1.0 / QK_HEAD) + EPS)
        return (kn * rs * gn).astype(BF), (kp * rs * gr).astype(BF), kv[:, QK_NOPE:].astype(BF)

    hw = 2 * LANE
    for h in range(MLA_HEADS):
        if h % 2 == 0:
            kv2p = _mm(cp, w_ref[:, h * hw:(h + 2) * hw])
            kv2n = _mm(cn, w_ref[:, h * hw:(h + 2) * hw])
        kb, krb, vb = keys_values(kv2p[:, (h % 2) * hw:(h % 2 + 1) * hw], kpp, kps_p)
        kbn, krbn, vbn = keys_values(kv2n[:, (h % 2) * hw:(h % 2 + 1) * hw], kpn, kps_n)
        qn = q_ref[:, h * QPAD:h * QPAD + QK_NOPE]
        qr = q_ref[:, h * QPAD + QK_NOPE:h * QPAD + QK_HEAD]
        s_p = _nt(qn, kb) + _nt(qr, krb)
        s_n = jnp.where(ok_new, _nt(qn, kbn) + _nt(qr, krbn), NEG_BIG)
        m = jnp.maximum(jnp.max(s_p, axis=-1, keepdims=True), jnp.max(s_n, axis=-1, keepdims=True))
        pp = jnp.exp2(s_p - m)
        pn = jnp.exp2(s_n - m)
        l = jnp.sum(pp, axis=-1, keepdims=True) + jnp.sum(pn, axis=-1, keepdims=True)
        o = (_mm(pp.astype(BF), vb) + _mm(pn.astype(BF), vbn)) / l
        o_ref[:, h * V_DIM:(h + 1) * V_DIM] = _rms(o, g_out).astype(BF)


def _cached_attention(q, ckv_past, kpe_past, ckv_new, kpe_new, w, gn, gr, g_out, nb):
    s_len = q.shape[0] // nb
    p_len = ckv_past.shape[1]
    assert s_len % 16 == 0 or nb == 1
    body = functools.partial(_cached_attn_body, p_len=p_len, s_len=s_len)
    const = lambda b: (0, 0)
    return pl.pallas_call(
        body,
        grid=(nb,),
        in_specs=[
            pl.BlockSpec((s_len, MLA_HEADS * QPAD), lambda b: (b, 0)),
            pl.BlockSpec((1, p_len, KV_RANK), lambda b: (b, 0, 0)),
            pl.BlockSpec((1, p_len, QK_ROPE), lambda b: (b, 0, 0)),
            pl.BlockSpec((s_len, KV_RANK), lambda b: (b, 0)),
            pl.BlockSpec((s_len, LANE), lambda b: (b, 0)),
            pl.BlockSpec(w.shape, const),
            pl.BlockSpec((1, LANE), const),
            pl.BlockSpec((1, LANE), const),
            pl.BlockSpec((1, V_DIM), const),
        ],
        out_specs=pl.BlockSpec((s_len, MLA_HEADS * V_DIM), lambda b: (b, 0)),
        out_shape=jax.ShapeDtypeStruct((q.shape[0], MLA_HEADS * V_DIM), BF),
        compiler_params=_params(("arbitrary",)),
        name="cached_attention",
    )(q, ckv_past, kpe_past, ckv_new, kpe_new, w, gn, gr, g_out)


def _gdn_body(q_ref, k_ref, v_ref, wq_ref, wk_ref, wv_ref, pq_ref, pk_ref, pv_ref,
              ba_ref, bat_ref, alog_l_ref, dtb_l_ref, alog_c_ref, dtb_c_ref,
              z_ref, s0_ref, gout_ref,
              o_ref, sout_ref,
              s_scr, halo_scr, vnew_scr, *, tm, L, hg):
    t = pl.program_id(2)
    nt = pl.num_programs(2)
    n_chunks = tm // L
    n_factors = int(math.log2(L)) - 1

    @pl.when(t == 0)
    def _():
        s_scr[...] = s0_ref[0]
        halo_scr[0] = pq_ref[...]
        halo_scr[1] = pk_ref[...]
        halo_scr[2] = pv_ref[...]
        vnew_scr[...] = jnp.zeros(vnew_scr.shape, BF)

    ba = ba_ref[...]
    beta_all = _sigmoid(ba)
    g_all = -jnp.exp(alog_l_ref[...]) * _softplus(ba + dtb_l_ref[...])
    gt_all = -jnp.exp(alog_c_ref[...]) * _softplus(bat_ref[0] + dtb_c_ref[...])

    r = lax.broadcasted_iota(jnp.int32, (tm, tm), 0)
    c = lax.broadcasted_iota(jnp.int32, (tm, tm), 1)
    same = (r // L) == (c // L)
    lower = same & (c <= r)
    strict = same & (c < r)
    cs = jnp.where(lower, 1.0, 0.0).astype(BF)
    bd = jnp.where(same, 1.0, 0.0).astype(BF)
    g3 = _split3(g_all)
    gcol_all = _mm(cs, g3[0]) + _mm(cs, g3[1]) + _mm(cs, g3[2])
    glast_all = _mm(bd, g3[0]) + _mm(bd, g3[1]) + _mm(bd, g3[2])
    gt3 = _split3(gt_all)
    grow_all = _nt(gt3[0], cs) + _nt(gt3[1], cs) + _nt(gt3[2], cs)

    gout = gout_ref[...]
    eye_t = jnp.where(r == c, 1.0, 0.0)
    eye_k = jnp.where(lax.broadcasted_iota(jnp.int32, (GDN_DK, GDN_DK), 0)
                      == lax.broadcasted_iota(jnp.int32, (GDN_DK, GDN_DK), 1), 1.0, 0.0).astype(BF)
    col_chunk = lax.broadcasted_iota(jnp.int32, (1, tm), 1) // L

    def make_wave(heads):
        c0, c1 = heads[0] * GDN_DK, (heads[-1] + 1) * GDN_DK
        loc = lambda h: slice((h - heads[0]) * GDN_DK, (h - heads[0] + 1) * GDN_DK)
        w = {}

        def conv(idx, x_ref, w_ref):
            xs = jnp.concatenate([halo_scr[idx, :, c0:c1], x_ref[:, c0:c1]], axis=0)
            cw = w_ref[:, c0:c1]
            acc = xs * cw[0:1]
            for i in range(1, CONV_W):
                acc = pltpu.roll(acc, 1, axis=0) + xs * cw[i:i + 1]
            halo_scr[idx, :, c0:c1] = xs[tm:tm + HALO]
            y = acc[HALO:HALO + tm]
            return y * _sigmoid(y)

        def p_q():
            w["yq"] = conv(0, q_ref, wq_ref)

        def p_k():
            w["yk"] = conv(1, k_ref, wk_ref)

        def p_v():
            w["yv"] = conv(2, v_ref, wv_ref)

        def p_gate():
            w["beta"], w["gc"], w["glast"], w["gr"] = {}, {}, {}, {}
            for h in heads:
                g = GDN_HEADS + h
                w["beta"][h] = beta_all[:, h:h + 1]
                w["gc"][h] = gcol_all[:, g:g + 1]
                w["glast"][h] = glast_all[:, g:g + 1]
                w["gr"][h] = grow_all[g:g + 1, :]

        def p_norm():
            w["qn"], w["kn"] = {}, {}
            for h in heads:
                qh = w["yq"][:, loc(h)]
                kh = w["yk"][:, loc(h)]
                w["qn"][h] = qh * lax.rsqrt(jnp.sum(qh * qh, axis=-1, keepdims=True) + EPS) * (GDN_DK ** -0.5)
                w["kn"][h] = kh * lax.rsqrt(jnp.sum(kh * kh, axis=-1, keepdims=True) + EPS)
            kb = {h: w["kn"][h].astype(BF) for h in heads}
            w["kk"] = {h: _nt(kb[h], kb[h]) for h in heads}
            w["qk"] = {h: _nt(w["qn"][h].astype(BF), kb[h]) for h in heads}

        def p_mat():
            beta, gc = w["beta"], w["gc"]
            decay = {h: jnp.where(lower, jnp.exp(gc[h] - w["gr"][h]), 0.0) for h in heads}
            w["nm"] = {h: jnp.where(strict, beta[h] * w["kk"][h] * decay[h], 0.0) for h in heads}
            w["attn"] = {h: (w["qk"][h] * decay[h]).astype(BF) for h in heads}
            w["eg"] = {h: jnp.exp(gc[h]) for h in heads}
            w["rhs"] = {h: jnp.concatenate([w["yv"][:, loc(h)] * beta[h],
                                            w["kn"][h] * (beta[h] * w["eg"][h])], axis=1) for h in heads}
            w["nb"] = {h: w["nm"][h].astype(BF) for h in heads}

        def s_first():
            w["m"] = {h: _mm(w["nb"][h], w["nb"][h]) for h in heads}
            w["T"] = {h: eye_t - w["nm"][h] for h in heads}

        def s_level(s):
            xb = {h: jnp.concatenate([w["m"][h], w["T"][h]], axis=0).astype(BF) for h in heads}
            if s < n_factors - 1:
                prod = {h: _mm(xb[h], xb[h][:tm]) for h in heads}
                w["T"] = {h: w["T"][h] + prod[h][tm:] for h in heads}
                w["m"] = {h: prod[h][:tm] for h in heads}
            else:
                w["T"] = {h: w["T"][h] + _mm(xb[h][tm:], xb[h][:tm]) for h in heads}

        def s_y0():
            w["tb"] = {h: w["T"][h].astype(BF) for h in heads}
            w["y0"] = {h: _mm(w["tb"][h], w["rhs"][h].astype(BF)) for h in heads}

        def s_res():
            res = {}
            for h in heads:
                nbf = w["nb"][h].astype(F32)
                nx = jnp.concatenate([nbf, w["nm"][h] - nbf], axis=0).astype(BF)
                yh = w["y0"][h].astype(BF)
                yl = (w["y0"][h] - yh.astype(F32)).astype(BF)
                p1 = _mm(nx, yh)
                res[h] = w["rhs"][h] - w["y0"][h] - (p1[:tm] + p1[tm:] + _mm(nx[:tm], yl))
            w["res"] = res

        def s_fix():
            w["y"] = {h: w["y0"][h] + _mm(w["tb"][h], w["res"][h].astype(BF)) for h in heads}

        def c_prep():
            y = w["y"]
            w["u"] = {h: y[h][:, :GDN_DV] for h in heads}
            w["wq"] = {}
            for h in heads:
                qd = w["qn"][h] * w["eg"][h]
                parts = []
                for ci in range(n_chunks):
                    parts += [y[h][ci * L:(ci + 1) * L, GDN_DV:], qd[ci * L:(ci + 1) * L]]
                w["wq"][h] = jnp.concatenate(parts, axis=0).astype(BF)
            kdb = {h: (w["kn"][h] * jnp.exp(w["glast"][h] - w["gc"][h])).astype(BF) for h in heads}
            w["gtot"] = {h: jnp.exp(w["glast"][h]) for h in heads}
            kdt = {h: _nt(eye_k, kdb[h]) for h in heads}
            w["kdt"] = {h: [jnp.where(col_chunk == ci, kdt[h], 0.0).astype(BF) for ci in range(n_chunks)]
                        for h in heads}
            w["st"] = {h: s_scr[h] for h in heads}
            w["outs"] = {h: [] for h in heads}

        def c_chunk(ci):
            r0 = ci * L
            st = w["st"]
            ws = {h: _mm(w["wq"][h][2 * r0:2 * r0 + 2 * L], st[h].astype(BF)) for h in heads}
            for h in heads:
                vnew_scr[h, r0:r0 + L, :] = (w["u"][h][r0:r0 + L] - ws[h][:L]).astype(BF)
            for h in heads:
                w["outs"][h].append(ws[h][L:] + _mm(w["attn"][h][r0:r0 + L, :], vnew_scr[h]))
            w["st"] = {h: st[h] * w["gtot"][h][r0:r0 + 1, :] + _mm(w["kdt"][h][ci], vnew_scr[h]) for h in heads}

        def c_out():
            for h in heads:
                s_scr[h] = w["st"][h]
                o = w["outs"][h][0] if n_chunks == 1 else jnp.concatenate(w["outs"][h], axis=0)
                zz = z_ref[:, h * GDN_DV:(h + 1) * GDN_DV].astype(F32)
                o_ref[:, h * GDN_DV:(h + 1) * GDN_DV] = (_rms(o, gout) * (zz * _sigmoid(zz))).astype(BF)

        prep = [p_q, p_k, p_v, p_gate, p_norm, p_mat]
        solve = ([s_first] + [functools.partial(s_level, s) for s in range(n_factors)] + [s_y0, s_res, s_fix])
        scan = [c_prep] + [functools.partial(c_chunk, ci) for ci in range(n_chunks)] + [c_out]
        return [prep, solve, scan]

    nw = min(GDN_WAVES, hg)
    per = hg // nw
    waves = [make_wave(list(range(i * per, (i + 1) * per))) for i in range(nw)]
    for slot in range(nw + 2):
        active = [waves[i][slot - i] for i in range(nw) if 0 <= slot - i < 3]
        for stage in _interleave(active):
            stage()

    @pl.when(t == nt - 1)
    def _():
        sout_ref[0] = s_scr[...]


def _gdn(qkv, conv_w, conv_past8, ba, bat, alog_l, dtb_l, alog_c, dtb_c, z, s0, gout, nb):
    t = qkv.shape[0]
    s_len = t // nb
    L = min(s_len, CHUNK)
    tm = _tile(s_len, GDN_TM, mult=L)
    nt = s_len // tm
    hg = GDN_HEADS
    hw = hg * GDN_DK
    body = functools.partial(_gdn_body, tm=tm, L=L, hg=hg)
    rows = lambda off: (lambda b, h, i: (b * nt + i, off + h))
    wcol = lambda off: (lambda b, h, i: (0, off + h))
    pcol = lambda off: (lambda b, h, i: (b, off + h))
    const = lambda b, h, i: (0, 0)
    ng = GDN_HEADS // hg
    return pl.pallas_call(
        body,
        grid=(nb, ng, nt),
        in_specs=[
            pl.BlockSpec((tm, hw), rows(0)),
            pl.BlockSpec((tm, hw), rows(ng)),
            pl.BlockSpec((tm, hw), rows(2 * ng)),
            pl.BlockSpec((CONV_W, hw), wcol(0)),
            pl.BlockSpec((CONV_W, hw), wcol(ng)),
            pl.BlockSpec((CONV_W, hw), wcol(2 * ng)),
            pl.BlockSpec((HALO, hw), pcol(0)),
            pl.BlockSpec((HALO, hw), pcol(ng)),
            pl.BlockSpec((HALO, hw), pcol(2 * ng)),
            pl.BlockSpec((tm, LANE), lambda b, h, i: (b * nt + i, 0)),
            pl.BlockSpec((1, 16, tm), lambda b, h, i: (b, 0, i)),
            pl.BlockSpec((1, LANE), const),
            pl.BlockSpec((1, LANE), const),
            pl.BlockSpec((16, 1), const),
            pl.BlockSpec((16, 1), const),
            pl.BlockSpec((tm, hw), rows(0)),
            pl.BlockSpec((1, hg, GDN_DK, GDN_DV), lambda b, h, i: (b, h, 0, 0)),
            pl.BlockSpec((1, GDN_DV), const),
        ],
        out_specs=[
            pl.BlockSpec((tm, hw), rows(0)),
            pl.BlockSpec((1, hg, GDN_DK, GDN_DV), lambda b, h, i: (b, h, 0, 0)),
        ],
        out_shape=[
            jax.ShapeDtypeStruct((t, GDN_HEADS * GDN_DV), BF),
            jax.ShapeDtypeStruct((nb, GDN_HEADS, GDN_DK, GDN_DV), F32),
        ],
        scratch_shapes=[
            pltpu.VMEM((hg, GDN_DK, GDN_DV), F32),
            pltpu.VMEM((3, HALO, hw), F32),
            pltpu.VMEM((hg, tm, GDN_DV), BF),
        ],
        compiler_params=_params(("arbitrary", "arbitrary", "arbitrary")),
        name="gdn",
    )(qkv, qkv, qkv, conv_w, conv_w, conv_w, conv_past8, conv_past8, conv_past8,
      ba, bat, alog_l, dtb_l, alog_c, dtb_c, z, s0, gout)


def _mix_mem_body(x_ref, a1_ref, a2_ref, wo1_ref, wo2_ref, gq_ref, wmq_ref, gmq_ref,
                  mk_ref, mv_ref, wmo_ref, o_ref):
    h1 = x_ref[...] + _mm(a1_ref[...], wo1_ref[...]) + _mm(a2_ref[...], wo2_ref[...])
    hn = _rms(h1, gq_ref[...]).astype(BF)
    q = _mm(hn, wmq_ref[...])
    gmq = gmq_ref[...]
    mk = mk_ref[0]
    mv = mv_ref[0]
    outs = []
    for h in range(MEM_HEADS):
        sl = slice(h * MEM_DIM, (h + 1) * MEM_DIM)
        qh = _rms(q[:, sl], gmq).astype(BF)
        s = _nt(qh, mk[:, sl]) * (MEM_DIM ** -0.5)
        p = jnp.exp(s - jnp.max(s, axis=-1, keepdims=True))
        p = p / jnp.sum(p, axis=-1, keepdims=True)
        outs.append(_mm(p.astype(BF), mv[:, sl]))
    o = jnp.concatenate(outs, axis=1).astype(BF)
    o_ref[...] = h1 + _mm(o, wmo_ref[...])


def _mix_mem(x, a1, a2, wo1, wo2, gq, wmq, gmq, mk, mv, wmo, nb):
    t, d = x.shape
    s_len = t // nb
    tm = _tile(s_len, MIX_TM)
    nt = s_len // tm
    n_mem = mk.shape[1]
    mw = MEM_HEADS * MEM_DIM
    row = lambda i: (i, 0)
    const = lambda i: (0, 0)
    return pl.pallas_call(
        _mix_mem_body,
        grid=(t // tm,),
        in_specs=[
            pl.BlockSpec((tm, d), row),
            pl.BlockSpec((tm, a1.shape[1]), row),
            pl.BlockSpec((tm, a2.shape[1]), row),
            pl.BlockSpec(wo1.shape, const),
            pl.BlockSpec(wo2.shape, const),
            pl.BlockSpec((1, d), const),
            pl.BlockSpec(wmq.shape, const),
            pl.BlockSpec((1, MEM_DIM), const),
            pl.BlockSpec((1, n_mem, mw), lambda i: (i // nt, 0, 0)),
            pl.BlockSpec((1, n_mem, mw), lambda i: (i // nt, 0, 0)),
            pl.BlockSpec(wmo.shape, const),
        ],
        out_specs=pl.BlockSpec((tm, d), row),
        out_shape=jax.ShapeDtypeStruct((t, d), F32),
        compiler_params=_params(("arbitrary",)),
        name="mix_mem",
    )(x, a1, a2, wo1, wo2, gq, wmq, gmq, mk, mv, wmo)


def _ffn_body(x_ref, g_ref, w1_ref, w2_ref, o_ref, hn_scr):
    f = pl.program_id(1)

    @pl.when(f == 0)
    def _():
        x = x_ref[...]
        hn_scr[...] = _rms(x, g_ref[...]).astype(BF)
        o_ref[...] = x

    a = jnp.maximum(_mm(hn_scr[...], w1_ref[...]), 0.0)
    o_ref[...] += _mm((a * a).astype(BF), w2_ref[...])


def _ffn(x, g, w1, w2):
    t, d = x.shape
    dff = w1.shape[1]
    tm = _tile(t, FFN_TM)
    tf = _tile(dff, FFN_TF, mult=LANE)
    return pl.pallas_call(
        _ffn_body,
        grid=(t // tm, dff // tf),
        in_specs=[
            pl.BlockSpec((tm, d), lambda i, f: (i, 0)),
            pl.BlockSpec((1, d), lambda i, f: (0, 0)),
            pl.BlockSpec((d, tf), lambda i, f: (0, f)),
            pl.BlockSpec((tf, d), lambda i, f: (f, 0)),
        ],
        out_specs=pl.BlockSpec((tm, d), lambda i, f: (i, 0)),
        out_shape=jax.ShapeDtypeStruct((t, d), F32),
        scratch_shapes=[pltpu.VMEM((tm, d), BF)],
        compiler_params=_params(("arbitrary", "arbitrary")),
        name="ffn",
    )(x, g, w1, w2)


def _mem_kv_body(m_ref, g_ref, wk_ref, wv_ref, gk_ref, k_ref, v_ref):
    mn = _rms(m_ref[...], g_ref[...]).astype(BF)
    k = _mm(mn, wk_ref[...])
    gk = gk_ref[...]
    for h in range(MEM_HEADS):
        sl = slice(h * MEM_DIM, (h + 1) * MEM_DIM)
        k_ref[:, sl] = _rms(k[:, sl], gk)
    v_ref[...] = _mm(mn, wv_ref[...])


def _mem_kv(mem, g, wk, wv, gk):
    t, d = mem.shape
    tm = _tile(t, 256)
    mw = MEM_HEADS * MEM_DIM
    const = lambda i: (0, 0)
    return pl.pallas_call(
        _mem_kv_body,
        grid=(t // tm,),
        in_specs=[
            pl.BlockSpec((tm, d), lambda i: (i, 0)),
            pl.BlockSpec((1, d), const),
            pl.BlockSpec(wk.shape, const),
            pl.BlockSpec(wv.shape, const),
            pl.BlockSpec((1, MEM_DIM), const),
        ],
        out_specs=[pl.BlockSpec((tm, mw), lambda i: (i, 0)), pl.BlockSpec((tm, mw), lambda i: (i, 0))],
        out_shape=[jax.ShapeDtypeStruct((t, mw), F32), jax.ShapeDtypeStruct((t, mw), F32)],
        compiler_params=_params(("arbitrary",)),
        name="mem_kv",
    )(mem, g, wk, wv, gk)


def _rope_tables(pos):
    half = QK_ROPE // 2
    inv_freq = ROPE_THETA ** (-np.arange(half, dtype=np.float64) / half)
    ang = np.asarray(pos, np.float64)[:, None] * inv_freq[None, :]
    cos, sin = np.cos(ang), np.sin(ang)
    zeros = np.zeros((ang.shape[0], LANE - QK_ROPE))
    return (jnp.asarray(np.concatenate([cos, cos, zeros], axis=1), F32),
            jnp.asarray(np.concatenate([-sin, sin, zeros], axis=1), F32))


def _pad_lanes(v, n=LANE):
    return jnp.pad(v, ((0, 0), (0, n - v.shape[1])))


def _prep_weights(w_in, g_cq, w_uq, g_ckv, w_ukv, g_q, g_k, w_o, w_mq, w_mo, w_ff1, w_ff2, w_mk, w_mv):
    d = w_in.shape[0]
    n_conv = 2 * GDN_HEADS * GDN_DK + GDN_HEADS * GDN_DV
    n_z = GDN_HEADS * GDN_DV
    wt = w_in.T.astype(BF)
    o = 0
    w_cq = wt[o:o + Q_RANK]; o += Q_RANK
    w_ckv = wt[o:o + KV_RANK]; o += KV_RANK
    w_kpe = wt[o:o + QK_ROPE]; o += QK_ROPE
    w_qkv = wt[o:o + n_conv]; o += n_conv
    w_z = wt[o:o + n_z]; o += n_z
    w_b = wt[o:o + GDN_HEADS]; o += GDN_HEADS
    w_a = wt[o:o + GDN_HEADS]
    half = QK_ROPE // 2
    swap = jnp.concatenate([jnp.arange(half, QK_ROPE), jnp.arange(0, half)])
    z64 = jnp.zeros((LANE - QK_ROPE, d), BF)
    misc = jnp.concatenate([w_kpe, z64, w_kpe[swap], z64, w_b, w_a], axis=0)
    misc = jnp.pad(misc, ((0, MISC_ROWS - misc.shape[0]), (0, 0)))
    w_in_p = jnp.concatenate([w_cq, w_ckv, w_qkv, w_z, misc], axis=0)
    wbat = jnp.concatenate([w_b, w_a], axis=0)

    r = w_uq.shape[0]
    wq3 = w_uq.reshape(r, MLA_HEADS, QK_HEAD)
    zq = jnp.zeros((r, MLA_HEADS, LANE - QK_ROPE), w_uq.dtype)
    w1 = jnp.concatenate([wq3, zq], axis=2).reshape(r, MLA_HEADS * QPAD).astype(BF)
    w2 = jnp.concatenate([wq3[:, :, QK_NOPE:][:, :, swap], zq], axis=2).reshape(r, MLA_HEADS * LANE).astype(BF)

    dmla = MLA_HEADS * V_DIM
    return dict(
        n_conv=n_conv, n_z=n_z, w_in_p=w_in_p, wbat=wbat, w1=w1, w2=w2,
        gcq=g_cq[None, :], gckv=g_ckv[None, :],
        gq_n=g_q[None, :QK_NOPE], gq_r=_pad_lanes(g_q[None, QK_NOPE:]),
        gk_n=g_k[None, :QK_NOPE], gk_r=_pad_lanes(g_k[None, QK_NOPE:]),
        w_ukv=w_ukv.astype(BF), wo1=w_o[:dmla].astype(BF), wo2=w_o[dmla:].astype(BF),
        w_mq=w_mq.astype(BF), w_mo=w_mo.astype(BF), w_ff1=w_ff1.astype(BF), w_ff2=w_ff2.astype(BF),
        w_mk=w_mk.astype(BF), w_mv=w_mv.astype(BF),
    )


def _gate_params(a_log, dt_bias):
    z8 = jnp.zeros((GDN_HEADS,), F32)
    al = jnp.concatenate([z8, a_log.astype(F32)])
    db = jnp.concatenate([z8, dt_bias.astype(F32)])
    return _pad_lanes(al[None, :]), _pad_lanes(db[None, :]), al[:, None], db[:, None]


def _layer(x, pos, past, conv_past, s0, mem_k, mem_v, wp, lw):
    nb, s_len, d = x.shape
    t = nb * s_len
    xf = x.reshape(t, d)
    ctab, stab = _rope_tables(pos)
    if ctab.shape[0] % 8 != 0 or (s_len < IN_TM and nb > 1):
        ctab, stab = jnp.tile(ctab, (nb, 1)), jnp.tile(stab, (nb, 1))

    cqn, ckv, kpe_pad, qkv, z, ba, bat = _in_proj(
        xf, lw["g_norm_mix"], wp["w_in_p"], wp["wbat"], wp["gcq"], wp["gckv"], ctab, stab,
        wp["n_conv"], wp["n_z"])
    q = _q_proj(cqn, wp["w1"], wp["w2"], ctab, stab, wp["gq_n"], wp["gq_r"])

    if past is None:
        k, v = _kv_proj(ckv, kpe_pad, wp["w_ukv"], wp["gk_n"], wp["gk_r"])
        o_mla = _attention(q, k, v, lw["g_mla_out"], nb)
    else:
        ckv_past, kpe_past = past
        o_mla = _cached_attention(q, ckv_past.astype(F32), kpe_past.astype(F32), ckv, kpe_pad, wp["w_ukv"],
                                  wp["gk_n"], wp["gk_r"], lw["g_mla_out"], nb)

    conv_past8 = jnp.pad(conv_past.astype(F32), ((0, 0), (HALO - (CONV_W - 1), 0), (0, 0)))
    conv_past8 = conv_past8.reshape(nb * HALO, -1)
    bat3 = bat.reshape(16, nb, s_len).transpose(1, 0, 2)
    alog_l, dtb_l, alog_c, dtb_c = _gate_params(lw["a_log"], lw["dt_bias"])
    o_gdn, s_new = _gdn(qkv, lw["conv_w"], conv_past8, ba, bat3, alog_l, dtb_l, alog_c, dtb_c,
                        z, s0.astype(F32), lw["g_gdn_out"], nb)

    h2 = _mix_mem(xf, o_mla, o_gdn, wp["wo1"], wp["wo2"], lw["g_norm_mem_q"], wp["w_mq"],
                  lw["g_mq"], mem_k, mem_v, wp["w_mo"], nb)
    y = _ffn(h2, lw["g_norm_ffn"], wp["w_ff1"], wp["w_ff2"])

    conv_in_tail = jnp.concatenate([conv_past.astype(F32), qkv.reshape(nb, s_len, -1)[:, -(CONV_W - 1):]], axis=1)
    conv_new = conv_in_tail[:, -(CONV_W - 1):]
    return (y.reshape(nb, s_len, d), ckv.reshape(nb, s_len, KV_RANK),
            kpe_pad[:, :QK_ROPE].reshape(nb, s_len, QK_ROPE), conv_new, s_new)


def kernel(x_prompt, x_sample, mem_prompt, cache_mla_ckv, cache_mla_kpe, cache_gdn_conv, state_gdn, cache_mem_k, cache_mem_v, g_norm_mix, w_in, g_cq, w_uq, g_ckv, w_ukv, g_q_mla, g_k_mla, g_mla_out, conv_w, a_log, dt_bias, g_gdn_out, w_o, g_norm_mem_q, g_norm_mem_kv, w_mq, w_mk, w_mv, g_mq, g_mk, w_mo, g_norm_ffn, w_ff1, w_ff2):
    depth = w_in.shape[0]
    nbp, sp, d = x_prompt.shape
    nbs, ss, _ = x_sample.shape
    n_mem = mem_prompt.shape[1]
    mw = MEM_HEADS * MEM_DIM
    n_conv = 2 * GDN_HEADS * GDN_DK + GDN_HEADS * GDN_DV
    pos_p = np.arange(sp)
    pos_s = cache_mla_ckv.shape[2] + np.arange(ss)
    zeros_conv = jnp.zeros((nbp, CONV_W - 1, n_conv), F32)
    zeros_state = jnp.zeros((nbp, GDN_HEADS, GDN_DK, GDN_DV), F32)
    hp, hs = x_prompt, x_sample
    outs_p = [[] for _ in range(6)]
    outs_s = [[] for _ in range(4)]
    for l in range(depth):
        wp = _prep_weights(w_in[l], g_cq[l], w_uq[l], g_ckv[l], w_ukv[l], g_q_mla[l], g_k_mla[l], w_o[l],
                           w_mq[l], w_mo[l], w_ff1[l], w_ff2[l], w_mk[l], w_mv[l])
        lw = dict(g_norm_mix=g_norm_mix[l][None, :], g_mla_out=g_mla_out[l][None, :], conv_w=conv_w[l],
                  a_log=a_log[l], dt_bias=dt_bias[l], g_gdn_out=g_gdn_out[l][None, :],
                  g_norm_mem_q=g_norm_mem_q[l][None, :], g_mq=g_mq[l][None, :],
                  g_norm_ffn=g_norm_ffn[l][None, :])
        mk, mv = _mem_kv(mem_prompt.reshape(nbp * n_mem, d), g_norm_mem_kv[l][None, :], wp["w_mk"], wp["w_mv"],
                         g_mk[l][None, :])
        mk3, mv3 = mk.reshape(nbp, n_mem, mw), mv.reshape(nbp, n_mem, mw)
        hp, c1, c2, c3, c4 = _layer(hp, pos_p, None, zeros_conv, zeros_state, mk3.astype(BF), mv3.astype(BF), wp, lw)
        for lst, val in zip(outs_p, (c1, c2, c3, c4, mk3.reshape(nbp, n_mem, MEM_HEADS, MEM_DIM),
                                     mv3.reshape(nbp, n_mem, MEM_HEADS, MEM_DIM))):
            lst.append(val)
        hs, d1, d2, d3, d4 = _layer(hs, pos_s, (cache_mla_ckv[l], cache_mla_kpe[l]), cache_gdn_conv[l], state_gdn[l],
                                    cache_mem_k[l].reshape(nbs, n_mem, mw).astype(BF),
                                    cache_mem_v[l].reshape(nbs, n_mem, mw).astype(BF), wp, lw)
        for lst, val in zip(outs_s, (d1, d2, d3, d4)):
            lst.append(val)
    return (hp, hs, *(jnp.stack(v) for v in outs_p), *(jnp.stack(v) for v in outs_s))
```

```python
import functools
import math

import jax
import jax.numpy as jnp
import numpy as np
from jax import lax
from jax.experimental import pallas as pl
from jax.experimental.pallas import tpu as pltpu

F32 = jnp.float32
BF = jnp.bfloat16

EPS = 1e-6
CHUNK = 64
ROPE_THETA = 10000.0
MLA_HEADS = 8
QK_NOPE = 128
QK_ROPE = 64
QK_HEAD = QK_NOPE + QK_ROPE
V_DIM = 128
Q_RANK = 512
KV_RANK = 512
GDN_HEADS = 8
GDN_DK = 128
GDN_DV = 128
CONV_W = 4
MEM_HEADS = 4
MEM_DIM = 128
LANE = 128
QPAD = 2 * LANE
COL_TILE = 512
MISC_ROWS = 3 * LANE
HALO = 8
NEG_BIG = -1e30
QSCALE = (QK_HEAD ** -0.5) * math.log2(math.e)

VMEM_LIMIT = 56 * 1024 * 1024

IN_TM = 512
PROJ_TM = 1024
ATTN_TQ = 1024
ATTN_TK = 1024
ATTN_WIDE = 3
GDN_TM = 256
GDN_WAVES = 2
MIX_TM = 512
FFN_TM = 1024
FFN_TF = 512


def _tile(n, pref, mult=8):
    if n <= pref:
        return n
    t = (pref // mult) * mult
    while t >= mult:
        if n % t == 0:
            return t
        t -= mult
    return n


def _nt(a, b):
    return lax.dot_general(a, b, (((1,), (1,)), ((), ())), preferred_element_type=F32)


def _mm(a, b):
    return jnp.dot(a, b, preferred_element_type=F32)


def _sigmoid(x):
    return 1.0 / (1.0 + jnp.exp(-x))


def _softplus(x):
    return jnp.maximum(x, 0.0) + jnp.log(1.0 + jnp.exp(-jnp.abs(x)))


def _rms(x, g, n=None):
    n = x.shape[-1] if n is None else n
    ms = jnp.sum(x * x, axis=-1, keepdims=True) * (1.0 / n)
    return (x * lax.rsqrt(ms + EPS)) * g


def _split3(x):
    hi = x.astype(BF)
    r1 = x - hi.astype(F32)
    mid = r1.astype(BF)
    lo = (r1 - mid.astype(F32)).astype(BF)
    return hi, mid, lo


def _interleave(lists):
    items = []
    for li, lst in enumerate(lists):
        items += [((i + 0.5) / len(lst), li, f) for i, f in enumerate(lst)]
    return [f for _, _, f in sorted(items, key=lambda it: (it[0], it[1]))]


def _params(sem):
    return pltpu.CompilerParams(dimension_semantics=sem, vmem_limit_bytes=VMEM_LIMIT)


def _in_proj_body(x_ref, g_ref, w_ref, wbat_ref, gcq_ref, gckv_ref, ctab_ref, stab_ref,
                  cqn_ref, ckv_ref, kpe_ref, qkv_ref, z_ref, ba_ref, bat_ref, xn_scr, *, n_conv, n_z):
    xn_scr[...] = _rms(x_ref[...], g_ref[...]).astype(BF)

    def prod(r0, n):
        return _nt(xn_scr[...], w_ref[r0:r0 + n, :])

    cqn_ref[...] = _rms(prod(0, Q_RANK), gcq_ref[...]).astype(BF)
    ckv_ref[...] = _rms(prod(Q_RANK, KV_RANK), gckv_ref[...])
    r0 = Q_RANK + KV_RANK
    for c0 in range(0, n_conv, COL_TILE):
        qkv_ref[:, c0:c0 + COL_TILE] = prod(r0 + c0, COL_TILE)
    r0 += n_conv
    for c0 in range(0, n_z, COL_TILE):
        z_ref[:, c0:c0 + COL_TILE] = prod(r0 + c0, COL_TILE).astype(BF)
    r0 += n_z
    acc = prod(r0, MISC_ROWS)
    kpe_ref[...] = acc[:, 0:LANE] * ctab_ref[...] + acc[:, LANE:2 * LANE] * stab_ref[...]
    ba_ref[...] = acc[:, 2 * LANE:3 * LANE]
    bat_ref[...] = _nt(wbat_ref[...], xn_scr[...])


def _in_proj(x, g, w_p, wbat, gcq, gckv, ctab, stab, n_conv, n_z):
    t, d = x.shape
    tm = _tile(min(t, ctab.shape[0]), IN_TM)
    ntab = ctab.shape[0] // tm
    assert w_p.shape[0] == Q_RANK + KV_RANK + n_conv + n_z + MISC_ROWS
    assert n_conv % COL_TILE == 0 and n_z % COL_TILE == 0
    row = lambda i: (i, 0)
    const = lambda i: (0, 0)
    body = functools.partial(_in_proj_body, n_conv=n_conv, n_z=n_z)
    return pl.pallas_call(
        body,
        grid=(t // tm,),
        in_specs=[
            pl.BlockSpec((tm, d), row),
            pl.BlockSpec((1, d), const),
            pl.BlockSpec(w_p.shape, const, pipeline_mode=pl.Buffered(1)),
            pl.BlockSpec((16, d), const),
            pl.BlockSpec((1, Q_RANK), const),
            pl.BlockSpec((1, KV_RANK), const),
            pl.BlockSpec((tm, LANE), lambda i: (i % ntab, 0)),
            pl.BlockSpec((tm, LANE), lambda i: (i % ntab, 0)),
        ],
        out_specs=[
            pl.BlockSpec((tm, Q_RANK), row),
            pl.BlockSpec((tm, KV_RANK), row),
            pl.BlockSpec((tm, LANE), row),
            pl.BlockSpec((tm, n_conv), row),
            pl.BlockSpec((tm, n_z), row),
            pl.BlockSpec((tm, LANE), row),
            pl.BlockSpec((16, tm), lambda i: (0, i)),
        ],
        out_shape=[
            jax.ShapeDtypeStruct((t, Q_RANK), BF),
            jax.ShapeDtypeStruct((t, KV_RANK), F32),
            jax.ShapeDtypeStruct((t, LANE), F32),
            jax.ShapeDtypeStruct((t, n_conv), F32),
            jax.ShapeDtypeStruct((t, n_z), BF),
            jax.ShapeDtypeStruct((t, LANE), F32),
            jax.ShapeDtypeStruct((16, t), F32),
        ],
        scratch_shapes=[pltpu.VMEM((tm, d), BF)],
        compiler_params=_params(("arbitrary",)),
        name="in_proj",
    )(x, g, w_p, wbat, gcq, gckv, ctab, stab)


def _q_proj_body(c_ref, w1_ref, w2_ref, ctab_ref, stab_ref, gn_ref, gr_ref, q_ref):
    c = c_ref[...]
    qf = _mm(c, w1_ref[...])
    qs = _mm(c, w2_ref[...])
    ct = ctab_ref[...]
    st = stab_ref[...]
    gn = gn_ref[...]
    gr = gr_ref[...]
    for h in range(MLA_HEADS):
        nope = qf[:, h * QPAD:h * QPAD + LANE]
        rot = qf[:, h * QPAD + LANE:(h + 1) * QPAD] * ct + qs[:, h * LANE:(h + 1) * LANE] * st
        ss = jnp.sum(nope * nope, axis=-1, keepdims=True) + jnp.sum(rot * rot, axis=-1, keepdims=True)
        rs = lax.rsqrt(ss * (1.0 / QK_HEAD) + EPS) * QSCALE
        q_ref[:, h * QPAD:h * QPAD + LANE] = (nope * rs * gn).astype(BF)
        q_ref[:, h * QPAD + LANE:(h + 1) * QPAD] = (rot * rs * gr).astype(BF)


def _q_proj(cqn, w1, w2, ctab, stab, gn, gr):
    t = cqn.shape[0]
    tm = _tile(min(t, ctab.shape[0]), PROJ_TM)
    ntab = ctab.shape[0] // tm
    const = lambda i: (0, 0)
    return pl.pallas_call(
        _q_proj_body,
        grid=(t // tm,),
        in_specs=[
            pl.BlockSpec((tm, Q_RANK), lambda i: (i, 0)),
            pl.BlockSpec(w1.shape, const),
            pl.BlockSpec(w2.shape, const),
            pl.BlockSpec((tm, LANE), lambda i: (i % ntab, 0)),
            pl.BlockSpec((tm, LANE), lambda i: (i % ntab, 0)),
            pl.BlockSpec((1, LANE), const),
            pl.BlockSpec((1, LANE), const),
        ],
        out_specs=pl.BlockSpec((tm, MLA_HEADS * QPAD), lambda i: (i, 0)),
        out_shape=jax.ShapeDtypeStruct((t, MLA_HEADS * QPAD), BF),
        compiler_params=_params(("arbitrary",)),
        name="q_proj",
    )(cqn, w1, w2, ctab, stab, gn, gr)


def _kv_proj_body(c_ref, kpe_ref, w_ref, gn_ref, gr_ref, k_ref, v_ref):
    kv = _mm(c_ref[...].astype(BF), w_ref[...])
    kp = kpe_ref[...]
    kps = jnp.sum(kp * kp, axis=-1, keepdims=True)
    gn = gn_ref[...]
    gr = gr_ref[...]
    for h in range(MLA_HEADS):
        kn = kv[:, h * 2 * LANE:h * 2 * LANE + LANE]
        rs = lax.rsqrt((jnp.sum(kn * kn, axis=-1, keepdims=True) + kps) * (1.0 / QK_HEAD) + EPS)
        k_ref[:, h * QPAD:h * QPAD + LANE] = (kn * rs * gn).astype(BF)
        k_ref[:, h * QPAD + LANE:(h + 1) * QPAD] = (kp * rs * gr).astype(BF)
        v_ref[:, h * V_DIM:(h + 1) * V_DIM] = kv[:, h * 2 * LANE + LANE:(h + 1) * 2 * LANE].astype(BF)


def _kv_proj(ckv, kpe_pad, w, gn, gr):
    t = ckv.shape[0]
    tm = _tile(t, PROJ_TM)
    const = lambda i: (0, 0)
    return pl.pallas_call(
        _kv_proj_body,
        grid=(t // tm,),
        in_specs=[
            pl.BlockSpec((tm, KV_RANK), lambda i: (i, 0)),
            pl.BlockSpec((tm, LANE), lambda i: (i, 0)),
            pl.BlockSpec(w.shape, const),
            pl.BlockSpec((1, LANE), const),
            pl.BlockSpec((1, LANE), const),
        ],
        out_specs=[
            pl.BlockSpec((tm, MLA_HEADS * QPAD), lambda i: (i, 0)),
            pl.BlockSpec((tm, MLA_HEADS * V_DIM), lambda i: (i, 0)),
        ],
        out_shape=[
            jax.ShapeDtypeStruct((t, MLA_HEADS * QPAD), BF),
            jax.ShapeDtypeStruct((t, MLA_HEADS * V_DIM), BF),
        ],
        compiler_params=_params(("arbitrary",)),
        name="kv_proj",
    )(ckv, kpe_pad, w, gn, gr)


def _attn_body(q_ref, k_ref, v_ref, g_ref, o_ref, m_scr, l_scr, acc_scr, *, tq, tk, wide, t_valid):
    qi = pl.program_id(2)
    qpos0 = qi * tq
    n_full = jnp.minimum((qpos0 // CHUNK * CHUNK + CHUNK) // tk, t_valid // tk)
    hi = jnp.minimum((qpos0 + tq - 1) // CHUNK * CHUNK + CHUNK, t_valid)
    n_total = (hi + tk - 1) // tk

    m_scr[...] = jnp.full(m_scr.shape, NEG_BIG, F32)
    l_scr[...] = jnp.zeros(l_scr.shape, F32)
    acc_scr[...] = jnp.zeros(acc_scr.shape, F32)
    diag = tq == tk and t_valid % tk == 0 and tk % (2 * LANE) == 0 and tq % 32 == 0

    def step(kc, width, masked, r0=0, nr=tq):
        rows = slice(r0, r0 + nr)
        k0 = pl.multiple_of(kc * tk, tk)
        s = _nt(q_ref[rows, :], k_ref[pl.ds(k0, width), :])
        if masked:
            qpos = qpos0 + r0 + lax.broadcasted_iota(jnp.int32, (nr, 1), 0)
            last = jnp.minimum(qpos | (CHUNK - 1), t_valid - 1) - k0
            s = jnp.where(lax.broadcasted_iota(jnp.int32, (nr, width), 1) <= last, s, NEG_BIG)
        m_prev = m_scr[rows, :]
        m_new = jnp.maximum(m_prev, jnp.max(s, axis=-1, keepdims=True))
        alpha = jnp.exp2(m_prev - m_new)
        ps = [jnp.exp2(s[:, j * LANE:(j + 1) * LANE] - m_new) for j in range(width // LANE)]
        psum = ps[0]
        for pj in ps[1:]:
            psum = psum + pj
        l_scr[rows, :] = alpha * l_scr[rows, :] + psum
        p = jnp.concatenate(ps, axis=1).astype(BF) if len(ps) > 1 else ps[0].astype(BF)
        acc_scr[rows, :] = acc_scr[rows, :] * alpha + _mm(p, v_ref[pl.ds(k0, width), :])
        m_scr[rows, :] = m_new

    def loop(lo, hi, fn):
        lax.fori_loop(lo, hi, lambda i, c: (fn(i), c)[1], 0)

    done = 0
    for w in range(wide, 0, -1):
        cnt = (n_full - done) // w
        loop(0, cnt, lambda i, w=w, done=done: step(done + i * w, w * tk, False))
        done = done + cnt * w
    if diag:
        def masked(kc):
            step(kc, tk // 2, True, 0, tq // 2)
            step(kc, tk, True, tq // 2, tq // 2)
    else:
        def masked(kc):
            step(kc, tk, True)
    loop(n_full, n_total, masked)
    o = acc_scr[...] / jnp.sum(l_scr[...], axis=-1, keepdims=True)
    o_ref[...] = _rms(o, g_ref[...]).astype(BF)


def _attention(q, k, v, g_out, nb):
    tq_total = q.shape[0] // nb
    tk_total = k.shape[0] // nb
    assert tq_total == tk_total
    tq = _tile(tq_total, ATTN_TQ)
    tk = _tile(tk_total, ATTN_TK, mult=LANE)
    nq = tq_total // tq
    body = functools.partial(_attn_body, tq=tq, tk=tk, wide=ATTN_WIDE, t_valid=tk_total)
    return pl.pallas_call(
        body,
        grid=(nb, MLA_HEADS, nq),
        in_specs=[
            pl.BlockSpec((tq, QPAD), lambda b, h, i: (b * nq + i, h)),
            pl.BlockSpec((tk_total, QPAD), lambda b, h, i: (b, h)),
            pl.BlockSpec((tk_total, V_DIM), lambda b, h, i: (b, h)),
            pl.BlockSpec((1, V_DIM), lambda b, h, i: (0, 0)),
        ],
        out_specs=pl.BlockSpec((tq, V_DIM), lambda b, h, i: (b * nq + i, h)),
        out_shape=jax.ShapeDtypeStruct((q.shape[0], MLA_HEADS * V_DIM), BF),
        scratch_shapes=[
            pltpu.VMEM((tq, LANE), F32),
            pltpu.VMEM((tq, LANE), F32),
            pltpu.VMEM((tq, V_DIM), F32),
        ],
        compiler_params=_params(("arbitrary", "arbitrary", "arbitrary")),
        name="mla_attention",
    )(q, k, v, g_out)


def _cached_attn_body(q_ref, cp_ref, kp_ref, cn_ref, kn_ref, w_ref, gn_ref, gr_ref, g_ref, o_ref, *, p_len, s_len):
    cp = cp_ref[0].astype(BF)
    cn = cn_ref[...].astype(BF)
    kpp = kp_ref[0]
    kpn = kn_ref[:, :QK_ROPE]
    kps_p = jnp.sum(kpp * kpp, axis=-1, keepdims=True)
    kps_n = jnp.sum(kpn * kpn, axis=-1, keepdims=True)
    gn = gn_ref[...]
    gr = gr_ref[:, :QK_ROPE]
    g_out = g_ref[...]
    qpos = p_len + lax.broadcasted_iota(jnp.int32, (s_len, 1), 0)
    kpos = p_len + lax.broadcasted_iota(jnp.int32, (1, s_len), 1)
    ok_new = kpos <= (qpos | (CHUNK - 1))

    def keys_values(kv, kp, kps):
        kn = kv[:, :QK_NOPE]
        rs = lax.rsqrt((jnp.sum(kn * kn, axis=-1, keepdims=True) + kps) * (1.0 / QK_HEAD) + EPS)
        return (kn * rs * gn).astype(BF), (kp * rs * gr).astype(BF), kv[:, QK_NOPE:].astype(BF)

    hw = 2 * LANE
    for h in range(MLA_HEADS):
        if h % 2 == 0:
            kv2p = _mm(cp, w_ref[:, h * hw:(h + 2) * hw])
            kv2n = _mm(cn, w_ref[:, h * hw:(h + 2) * hw])
        kb, krb, vb = keys_values(kv2p[:, (h % 2) * hw:(h % 2 + 1) * hw], kpp, kps_p)
        kbn, krbn, vbn = keys_values(kv2n[:, (h % 2) * hw:(h % 2 + 1) * hw], kpn, kps_n)
        qn = q_ref[:, h * QPAD:h * QPAD + QK_NOPE]
        qr = q_ref[:, h * QPAD + QK_NOPE:h * QPAD + QK_HEAD]
        s_p = _nt(qn, kb) + _nt(qr, krb)
        s_n = jnp.where(ok_new, _nt(qn, kbn) + _nt(qr, krbn), NEG_BIG)
        m = jnp.maximum(jnp.max(s_p, axis=-1, keepdims=True), jnp.max(s_n, axis=-1, keepdims=True))
        pp = jnp.exp2(s_p - m)
        pn = jnp.exp2(s_n - m)
        l = jnp.sum(pp, axis=-1, keepdims=True) + jnp.sum(pn, axis=-1, keepdims=True)
        o = (_mm(pp.astype(BF), vb) + _mm(pn.astype(BF), vbn)) / l
        o_ref[:, h * V_DIM:(h + 1) * V_DIM] = _rms(o, g_out).astype(BF)


def _cached_attention(q, ckv_past, kpe_past, ckv_new, kpe_new, w, gn, gr, g_out, nb):
    s_len = q.shape[0] // nb
    p_len = ckv_past.shape[1]
    assert s_len % 16 == 0 or nb == 1
    body = functools.partial(_cached_attn_body, p_len=p_len, s_len=s_len)
    const = lambda b: (0, 0)
    return pl.pallas_call(
        body,
        grid=(nb,),
        in_specs=[
            pl.BlockSpec((s_len, MLA_HEADS * QPAD), lambda b: (b, 0)),
            pl.BlockSpec((1, p_len, KV_RANK), lambda b: (b, 0, 0)),
            pl.BlockSpec((1, p_len, QK_ROPE), lambda b: (b, 0, 0)),
            pl.BlockSpec((s_len, KV_RANK), lambda b: (b, 0)),
            pl.BlockSpec((s_len, LANE), lambda b: (b, 0)),
            pl.BlockSpec(w.shape, const),
            pl.BlockSpec((1, LANE), const),
            pl.BlockSpec((1, LANE), const),
            pl.BlockSpec((1, V_DIM), const),
        ],
        out_specs=pl.BlockSpec((s_len, MLA_HEADS * V_DIM), lambda b: (b, 0)),
        out_shape=jax.ShapeDtypeStruct((q.shape[0], MLA_HEADS * V_DIM), BF),
        compiler_params=_params(("arbitrary",)),
        name="cached_attention",
    )(q, ckv_past, kpe_past, ckv_new, kpe_new, w, gn, gr, g_out)


def _gdn_body(q_ref, k_ref, v_ref, wq_ref, wk_ref, wv_ref, pq_ref, pk_ref, pv_ref,
              ba_ref, bat_ref, alog_l_ref, dtb_l_ref, alog_c_ref, dtb_c_ref,
              z_ref, s0_ref, gout_ref,
              o_ref, sout_ref,
              s_scr, halo_scr, vnew_scr, *, tm, L, hg):
    t = pl.program_id(2)
    nt = pl.num_programs(2)
    n_chunks = tm // L
    n_factors = int(math.log2(L)) - 1

    @pl.when(t == 0)
    def _():
        s_scr[...] = s0_ref[0]
        halo_scr[0] = pq_ref[...]
        halo_scr[1] = pk_ref[...]
        halo_scr[2] = pv_ref[...]
        vnew_scr[...] = jnp.zeros(vnew_scr.shape, BF)

    ba = ba_ref[...]
    beta_all = _sigmoid(ba)
    g_all = -jnp.exp(alog_l_ref[...]) * _softplus(ba + dtb_l_ref[...])
    gt_all = -jnp.exp(alog_c_ref[...]) * _softplus(bat_ref[0] + dtb_c_ref[...])

    r = lax.broadcasted_iota(jnp.int32, (tm, tm), 0)
    c = lax.broadcasted_iota(jnp.int32, (tm, tm), 1)
    same = (r // L) == (c // L)
    lower = same & (c <= r)
    strict = same & (c < r)
    cs = jnp.where(lower, 1.0, 0.0).astype(BF)
    bd = jnp.where(same, 1.0, 0.0).astype(BF)
    g3 = _split3(g_all)
    gcol_all = _mm(cs, g3[0]) + _mm(cs, g3[1]) + _mm(cs, g3[2])
    glast_all = _mm(bd, g3[0]) + _mm(bd, g3[1]) + _mm(bd, g3[2])
    gt3 = _split3(gt_all)
    grow_all = _nt(gt3[0], cs) + _nt(gt3[1], cs) + _nt(gt3[2], cs)

    gout = gout_ref[...]
    eye_t = jnp.where(r == c, 1.0, 0.0)
    eye_k = jnp.where(lax.broadcasted_iota(jnp.int32, (GDN_DK, GDN_DK), 0)
                      == lax.broadcasted_iota(jnp.int32, (GDN_DK, GDN_DK), 1), 1.0, 0.0).astype(BF)
    col_chunk = lax.broadcasted_iota(jnp.int32, (1, tm), 1) // L

    def make_wave(heads):
        c0, c1 = heads[0] * GDN_DK, (heads[-1] + 1) * GDN_DK
        loc = lambda h: slice((h - heads[0]) * GDN_DK, (h - heads[0] + 1) * GDN_DK)
        w = {}

        def conv(idx, x_ref, w_ref):
            xs = jnp.concatenate([halo_scr[idx, :, c0:c1], x_ref[:, c0:c1]], axis=0)
            cw = w_ref[:, c0:c1]
            acc = xs * cw[0:1]
            for i in range(1, CONV_W):
                acc = pltpu.roll(acc, 1, axis=0) + xs * cw[i:i + 1]
            halo_scr[idx, :, c0:c1] = xs[tm:tm + HALO]
            y = acc[HALO:HALO + tm]
            return y * _sigmoid(y)

        def p_q():
            w["yq"] = conv(0, q_ref, wq_ref)

        def p_k():
            w["yk"] = conv(1, k_ref, wk_ref)

        def p_v():
            w["yv"] = conv(2, v_ref, wv_ref)

        def p_gate():
            w["beta"], w["gc"], w["glast"], w["gr"] = {}, {}, {}, {}
            for h in heads:
                g = GDN_HEADS + h
                w["beta"][h] = beta_all[:, h:h + 1]
                w["gc"][h] = gcol_all[:, g:g + 1]
                w["glast"][h] = glast_all[:, g:g + 1]
                w["gr"][h] = grow_all[g:g + 1, :]

        def p_norm():
            w["qn"], w["kn"] = {}, {}
            for h in heads:
                qh = w["yq"][:, loc(h)]
                kh = w["yk"][:, loc(h)]
                w["qn"][h] = qh * lax.rsqrt(jnp.sum(qh * qh, axis=-1, keepdims=True) + EPS) * (GDN_DK ** -0.5)
                w["kn"][h] = kh * lax.rsqrt(jnp.sum(kh * kh, axis=-1, keepdims=True) + EPS)
            kb = {h: w["kn"][h].astype(BF) for h in heads}
            w["kk"] = {h: _nt(kb[h], kb[h]) for h in heads}
            w["qk"] = {h: _nt(w["qn"][h].astype(BF), kb[h]) for h in heads}

        def p_mat():
            beta, gc = w["beta"], w["gc"]
            decay = {h: jnp.where(lower, jnp.exp(gc[h] - w["gr"][h]), 0.0) for h in heads}
            w["nm"] = {h: jnp.where(strict, beta[h] * w["kk"][h] * decay[h], 0.0) for h in heads}
            w["attn"] = {h: (w["qk"][h] * decay[h]).astype(BF) for h in heads}
            w["eg"] = {h: jnp.exp(gc[h]) for h in heads}
            w["rhs"] = {h: jnp.concatenate([w["yv"][:, loc(h)] * beta[h],
                                            w["kn"][h] * (beta[h] * w["eg"][h])], axis=1) for h in heads}
            w["nb"] = {h: w["nm"][h].astype(BF) for h in heads}

        def s_first():
            w["m"] = {h: _mm(w["nb"][h], w["nb"][h]) for h in heads}
            w["T"] = {h: eye_t - w["nm"][h] for h in heads}

        def s_level(s):
            xb = {h: jnp.concatenate([w["m"][h], w["T"][h]], axis=0).astype(BF) for h in heads}
            if s < n_factors - 1:
                prod = {h: _mm(xb[h], xb[h][:tm]) for h in heads}
                w["T"] = {h: w["T"][h] + prod[h][tm:] for h in heads}
                w["m"] = {h: prod[h][:tm] for h in heads}
            else:
                w["T"] = {h: w["T"][h] + _mm(xb[h][tm:], xb[h][:tm]) for h in heads}

        def s_y0():
            w["tb"] = {h: w["T"][h].astype(BF) for h in heads}
            w["y0"] = {h: _mm(w["tb"][h], w["rhs"][h].astype(BF)) for h in heads}

        def s_res():
            res = {}
            for h in heads:
                nbf = w["nb"][h].astype(F32)
                nx = jnp.concatenate([nbf, w["nm"][h] - nbf], axis=0).astype(BF)
                yh = w["y0"][h].astype(BF)
                yl = (w["y0"][h] - yh.astype(F32)).astype(BF)
                p1 = _mm(nx, yh)
                res[h] = w["rhs"][h] - w["y0"][h] - (p1[:tm] + p1[tm:] + _mm(nx[:tm], yl))
            w["res"] = res

        def s_fix():
            w["y"] = {h: w["y0"][h] + _mm(w["tb"][h], w["res"][h].astype(BF)) for h in heads}

        def c_prep():
            y = w["y"]
            w["u"] = {h: y[h][:, :GDN_DV] for h in heads}
            w["wq"] = {}
            for h in heads:
                qd = w["qn"][h] * w["eg"][h]
                parts = []
                for ci in range(n_chunks):
                    parts += [y[h][ci * L:(ci + 1) * L, GDN_DV:], qd[ci * L:(ci + 1) * L]]
                w["wq"][h] = jnp.concatenate(parts, axis=0).astype(BF)
            kdb = {h: (w["kn"][h] * jnp.exp(w["glast"][h] - w["gc"][h])).astype(BF) for h in heads}
            w["gtot"] = {h: jnp.exp(w["glast"][h]) for h in heads}
            kdt = {h: _nt(eye_k, kdb[h]) for h in heads}
            w["kdt"] = {h: [jnp.where(col_chunk == ci, kdt[h], 0.0).astype(BF) for ci in range(n_chunks)]
                        for h in heads}
            w["st"] = {h: s_scr[h] for h in heads}
            w["outs"] = {h: [] for h in heads}

        def c_chunk(ci):
            r0 = ci * L
            st = w["st"]
            ws = {h: _mm(w["wq"][h][2 * r0:2 * r0 + 2 * L], st[h].astype(BF)) for h in heads}
            for h in heads:
                vnew_scr[h, r0:r0 + L, :] = (w["u"][h][r0:r0 + L] - ws[h][:L]).astype(BF)
            for h in heads:
                w["outs"][h].append(ws[h][L:] + _mm(w["attn"][h][r0:r0 + L, :], vnew_scr[h]))
            w["st"] = {h: st[h] * w["gtot"][h][r0:r0 + 1, :] + _mm(w["kdt"][h][ci], vnew_scr[h]) for h in heads}

        def c_out():
            for h in heads:
                s_scr[h] = w["st"][h]
                o = w["outs"][h][0] if n_chunks == 1 else jnp.concatenate(w["outs"][h], axis=0)
                zz = z_ref[:, h * GDN_DV:(h + 1) * GDN_DV].astype(F32)
                o_ref[:, h * GDN_DV:(h + 1) * GDN_DV] = (_rms(o, gout) * (zz * _sigmoid(zz))).astype(BF)

        prep = [p_q, p_k, p_v, p_gate, p_norm, p_mat]
        solve = ([s_first] + [functools.partial(s_level, s) for s in range(n_factors)] + [s_y0, s_res, s_fix])
        scan = [c_prep] + [functools.partial(c_chunk, ci) for ci in range(n_chunks)] + [c_out]
        return [prep, solve, scan]

    nw = min(GDN_WAVES, hg)
    per = hg // nw
    waves = [make_wave(list(range(i * per, (i + 1) * per))) for i in range(nw)]
    for slot in range(nw + 1):
        active = [waves[i][slot - i] for i in range(nw) if 0 <= slot - i < 2]
        for stage in _interleave(active):
            stage()
    for stage in _interleave([wv[2] for wv in waves]):
        stage()

    @pl.when(t == nt - 1)
    def _():
        sout_ref[0] = s_scr[...]


def _gdn(qkv, conv_w, conv_past8, ba, bat, alog_l, dtb_l, alog_c, dtb_c, z, s0, gout, nb):
    t = qkv.shape[0]
    s_len = t // nb
    L = min(s_len, CHUNK)
    tm = _tile(s_len, GDN_TM, mult=L)
    nt = s_len // tm
    hg = GDN_HEADS
    hw = hg * GDN_DK
    body = functools.partial(_gdn_body, tm=tm, L=L, hg=hg)
    rows = lambda off: (lambda b, h, i: (b * nt + i, off + h))
    wcol = lambda off: (lambda b, h, i: (0, off + h))
    pcol = lambda off: (lambda b, h, i: (b, off + h))
    const = lambda b, h, i: (0, 0)
    ng = GDN_HEADS // hg
    return pl.pallas_call(
        body,
        grid=(nb, ng, nt),
        in_specs=[
            pl.BlockSpec((tm, hw), rows(0)),
            pl.BlockSpec((tm, hw), rows(ng)),
            pl.BlockSpec((tm, hw), rows(2 * ng)),
            pl.BlockSpec((CONV_W, hw), wcol(0)),
            pl.BlockSpec((CONV_W, hw), wcol(ng)),
            pl.BlockSpec((CONV_W, hw), wcol(2 * ng)),
            pl.BlockSpec((HALO, hw), pcol(0)),
            pl.BlockSpec((HALO, hw), pcol(ng)),
            pl.BlockSpec((HALO, hw), pcol(2 * ng)),
            pl.BlockSpec((tm, LANE), lambda b, h, i: (b * nt + i, 0)),
            pl.BlockSpec((1, 16, tm), lambda b, h, i: (b, 0, i)),
            pl.BlockSpec((1, LANE), const),
            pl.BlockSpec((1, LANE), const),
            pl.BlockSpec((16, 1), const),
            pl.BlockSpec((16, 1), const),
            pl.BlockSpec((tm, hw), rows(0)),
            pl.BlockSpec((1, hg, GDN_DK, GDN_DV), lambda b, h, i: (b, h, 0, 0)),
            pl.BlockSpec((1, GDN_DV), const),
        ],
        out_specs=[
            pl.BlockSpec((tm, hw), rows(0)),
            pl.BlockSpec((1, hg, GDN_DK, GDN_DV), lambda b, h, i: (b, h, 0, 0)),
        ],
        out_shape=[
            jax.ShapeDtypeStruct((t, GDN_HEADS * GDN_DV), BF),
            jax.ShapeDtypeStruct((nb, GDN_HEADS, GDN_DK, GDN_DV), F32),
        ],
        scratch_shapes=[
            pltpu.VMEM((hg, GDN_DK, GDN_DV), F32),
            pltpu.VMEM((3, HALO, hw), F32),
            pltpu.VMEM((hg, tm, GDN_DV), BF),
        ],
        compiler_params=_params(("arbitrary", "arbitrary", "arbitrary")),
        name="gdn",
    )(qkv, qkv, qkv, conv_w, conv_w, conv_w, conv_past8, conv_past8, conv_past8,
      ba, bat, alog_l, dtb_l, alog_c, dtb_c, z, s0, gout)


def _mix_mem_body(x_ref, a1_ref, a2_ref, wo1_ref, wo2_ref, gq_ref, wmq_ref, gmq_ref,
                  mk_ref, mv_ref, wmo_ref, o_ref):
    h1 = x_ref[...] + _mm(a1_ref[...], wo1_ref[...]) + _mm(a2_ref[...], wo2_ref[...])
    hn = _rms(h1, gq_ref[...]).astype(BF)
    q = _mm(hn, wmq_ref[...])
    gmq = gmq_ref[...]
    mk = mk_ref[0]
    mv = mv_ref[0]
    outs = []
    for h in range(MEM_HEADS):
        sl = slice(h * MEM_DIM, (h + 1) * MEM_DIM)
        qh = _rms(q[:, sl], gmq).astype(BF)
        s = _nt(qh, mk[:, sl]) * (MEM_DIM ** -0.5)
        p = jnp.exp(s - jnp.max(s, axis=-1, keepdims=True))
        p = p / jnp.sum(p, axis=-1, keepdims=True)
        outs.append(_mm(p.astype(BF), mv[:, sl]))
    o = jnp.concatenate(outs, axis=1).astype(BF)
    o_ref[...] = h1 + _mm(o, wmo_ref[...])


def _mix_mem(x, a1, a2, wo1, wo2, gq, wmq, gmq, mk, mv, wmo, nb):
    t, d = x.shape
    s_len = t // nb
    tm = _tile(s_len, MIX_TM)
    nt = s_len // tm
    n_mem = mk.shape[1]
    mw = MEM_HEADS * MEM_DIM
    row = lambda i: (i, 0)
    const = lambda i: (0, 0)
    return pl.pallas_call(
        _mix_mem_body,
        grid=(t // tm,),
        in_specs=[
            pl.BlockSpec((tm, d), row),
            pl.BlockSpec((tm, a1.shape[1]), row),
            pl.BlockSpec((tm, a2.shape[1]), row),
            pl.BlockSpec(wo1.shape, const),
            pl.BlockSpec(wo2.shape, const),
            pl.BlockSpec((1, d), const),
            pl.BlockSpec(wmq.shape, const),
            pl.BlockSpec((1, MEM_DIM), const),
            pl.BlockSpec((1, n_mem, mw), lambda i: (i // nt, 0, 0)),
            pl.BlockSpec((1, n_mem, mw), lambda i: (i // nt, 0, 0)),
            pl.BlockSpec(wmo.shape, const),
        ],
        out_specs=pl.BlockSpec((tm, d), row),
        out_shape=jax.ShapeDtypeStruct((t, d), F32),
        compiler_params=_params(("arbitrary",)),
        name="mix_mem",
    )(x, a1, a2, wo1, wo2, gq, wmq, gmq, mk, mv, wmo)


def _ffn_body(x_ref, g_ref, w1_ref, w2_ref, o_ref, hn_scr):
    f = pl.program_id(1)

    @pl.when(f == 0)
    def _():
        x = x_ref[...]
        hn_scr[...] = _rms(x, g_ref[...]).astype(BF)
        o_ref[...] = x

    a = jnp.maximum(_mm(hn_scr[...], w1_ref[...]), 0.0)
    o_ref[...] += _mm((a * a).astype(BF), w2_ref[...])


def _ffn(x, g, w1, w2):
    t, d = x.shape
    dff = w1.shape[1]
    tm = _tile(t, FFN_TM)
    tf = _tile(dff, FFN_TF, mult=LANE)
    return pl.pallas_call(
        _ffn_body,
        grid=(t // tm, dff // tf),
        in_specs=[
            pl.BlockSpec((tm, d), lambda i, f: (i, 0)),
            pl.BlockSpec((1, d), lambda i, f: (0, 0)),
            pl.BlockSpec((d, tf), lambda i, f: (0, f)),
            pl.BlockSpec((tf, d), lambda i, f: (f, 0)),
        ],
        out_specs=pl.BlockSpec((tm, d), lambda i, f: (i, 0)),
        out_shape=jax.ShapeDtypeStruct((t, d), F32),
        scratch_shapes=[pltpu.VMEM((tm, d), BF)],
        compiler_params=_params(("arbitrary", "arbitrary")),
        name="ffn",
    )(x, g, w1, w2)


def _mem_kv_body(m_ref, g_ref, wk_ref, wv_ref, gk_ref, k_ref, v_ref):
    mn = _rms(m_ref[...], g_ref[...]).astype(BF)
    k = _mm(mn, wk_ref[...])
    gk = gk_ref[...]
    for h in range(MEM_HEADS):
        sl = slice(h * MEM_DIM, (h + 1) * MEM_DIM)
        k_ref[:, sl] = _rms(k[:, sl], gk)
    v_ref[...] = _mm(mn, wv_ref[...])


def _mem_kv(mem, g, wk, wv, gk):
    t, d = mem.shape
    tm = _tile(t, 256)
    mw = MEM_HEADS * MEM_DIM
    const = lambda i: (0, 0)
    return pl.pallas_call(
        _mem_kv_body,
        grid=(t // tm,),
        in_specs=[
            pl.BlockSpec((tm, d), lambda i: (i, 0)),
            pl.BlockSpec((1, d), const),
            pl.BlockSpec(wk.shape, const),
            pl.BlockSpec(wv.shape, const),
            pl.BlockSpec((1, MEM_DIM), const),
        ],
        out_specs=[pl.BlockSpec((tm, mw), lambda i: (i, 0)), pl.BlockSpec((tm, mw), lambda i: (i, 0))],
        out_shape=[jax.ShapeDtypeStruct((t, mw), F32), jax.ShapeDtypeStruct((t, mw), F32)],
        compiler_params=_params(("arbitrary",)),
        name="mem_kv",
    )(mem, g, wk, wv, gk)


def _rope_tables(pos):
    half = QK_ROPE // 2
    inv_freq = ROPE_THETA ** (-np.arange(half, dtype=np.float64) / half)
    ang = np.asarray(pos, np.float64)[:, None] * inv_freq[None, :]
    cos, sin = np.cos(ang), np.sin(ang)
    zeros = np.zeros((ang.shape[0], LANE - QK_ROPE))
    return (jnp.asarray(np.concatenate([cos, cos, zeros], axis=1), F32),
            jnp.asarray(np.concatenate([-sin, sin, zeros], axis=1), F32))


def _pad_lanes(v, n=LANE):
    return jnp.pad(v, ((0, 0), (0, n - v.shape[1])))


def _prep_weights(w_in, g_cq, w_uq, g_ckv, w_ukv, g_q, g_k, w_o, w_mq, w_mo, w_ff1, w_ff2, w_mk, w_mv):
    d = w_in.shape[0]
    n_conv = 2 * GDN_HEADS * GDN_DK + GDN_HEADS * GDN_DV
    n_z = GDN_HEADS * GDN_DV
    wt = w_in.T.astype(BF)
    o = 0
    w_cq = wt[o:o + Q_RANK]; o += Q_RANK
    w_ckv = wt[o:o + KV_RANK]; o += KV_RANK
    w_kpe = wt[o:o + QK_ROPE]; o += QK_ROPE
    w_qkv = wt[o:o + n_conv]; o += n_conv
    w_z = wt[o:o + n_z]; o += n_z
    w_b = wt[o:o + GDN_HEADS]; o += GDN_HEADS
    w_a = wt[o:o + GDN_HEADS]
    half = QK_ROPE // 2
    swap = jnp.concatenate([jnp.arange(half, QK_ROPE), jnp.arange(0, half)])
    z64 = jnp.zeros((LANE - QK_ROPE, d), BF)
    misc = jnp.concatenate([w_kpe, z64, w_kpe[swap], z64, w_b, w_a], axis=0)
    misc = jnp.pad(misc, ((0, MISC_ROWS - misc.shape[0]), (0, 0)))
    w_in_p = jnp.concatenate([w_cq, w_ckv, w_qkv, w_z, misc], axis=0)
    wbat = jnp.concatenate([w_b, w_a], axis=0)

    r = w_uq.shape[0]
    wq3 = w_uq.reshape(r, MLA_HEADS, QK_HEAD)
    zq = jnp.zeros((r, MLA_HEADS, LANE - QK_ROPE), w_uq.dtype)
    w1 = jnp.concatenate([wq3, zq], axis=2).reshape(r, MLA_HEADS * QPAD).astype(BF)
    w2 = jnp.concatenate([wq3[:, :, QK_NOPE:][:, :, swap], zq], axis=2).reshape(r, MLA_HEADS * LANE).astype(BF)

    dmla = MLA_HEADS * V_DIM
    return dict(
        n_conv=n_conv, n_z=n_z, w_in_p=w_in_p, wbat=wbat, w1=w1, w2=w2,
        gcq=g_cq[None, :], gckv=g_ckv[None, :],
        gq_n=g_q[None, :QK_NOPE], gq_r=_pad_lanes(g_q[None, QK_NOPE:]),
        gk_n=g_k[None, :QK_NOPE], gk_r=_pad_lanes(g_k[None, QK_NOPE:]),
        w_ukv=w_ukv.astype(BF), wo1=w_o[:dmla].astype(BF), wo2=w_o[dmla:].astype(BF),
        w_mq=w_mq.astype(BF), w_mo=w_mo.astype(BF), w_ff1=w_ff1.astype(BF), w_ff2=w_ff2.astype(BF),
        w_mk=w_mk.astype(BF), w_mv=w_mv.astype(BF),
    )


def _gate_params(a_log, dt_bias):
    z8 = jnp.zeros((GDN_HEADS,), F32)
    al = jnp.concatenate([z8, a_log.astype(F32)])
    db = jnp.concatenate([z8, dt_bias.astype(F32)])
    return _pad_lanes(al[None, :]), _pad_lanes(db[None, :]), al[:, None], db[:, None]


def _layer(x, pos, past, conv_past, s0, mem_k, mem_v, wp, lw):
    nb, s_len, d = x.shape
    t = nb * s_len
    xf = x.reshape(t, d)
    ctab, stab = _rope_tables(pos)
    if ctab.shape[0] % 8 != 0 or (s_len < IN_TM and nb > 1):
        ctab, stab = jnp.tile(ctab, (nb, 1)), jnp.tile(stab, (nb, 1))

    cqn, ckv, kpe_pad, qkv, z, ba, bat = _in_proj(
        xf, lw["g_norm_mix"], wp["w_in_p"], wp["wbat"], wp["gcq"], wp["gckv"], ctab, stab,
        wp["n_conv"], wp["n_z"])
    q = _q_proj(cqn, wp["w1"], wp["w2"], ctab, stab, wp["gq_n"], wp["gq_r"])

    if past is None:
        k, v = _kv_proj(ckv, kpe_pad, wp["w_ukv"], wp["gk_n"], wp["gk_r"])
        o_mla = _attention(q, k, v, lw["g_mla_out"], nb)
    else:
        ckv_past, kpe_past = past
        o_mla = _cached_attention(q, ckv_past.astype(F32), kpe_past.astype(F32), ckv, kpe_pad, wp["w_ukv"],
                                  wp["gk_n"], wp["gk_r"], lw["g_mla_out"], nb)

    conv_past8 = jnp.pad(conv_past.astype(F32), ((0, 0), (HALO - (CONV_W - 1), 0), (0, 0)))
    conv_past8 = conv_past8.reshape(nb * HALO, -1)
    bat3 = bat.reshape(16, nb, s_len).transpose(1, 0, 2)
    alog_l, dtb_l, alog_c, dtb_c = _gate_params(lw["a_log"], lw["dt_bias"])
    o_gdn, s_new = _gdn(qkv, lw["conv_w"], conv_past8, ba, bat3, alog_l, dtb_l, alog_c, dtb_c,
                        z, s0.astype(F32), lw["g_gdn_out"], nb)

    h2 = _mix_mem(xf, o_mla, o_gdn, wp["wo1"], wp["wo2"], lw["g_norm_mem_q"], wp["w_mq"],
                  lw["g_mq"], mem_k, mem_v, wp["w_mo"], nb)
    y = _ffn(h2, lw["g_norm_ffn"], wp["w_ff1"], wp["w_ff2"])

    conv_in_tail = jnp.concatenate([conv_past.astype(F32), qkv.reshape(nb, s_len, -1)[:, -(CONV_W - 1):]], axis=1)
    conv_new = conv_in_tail[:, -(CONV_W - 1):]
    return (y.reshape(nb, s_len, d), ckv.reshape(nb, s_len, KV_RANK),
            kpe_pad[:, :QK_ROPE].reshape(nb, s_len, QK_ROPE), conv_new, s_new)


def kernel(x_prompt, x_sample, mem_prompt, cache_mla_ckv, cache_mla_kpe, cache_gdn_conv, state_gdn, cache_mem_k, cache_mem_v, g_norm_mix, w_in, g_cq, w_uq, g_ckv, w_ukv, g_q_mla, g_k_mla, g_mla_out, conv_w, a_log, dt_bias, g_gdn_out, w_o, g_norm_mem_q, g_norm_mem_kv, w_mq, w_mk, w_mv, g_mq, g_mk, w_mo, g_norm_ffn, w_ff1, w_ff2):
    depth = w_in.shape[0]
    nbp, sp, d = x_prompt.shape
    nbs, ss, _ = x_sample.shape
    n_mem = mem_prompt.shape[1]
    mw = MEM_HEADS * MEM_DIM
    n_conv = 2 * GDN_HEADS * GDN_DK + GDN_HEADS * GDN_DV
    pos_p = np.arange(sp)
    pos_s = cache_mla_ckv.shape[2] + np.arange(ss)
    zeros_conv = jnp.zeros((nbp, CONV_W - 1, n_conv), F32)
    zeros_state = jnp.zeros((nbp, GDN_HEADS, GDN_DK, GDN_DV), F32)
    hp, hs = x_prompt, x_sample
    outs_p = [[] for _ in range(6)]
    outs_s = [[] for _ in range(4)]
    for l in range(depth):
        wp = _prep_weights(w_in[l], g_cq[l], w_uq[l], g_ckv[l], w_ukv[l], g_q_mla[l], g_k_mla[l], w_o[l],
                           w_mq[l], w_mo[l], w_ff1[l], w_ff2[l], w_mk[l], w_mv[l])
        lw = dict(g_norm_mix=g_norm_mix[l][None, :], g_mla_out=g_mla_out[l][None, :], conv_w=conv_w[l],
                  a_log=a_log[l], dt_bias=dt_bias[l], g_gdn_out=g_gdn_out[l][None, :],
                  g_norm_mem_q=g_norm_mem_q[l][None, :], g_mq=g_mq[l][None, :],
                  g_norm_ffn=g_norm_ffn[l][None, :])
        mk, mv = _mem_kv(mem_prompt.reshape(nbp * n_mem, d), g_norm_mem_kv[l][None, :], wp["w_mk"], wp["w_mv"],
                         g_mk[l][None, :])
        mk3, mv3 = mk.reshape(nbp, n_mem, mw), mv.reshape(nbp, n_mem, mw)
        hp, c1, c2, c3, c4 = _layer(hp, pos_p, None, zeros_conv, zeros_state, mk3.astype(BF), mv3.astype(BF), wp, lw)
        for lst, val in zip(outs_p, (c1, c2, c3, c4, mk3.reshape(nbp, n_mem, MEM_HEADS, MEM_DIM),
                                     mv3.reshape(nbp, n_mem, MEM_HEADS, MEM_DIM))):
            lst.append(val)
        hs, d1, d2, d3, d4 = _layer(hs, pos_s, (cache_mla_ckv[l], cache_mla_kpe[l]), cache_gdn_conv[l], state_gdn[l],
                                    cache_mem_k[l].reshape(nbs, n_mem, mw).astype(BF),
                                    cache_mem_v[l].reshape(nbs, n_mem, mw).astype(BF), wp, lw)
        for lst, val in zip(outs_s, (d1, d2, d3, d4)):
            lst.append(val)
    return (hp, hs, *(jnp.stack(v) for v in outs_p), *(jnp.stack(v) for v in outs_s))
```

```python
import functools
import math

import jax
import jax.numpy as jnp
import numpy as np
from jax import lax
from jax.experimental import pallas as pl
from jax.experimental.pallas import tpu as pltpu

F32 = jnp.float32
BF = jnp.bfloat16

EPS = 1e-6
CHUNK = 64
ROPE_THETA = 10000.0
MLA_HEADS = 8
QK_NOPE = 128
QK_ROPE = 64
QK_HEAD = QK_NOPE + QK_ROPE
V_DIM = 128
Q_RANK = 512
KV_RANK = 512
GDN_HEADS = 8
GDN_DK = 128
GDN_DV = 128
CONV_W = 4
MEM_HEADS = 4
MEM_DIM = 128
LANE = 128
QPAD = 2 * LANE
COL_TILE = 512
MISC_ROWS = 3 * LANE
HALO = 8
NEG_BIG = -1e30
QSCALE = (QK_HEAD ** -0.5) * math.log2(math.e)

VMEM_LIMIT = 56 * 1024 * 1024

IN_TM = 512
PROJ_TM = 1024
ATTN_TQ = 1024
ATTN_TK = 1024
ATTN_WIDE = 3
GDN_TM = 256
GDN_WAVES = 2
MIX_TM = 512
FFN_TM = 1024
FFN_TF = 512


def _tile(n, pref, mult=8):
    if n <= pref:
        return n
    t = (pref // mult) * mult
    while t >= mult:
        if n % t == 0:
            return t
        t -= mult
    return n


def _nt(a, b):
    return lax.dot_general(a, b, (((1,), (1,)), ((), ())), preferred_element_type=F32)


def _mm(a, b):
    return jnp.dot(a, b, preferred_element_type=F32)


def _sigmoid(x):
    return 1.0 / (1.0 + jnp.exp(-x))


def _softplus(x):
    return jnp.maximum(x, 0.0) + jnp.log(1.0 + jnp.exp(-jnp.abs(x)))


def _rms(x, g, n=None):
    n = x.shape[-1] if n is None else n
    ms = jnp.sum(x * x, axis=-1, keepdims=True) * (1.0 / n)
    return (x * lax.rsqrt(ms + EPS)) * g


def _split3(x):
    hi = x.astype(BF)
    r1 = x - hi.astype(F32)
    mid = r1.astype(BF)
    lo = (r1 - mid.astype(F32)).astype(BF)
    return hi, mid, lo


def _interleave(lists):
    items = []
    for li, lst in enumerate(lists):
        items += [((i + 0.5) / len(lst), li, f) for i, f in enumerate(lst)]
    return [f for _, _, f in sorted(items, key=lambda it: (it[0], it[1]))]


def _params(sem):
    return pltpu.CompilerParams(dimension_semantics=sem, vmem_limit_bytes=VMEM_LIMIT)


def _in_proj_body(x_ref, g_ref, w_ref, wbat_ref, gcq_ref, gckv_ref, ctab_ref, stab_ref,
                  cqn_ref, ckv_ref, kpe_ref, qkv_ref, z_ref, ba_ref, bat_ref, xn_scr, *, n_conv, n_z):
    xn_scr[...] = _rms(x_ref[...], g_ref[...]).astype(BF)

    def prod(r0, n):
        return _nt(xn_scr[...], w_ref[r0:r0 + n, :])

    cqn_ref[...] = _rms(prod(0, Q_RANK), gcq_ref[...]).astype(BF)
    ckv_ref[...] = _rms(prod(Q_RANK, KV_RANK), gckv_ref[...])
    r0 = Q_RANK + KV_RANK
    for c0 in range(0, n_conv, COL_TILE):
        qkv_ref[:, c0:c0 + COL_TILE] = prod(r0 + c0, COL_TILE)
    r0 += n_conv
    for c0 in range(0, n_z, COL_TILE):
        z_ref[:, c0:c0 + COL_TILE] = prod(r0 + c0, COL_TILE).astype(BF)
    r0 += n_z
    acc = prod(r0, MISC_ROWS)
    kpe_ref[...] = acc[:, 0:LANE] * ctab_ref[...] + acc[:, LANE:2 * LANE] * stab_ref[...]
    ba_ref[...] = acc[:, 2 * LANE:3 * LANE]
    bat_ref[...] = _nt(wbat_ref[...], xn_scr[...])


def _in_proj(x, g, w_p, wbat, gcq, gckv, ctab, stab, n_conv, n_z):
    t, d = x.shape
    tm = _tile(min(t, ctab.shape[0]), IN_TM)
    ntab = ctab.shape[0] // tm
    assert w_p.shape[0] == Q_RANK + KV_RANK + n_conv + n_z + MISC_ROWS
    assert n_conv % COL_TILE == 0 and n_z % COL_TILE == 0
    row = lambda i: (i, 0)
    const = lambda i: (0, 0)
    body = functools.partial(_in_proj_body, n_conv=n_conv, n_z=n_z)
    return pl.pallas_call(
        body,
        grid=(t // tm,),
        in_specs=[
            pl.BlockSpec((tm, d), row),
            pl.BlockSpec((1, d), const),
            pl.BlockSpec(w_p.shape, const, pipeline_mode=pl.Buffered(1)),
            pl.BlockSpec((16, d), const),
            pl.BlockSpec((1, Q_RANK), const),
            pl.BlockSpec((1, KV_RANK), const),
            pl.BlockSpec((tm, LANE), lambda i: (i % ntab, 0)),
            pl.BlockSpec((tm, LANE), lambda i: (i % ntab, 0)),
        ],
        out_specs=[
            pl.BlockSpec((tm, Q_RANK), row),
            pl.BlockSpec((tm, KV_RANK), row),
            pl.BlockSpec((tm, LANE), row),
            pl.BlockSpec((tm, n_conv), row),
            pl.BlockSpec((tm, n_z), row),
            pl.BlockSpec((tm, LANE), row),
            pl.BlockSpec((16, tm), lambda i: (0, i)),
        ],
        out_shape=[
            jax.ShapeDtypeStruct((t, Q_RANK), BF),
            jax.ShapeDtypeStruct((t, KV_RANK), F32),
            jax.ShapeDtypeStruct((t, LANE), F32),
            jax.ShapeDtypeStruct((t, n_conv), F32),
            jax.ShapeDtypeStruct((t, n_z), BF),
            jax.ShapeDtypeStruct((t, LANE), F32),
            jax.ShapeDtypeStruct((16, t), F32),
        ],
        scratch_shapes=[pltpu.VMEM((tm, d), BF)],
        compiler_params=_params(("arbitrary",)),
        name="in_proj",
    )(x, g, w_p, wbat, gcq, gckv, ctab, stab)


def _q_proj_body(c_ref, w1_ref, w2_ref, ctab_ref, stab_ref, gn_ref, gr_ref, q_ref):
    c = c_ref[...]
    qf = _mm(c, w1_ref[...])
    qs = _mm(c, w2_ref[...])
    ct = ctab_ref[...]
    st = stab_ref[...]
    gn = gn_ref[...]
    gr = gr_ref[...]
    for h in range(MLA_HEADS):
        nope = qf[:, h * QPAD:h * QPAD + LANE]
        rot = qf[:, h * QPAD + LANE:(h + 1) * QPAD] * ct + qs[:, h * LANE:(h + 1) * LANE] * st
        ss = jnp.sum(nope * nope + rot * rot, axis=-1, keepdims=True)
        rs = lax.rsqrt(ss * (1.0 / QK_HEAD) + EPS) * QSCALE
        q_ref[:, h * QPAD:h * QPAD + LANE] = (nope * rs * gn).astype(BF)
        q_ref[:, h * QPAD + LANE:(h + 1) * QPAD] = (rot * rs * gr).astype(BF)


def _q_proj(cqn, w1, w2, ctab, stab, gn, gr):
    t = cqn.shape[0]
    tm = _tile(min(t, ctab.shape[0]), PROJ_TM)
    ntab = ctab.shape[0] // tm
    const = lambda i: (0, 0)
    return pl.pallas_call(
        _q_proj_body,
        grid=(t // tm,),
        in_specs=[
            pl.BlockSpec((tm, Q_RANK), lambda i: (i, 0)),
            pl.BlockSpec(w1.shape, const),
            pl.BlockSpec(w2.shape, const),
            pl.BlockSpec((tm, LANE), lambda i: (i % ntab, 0)),
            pl.BlockSpec((tm, LANE), lambda i: (i % ntab, 0)),
            pl.BlockSpec((1, LANE), const),
            pl.BlockSpec((1, LANE), const),
        ],
        out_specs=pl.BlockSpec((tm, MLA_HEADS * QPAD), lambda i: (i, 0)),
        out_shape=jax.ShapeDtypeStruct((t, MLA_HEADS * QPAD), BF),
        compiler_params=_params(("arbitrary",)),
        name="q_proj",
    )(cqn, w1, w2, ctab, stab, gn, gr)


def _kv_proj_body(c_ref, kpe_ref, w_ref, gn_ref, gr_ref, k_ref, v_ref):
    kv = _mm(c_ref[...].astype(BF), w_ref[...])
    kp = kpe_ref[...]
    kps = jnp.sum(kp * kp, axis=-1, keepdims=True)
    gn = gn_ref[...]
    gr = gr_ref[...]
    for h in range(MLA_HEADS):
        kn = kv[:, h * 2 * LANE:h * 2 * LANE + LANE]
        rs = lax.rsqrt((jnp.sum(kn * kn, axis=-1, keepdims=True) + kps) * (1.0 / QK_HEAD) + EPS)
        k_ref[:, h * QPAD:h * QPAD + LANE] = (kn * rs * gn).astype(BF)
        k_ref[:, h * QPAD + LANE:(h + 1) * QPAD] = (kp * rs * gr).astype(BF)
        v_ref[:, h * V_DIM:(h + 1) * V_DIM] = kv[:, h * 2 * LANE + LANE:(h + 1) * 2 * LANE].astype(BF)


def _kv_proj(ckv, kpe_pad, w, gn, gr):
    t = ckv.shape[0]
    tm = _tile(t, PROJ_TM)
    const = lambda i: (0, 0)
    return pl.pallas_call(
        _kv_proj_body,
        grid=(t // tm,),
        in_specs=[
            pl.BlockSpec((tm, KV_RANK), lambda i: (i, 0)),
            pl.BlockSpec((tm, LANE), lambda i: (i, 0)),
            pl.BlockSpec(w.shape, const),
            pl.BlockSpec((1, LANE), const),
            pl.BlockSpec((1, LANE), const),
        ],
        out_specs=[
            pl.BlockSpec((tm, MLA_HEADS * QPAD), lambda i: (i, 0)),
            pl.BlockSpec((tm, MLA_HEADS * V_DIM), lambda i: (i, 0)),
        ],
        out_shape=[
            jax.ShapeDtypeStruct((t, MLA_HEADS * QPAD), BF),
            jax.ShapeDtypeStruct((t, MLA_HEADS * V_DIM), BF),
        ],
        compiler_params=_params(("arbitrary",)),
        name="kv_proj",
    )(ckv, kpe_pad, w, gn, gr)


def _attn_body(q_ref, k_ref, v_ref, g_ref, o_ref, m_scr, l_scr, acc_scr, *, tq, tk, wide, t_valid):
    qi = pl.program_id(2)
    qpos0 = qi * tq
    n_full = jnp.minimum((qpos0 // CHUNK * CHUNK + CHUNK) // tk, t_valid // tk)
    hi = jnp.minimum((qpos0 + tq - 1) // CHUNK * CHUNK + CHUNK, t_valid)
    n_total = (hi + tk - 1) // tk

    m_scr[...] = jnp.full(m_scr.shape, NEG_BIG, F32)
    l_scr[...] = jnp.zeros(l_scr.shape, F32)
    acc_scr[...] = jnp.zeros(acc_scr.shape, F32)
    diag = tq == tk and t_valid % tk == 0 and tk % (2 * LANE) == 0 and tq % 32 == 0

    def step(kc, width, masked, r0=0, nr=tq):
        rows = slice(r0, r0 + nr)
        k0 = pl.multiple_of(kc * tk, tk)
        s = _nt(q_ref[rows, :], k_ref[pl.ds(k0, width), :])
        if masked:
            qpos = qpos0 + r0 + lax.broadcasted_iota(jnp.int32, (nr, 1), 0)
            last = jnp.minimum(qpos | (CHUNK - 1), t_valid - 1) - k0
            s = jnp.where(lax.broadcasted_iota(jnp.int32, (nr, width), 1) <= last, s, NEG_BIG)
        m_prev = m_scr[rows, :]
        m_new = jnp.maximum(m_prev, jnp.max(s, axis=-1, keepdims=True))
        alpha = jnp.exp2(m_prev - m_new)
        ps = [jnp.exp2(s[:, j * LANE:(j + 1) * LANE] - m_new) for j in range(width // LANE)]
        psum = ps[0]
        for pj in ps[1:]:
            psum = psum + pj
        l_scr[rows, :] = alpha * l_scr[rows, :] + psum
        p = jnp.concatenate(ps, axis=1).astype(BF) if len(ps) > 1 else ps[0].astype(BF)
        acc_scr[rows, :] = acc_scr[rows, :] * alpha + _mm(p, v_ref[pl.ds(k0, width), :])
        m_scr[rows, :] = m_new

    def loop(lo, hi, fn):
        lax.fori_loop(lo, hi, lambda i, c: (fn(i), c)[1], 0)

    done = 0
    for w in range(wide, 0, -1):
        cnt = (n_full - done) // w
        loop(0, cnt, lambda i, w=w, done=done: step(done + i * w, w * tk, False))
        done = done + cnt * w
    if diag:
        def masked(kc):
            step(kc, tk // 2, True, 0, tq // 2)
            step(kc, tk, True, tq // 2, tq // 2)
    else:
        def masked(kc):
            step(kc, tk, True)
    loop(n_full, n_total, masked)
    o = acc_scr[...] / jnp.sum(l_scr[...], axis=-1, keepdims=True)
    o_ref[...] = _rms(o, g_ref[...]).astype(BF)


def _attention(q, k, v, g_out, nb):
    tq_total = q.shape[0] // nb
    tk_total = k.shape[0] // nb
    assert tq_total == tk_total
    tq = _tile(tq_total, ATTN_TQ)
    tk = _tile(tk_total, ATTN_TK, mult=LANE)
    nq = tq_total // tq
    body = functools.partial(_attn_body, tq=tq, tk=tk, wide=ATTN_WIDE, t_valid=tk_total)
    return pl.pallas_call(
        body,
        grid=(nb, MLA_HEADS, nq),
        in_specs=[
            pl.BlockSpec((tq, QPAD), lambda b, h, i: (b * nq + i, h)),
            pl.BlockSpec((tk_total, QPAD), lambda b, h, i: (b, h)),
            pl.BlockSpec((tk_total, V_DIM), lambda b, h, i: (b, h)),
            pl.BlockSpec((1, V_DIM), lambda b, h, i: (0, 0)),
        ],
        out_specs=pl.BlockSpec((tq, V_DIM), lambda b, h, i: (b * nq + i, h)),
        out_shape=jax.ShapeDtypeStruct((q.shape[0], MLA_HEADS * V_DIM), BF),
        scratch_shapes=[
            pltpu.VMEM((tq, LANE), F32),
            pltpu.VMEM((tq, LANE), F32),
            pltpu.VMEM((tq, V_DIM), F32),
        ],
        compiler_params=_params(("arbitrary", "arbitrary", "arbitrary")),
        name="mla_attention",
    )(q, k, v, g_out)


def _cached_attn_body(q_ref, cp_ref, kp_ref, cn_ref, kn_ref, w_ref, gn_ref, gr_ref, g_ref, o_ref, *, p_len, s_len):
    cp = cp_ref[0].astype(BF)
    cn = cn_ref[...].astype(BF)
    kpp = kp_ref[0]
    kpn = kn_ref[:, :QK_ROPE]
    kps_p = jnp.sum(kpp * kpp, axis=-1, keepdims=True)
    kps_n = jnp.sum(kpn * kpn, axis=-1, keepdims=True)
    gn = gn_ref[...]
    gr = gr_ref[:, :QK_ROPE]
    g_out = g_ref[...]
    qpos = p_len + lax.broadcasted_iota(jnp.int32, (s_len, 1), 0)
    kpos = p_len + lax.broadcasted_iota(jnp.int32, (1, s_len), 1)
    ok_new = kpos <= (qpos | (CHUNK - 1))

    def keys_values(kv, kp, kps):
        kn = kv[:, :QK_NOPE]
        rs = lax.rsqrt((jnp.sum(kn * kn, axis=-1, keepdims=True) + kps) * (1.0 / QK_HEAD) + EPS)
        return (kn * rs * gn).astype(BF), (kp * rs * gr).astype(BF), kv[:, QK_NOPE:].astype(BF)

    hw = 2 * LANE
    for h in range(MLA_HEADS):
        if h % 2 == 0:
            kv2p = _mm(cp, w_ref[:, h * hw:(h + 2) * hw])
            kv2n = _mm(cn, w_ref[:, h * hw:(h + 2) * hw])
        kb, krb, vb = keys_values(kv2p[:, (h % 2) * hw:(h % 2 + 1) * hw], kpp, kps_p)
        kbn, krbn, vbn = keys_values(kv2n[:, (h % 2) * hw:(h % 2 + 1) * hw], kpn, kps_n)
        qn = q_ref[:, h * QPAD:h * QPAD + QK_NOPE]
        qr = q_ref[:, h * QPAD + QK_NOPE:h * QPAD + QK_HEAD]
        s_p = _nt(qn, kb) + _nt(qr, krb)
        s_n = jnp.where(ok_new, _nt(qn, kbn) + _nt(qr, krbn), NEG_BIG)
        m = jnp.maximum(jnp.max(s_p, axis=-1, keepdims=True), jnp.max(s_n, axis=-1, keepdims=True))
        pp = jnp.exp2(s_p - m)
        pn = jnp.exp2(s_n - m)
        l = jnp.sum(pp, axis=-1, keepdims=True) + jnp.sum(pn, axis=-1, keepdims=True)
        o = (_mm(pp.astype(BF), vb) + _mm(pn.astype(BF), vbn)) / l
        o_ref[:, h * V_DIM:(h + 1) * V_DIM] = _rms(o, g_out).astype(BF)


def _cached_attention(q, ckv_past, kpe_past, ckv_new, kpe_new, w, gn, gr, g_out, nb):
    s_len = q.shape[0] // nb
    p_len = ckv_past.shape[1]
    assert s_len % 16 == 0 or nb == 1
    body = functools.partial(_cached_attn_body, p_len=p_len, s_len=s_len)
    const = lambda b: (0, 0)
    return pl.pallas_call(
        body,
        grid=(nb,),
        in_specs=[
            pl.BlockSpec((s_len, MLA_HEADS * QPAD), lambda b: (b, 0)),
            pl.BlockSpec((1, p_len, KV_RANK), lambda b: (b, 0, 0)),
            pl.BlockSpec((1, p_len, QK_ROPE), lambda b: (b, 0, 0)),
            pl.BlockSpec((s_len, KV_RANK), lambda b: (b, 0)),
            pl.BlockSpec((s_len, LANE), lambda b: (b, 0)),
            pl.BlockSpec(w.shape, const),
            pl.BlockSpec((1, LANE), const),
            pl.BlockSpec((1, LANE), const),
            pl.BlockSpec((1, V_DIM), const),
        ],
        out_specs=pl.BlockSpec((s_len, MLA_HEADS * V_DIM), lambda b: (b, 0)),
        out_shape=jax.ShapeDtypeStruct((q.shape[0], MLA_HEADS * V_DIM), BF),
        compiler_params=_params(("arbitrary",)),
        name="cached_attention",
    )(q, ckv_past, kpe_past, ckv_new, kpe_new, w, gn, gr, g_out)


def _gdn_body(q_ref, k_ref, v_ref, wq_ref, wk_ref, wv_ref, pq_ref, pk_ref, pv_ref,
              ba_ref, bat_ref, alog_l_ref, dtb_l_ref, alog_c_ref, dtb_c_ref,
              z_ref, s0_ref, gout_ref,
              o_ref, sout_ref,
              s_scr, halo_scr, vnew_scr, *, tm, L, hg):
    t = pl.program_id(2)
    nt = pl.num_programs(2)
    n_chunks = tm // L
    n_factors = int(math.log2(L)) - 1

    @pl.when(t == 0)
    def _():
        s_scr[...] = s0_ref[0]
        halo_scr[0] = pq_ref[...]
        halo_scr[1] = pk_ref[...]
        halo_scr[2] = pv_ref[...]
        vnew_scr[...] = jnp.zeros(vnew_scr.shape, BF)

    ba = ba_ref[...]
    beta_all = _sigmoid(ba)
    g_all = -jnp.exp(alog_l_ref[...]) * _softplus(ba + dtb_l_ref[...])
    gt_all = -jnp.exp(alog_c_ref[...]) * _softplus(bat_ref[0] + dtb_c_ref[...])

    r = lax.broadcasted_iota(jnp.int32, (tm, tm), 0)
    c = lax.broadcasted_iota(jnp.int32, (tm, tm), 1)
    same = (r // L) == (c // L)
    lower = same & (c <= r)
    strict = same & (c < r)
    cs = jnp.where(lower, 1.0, 0.0).astype(BF)
    bd = jnp.where(same, 1.0, 0.0).astype(BF)
    g3 = _split3(g_all)
    gcol_all = _mm(cs, g3[0]) + _mm(cs, g3[1]) + _mm(cs, g3[2])
    glast_all = _mm(bd, g3[0]) + _mm(bd, g3[1]) + _mm(bd, g3[2])
    gt3 = _split3(gt_all)
    grow_all = _nt(gt3[0], cs) + _nt(gt3[1], cs) + _nt(gt3[2], cs)

    gout = gout_ref[...]
    eye_t = jnp.where(r == c, 1.0, 0.0)
    eye_k = jnp.where(lax.broadcasted_iota(jnp.int32, (GDN_DK, GDN_DK), 0)
                      == lax.broadcasted_iota(jnp.int32, (GDN_DK, GDN_DK), 1), 1.0, 0.0).astype(BF)
    col_chunk = lax.broadcasted_iota(jnp.int32, (1, tm), 1) // L

    def make_wave(heads):
        c0, c1 = heads[0] * GDN_DK, (heads[-1] + 1) * GDN_DK
        loc = lambda h: slice((h - heads[0]) * GDN_DK, (h - heads[0] + 1) * GDN_DK)
        w = {}

        def conv(idx, x_ref, w_ref):
            xs = jnp.concatenate([halo_scr[idx, :, c0:c1], x_ref[:, c0:c1]], axis=0)
            cw = w_ref[:, c0:c1]
            acc = xs * cw[0:1]
            for i in range(1, CONV_W):
                acc = pltpu.roll(acc, 1, axis=0) + xs * cw[i:i + 1]
            halo_scr[idx, :, c0:c1] = xs[tm:tm + HALO]
            y = acc[HALO:HALO + tm]
            return y * _sigmoid(y)

        def p_q():
            w["yq"] = conv(0, q_ref, wq_ref)

        def p_k():
            w["yk"] = conv(1, k_ref, wk_ref)

        def p_v():
            w["yv"] = conv(2, v_ref, wv_ref)

        def p_gate():
            w["beta"], w["gc"], w["glast"], w["gr"] = {}, {}, {}, {}
            for h in heads:
                g = GDN_HEADS + h
                w["beta"][h] = beta_all[:, h:h + 1]
                w["gc"][h] = gcol_all[:, g:g + 1]
                w["glast"][h] = glast_all[:, g:g + 1]
                w["gr"][h] = grow_all[g:g + 1, :]

        def p_norm():
            w["qn"], w["kn"] = {}, {}
            for h in heads:
                qh = w["yq"][:, loc(h)]
                kh = w["yk"][:, loc(h)]
                w["qn"][h] = qh * lax.rsqrt(jnp.sum(qh * qh, axis=-1, keepdims=True) + EPS) * (GDN_DK ** -0.5)
                w["kn"][h] = kh * lax.rsqrt(jnp.sum(kh * kh, axis=-1, keepdims=True) + EPS)
            kb = {h: w["kn"][h].astype(BF) for h in heads}
            w["kk"] = {h: _nt(kb[h], kb[h]) for h in heads}
            w["qk"] = {h: _nt(w["qn"][h].astype(BF), kb[h]) for h in heads}

        def p_mat():
            beta, gc = w["beta"], w["gc"]
            decay = {h: jnp.where(lower, jnp.exp(gc[h] - w["gr"][h]), 0.0) for h in heads}
            w["nm"] = {h: jnp.where(strict, beta[h] * w["kk"][h] * decay[h], 0.0) for h in heads}
            w["attn"] = {h: (w["qk"][h] * decay[h]).astype(BF) for h in heads}
            w["eg"] = {h: jnp.exp(gc[h]) for h in heads}
            w["rhs"] = {h: jnp.concatenate([w["yv"][:, loc(h)] * beta[h],
                                            w["kn"][h] * (beta[h] * w["eg"][h])], axis=1) for h in heads}
            w["nb"] = {h: w["nm"][h].astype(BF) for h in heads}

        def s_first():
            w["m"] = {h: _mm(w["nb"][h], w["nb"][h]) for h in heads}
            w["T"] = {h: eye_t - w["nm"][h] for h in heads}

        def s_level(s):
            xb = {h: jnp.concatenate([w["m"][h], w["T"][h]], axis=0).astype(BF) for h in heads}
            if s < n_factors - 1:
                prod = {h: _mm(xb[h], xb[h][:tm]) for h in heads}
                w["T"] = {h: w["T"][h] + prod[h][tm:] for h in heads}
                w["m"] = {h: prod[h][:tm] for h in heads}
            else:
                w["T"] = {h: w["T"][h] + _mm(xb[h][tm:], xb[h][:tm]) for h in heads}

        def s_y0():
            w["tb"] = {h: w["T"][h].astype(BF) for h in heads}
            w["y0"] = {h: _mm(w["tb"][h], w["rhs"][h].astype(BF)) for h in heads}

        def s_res():
            res = {}
            for h in heads:
                nbf = w["nb"][h].astype(F32)
                nx = jnp.concatenate([nbf, w["nm"][h] - nbf], axis=0).astype(BF)
                yh = w["y0"][h].astype(BF)
                yl = (w["y0"][h] - yh.astype(F32)).astype(BF)
                p1 = _mm(nx, yh)
                res[h] = w["rhs"][h] - w["y0"][h] - (p1[:tm] + p1[tm:] + _mm(nx[:tm], yl))
            w["res"] = res

        def s_fix():
            w["y"] = {h: w["y0"][h] + _mm(w["tb"][h], w["res"][h].astype(BF)) for h in heads}

        def c_prep():
            y = w["y"]
            w["u"] = {h: y[h][:, :GDN_DV] for h in heads}
            w["wq"] = {}
            for h in heads:
                qd = w["qn"][h] * w["eg"][h]
                parts = []
                for ci in range(n_chunks):
                    parts += [y[h][ci * L:(ci + 1) * L, GDN_DV:], qd[ci * L:(ci + 1) * L]]
                w["wq"][h] = jnp.concatenate(parts, axis=0).astype(BF)
            kdb = {h: (w["kn"][h] * jnp.exp(w["glast"][h] - w["gc"][h])).astype(BF) for h in heads}
            w["gtot"] = {h: jnp.exp(w["glast"][h]) for h in heads}
            kdt = {h: _nt(eye_k, kdb[h]) for h in heads}
            w["kdt"] = {h: [jnp.where(col_chunk == ci, kdt[h], 0.0).astype(BF) for ci in range(n_chunks)]
                        for h in heads}
            w["st"] = {h: s_scr[h] for h in heads}
            w["outs"] = {h: [] for h in heads}

        def c_chunk(ci):
            r0 = ci * L
            st = w["st"]
            ws = {h: _mm(w["wq"][h][2 * r0:2 * r0 + 2 * L], st[h].astype(BF)) for h in heads}
            for h in heads:
                vnew_scr[h, r0:r0 + L, :] = (w["u"][h][r0:r0 + L] - ws[h][:L]).astype(BF)
            for h in heads:
                w["outs"][h].append(ws[h][L:] + _mm(w["attn"][h][r0:r0 + L, :], vnew_scr[h]))
            w["st"] = {h: st[h] * w["gtot"][h][r0:r0 + 1, :] + _mm(w["kdt"][h][ci], vnew_scr[h]) for h in heads}

        def c_out():
            for h in heads:
                s_scr[h] = w["st"][h]
                o = w["outs"][h][0] if n_chunks == 1 else jnp.concatenate(w["outs"][h], axis=0)
                zz = z_ref[:, h * GDN_DV:(h + 1) * GDN_DV].astype(F32)
                o_ref[:, h * GDN_DV:(h + 1) * GDN_DV] = (_rms(o, gout) * (zz * _sigmoid(zz))).astype(BF)

        prep = [p_q, p_k, p_v, p_gate, p_norm, p_mat]
        solve = ([s_first] + [functools.partial(s_level, s) for s in range(n_factors)] + [s_y0, s_res, s_fix])
        scan = [c_prep] + [functools.partial(c_chunk, ci) for ci in range(n_chunks)] + [c_out]
        return [prep, solve, scan]

    nw = min(GDN_WAVES, hg)
    per = hg // nw
    waves = [make_wave(list(range(i * per, (i + 1) * per))) for i in range(nw)]
    for slot in range(nw + 1):
        active = [waves[i][slot - i] for i in range(nw) if 0 <= slot - i < 2]
        for stage in _interleave(active):
            stage()
    for stage in _interleave([wv[2] for wv in waves]):
        stage()

    @pl.when(t == nt - 1)
    def _():
        sout_ref[0] = s_scr[...]


def _gdn(qkv, conv_w, conv_past8, ba, bat, alog_l, dtb_l, alog_c, dtb_c, z, s0, gout, nb):
    t = qkv.shape[0]
    s_len = t // nb
    L = min(s_len, CHUNK)
    tm = _tile(s_len, GDN_TM, mult=L)
    nt = s_len // tm
    hg = GDN_HEADS
    hw = hg * GDN_DK
    body = functools.partial(_gdn_body, tm=tm, L=L, hg=hg)
    rows = lambda off: (lambda b, h, i: (b * nt + i, off + h))
    wcol = lambda off: (lambda b, h, i: (0, off + h))
    pcol = lambda off: (lambda b, h, i: (b, off + h))
    const = lambda b, h, i: (0, 0)
    ng = GDN_HEADS // hg
    return pl.pallas_call(
        body,
        grid=(nb, ng, nt),
        in_specs=[
            pl.BlockSpec((tm, hw), rows(0)),
            pl.BlockSpec((tm, hw), rows(ng)),
            pl.BlockSpec((tm, hw), rows(2 * ng)),
            pl.BlockSpec((CONV_W, hw), wcol(0)),
            pl.BlockSpec((CONV_W, hw), wcol(ng)),
            pl.BlockSpec((CONV_W, hw), wcol(2 * ng)),
            pl.BlockSpec((HALO, hw), pcol(0)),
            pl.BlockSpec((HALO, hw), pcol(ng)),
            pl.BlockSpec((HALO, hw), pcol(2 * ng)),
            pl.BlockSpec((tm, LANE), lambda b, h, i: (b * nt + i, 0)),
            pl.BlockSpec((1, 16, tm), lambda b, h, i: (b, 0, i)),
            pl.BlockSpec((1, LANE), const),
            pl.BlockSpec((1, LANE), const),
            pl.BlockSpec((16, 1), const),
            pl.BlockSpec((16, 1), const),
            pl.BlockSpec((tm, hw), rows(0)),
            pl.BlockSpec((1, hg, GDN_DK, GDN_DV), lambda b, h, i: (b, h, 0, 0)),
            pl.BlockSpec((1, GDN_DV), const),
        ],
        out_specs=[
            pl.BlockSpec((tm, hw), rows(0)),
            pl.BlockSpec((1, hg, GDN_DK, GDN_DV), lambda b, h, i: (b, h, 0, 0)),
        ],
        out_shape=[
            jax.ShapeDtypeStruct((t, GDN_HEADS * GDN_DV), BF),
            jax.ShapeDtypeStruct((nb, GDN_HEADS, GDN_DK, GDN_DV), F32),
        ],
        scratch_shapes=[
            pltpu.VMEM((hg, GDN_DK, GDN_DV), F32),
            pltpu.VMEM((3, HALO, hw), F32),
            pltpu.VMEM((hg, tm, GDN_DV), BF),
        ],
        compiler_params=_params(("arbitrary", "arbitrary", "arbitrary")),
        name="gdn",
    )(qkv, qkv, qkv, conv_w, conv_w, conv_w, conv_past8, conv_past8, conv_past8,
      ba, bat, alog_l, dtb_l, alog_c, dtb_c, z, s0, gout)


def _mix_mem_body(x_ref, a1_ref, a2_ref, wo1_ref, wo2_ref, gq_ref, wmq_ref, gmq_ref,
                  mk_ref, mv_ref, wmo_ref, o_ref):
    h1 = x_ref[...] + _mm(a1_ref[...], wo1_ref[...]) + _mm(a2_ref[...], wo2_ref[...])
    hn = _rms(h1, gq_ref[...]).astype(BF)
    q = _mm(hn, wmq_ref[...])
    gmq = gmq_ref[...]
    mk = mk_ref[0]
    mv = mv_ref[0]
    outs = []
    for h in range(MEM_HEADS):
        sl = slice(h * MEM_DIM, (h + 1) * MEM_DIM)
        qh = _rms(q[:, sl], gmq).astype(BF)
        s = _nt(qh, mk[:, sl]) * (MEM_DIM ** -0.5)
        p = jnp.exp(s - jnp.max(s, axis=-1, keepdims=True))
        p = p / jnp.sum(p, axis=-1, keepdims=True)
        outs.append(_mm(p.astype(BF), mv[:, sl]))
    o = jnp.concatenate(outs, axis=1).astype(BF)
    o_ref[...] = h1 + _mm(o, wmo_ref[...])


def _mix_mem(x, a1, a2, wo1, wo2, gq, wmq, gmq, mk, mv, wmo, nb):
    t, d = x.shape
    s_len = t // nb
    tm = _tile(s_len, MIX_TM)
    nt = s_len // tm
    n_mem = mk.shape[1]
    mw = MEM_HEADS * MEM_DIM
    row = lambda i: (i, 0)
    const = lambda i: (0, 0)
    return pl.pallas_call(
        _mix_mem_body,
        grid=(t // tm,),
        in_specs=[
            pl.BlockSpec((tm, d), row),
            pl.BlockSpec((tm, a1.shape[1]), row),
            pl.BlockSpec((tm, a2.shape[1]), row),
            pl.BlockSpec(wo1.shape, const),
            pl.BlockSpec(wo2.shape, const),
            pl.BlockSpec((1, d), const),
            pl.BlockSpec(wmq.shape, const),
            pl.BlockSpec((1, MEM_DIM), const),
            pl.BlockSpec((1, n_mem, mw), lambda i: (i // nt, 0, 0)),
            pl.BlockSpec((1, n_mem, mw), lambda i: (i // nt, 0, 0)),
            pl.BlockSpec(wmo.shape, const),
        ],
        out_specs=pl.BlockSpec((tm, d), row),
        out_shape=jax.ShapeDtypeStruct((t, d), F32),
        compiler_params=_params(("arbitrary",)),
        name="mix_mem",
    )(x, a1, a2, wo1, wo2, gq, wmq, gmq, mk, mv, wmo)


def _ffn_body(x_ref, g_ref, w1_ref, w2_ref, o_ref, hn_scr):
    f = pl.program_id(1)

    @pl.when(f == 0)
    def _():
        x = x_ref[...]
        hn_scr[...] = _rms(x, g_ref[...]).astype(BF)
        o_ref[...] = x

    a = jnp.maximum(_mm(hn_scr[...], w1_ref[...]), 0.0)
    o_ref[...] += _mm((a * a).astype(BF), w2_ref[...])


def _ffn(x, g, w1, w2):
    t, d = x.shape
    dff = w1.shape[1]
    tm = _tile(t, FFN_TM)
    tf = _tile(dff, FFN_TF, mult=LANE)
    return pl.pallas_call(
        _ffn_body,
        grid=(t // tm, dff // tf),
        in_specs=[
            pl.BlockSpec((tm, d), lambda i, f: (i, 0)),
            pl.BlockSpec((1, d), lambda i, f: (0, 0)),
            pl.BlockSpec((d, tf), lambda i, f: (0, f)),
            pl.BlockSpec((tf, d), lambda i, f: (f, 0)),
        ],
        out_specs=pl.BlockSpec((tm, d), lambda i, f: (i, 0)),
        out_shape=jax.ShapeDtypeStruct((t, d), F32),
        scratch_shapes=[pltpu.VMEM((tm, d), BF)],
        compiler_params=_params(("arbitrary", "arbitrary")),
        name="ffn",
    )(x, g, w1, w2)


def _mem_kv_body(m_ref, g_ref, wk_ref, wv_ref, gk_ref, k_ref, v_ref):
    mn = _rms(m_ref[...], g_ref[...]).astype(BF)
    k = _mm(mn, wk_ref[...])
    gk = gk_ref[...]
    for h in range(MEM_HEADS):
        sl = slice(h * MEM_DIM, (h + 1) * MEM_DIM)
        k_ref[:, sl] = _rms(k[:, sl], gk)
    v_ref[...] = _mm(mn, wv_ref[...])


def _mem_kv(mem, g, wk, wv, gk):
    t, d = mem.shape
    tm = _tile(t, 256)
    mw = MEM_HEADS * MEM_DIM
    const = lambda i: (0, 0)
    return pl.pallas_call(
        _mem_kv_body,
        grid=(t // tm,),
        in_specs=[
            pl.BlockSpec((tm, d), lambda i: (i, 0)),
            pl.BlockSpec((1, d), const),
            pl.BlockSpec(wk.shape, const),
            pl.BlockSpec(wv.shape, const),
            pl.BlockSpec((1, MEM_DIM), const),
        ],
        out_specs=[pl.BlockSpec((tm, mw), lambda i: (i, 0)), pl.BlockSpec((tm, mw), lambda i: (i, 0))],
        out_shape=[jax.ShapeDtypeStruct((t, mw), F32), jax.ShapeDtypeStruct((t, mw), F32)],
        compiler_params=_params(("arbitrary",)),
        name="mem_kv",
    )(mem, g, wk, wv, gk)


def _rope_tables(pos):
    half = QK_ROPE // 2
    inv_freq = ROPE_THETA ** (-np.arange(half, dtype=np.float64) / half)
    ang = np.asarray(pos, np.float64)[:, None] * inv_freq[None, :]
    cos, sin = np.cos(ang), np.sin(ang)
    zeros = np.zeros((ang.shape[0], LANE - QK_ROPE))
    return (jnp.asarray(np.concatenate([cos, cos, zeros], axis=1), F32),
            jnp.asarray(np.concatenate([-sin, sin, zeros], axis=1), F32))


def _pad_lanes(v, n=LANE):
    return jnp.pad(v, ((0, 0), (0, n - v.shape[1])))


def _prep_weights(w_in, g_cq, w_uq, g_ckv, w_ukv, g_q, g_k, w_o, w_mq, w_mo, w_ff1, w_ff2, w_mk, w_mv):
    d = w_in.shape[0]
    n_conv = 2 * GDN_HEADS * GDN_DK + GDN_HEADS * GDN_DV
    n_z = GDN_HEADS * GDN_DV
    wt = w_in.T.astype(BF)
    o = 0
    w_cq = wt[o:o + Q_RANK]; o += Q_RANK
    w_ckv = wt[o:o + KV_RANK]; o += KV_RANK
    w_kpe = wt[o:o + QK_ROPE]; o += QK_ROPE
    w_qkv = wt[o:o + n_conv]; o += n_conv
    w_z = wt[o:o + n_z]; o += n_z
    w_b = wt[o:o + GDN_HEADS]; o += GDN_HEADS
    w_a = wt[o:o + GDN_HEADS]
    half = QK_ROPE // 2
    swap = jnp.concatenate([jnp.arange(half, QK_ROPE), jnp.arange(0, half)])
    z64 = jnp.zeros((LANE - QK_ROPE, d), BF)
    misc = jnp.concatenate([w_kpe, z64, w_kpe[swap], z64, w_b, w_a], axis=0)
    misc = jnp.pad(misc, ((0, MISC_ROWS - misc.shape[0]), (0, 0)))
    w_in_p = jnp.concatenate([w_cq, w_ckv, w_qkv, w_z, misc], axis=0)
    wbat = jnp.concatenate([w_b, w_a], axis=0)

    r = w_uq.shape[0]
    wq3 = w_uq.reshape(r, MLA_HEADS, QK_HEAD)
    zq = jnp.zeros((r, MLA_HEADS, LANE - QK_ROPE), w_uq.dtype)
    w1 = jnp.concatenate([wq3, zq], axis=2).reshape(r, MLA_HEADS * QPAD).astype(BF)
    w2 = jnp.concatenate([wq3[:, :, QK_NOPE:][:, :, swap], zq], axis=2).reshape(r, MLA_HEADS * LANE).astype(BF)

    dmla = MLA_HEADS * V_DIM
    return dict(
        n_conv=n_conv, n_z=n_z, w_in_p=w_in_p, wbat=wbat, w1=w1, w2=w2,
        gcq=g_cq[None, :], gckv=g_ckv[None, :],
        gq_n=g_q[None, :QK_NOPE], gq_r=_pad_lanes(g_q[None, QK_NOPE:]),
        gk_n=g_k[None, :QK_NOPE], gk_r=_pad_lanes(g_k[None, QK_NOPE:]),
        w_ukv=w_ukv.astype(BF), wo1=w_o[:dmla].astype(BF), wo2=w_o[dmla:].astype(BF),
        w_mq=w_mq.astype(BF), w_mo=w_mo.astype(BF), w_ff1=w_ff1.astype(BF), w_ff2=w_ff2.astype(BF),
        w_mk=w_mk.astype(BF), w_mv=w_mv.astype(BF),
    )


def _gate_params(a_log, dt_bias):
    z8 = jnp.zeros((GDN_HEADS,), F32)
    al = jnp.concatenate([z8, a_log.astype(F32)])
    db = jnp.concatenate([z8, dt_bias.astype(F32)])
    return _pad_lanes(al[None, :]), _pad_lanes(db[None, :]), al[:, None], db[:, None]


def _layer(x, pos, past, conv_past, s0, mem_k, mem_v, wp, lw):
    nb, s_len, d = x.shape
    t = nb * s_len
    xf = x.reshape(t, d)
    ctab, stab = _rope_tables(pos)
    if ctab.shape[0] % 8 != 0 or (s_len < IN_TM and nb > 1):
        ctab, stab = jnp.tile(ctab, (nb, 1)), jnp.tile(stab, (nb, 1))

    cqn, ckv, kpe_pad, qkv, z, ba, bat = _in_proj(
        xf, lw["g_norm_mix"], wp["w_in_p"], wp["wbat"], wp["gcq"], wp["gckv"], ctab, stab,
        wp["n_conv"], wp["n_z"])
    q = _q_proj(cqn, wp["w1"], wp["w2"], ctab, stab, wp["gq_n"], wp["gq_r"])

    if past is None:
        k, v = _kv_proj(ckv, kpe_pad, wp["w_ukv"], wp["gk_n"], wp["gk_r"])
        o_mla = _attention(q, k, v, lw["g_mla_out"], nb)
    else:
        ckv_past, kpe_past = past
        o_mla = _cached_attention(q, ckv_past.astype(F32), kpe_past.astype(F32), ckv, kpe_pad, wp["w_ukv"],
                                  wp["gk_n"], wp["gk_r"], lw["g_mla_out"], nb)

    conv_past8 = jnp.pad(conv_past.astype(F32), ((0, 0), (HALO - (CONV_W - 1), 0), (0, 0)))
    conv_past8 = conv_past8.reshape(nb * HALO, -1)
    bat3 = bat.reshape(16, nb, s_len).transpose(1, 0, 2)
    alog_l, dtb_l, alog_c, dtb_c = _gate_params(lw["a_log"], lw["dt_bias"])
    o_gdn, s_new = _gdn(qkv, lw["conv_w"], conv_past8, ba, bat3, alog_l, dtb_l, alog_c, dtb_c,
                        z, s0.astype(F32), lw["g_gdn_out"], nb)

    h2 = _mix_mem(xf, o_mla, o_gdn, wp["wo1"], wp["wo2"], lw["g_norm_mem_q"], wp["w_mq"],
                  lw["g_mq"], mem_k, mem_v, wp["w_mo"], nb)
    y = _ffn(h2, lw["g_norm_ffn"], wp["w_ff1"], wp["w_ff2"])

    conv_in_tail = jnp.concatenate([conv_past.astype(F32), qkv.reshape(nb, s_len, -1)[:, -(CONV_W - 1):]], axis=1)
    conv_new = conv_in_tail[:, -(CONV_W - 1):]
    return (y.reshape(nb, s_len, d), ckv.reshape(nb, s_len, KV_RANK),
            kpe_pad[:, :QK_ROPE].reshape(nb, s_len, QK_ROPE), conv_new, s_new)


def kernel(x_prompt, x_sample, mem_prompt, cache_mla_ckv, cache_mla_kpe, cache_gdn_conv, state_gdn, cache_mem_k, cache_mem_v, g_norm_mix, w_in, g_cq, w_uq, g_ckv, w_ukv, g_q_mla, g_k_mla, g_mla_out, conv_w, a_log, dt_bias, g_gdn_out, w_o, g_norm_mem_q, g_norm_mem_kv, w_mq, w_mk, w_mv, g_mq, g_mk, w_mo, g_norm_ffn, w_ff1, w_ff2):
    depth = w_in.shape[0]
    nbp, sp, d = x_prompt.shape
    nbs, ss, _ = x_sample.shape
    n_mem = mem_prompt.shape[1]
    mw = MEM_HEADS * MEM_DIM
    n_conv = 2 * GDN_HEADS * GDN_DK + GDN_HEADS * GDN_DV
    pos_p = np.arange(sp)
    pos_s = cache_mla_ckv.shape[2] + np.arange(ss)
    zeros_conv = jnp.zeros((nbp, CONV_W - 1, n_conv), F32)
    zeros_state = jnp.zeros((nbp, GDN_HEADS, GDN_DK, GDN_DV), F32)
    hp, hs = x_prompt, x_sample
    outs_p = [[] for _ in range(6)]
    outs_s = [[] for _ in range(4)]
    for l in range(depth):
        wp = _prep_weights(w_in[l], g_cq[l], w_uq[l], g_ckv[l], w_ukv[l], g_q_mla[l], g_k_mla[l], w_o[l],
                           w_mq[l], w_mo[l], w_ff1[l], w_ff2[l], w_mk[l], w_mv[l])
        lw = dict(g_norm_mix=g_norm_mix[l][None, :], g_mla_out=g_mla_out[l][None, :], conv_w=conv_w[l],
                  a_log=a_log[l], dt_bias=dt_bias[l], g_gdn_out=g_gdn_out[l][None, :],
                  g_norm_mem_q=g_norm_mem_q[l][None, :], g_mq=g_mq[l][None, :],
                  g_norm_ffn=g_norm_ffn[l][None, :])
        mk, mv = _mem_kv(mem_prompt.reshape(nbp * n_mem, d), g_norm_mem_kv[l][None, :], wp["w_mk"], wp["w_mv"],
                         g_mk[l][None, :])
        mk3, mv3 = mk.reshape(nbp, n_mem, mw), mv.reshape(nbp, n_mem, mw)
        hp, c1, c2, c3, c4 = _layer(hp, pos_p, None, zeros_conv, zeros_state, mk3.astype(BF), mv3.astype(BF), wp, lw)
        for lst, val in zip(outs_p, (c1, c2, c3, c4, mk3.reshape(nbp, n_mem, MEM_HEADS, MEM_DIM),
                                     mv3.reshape(nbp, n_mem, MEM_HEADS, MEM_DIM))):
            lst.append(val)
        hs, d1, d2, d3, d4 = _layer(hs, pos_s, (cache_mla_ckv[l], cache_mla_kpe[l]), cache_gdn_conv[l], state_gdn[l],
                                    cache_mem_k[l].reshape(nbs, n_mem, mw).astype(BF),
                                    cache_mem_v[l].reshape(nbs, n_mem, mw).astype(BF), wp, lw)
        for lst, val in zip(outs_s, (d1, d2, d3, d4)):
            lst.append(val)
    return (hp, hs, *(jnp.stack(v) for v in outs_p), *(jnp.stack(v) for v in outs_s))
```
